```python
import jax
import jax.numpy as jnp
from jax import lax
import numpy as np

D_MODEL = 1024
BATCH = 2
SEQ = 8192
DEPTH = 2

D_MIX = D_MODEL
D_GROUP = D_MIX // 4
HEAD_DIM = 64
N_HEADS = D_GROUP // HEAD_DIM
Q_BLOCK = 128
RWKV_LORA = 32
CONV_WIDTH = 4
LRU_C = 8.0
RMS_EPS = 1e-6
GN_EPS = 64e-5
N_RWKV_SHIFT = 3 * D_GROUP + 2 * RWKV_LORA
N_IN = 14 * D_GROUP + N_HEADS + 2 * RWKV_LORA

kernel_name = "hybrid_fox_stickbreak_rwkv7_rglru"


def rms_norm(x, g):
    x32 = x.astype(jnp.float32)
    y = x32 * lax.rsqrt(jnp.mean(x32 * x32, axis=-1, keepdims=True) + RMS_EPS)
    return (y * g.astype(jnp.float32)).astype(x.dtype)


def split_columns(p):
    G, H = D_GROUP, N_HEADS
    sizes = [G, G, G, G, H, G, G, G, G, N_RWKV_SHIFT, G, G, G]
    cuts, acc = [], 0
    for n in sizes[:-1]:
        acc += n
        cuts.append(acc)
    return jnp.split(p, cuts, axis=-1)


def to_heads(t):
    b, s, _ = t.shape
    return t.reshape(b, s, N_HEADS, HEAD_DIM).transpose(0, 2, 1, 3).astype(jnp.float32)


def query_blocks(t):
    b, h, s = t.shape[:3]
    t = t.reshape((b, h, s // Q_BLOCK, Q_BLOCK) + t.shape[3:])
    return jnp.moveaxis(t, 2, 0)


def merge_blocks(o):
    nb, b, h, q, d = o.shape
    return o.transpose(1, 0, 3, 2, 4).reshape(b, nb * q, h * d)


def forgetting_attention(q, k, v, log_f):
    s = q.shape[2]
    scale = HEAD_DIM ** -0.5
    cum = jnp.cumsum(log_f, axis=-1)
    key_pos = jnp.arange(s)

    def block(args):
        qb, cb, i = args
        q_pos = i * Q_BLOCK + jnp.arange(Q_BLOCK)
        logits = (jnp.einsum('bhqd,bhkd->bhqk', qb, k) * scale
                  + cb[..., None] - cum[:, :, None, :])
        logits = jnp.where(key_pos[None, :] <= q_pos[:, None], logits, -jnp.inf)
        return jnp.einsum('bhqk,bhkd->bhqd', jax.nn.softmax(logits, axis=-1), v)

    out = lax.map(block, (query_blocks(q), query_blocks(cum), jnp.arange(s // Q_BLOCK)))
    return merge_blocks(out)


def stick_breaking_attention(q, k, v):
    s = q.shape[2]
    scale = HEAD_DIM ** -0.5
    key_pos = jnp.arange(s)

    def block(args):
        qb, i = args
        q_pos = i * Q_BLOCK + jnp.arange(Q_BLOCK)
        z = jnp.einsum('bhqd,bhkd->bhqk', qb, k) * scale
        mask = key_pos[None, :] < q_pos[:, None]
        log_keep = jnp.where(mask, jax.nn.log_sigmoid(-z), 0.0)
        log_rest = lax.cumsum(log_keep, axis=3, reverse=True) - log_keep
        att = jnp.where(mask, jnp.exp(jax.nn.log_sigmoid(z) + log_rest), 0.0)
        return jnp.einsum('bhqk,bhkd->bhqd', att, v)

    out = lax.map(block, (query_blocks(q), jnp.arange(s // Q_BLOCK)))
    return merge_blocks(out)


def rwkv7_time_mix(p, mu, w0, w2, a0, a2, k_k, k_a, r_k, ln_g, ln_b):
    b, s, _ = p.shape
    G, R = D_GROUP, RWKV_LORA
    p = p.astype(jnp.float32)
    prev = jnp.pad(p, ((0, 0), (1, 0), (0, 0)))[:, :-1]
    p = p + (prev - p) * mu
    r, k, v, wl, al = jnp.split(p, [G, 2 * G, 3 * G, 3 * G + R], axis=-1)
    w = -jax.nn.softplus(-(w0 + jnp.tanh(wl) @ w2)) - 0.5
    decay = jnp.exp(-jnp.exp(w))
    a = jax.nn.sigmoid(a0 + al @ a2)
    hs = lambda t: t.reshape(b, s, N_HEADS, HEAD_DIM)
    kk = hs(k * k_k)
    kk = kk * lax.rsqrt(jnp.maximum(jnp.sum(kk * kk, axis=-1, keepdims=True), 1e-12))
    k = hs(k * (1.0 + (a - 1.0) * k_a))
    r, v, decay, a = hs(r), hs(v), hs(decay), hs(a)

    def step(state, inp):
        r_t, w_t, k_t, v_t, kk_t, a_t = inp
        sa = jnp.einsum('bhij,bhj->bhi', state, -kk_t)
        state = (state * w_t[:, :, None, :]
                 + sa[..., None] * (kk_t * a_t)[:, :, None, :]
                 + v_t[..., None] * k_t[:, :, None, :])
        return state, jnp.einsum('bhij,bhj->bhi', state, r_t)

    xs = tuple(jnp.moveaxis(t, 1, 0) for t in (r, decay, k, v, kk, a))
    state0 = jnp.zeros((b, N_HEADS, HEAD_DIM, HEAD_DIM), jnp.float32)
    _, y = lax.scan(step, state0, xs)
    y = jnp.moveaxis(y, 0, 1)
    mean = jnp.mean(y, axis=-1, keepdims=True)
    var = jnp.mean(jnp.square(y - mean), axis=-1, keepdims=True)
    y = ((y - mean) * lax.rsqrt(var + GN_EPS)).reshape(b, s, G) * ln_g + ln_b
    bonus = jnp.sum(r * k * r_k, axis=-1, keepdims=True) * v
    return y + bonus.reshape(b, s, G)


def rg_lru(x, conv_w, conv_b, w_a, b_a, w_x, b_x, lam):
    x = x.astype(jnp.float32)
    b, s, c = x.shape
    xc = lax.conv_general_dilated(
        x, conv_w.astype(jnp.float32)[:, None, :], window_strides=(1,),
        padding=[(CONV_WIDTH - 1, 0)], dimension_numbers=('NWC', 'WIO', 'NWC'),
        feature_group_count=c) + conv_b
    xh = xc.reshape(b, s, N_HEADS, HEAD_DIM)
    r = jax.nn.sigmoid(jnp.einsum('bsni,nij->bsnj', xh, w_a).reshape(b, s, c) + b_a)
    i = jax.nn.sigmoid(jnp.einsum('bsni,nij->bsnj', xh, w_x).reshape(b, s, c) + b_x)
    log_a = -LRU_C * r * jax.nn.softplus(-lam)
    a = jnp.exp(log_a)
    u = jnp.sqrt(-jnp.expm1(2.0 * log_a)) * (i * xc)

    def combine(lhs, rhs):
        a1, b1 = lhs
        a2, b2 = rhs
        return a1 * a2, a2 * b1 + b2

    _, h = lax.associative_scan(combine, (a, u), axis=1)
    return h


def setup_inputs(seed: int = 0) -> dict:
    key = jax.random.key(seed)
    ks = jax.random.split(key, 24)
    f32 = jnp.float32
    L, D, G, H, R, N = DEPTH, D_MODEL, D_GROUP, N_HEADS, RWKV_LORA, HEAD_DIM

    def nrm(k, shape, scale):
        return scale * jax.random.normal(k, shape, f32)

    x = jax.random.normal(ks[0], (BATCH, SEQ, D), f32)
    norm_g = 1.0 + nrm(ks[1], (L, D), 0.05)
    w_in = nrm(ks[2], (L, D, N_IN), D ** -0.5)
    b_forget = 2.0 + nrm(ks[3], (L, H), 0.5)
    rwkv_mu = jax.random.uniform(ks[4], (L, N_RWKV_SHIFT), f32)
    rwkv_w0 = jax.random.uniform(ks[5], (L, G), f32, -6.0, 0.0)
    rwkv_w2 = nrm(ks[6], (L, R, G), 0.1 * R ** -0.5)
    rwkv_a0 = nrm(ks[7], (L, G), 0.1)
    rwkv_a2 = nrm(ks[8], (L, R, G), 0.1 * R ** -0.5)
    rwkv_k_k = 0.85 + nrm(ks[9], (L, G), 0.05)
    rwkv_k_a = 1.0 + nrm(ks[10], (L, G), 0.05)
    rwkv_r_k = nrm(ks[11], (L, H, N), 0.1)
    rwkv_ln_g = 1.0 + nrm(ks[12], (L, G), 0.05)
    rwkv_ln_b = nrm(ks[13], (L, G), 0.02)
    lru_conv_w = nrm(ks[14], (L, CONV_WIDTH, G), CONV_WIDTH ** -0.5)
    lru_conv_b = nrm(ks[15], (L, G), 0.02)
    lru_w_a = nrm(ks[16], (L, H, N, N), N ** -0.5)
    lru_b_a = nrm(ks[17], (L, G), 0.02)
    lru_w_x = nrm(ks[18], (L, H, N, N), N ** -0.5)
    lru_b_x = nrm(ks[19], (L, G), 0.02)
    a_pow_c = jax.random.uniform(ks[20], (L, G), f32, 0.9, 0.999)
    a_base = a_pow_c ** (1.0 / LRU_C)
    lru_lambda = jnp.log(a_base) - jnp.log1p(-a_base)
    w_out = nrm(ks[21], (L, D_MIX, D), D_MIX ** -0.5)
    final_g = 1.0 + nrm(ks[22], (D,), 0.05)
    return {"x": x, "norm_g": norm_g, "w_in": w_in, "b_forget": b_forget,
            "rwkv_mu": rwkv_mu, "rwkv_w0": rwkv_w0, "rwkv_w2": rwkv_w2,
            "rwkv_a0": rwkv_a0, "rwkv_a2": rwkv_a2, "rwkv_k_k": rwkv_k_k,
            "rwkv_k_a": rwkv_k_a, "rwkv_r_k": rwkv_r_k, "rwkv_ln_g": rwkv_ln_g,
            "rwkv_ln_b": rwkv_ln_b, "lru_conv_w": lru_conv_w, "lru_conv_b": lru_conv_b,
            "lru_w_a": lru_w_a, "lru_b_a": lru_b_a, "lru_w_x": lru_w_x,
            "lru_b_x": lru_b_x, "lru_lambda": lru_lambda, "w_out": w_out,
            "final_g": final_g}


def reference(x, norm_g, w_in, b_forget, rwkv_mu, rwkv_w0, rwkv_w2, rwkv_a0, rwkv_a2,
              rwkv_k_k, rwkv_k_a, rwkv_r_k, rwkv_ln_g, rwkv_ln_b, lru_conv_w, lru_conv_b,
              lru_w_a, lru_b_a, lru_w_x, lru_b_x, lru_lambda, w_out, final_g):
    f32 = jnp.float32
    for l in range(DEPTH):
        h = rms_norm(x, norm_g[l])
        p = h @ w_in[l]
        (fq, fk, fv, fg, ff, sq, sk, sv, sg, rw, rg, lx, lg) = split_columns(p)
        log_f = jax.nn.log_sigmoid(ff.astype(f32) + b_forget[l]).transpose(0, 2, 1)
        y_fox = forgetting_attention(to_heads(fq), to_heads(fk), to_heads(fv), log_f)
        y_sb = stick_breaking_attention(to_heads(sq), to_heads(sk), to_heads(sv))
        y_rw = rwkv7_time_mix(rw, rwkv_mu[l], rwkv_w0[l], rwkv_w2[l], rwkv_a0[l], rwkv_a2[l],
                              rwkv_k_k[l], rwkv_k_a[l], rwkv_r_k[l], rwkv_ln_g[l], rwkv_ln_b[l])
        y_lru = rg_lru(lx, lru_conv_w[l], lru_conv_b[l], lru_w_a[l], lru_b_a[l],
                       lru_w_x[l], lru_b_x[l], lru_lambda[l])
        y = jnp.concatenate([
            y_fox * jax.nn.silu(fg.astype(f32)),
            y_sb * jax.nn.silu(sg.astype(f32)),
            y_rw * jax.nn.silu(rg.astype(f32)),
            y_lru * jax.nn.silu(lg.astype(f32)),
        ], axis=-1).astype(x.dtype)
        x = x + y @ w_out[l]
    return rms_norm(x, final_g)
```

```python
import functools

import jax
import jax.numpy as jnp
from jax import lax
from jax.experimental import pallas as pl
from jax.experimental.pallas import tpu as pltpu

F32 = jnp.float32
BF16 = jnp.bfloat16

D_MODEL = 1024
D_GROUP = 256
N_HEADS = 4
HEAD_DIM = 64
RWKV_LORA = 32
CONV_WIDTH = 4
LRU_C = 8.0
RMS_EPS = 1e-6
GN_EPS = 64e-5
N_RKV = 3 * D_GROUP

LANES = 128
SUBLANES = 8
VMEM_LIMIT_BYTES = 48 * 1024 * 1024

C_FQKV = 0
C_SQKV = 3 * D_GROUP
C_GATES = 6 * D_GROUP
C_RKV = 10 * D_GROUP
C_LX = 13 * D_GROUP
C_MISC = 14 * D_GROUP
N_PROJ = C_MISC + LANES

QK_AUG = LANES
NEG_BIG = -1e30
SB_SKIP_LOG = -106.0
RWKV_CHUNK = 64


def _dot(a, b):
    return jnp.dot(a, b, preferred_element_type=F32)


def _dot_nt(a, b):
    return lax.dot_general(a, b, (((1,), (1,)), ((), ())), preferred_element_type=F32)


def _split2(x):
    hi = x.astype(BF16)
    lo = (x - hi.astype(F32)).astype(BF16)
    return hi, lo


def _split3(x):
    hi = x.astype(BF16)
    r1 = x - hi.astype(F32)
    mid = r1.astype(BF16)
    lo = (r1 - mid.astype(F32)).astype(BF16)
    return hi, mid, lo


def _dot3(a, b):
    ah, al = _split2(a)
    bh, bl = _split2(b)
    return _dot(ah, bh) + (_dot(al, bh) + _dot(ah, bl))


def _dot3_nt(a, b):
    ah, al = _split2(a)
    bh, bl = _split2(b)
    return _dot_nt(ah, bh) + (_dot_nt(al, bh) + _dot_nt(ah, bl))


def _dot_sel_lhs(sel, x):
    hi, mid, lo = _split3(x)
    return _dot(sel, hi) + (_dot(sel, mid) + _dot(sel, lo))


def _dot_sel_rhs(x, sel):
    hi, mid, lo = _split3(x)
    return _dot(hi, sel) + (_dot(mid, sel) + _dot(lo, sel))


def _softplus(x):
    return jnp.maximum(x, 0.0) + jnp.log1p(jnp.exp(-jnp.abs(x)))


def _sigmoid(x):
    return 1.0 / (1.0 + jnp.exp(-x))


def _iota(shape, dim):
    return lax.broadcasted_iota(jnp.int32, shape, dim)


def _inproj_kernel(x_ref, g_ref, w_ref, fqkv_ref, sqkv_ref, gates_ref, rkv_ref, lx_ref, misc_ref):
    x = x_ref[...]
    ms = jnp.mean(x * x, axis=-1, keepdims=True)
    h = (x * lax.rsqrt(ms + RMS_EPS) * g_ref[...]).astype(BF16)
    scale = HEAD_DIM ** -0.5

    def proj(c0, width):
        return _dot(h, w_ref[:, c0:c0 + width])

    for c0, ref in ((C_FQKV, fqkv_ref), (C_SQKV, sqkv_ref)):
        ref[:, 0:D_GROUP] = (proj(c0, D_GROUP) * scale).astype(BF16)
        ref[:, D_GROUP:3 * D_GROUP] = proj(c0 + D_GROUP, 2 * D_GROUP).astype(BF16)
    gates_ref[...] = proj(C_GATES, 4 * D_GROUP)
    rkv_ref[...] = proj(C_RKV, N_RKV)
    lx_ref[...] = proj(C_LX, D_GROUP)
    misc_ref[...] = proj(C_MISC, LANES)


def _inproj(x2, g, w, *, tm):
    t = x2.shape[0]
    row = lambda width: pl.BlockSpec((tm, width), lambda i: (i, 0))
    full = lambda a: pl.BlockSpec(a.shape, lambda i: (0,) * a.ndim)
    return pl.pallas_call(
        _inproj_kernel,
        grid=(t // tm,),
        in_specs=[row(D_MODEL), full(g), full(w)],
        out_specs=[row(3 * D_GROUP), row(3 * D_GROUP), row(4 * D_GROUP), row(N_RKV),
                   row(D_GROUP), row(LANES)],
        out_shape=[jax.ShapeDtypeStruct((t, 3 * D_GROUP), BF16),
                   jax.ShapeDtypeStruct((t, 3 * D_GROUP), BF16),
                   jax.ShapeDtypeStruct((t, 4 * D_GROUP), F32),
                   jax.ShapeDtypeStruct((t, N_RKV), F32),
                   jax.ShapeDtypeStruct((t, D_GROUP), F32),
                   jax.ShapeDtypeStruct((t, LANES), F32)],
        compiler_params=pltpu.CompilerParams(
            dimension_semantics=("arbitrary",), vmem_limit_bytes=VMEM_LIMIT_BYTES),
        name="inproj",
    )(x2, g, w)


def _forget_kernel(ff_ref, b_ref, hi_ref, mid_ref, lo_ref):
    x = ff_ref[0] + b_ref[0]
    nr = x.shape[0]
    lf = jnp.minimum(x, 0.0) - jnp.log1p(jnp.exp(-jnp.abs(x)))
    upper = (_iota((LANES, LANES), 0) <= _iota((LANES, LANES), 1)).astype(BF16)
    within = _dot_sel_rhs(lf, upper)
    tot = jnp.broadcast_to(within[:, LANES - 1:LANES], (nr, LANES))
    before = (_iota((nr, nr), 1) < _iota((nr, nr), 0)).astype(BF16)
    f = within + _dot_sel_lhs(before, tot)
    hi, mid, lo = _split3(f)
    hi_ref[0] = hi
    mid_ref[0] = mid
    lo_ref[0] = lo


def _forget_cumsum(ff, bias):
    bh, nr, _ = ff.shape
    blk = pl.BlockSpec((1, nr, LANES), lambda i: (i, 0, 0))
    return pl.pallas_call(
        _forget_kernel,
        grid=(bh,),
        in_specs=[blk, pl.BlockSpec((1, 1, LANES), lambda i: (i, 0, 0))],
        out_specs=[blk, blk, blk],
        out_shape=[jax.ShapeDtypeStruct((bh, nr, LANES), BF16)] * 3,
        compiler_params=pltpu.CompilerParams(dimension_semantics=("arbitrary",)),
        name="forget_cumsum",
    )(ff, bias)


def _fox_kernel(q_ref, k_ref, vt_ref, o_ref, *, tq, tk):
    i = pl.program_id(1)
    q = q_ref[0]

    def step(j, carry, masked):
        m, l, acc = carry
        ks = pl.multiple_of(j * tk, tk)
        s = _dot_nt(k_ref[0, pl.ds(ks, tk), :], q)
        if masked:
            kpos = ks + _iota((tk, tq), 0)
            qpos = i * tq + _iota((tk, tq), 1)
            s = jnp.where(kpos <= qpos, s, NEG_BIG)
        m_new = jnp.maximum(m, jnp.max(s, axis=0, keepdims=True))
        alpha = jnp.exp(m - m_new)
        p = jnp.exp(s - m_new)
        l = alpha * l + jnp.sum(p, axis=0, keepdims=True)
        acc = alpha * acc + _dot(vt_ref[0, :, pl.ds(ks, tk)], p.astype(BF16))
        return m_new, l, acc

    init = (jnp.full((1, tq), NEG_BIG, F32), jnp.zeros((1, tq), F32),
            jnp.zeros((HEAD_DIM, tq), F32))
    n_full = i * (tq // tk)
    carry = lax.fori_loop(0, n_full, lambda j, c: step(j, c, False), init)
    for d in range(tq // tk):
        carry = step(n_full + d, carry, True)
    _, l, acc = carry
    o_ref[0] = acc / l


def _fox_attention(q, k, vt, *, tq, tk):
    bh, s, _ = q.shape
    return pl.pallas_call(
        functools.partial(_fox_kernel, tq=tq, tk=tk),
        grid=(bh, s // tq),
        in_specs=[pl.BlockSpec((1, tq, QK_AUG), lambda b, i: (b, i, 0)),
                  pl.BlockSpec((1, s, QK_AUG), lambda b, i: (b, 0, 0)),
                  pl.BlockSpec((1, HEAD_DIM, s), lambda b, i: (b, 0, 0))],
        out_specs=pl.BlockSpec((1, HEAD_DIM, tq), lambda b, i: (b, 0, i)),
        out_shape=jax.ShapeDtypeStruct((bh, HEAD_DIM, s), F32),
        compiler_params=pltpu.CompilerParams(
            dimension_semantics=("arbitrary", "arbitrary"), vmem_limit_bytes=VMEM_LIMIT_BYTES),
        name="fox_attention",
    )(q, k, vt)


def _sb_kernel(q_ref, k_ref, vt_ref, o_ref, *, tq):
    i = pl.program_id(1)
    tk = tq
    q = q_ref[0]
    later = (_iota((tk, tk), 0) < _iota((tk, tk), 1)).astype(BF16)

    def block(j, rest_q, acc, masked):
        ks = pl.multiple_of(j * tk, tk)
        z = _dot_nt(k_ref[0, pl.ds(ks, tk), :], q)
        log_keep = jnp.minimum(-z, 0.0) - jnp.log1p(jnp.exp(-jnp.abs(z)))
        if masked:
            mask = _iota((tk, tq), 0) < _iota((tk, tq), 1)
            log_keep = jnp.where(mask, log_keep, 0.0)
        hi, lo = _split2(log_keep)
        rest_in = _dot(later, hi) + _dot(later, lo)
        att = jnp.exp(z + log_keep + rest_in + rest_q)
        if masked:
            att = jnp.where(mask, att, 0.0)
        acc = acc + _dot(vt_ref[0, :, pl.ds(ks, tk)], att.astype(BF16))
        rest_q = rest_q + rest_in[0:1, :] + log_keep[0:1, :]
        return rest_q, acc

    rest_q, acc = block(i, jnp.zeros((1, tq), F32), jnp.zeros((HEAD_DIM, tq), F32), True)

    def cond(c):
        j, rest_q, _ = c
        return jnp.logical_and(j >= 0, jnp.max(rest_q) > SB_SKIP_LOG)

    def body(c):
        j, rest_q, acc = c
        rest_q, acc = block(j, rest_q, acc, False)
        return j - 1, rest_q, acc

    _, _, acc = lax.while_loop(cond, body, (i - 1, rest_q, acc))
    o_ref[0] = acc


def _sb_attention(q, k, vt, *, tq):
    bh, s, _ = q.shape
    return pl.pallas_call(
        functools.partial(_sb_kernel, tq=tq),
        grid=(bh, s // tq),
        in_specs=[pl.BlockSpec((1, tq, HEAD_DIM), lambda b, i: (b, i, 0)),
                  pl.BlockSpec((1, s, HEAD_DIM), lambda b, i: (b, 0, 0)),
                  pl.BlockSpec((1, HEAD_DIM, s), lambda b, i: (b, 0, 0))],
        out_specs=pl.BlockSpec((1, HEAD_DIM, tq), lambda b, i: (b, 0, i)),
        out_shape=jax.ShapeDtypeStruct((bh, HEAD_DIM, s), F32),
        compiler_params=pltpu.CompilerParams(
            dimension_semantics=("arbitrary", "arbitrary"), vmem_limit_bytes=VMEM_LIMIT_BYTES),
        name="sb_attention",
    )(q, k, vt)


def _rwkv_kernel(p_ref, m_ref, mu_rkv_ref, mu_misc_ref, w0_ref, w2_ref, a0_ref, a2_ref,
                 kk_ref, ka_ref, rk_ref, lng_ref, lnb_ref, bdm_ref, o_ref,
                 pad_rkv, pad_misc, r_s, k_s, v_s, kn_s, al_s, lw_s, y_s, ht_s, *, tt):
    c = RWKV_CHUNK
    g = D_GROUP
    t_idx = pl.program_id(1)

    @pl.when(t_idx == 0)
    def _():
        pad_rkv[0:SUBLANES, :] = jnp.zeros((SUBLANES, N_RKV), F32)
        pad_misc[0:SUBLANES, :] = jnp.zeros((SUBLANES, LANES), F32)
        ht_s[...] = jnp.zeros((g, g), F32)

    bdm = bdm_ref[...]
    bdm_f = bdm.astype(F32)

    p = p_ref[0]
    misc = m_ref[0]
    pad_rkv[SUBLANES:SUBLANES + tt, :] = p
    pad_misc[SUBLANES:SUBLANES + tt, :] = misc
    p_prev = pad_rkv[SUBLANES - 1:SUBLANES - 1 + tt, :]
    m_prev = pad_misc[SUBLANES - 1:SUBLANES - 1 + tt, :]
    pad_rkv[0:SUBLANES, :] = p[tt - SUBLANES:tt, :]
    pad_misc[0:SUBLANES, :] = misc[tt - SUBLANES:tt, :]
    p = p + (p_prev - p) * mu_rkv_ref[...]
    misc = misc + (m_prev - misc) * mu_misc_ref[...]
    r = p[:, 0:g]
    k = p[:, g:2 * g]
    v = p[:, 2 * g:3 * g]

    w = -_softplus(-(w0_ref[...] + _dot3(jnp.tanh(misc), w2_ref[...]))) - 0.5
    alpha = _sigmoid(a0_ref[...] + _dot3(misc, a2_ref[...]))
    kn = k * kk_ref[...]
    ss = _dot_sel_rhs(kn * kn, bdm)
    kn = kn * lax.rsqrt(jnp.maximum(ss, 1e-12))
    k = k * (1.0 + (alpha - 1.0) * ka_ref[...])
    r_s[...] = r
    k_s[...] = k
    v_s[...] = v
    kn_s[...] = kn
    al_s[...] = alpha
    lw_s[...] = -jnp.exp(w)

    row = _iota((c, g), 0)
    col = _iota((c, g), 1) % c
    strict = col < row
    incl = col <= row
    eye = (col == row).astype(F32)
    lower_c = (_iota((c, c), 1) <= _iota((c, c), 0)).astype(BF16)
    level_masks = []
    m = 1
    while m < c:
        level_masks.append(jnp.logical_and(
            strict, jnp.logical_and(row // (2 * m) == col // (2 * m), row // m != col // m)))
        m *= 2

    def bd(x):
        hi, lo = _split2(x)
        tile = lambda a: jnp.concatenate([a] * N_HEADS, axis=0) * bdm
        return tile(hi), tile(lo)

    def mm(a, b_bd):
        ah, al = _split2(a)
        bh, bl = b_bd
        return _dot(ah, bh) + (_dot(al, bh) + _dot(ah, bl))

    def mm_nt(a, b_bd):
        ah, al = _split2(a)
        bh, bl = b_bd
        return _dot_nt(ah, bh) + (_dot_nt(al, bh) + _dot_nt(ah, bl))

    def chunk(ci, carry):
        base = pl.multiple_of(ci * c, c)
        sl = pl.ds(base, c)
        r_c, k_c, v_c, kn_c, al_c, lw_c = (s[sl, :] for s in (r_s, k_s, v_s, kn_s, al_s, lw_s))
        cl = _dot_sel_lhs(lower_c, lw_c)
        cl_last = cl[c - 1:c, :]
        a_t = -kn_c * jnp.exp(cl - lw_c)
        r_t = r_c * jnp.exp(cl)
        q_inv = jnp.exp(-cl)
        p_rem = jnp.exp(cl_last - cl)
        kna = kn_c * al_c
        ar = jnp.concatenate([a_t, r_t], axis=0)
        s_b = mm_nt(ar, bd(kna * q_inv))
        s_k = mm_nt(ar, bd(k_c * q_inv))
        n = jnp.where(strict, s_b[0:c], 0.0)
        a_ak = jnp.where(strict, s_k[0:c], 0.0)
        m_rb = jnp.where(incl, s_b[c:2 * c], 0.0)
        m_rk = jnp.where(incl, s_k[c:2 * c], 0.0)

        inv = eye + jnp.where(level_masks[0], n, 0.0)
        for lm in level_masks[1:]:
            inv = inv + mm(mm(inv, bd(jnp.where(lm, n, 0.0))), bd(inv))

        ht = ht_s[...]
        arh = mm_nt(ar, _split2(ht))
        v_bd = bd(v_c)
        u = mm(inv, bd(arh[0:c] + mm(a_ak, v_bd)))
        y_s[sl, :] = arh[c:2 * c] + mm(m_rb, bd(u)) + mm(m_rk, v_bd)
        uv_t = jnp.concatenate([u, v_c], axis=0).T
        bk = jnp.concatenate([kna * p_rem, k_c * p_rem], axis=0)
        ht_s[...] = ht * jnp.exp(cl_last) + bdm_f * _dot3(uv_t, bk)
        return carry

    lax.fori_loop(0, tt // c, chunk, 0)

    y = y_s[...]
    r = r_s[...]
    k = k_s[...]
    v = v_s[...]
    inv_n = 1.0 / HEAD_DIM
    mean = _dot_sel_rhs(y, bdm) * inv_n
    yc = y - mean
    var = _dot_sel_rhs(yc * yc, bdm) * inv_n
    yn = yc * lax.rsqrt(var + GN_EPS) * lng_ref[...] + lnb_ref[...]
    bonus = _dot_sel_rhs(r * k * rk_ref[...], bdm) * v
    o_ref[0] = yn + bonus


def _rwkv(p_rkv, misc, prm, bdm, *, tt):
    b, s, _ = p_rkv.shape
    g = D_GROUP
    full = lambda a: pl.BlockSpec(a.shape, lambda bi, ti: (0,) * a.ndim)
    seq = lambda width: pl.BlockSpec((1, tt, width), lambda bi, ti: (bi, ti, 0))
    return pl.pallas_call(
        functools.partial(_rwkv_kernel, tt=tt),
        grid=(b, s // tt),
        in_specs=[seq(N_RKV), seq(LANES)] + [full(a) for a in prm] + [full(bdm)],
        out_specs=seq(g),
        out_shape=jax.ShapeDtypeStruct((b, s, g), F32),
        scratch_shapes=[pltpu.VMEM((tt + SUBLANES, N_RKV), F32),
                        pltpu.VMEM((tt + SUBLANES, LANES), F32)]
        + [pltpu.VMEM((tt, g), F32)] * 7
        + [pltpu.VMEM((g, g), F32)],
        compiler_params=pltpu.CompilerParams(
            dimension_semantics=("arbitrary", "arbitrary"), vmem_limit_bytes=VMEM_LIMIT_BYTES),
        name="rwkv7",
    )(p_rkv, misc, *prm, bdm)


def _lru_kernel(x_ref, cw_ref, cb_ref, wa_ref, ba_ref, wx_ref, bx_ref, lam_ref, o_ref,
                pad, a_s, u_s, h_s, *, tt):
    t_idx = pl.program_id(1)

    @pl.when(t_idx == 0)
    def _():
        pad[0:SUBLANES, :] = jnp.zeros((SUBLANES, D_GROUP), F32)
        h_s[...] = jnp.zeros((1, D_GROUP), F32)

    x = x_ref[0]
    pad[SUBLANES:SUBLANES + tt, :] = x
    xc = cw_ref[CONV_WIDTH - 1:CONV_WIDTH, :] * x + cb_ref[...]
    for d in range(1, CONV_WIDTH):
        tap = CONV_WIDTH - 1 - d
        xc = xc + cw_ref[tap:tap + 1, :] * pad[SUBLANES - d:SUBLANES - d + tt, :]
    pad[0:SUBLANES, :] = x[tt - SUBLANES:tt, :]

    xb = xc.astype(BF16)
    r = _sigmoid(_dot(xb, wa_ref[...]) + ba_ref[...])
    i = _sigmoid(_dot(xb, wx_ref[...]) + bx_ref[...])
    log_a = -LRU_C * r * _softplus(-lam_ref[...])
    a_s[...] = jnp.exp(log_a)
    th = jnp.tanh(log_a)
    u_s[...] = jnp.sqrt(-2.0 * th / (1.0 - th)) * (i * xc)

    def group(gi, h):
        base = pl.multiple_of(gi * SUBLANES, SUBLANES)
        a8 = a_s[pl.ds(base, SUBLANES), :]
        u8 = u_s[pl.ds(base, SUBLANES), :]
        rows = []
        for j in range(SUBLANES):
            h = a8[j:j + 1, :] * h + u8[j:j + 1, :]
            rows.append(h)
        o_ref[0, pl.ds(base, SUBLANES), :] = jnp.concatenate(rows, axis=0)
        return h

    h_s[...] = lax.fori_loop(0, tt // SUBLANES, group, h_s[...])


def _lru(x, prm, *, tt):
    b, s, g = x.shape
    full = lambda a: pl.BlockSpec(a.shape, lambda bi, ti: (0,) * a.ndim)
    seq = pl.BlockSpec((1, tt, g), lambda bi, ti: (bi, ti, 0))
    return pl.pallas_call(
        functools.partial(_lru_kernel, tt=tt),
        grid=(b, s // tt),
        in_specs=[seq] + [full(a) for a in prm],
        out_specs=seq,
        out_shape=jax.ShapeDtypeStruct((b, s, g), F32),
        scratch_shapes=[pltpu.VMEM((tt + SUBLANES, g), F32), pltpu.VMEM((tt, g), F32),
                        pltpu.VMEM((tt, g), F32), pltpu.VMEM((1, g), F32)],
        compiler_params=pltpu.CompilerParams(
            dimension_semantics=("arbitrary", "arbitrary"), vmem_limit_bytes=VMEM_LIMIT_BYTES),
        name="rg_lru",
    )(x, *prm)


def _outproj_kernel(x_ref, yf_ref, ys_ref, yr_ref, yl_ref, gates_ref, w_ref, fg_ref, o_ref, *,
                    final):
    acc = x_ref[...]
    for gi, y_ref in enumerate((yf_ref, ys_ref, yr_ref, yl_ref)):
        gate = gates_ref[:, gi * D_GROUP:(gi + 1) * D_GROUP]
        y = y_ref[...] * (gate * _sigmoid(gate))
        acc = acc + _dot(y.astype(BF16), w_ref[gi * D_GROUP:(gi + 1) * D_GROUP, :])
    if final:
        ms = jnp.mean(acc * acc, axis=-1, keepdims=True)
        acc = acc * lax.rsqrt(ms + RMS_EPS) * fg_ref[...]
    o_ref[...] = acc


def _outproj(x2, ys, gates, w, final_g, *, tm, final):
    t = x2.shape[0]
    row = lambda width: pl.BlockSpec((tm, width), lambda i: (i, 0))
    full = lambda a: pl.BlockSpec(a.shape, lambda i: (0,) * a.ndim)
    return pl.pallas_call(
        functools.partial(_outproj_kernel, final=final),
        grid=(t // tm,),
        in_specs=[row(D_MODEL)] + [row(D_GROUP)] * 4 + [row(4 * D_GROUP), full(w), full(final_g)],
        out_specs=row(D_MODEL),
        out_shape=jax.ShapeDtypeStruct((t, D_MODEL), F32),
        compiler_params=pltpu.CompilerParams(
            dimension_semantics=("arbitrary",), vmem_limit_bytes=VMEM_LIMIT_BYTES),
        name="outproj",
    )(x2, *ys, gates, w, final_g)


def _permute_w_in(w):
    g, h, r = D_GROUP, N_HEADS, RWKV_LORA
    o_ff = 4 * g
    o_sb = o_ff + h
    o_rw = o_sb + 4 * g
    o_rg = o_rw + 3 * g + 2 * r
    o_lx = o_rg + g
    o_lg = o_lx + g
    cols = [w[:, 0:3 * g], w[:, o_sb:o_sb + 3 * g],
            w[:, 3 * g:4 * g], w[:, o_sb + 3 * g:o_sb + 4 * g], w[:, o_rg:o_rg + g],
            w[:, o_lg:o_lg + g],
            w[:, o_rw:o_rw + 3 * g], w[:, o_lx:o_lx + g],
            w[:, o_rw + 3 * g:o_rw + 3 * g + 2 * r], w[:, o_ff:o_ff + h],
            jnp.zeros((w.shape[0], LANES - 2 * r - h), w.dtype)]
    return jnp.concatenate(cols, axis=1).astype(BF16)


def _block_diag(w):
    h, n, _ = w.shape
    eye = jnp.eye(h, dtype=w.dtype)
    return jnp.einsum('hij,hk->hikj', w, eye).reshape(h * n, h * n)


def _pick_tile(s, pref):
    t = pref
    while s % t:
        t //= 2
    return t


def kernel(x, norm_g, w_in, b_forget, rwkv_mu, rwkv_w0, rwkv_w2, rwkv_a0, rwkv_a2, rwkv_k_k,
           rwkv_k_a, rwkv_r_k, rwkv_ln_g, rwkv_ln_b, lru_conv_w, lru_conv_b, lru_w_a, lru_b_a,
           lru_w_x, lru_b_x, lru_lambda, w_out, final_g):
    b, s, d = x.shape
    depth = w_in.shape[0]
    g, h, dh, r = D_GROUP, N_HEADS, HEAD_DIM, RWKV_LORA
    t = b * s
    bh = b * h
    tm = _pick_tile(t, 512)
    tq = _pick_tile(s, 256)
    tt = _pick_tile(s, 512)
    row = lambda a: a.reshape(1, -1).astype(F32)

    bdm = _block_diag(jnp.ones((h, dh, dh), BF16))
    x2 = x.reshape(t, d)
    for l in range(depth):
        fqkv, sqkv, gates, rkv, lx, misc = _inproj(x2, row(norm_g[l]), _permute_w_in(w_in[l]), tm=tm)

        ff = misc[:, 2 * r:2 * r + h].reshape(b, s, h).transpose(0, 2, 1).reshape(bh, s // LANES, LANES)
        fbias = jnp.broadcast_to(jnp.tile(b_forget[l], b).reshape(bh, 1, 1), (bh, 1, LANES)).astype(F32)
        f_parts = [p.reshape(bh, s, 1) for p in _forget_cumsum(ff, fbias)]

        def heads(a3):
            a5 = a3.reshape(b, s, 3, h, dh).transpose(2, 0, 3, 1, 4).reshape(3, bh, s, dh)
            return a5[0], a5[1], a5[2].transpose(0, 2, 1)

        fq, fk, fvt = heads(fqkv)
        ones = jnp.ones((bh, s, 3), BF16)
        zpad = jnp.zeros((bh, s, QK_AUG - dh - 6), BF16)
        q_aug = jnp.concatenate([fq, ones] + f_parts + [zpad], axis=-1)
        k_aug = jnp.concatenate([fk] + [-p for p in f_parts] + [ones, zpad], axis=-1)
        y_fox = _fox_attention(q_aug, k_aug, fvt, tq=tq, tk=tq)

        sq, sk, svt = heads(sqkv)
        y_sb = _sb_attention(sq, sk, svt, tq=tq)

        def tokens(yt):
            return yt.reshape(b, h, dh, s).transpose(0, 3, 1, 2).reshape(t, g)

        mu = rwkv_mu[l]
        pad_rows = lambda a, lo: jnp.zeros((LANES, g), F32).at[lo:lo + r].set(a)
        rw_prm = [row(mu[:N_RKV]),
                  jnp.zeros((1, LANES), F32).at[0, :2 * r].set(mu[N_RKV:]),
                  row(rwkv_w0[l]), pad_rows(rwkv_w2[l], 0), row(rwkv_a0[l]), pad_rows(rwkv_a2[l], r),
                  row(rwkv_k_k[l]), row(rwkv_k_a[l]), row(rwkv_r_k[l]), row(rwkv_ln_g[l]),
                  row(rwkv_ln_b[l])]
        y_rw = _rwkv(rkv.reshape(b, s, N_RKV), misc.reshape(b, s, LANES), rw_prm, bdm, tt=tt)

        lru_prm = [lru_conv_w[l].astype(F32), row(lru_conv_b[l]),
                   _block_diag(lru_w_a[l]).astype(BF16), row(lru_b_a[l]),
                   _block_diag(lru_w_x[l]).astype(BF16), row(lru_b_x[l]), row(lru_lambda[l])]
        y_lru = _lru(lx.reshape(b, s, g), lru_prm, tt=tt)

        ys = [tokens(y_fox), tokens(y_sb), y_rw.reshape(t, g), y_lru.reshape(t, g)]
        x2 = _outproj(x2, ys, gates, w_out[l].astype(BF16), row(final_g), tm=tm,
                      final=(l == depth - 1))
    return x2.reshape(b, s, d)
```

```python
import functools

import jax
import jax.numpy as jnp
from jax import lax
from jax.experimental import pallas as pl
from jax.experimental.pallas import tpu as pltpu

F32 = jnp.float32
BF16 = jnp.bfloat16

D_MODEL = 1024
D_GROUP = 256
N_HEADS = 4
HEAD_DIM = 64
RWKV_LORA = 32
CONV_WIDTH = 4
LRU_C = 8.0
RMS_EPS = 1e-6
GN_EPS = 64e-5
N_RKV = 3 * D_GROUP

LANES = 128
SUBLANES = 8
VMEM_LIMIT_BYTES = 48 * 1024 * 1024

C_FQKV = 0
C_SQKV = 3 * D_GROUP
C_GATES = 6 * D_GROUP
C_RKV = 10 * D_GROUP
C_LX = 13 * D_GROUP
C_MISC = 14 * D_GROUP
N_PROJ = C_MISC + LANES

QK_AUG = LANES
NEG_BIG = -1e30
SB_SKIP_LOG = -106.0
RWKV_CHUNK = 64


def _dot(a, b):
    return jnp.dot(a, b, preferred_element_type=F32)


def _dot_nt(a, b):
    return lax.dot_general(a, b, (((1,), (1,)), ((), ())), preferred_element_type=F32)


def _split2(x):
    hi = x.astype(BF16)
    lo = (x - hi.astype(F32)).astype(BF16)
    return hi, lo


def _split3(x):
    hi = x.astype(BF16)
    r1 = x - hi.astype(F32)
    mid = r1.astype(BF16)
    lo = (r1 - mid.astype(F32)).astype(BF16)
    return hi, mid, lo


def _dot3(a, b):
    ah, al = _split2(a)
    bh, bl = _split2(b)
    return _dot(ah, bh) + (_dot(al, bh) + _dot(ah, bl))


def _dot3_nt(a, b):
    ah, al = _split2(a)
    bh, bl = _split2(b)
    return _dot_nt(ah, bh) + (_dot_nt(al, bh) + _dot_nt(ah, bl))


def _dot_sel_lhs(sel, x):
    hi, mid, lo = _split3(x)
    return _dot(sel, hi) + (_dot(sel, mid) + _dot(sel, lo))


def _dot_sel_rhs(x, sel):
    hi, mid, lo = _split3(x)
    return _dot(hi, sel) + (_dot(mid, sel) + _dot(lo, sel))


def _softplus(x):
    return jnp.maximum(x, 0.0) + jnp.log1p(jnp.exp(-jnp.abs(x)))


def _sigmoid(x):
    return 1.0 / (1.0 + jnp.exp(-x))


def _iota(shape, dim):
    return lax.broadcasted_iota(jnp.int32, shape, dim)


def _inproj_kernel(x_ref, g_ref, w_ref, fqkv_ref, sqkv_ref, gates_ref, rkv_ref, lx_ref, misc_ref):
    x = x_ref[...]
    ms = jnp.mean(x * x, axis=-1, keepdims=True)
    h = (x * lax.rsqrt(ms + RMS_EPS) * g_ref[...]).astype(BF16)
    scale = HEAD_DIM ** -0.5

    def proj(c0, width):
        return _dot(h, w_ref[:, c0:c0 + width])

    for c0, ref in ((C_FQKV, fqkv_ref), (C_SQKV, sqkv_ref)):
        ref[:, 0:D_GROUP] = (proj(c0, D_GROUP) * scale).astype(BF16)
        ref[:, D_GROUP:3 * D_GROUP] = proj(c0 + D_GROUP, 2 * D_GROUP).astype(BF16)
    gates_ref[...] = proj(C_GATES, 4 * D_GROUP)
    rkv_ref[...] = proj(C_RKV, N_RKV)
    lx_ref[...] = proj(C_LX, D_GROUP)
    misc_ref[...] = proj(C_MISC, LANES)


def _inproj(x2, g, w, *, tm):
    t = x2.shape[0]
    row = lambda width: pl.BlockSpec((tm, width), lambda i: (i, 0))
    full = lambda a: pl.BlockSpec(a.shape, lambda i: (0,) * a.ndim)
    return pl.pallas_call(
        _inproj_kernel,
        grid=(t // tm,),
        in_specs=[row(D_MODEL), full(g), full(w)],
        out_specs=[row(3 * D_GROUP), row(3 * D_GROUP), row(4 * D_GROUP), row(N_RKV),
                   row(D_GROUP), row(LANES)],
        out_shape=[jax.ShapeDtypeStruct((t, 3 * D_GROUP), BF16),
                   jax.ShapeDtypeStruct((t, 3 * D_GROUP), BF16),
                   jax.ShapeDtypeStruct((t, 4 * D_GROUP), F32),
                   jax.ShapeDtypeStruct((t, N_RKV), F32),
                   jax.ShapeDtypeStruct((t, D_GROUP), F32),
                   jax.ShapeDtypeStruct((t, LANES), F32)],
        compiler_params=pltpu.CompilerParams(
            dimension_semantics=("arbitrary",), vmem_limit_bytes=VMEM_LIMIT_BYTES),
        name="inproj",
    )(x2, g, w)


def _forget_kernel(ff_ref, b_ref, hi_ref, mid_ref, lo_ref):
    x = ff_ref[0] + b_ref[0]
    nr = x.shape[0]
    lf = jnp.minimum(x, 0.0) - jnp.log1p(jnp.exp(-jnp.abs(x)))
    upper = (_iota((LANES, LANES), 0) <= _iota((LANES, LANES), 1)).astype(BF16)
    within = _dot_sel_rhs(lf, upper)
    tot = jnp.broadcast_to(within[:, LANES - 1:LANES], (nr, LANES))
    before = (_iota((nr, nr), 1) < _iota((nr, nr), 0)).astype(BF16)
    f = within + _dot_sel_lhs(before, tot)
    hi, mid, lo = _split3(f)
    hi_ref[0] = hi
    mid_ref[0] = mid
    lo_ref[0] = lo


def _forget_cumsum(ff, bias):
    bh, nr, _ = ff.shape
    blk = pl.BlockSpec((1, nr, LANES), lambda i: (i, 0, 0))
    return pl.pallas_call(
        _forget_kernel,
        grid=(bh,),
        in_specs=[blk, pl.BlockSpec((1, 1, LANES), lambda i: (i, 0, 0))],
        out_specs=[blk, blk, blk],
        out_shape=[jax.ShapeDtypeStruct((bh, nr, LANES), BF16)] * 3,
        compiler_params=pltpu.CompilerParams(dimension_semantics=("arbitrary",)),
        name="forget_cumsum",
    )(ff, bias)


def _fox_kernel(q_ref, k_ref, vt_ref, o_ref, *, tq, tk):
    i = pl.program_id(1)
    q = q_ref[0]

    def scores(j):
        return _dot_nt(k_ref[0, pl.ds(pl.multiple_of(j * tk, tk), tk), :], q)

    def weighted_values(j, p):
        return _dot(vt_ref[0, :, pl.ds(pl.multiple_of(j * tk, tk), tk)], p)

    def softmax_step(s, m, l):
        m_new = jnp.maximum(m, jnp.max(s, axis=0, keepdims=True))
        alpha = jnp.exp(m - m_new)
        p = jnp.exp(s - m_new)
        return m_new, alpha * l + jnp.sum(p, axis=0, keepdims=True), alpha, p.astype(BF16)

    def body(j, carry):
        s, p_prev, alpha_prev, m, l, acc = carry
        s_next = scores(j + 1)
        pv_prev = weighted_values(jnp.maximum(j - 1, 0), p_prev)
        m, l, alpha, p = softmax_step(s, m, l)
        return s_next, p, alpha, m, l, alpha_prev * acc + pv_prev

    init = (scores(0), jnp.zeros((tk, tq), BF16), jnp.ones((1, tq), F32),
            jnp.full((1, tq), NEG_BIG, F32), jnp.zeros((1, tq), F32),
            jnp.zeros((HEAD_DIM, tq), F32))
    n_full = (i * tq) // tk
    s, p_prev, alpha_prev, m, l, acc = lax.fori_loop(0, n_full, body, init)
    acc = alpha_prev * acc + weighted_values(jnp.maximum(n_full - 1, 0), p_prev)
    kpos = n_full * tk + _iota((tk, tq), 0)
    s = jnp.where(kpos <= i * tq + _iota((tk, tq), 1), s, NEG_BIG)
    _, l, alpha, p = softmax_step(s, m, l)
    o_ref[0] = (alpha * acc + weighted_values(n_full, p)) / l


def _fox_attention(q, k, vt, *, tq, tk):
    bh, s, _ = q.shape
    return pl.pallas_call(
        functools.partial(_fox_kernel, tq=tq, tk=tk),
        grid=(bh, s // tq),
        in_specs=[pl.BlockSpec((1, tq, QK_AUG), lambda b, i: (b, i, 0)),
                  pl.BlockSpec((1, s, QK_AUG), lambda b, i: (b, 0, 0)),
                  pl.BlockSpec((1, HEAD_DIM, s), lambda b, i: (b, 0, 0))],
        out_specs=pl.BlockSpec((1, HEAD_DIM, tq), lambda b, i: (b, 0, i)),
        out_shape=jax.ShapeDtypeStruct((bh, HEAD_DIM, s), F32),
        compiler_params=pltpu.CompilerParams(
            dimension_semantics=("arbitrary", "arbitrary"), vmem_limit_bytes=VMEM_LIMIT_BYTES),
        name="fox_attention",
    )(q, k, vt)


def _sb_kernel(q_ref, k_ref, vt_ref, o_ref, *, tq):
    i = pl.program_id(1)
    tk = tq
    q = q_ref[0]
    later = (_iota((tk, tk), 0) < _iota((tk, tk), 1)).astype(BF16)

    def block(j, rest_q, acc, masked):
        ks = pl.multiple_of(j * tk, tk)
        z = _dot_nt(k_ref[0, pl.ds(ks, tk), :], q)
        log_keep = jnp.minimum(-z, 0.0) - jnp.log1p(jnp.exp(-jnp.abs(z)))
        if masked:
            mask = _iota((tk, tq), 0) < _iota((tk, tq), 1)
            log_keep = jnp.where(mask, log_keep, 0.0)
        hi, lo = _split2(log_keep)
        rest_in = _dot(later, hi) + _dot(later, lo)
        att = jnp.exp(z + log_keep + rest_in + rest_q)
        if masked:
            att = jnp.where(mask, att, 0.0)
        acc = acc + _dot(vt_ref[0, :, pl.ds(ks, tk)], att.astype(BF16))
        rest_q = rest_q + rest_in[0:1, :] + log_keep[0:1, :]
        return rest_q, acc

    rest_q, acc = block(i, jnp.zeros((1, tq), F32), jnp.zeros((HEAD_DIM, tq), F32), True)

    def cond(c):
        j, rest_q, _ = c
        return jnp.logical_and(j >= 0, jnp.max(rest_q) > SB_SKIP_LOG)

    def body(c):
        j, rest_q, acc = c
        rest_q, acc = block(j, rest_q, acc, False)
        return j - 1, rest_q, acc

    _, _, acc = lax.while_loop(cond, body, (i - 1, rest_q, acc))
    o_ref[0] = acc


def _sb_attention(q, k, vt, *, tq):
    bh, s, _ = q.shape
    return pl.pallas_call(
        functools.partial(_sb_kernel, tq=tq),
        grid=(bh, s // tq),
        in_specs=[pl.BlockSpec((1, tq, HEAD_DIM), lambda b, i: (b, i, 0)),
                  pl.BlockSpec((1, s, HEAD_DIM), lambda b, i: (b, 0, 0)),
                  pl.BlockSpec((1, HEAD_DIM, s), lambda b, i: (b, 0, 0))],
        out_specs=pl.BlockSpec((1, HEAD_DIM, tq), lambda b, i: (b, 0, i)),
        out_shape=jax.ShapeDtypeStruct((bh, HEAD_DIM, s), F32),
        compiler_params=pltpu.CompilerParams(
            dimension_semantics=("arbitrary", "arbitrary"), vmem_limit_bytes=VMEM_LIMIT_BYTES),
        name="sb_attention",
    )(q, k, vt)


def _rwkv_kernel(p_ref, m_ref, mu_rkv_ref, mu_misc_ref, w0_ref, w2_ref, a0_ref, a2_ref,
                 kk_ref, ka_ref, rk_ref, lng_ref, lnb_ref, bdm_ref, o_ref,
                 pad_rkv, pad_misc, r_s, k_s, v_s, kn_s, al_s, lw_s, y_s, ht_s, *, tt):
    c = RWKV_CHUNK
    g = D_GROUP
    t_idx = pl.program_id(1)

    @pl.when(t_idx == 0)
    def _():
        pad_rkv[0:SUBLANES, :] = jnp.zeros((SUBLANES, N_RKV), F32)
        pad_misc[0:SUBLANES, :] = jnp.zeros((SUBLANES, LANES), F32)
        ht_s[...] = jnp.zeros((g, g), F32)

    bdm = bdm_ref[...]
    bdm_f = bdm.astype(F32)

    p = p_ref[0]
    misc = m_ref[0]
    pad_rkv[SUBLANES:SUBLANES + tt, :] = p
    pad_misc[SUBLANES:SUBLANES + tt, :] = misc
    p_prev = pad_rkv[SUBLANES - 1:SUBLANES - 1 + tt, :]
    m_prev = pad_misc[SUBLANES - 1:SUBLANES - 1 + tt, :]
    pad_rkv[0:SUBLANES, :] = p[tt - SUBLANES:tt, :]
    pad_misc[0:SUBLANES, :] = misc[tt - SUBLANES:tt, :]
    p = p + (p_prev - p) * mu_rkv_ref[...]
    misc = misc + (m_prev - misc) * mu_misc_ref[...]
    r = p[:, 0:g]
    k = p[:, g:2 * g]
    v = p[:, 2 * g:3 * g]

    w = -_softplus(-(w0_ref[...] + _dot3(jnp.tanh(misc), w2_ref[...]))) - 0.5
    alpha = _sigmoid(a0_ref[...] + _dot3(misc, a2_ref[...]))
    kn = k * kk_ref[...]
    ss = _dot_sel_rhs(kn * kn, bdm)
    kn = kn * lax.rsqrt(jnp.maximum(ss, 1e-12))
    k = k * (1.0 + (alpha - 1.0) * ka_ref[...])
    r_s[...] = r
    k_s[...] = k
    v_s[...] = v
    kn_s[...] = kn
    al_s[...] = alpha
    lw_s[...] = -jnp.exp(w)

    row = _iota((c, g), 0)
    col = _iota((c, g), 1) % c
    strict = col < row
    incl = col <= row
    eye = (col == row).astype(F32)
    lower_c = (_iota((c, c), 1) <= _iota((c, c), 0)).astype(BF16)
    level_masks = []
    m = 1
    while m < c:
        level_masks.append(jnp.logical_and(
            strict, jnp.logical_and(row // (2 * m) == col // (2 * m), row // m != col // m)))
        m *= 2

    def bd(x):
        hi, lo = _split2(x)
        tile = lambda a: jnp.concatenate([a] * N_HEADS, axis=0) * bdm
        return tile(hi), tile(lo)

    def mm(a, b_bd):
        ah, al = _split2(a)
        bh, bl = b_bd
        return _dot(ah, bh) + (_dot(al, bh) + _dot(ah, bl))

    def mm_nt(a, b_bd):
        ah, al = _split2(a)
        bh, bl = b_bd
        return _dot_nt(ah, bh) + (_dot_nt(al, bh) + _dot_nt(ah, bl))

    def chunk(ci, carry):
        base = pl.multiple_of(ci * c, c)
        sl = pl.ds(base, c)
        r_c, k_c, v_c, kn_c, al_c, lw_c = (s[sl, :] for s in (r_s, k_s, v_s, kn_s, al_s, lw_s))
        cl = _dot_sel_lhs(lower_c, lw_c)
        cl_last = cl[c - 1:c, :]
        a_t = -kn_c * jnp.exp(cl - lw_c)
        r_t = r_c * jnp.exp(cl)
        q_inv = jnp.exp(-cl)
        p_rem = jnp.exp(cl_last - cl)
        kna = kn_c * al_c
        ar = jnp.concatenate([a_t, r_t], axis=0)
        s_b = mm_nt(ar, bd(kna * q_inv))
        s_k = mm_nt(ar, bd(k_c * q_inv))
        n = jnp.where(strict, s_b[0:c], 0.0)
        a_ak = jnp.where(strict, s_k[0:c], 0.0)
        m_rb = jnp.where(incl, s_b[c:2 * c], 0.0)
        m_rk = jnp.where(incl, s_k[c:2 * c], 0.0)

        inv = eye + jnp.where(level_masks[0], n, 0.0)
        for lm in level_masks[1:]:
            inv = inv + mm(mm(inv, bd(jnp.where(lm, n, 0.0))), bd(inv))

        ht = ht_s[...]
        arh = mm_nt(ar, _split2(ht))
        v_bd = bd(v_c)
        u = mm(inv, bd(arh[0:c] + mm(a_ak, v_bd)))
        y_s[sl, :] = arh[c:2 * c] + mm(m_rb, bd(u)) + mm(m_rk, v_bd)
        uv_t = jnp.concatenate([u, v_c], axis=0).T
        bk = jnp.concatenate([kna * p_rem, k_c * p_rem], axis=0)
        ht_s[...] = ht * jnp.exp(cl_last) + bdm_f * _dot3(uv_t, bk)
        return carry

    lax.fori_loop(0, tt // c, chunk, 0)

    y = y_s[...]
    r = r_s[...]
    k = k_s[...]
    v = v_s[...]
    inv_n = 1.0 / HEAD_DIM
    mean = _dot_sel_rhs(y, bdm) * inv_n
    yc = y - mean
    var = _dot_sel_rhs(yc * yc, bdm) * inv_n
    yn = yc * lax.rsqrt(var + GN_EPS) * lng_ref[...] + lnb_ref[...]
    bonus = _dot_sel_rhs(r * k * rk_ref[...], bdm) * v
    o_ref[0] = yn + bonus


def _rwkv(p_rkv, misc, prm, bdm, *, tt):
    b, s, _ = p_rkv.shape
    g = D_GROUP
    full = lambda a: pl.BlockSpec(a.shape, lambda bi, ti: (0,) * a.ndim)
    seq = lambda width: pl.BlockSpec((1, tt, width), lambda bi, ti: (bi, ti, 0))
    return pl.pallas_call(
        functools.partial(_rwkv_kernel, tt=tt),
        grid=(b, s // tt),
        in_specs=[seq(N_RKV), seq(LANES)] + [full(a) for a in prm] + [full(bdm)],
        out_specs=seq(g),
        out_shape=jax.ShapeDtypeStruct((b, s, g), F32),
        scratch_shapes=[pltpu.VMEM((tt + SUBLANES, N_RKV), F32),
                        pltpu.VMEM((tt + SUBLANES, LANES), F32)]
        + [pltpu.VMEM((tt, g), F32)] * 7
        + [pltpu.VMEM((g, g), F32)],
        compiler_params=pltpu.CompilerParams(
            dimension_semantics=("arbitrary", "arbitrary"), vmem_limit_bytes=VMEM_LIMIT_BYTES),
        name="rwkv7",
    )(p_rkv, misc, *prm, bdm)


def _lru_kernel(x_ref, cw_ref, cb_ref, wa_ref, ba_ref, wx_ref, bx_ref, lam_ref, o_ref,
                pad, a_s, u_s, h_s, *, tt):
    t_idx = pl.program_id(1)

    @pl.when(t_idx == 0)
    def _():
        pad[0:SUBLANES, :] = jnp.zeros((SUBLANES, D_GROUP), F32)
        h_s[...] = jnp.zeros((1, D_GROUP), F32)

    x = x_ref[0]
    pad[SUBLANES:SUBLANES + tt, :] = x
    xc = cw_ref[CONV_WIDTH - 1:CONV_WIDTH, :] * x + cb_ref[...]
    for d in range(1, CONV_WIDTH):
        tap = CONV_WIDTH - 1 - d
        xc = xc + cw_ref[tap:tap + 1, :] * pad[SUBLANES - d:SUBLANES - d + tt, :]
    pad[0:SUBLANES, :] = x[tt - SUBLANES:tt, :]

    xb = xc.astype(BF16)
    r = _sigmoid(_dot(xb, wa_ref[...]) + ba_ref[...])
    i = _sigmoid(_dot(xb, wx_ref[...]) + bx_ref[...])
    log_a = -LRU_C * r * _softplus(-lam_ref[...])
    a_s[...] = jnp.exp(log_a)
    th = jnp.tanh(log_a)
    u_s[...] = jnp.sqrt(-2.0 * th / (1.0 - th)) * (i * xc)

    def group(gi, h):
        base = pl.multiple_of(gi * SUBLANES, SUBLANES)
        a8 = a_s[pl.ds(base, SUBLANES), :]
        u8 = u_s[pl.ds(base, SUBLANES), :]
        rows = []
        for j in range(SUBLANES):
            h = a8[j:j + 1, :] * h + u8[j:j + 1, :]
            rows.append(h)
        o_ref[0, pl.ds(base, SUBLANES), :] = jnp.concatenate(rows, axis=0)
        return h

    h_s[...] = lax.fori_loop(0, tt // SUBLANES, group, h_s[...])


def _lru(x, prm, *, tt):
    b, s, g = x.shape
    full = lambda a: pl.BlockSpec(a.shape, lambda bi, ti: (0,) * a.ndim)
    seq = pl.BlockSpec((1, tt, g), lambda bi, ti: (bi, ti, 0))
    return pl.pallas_call(
        functools.partial(_lru_kernel, tt=tt),
        grid=(b, s // tt),
        in_specs=[seq] + [full(a) for a in prm],
        out_specs=seq,
        out_shape=jax.ShapeDtypeStruct((b, s, g), F32),
        scratch_shapes=[pltpu.VMEM((tt + SUBLANES, g), F32), pltpu.VMEM((tt, g), F32),
                        pltpu.VMEM((tt, g), F32), pltpu.VMEM((1, g), F32)],
        compiler_params=pltpu.CompilerParams(
            dimension_semantics=("arbitrary", "arbitrary"), vmem_limit_bytes=VMEM_LIMIT_BYTES),
        name="rg_lru",
    )(x, *prm)


def _outproj_kernel(x_ref, yf_ref, ys_ref, yr_ref, yl_ref, gates_ref, w_ref, fg_ref, o_ref, *,
                    final):
    acc = x_ref[...]
    for gi, y_ref in enumerate((yf_ref, ys_ref, yr_ref, yl_ref)):
        gate = gates_ref[:, gi * D_GROUP:(gi + 1) * D_GROUP]
        y = y_ref[...] * (gate * _sigmoid(gate))
        acc = acc + _dot(y.astype(BF16), w_ref[gi * D_GROUP:(gi + 1) * D_GROUP, :])
    if final:
        ms = jnp.mean(acc * acc, axis=-1, keepdims=True)
        acc = acc * lax.rsqrt(ms + RMS_EPS) * fg_ref[...]
    o_ref[...] = acc


def _outproj(x2, ys, gates, w, final_g, *, tm, final):
    t = x2.shape[0]
    row = lambda width: pl.BlockSpec((tm, width), lambda i: (i, 0))
    full = lambda a: pl.BlockSpec(a.shape, lambda i: (0,) * a.ndim)
    return pl.pallas_call(
        functools.partial(_outproj_kernel, final=final),
        grid=(t // tm,),
        in_specs=[row(D_MODEL)] + [row(D_GROUP)] * 4 + [row(4 * D_GROUP), full(w), full(final_g)],
        out_specs=row(D_MODEL),
        out_shape=jax.ShapeDtypeStruct((t, D_MODEL), F32),
        compiler_params=pltpu.CompilerParams(
            dimension_semantics=("arbitrary",), vmem_limit_bytes=VMEM_LIMIT_BYTES),
        name="outproj",
    )(x2, *ys, gates, w, final_g)


def _permute_w_in(w):
    g, h, r = D_GROUP, N_HEADS, RWKV_LORA
    o_ff = 4 * g
    o_sb = o_ff + h
    o_rw = o_sb + 4 * g
    o_rg = o_rw + 3 * g + 2 * r
    o_lx = o_rg + g
    o_lg = o_lx + g
    cols = [w[:, 0:3 * g], w[:, o_sb:o_sb + 3 * g],
            w[:, 3 * g:4 * g], w[:, o_sb + 3 * g:o_sb + 4 * g], w[:, o_rg:o_rg + g],
            w[:, o_lg:o_lg + g],
            w[:, o_rw:o_rw + 3 * g], w[:, o_lx:o_lx + g],
            w[:, o_rw + 3 * g:o_rw + 3 * g + 2 * r], w[:, o_ff:o_ff + h],
            jnp.zeros((w.shape[0], LANES - 2 * r - h), w.dtype)]
    return jnp.concatenate(cols, axis=1).astype(BF16)


def _block_diag(w):
    h, n, _ = w.shape
    eye = jnp.eye(h, dtype=w.dtype)
    return jnp.einsum('hij,hk->hikj', w, eye).reshape(h * n, h * n)


def _pick_tile(s, pref):
    t = pref
    while s % t:
        t //= 2
    return t


def kernel(x, norm_g, w_in, b_forget, rwkv_mu, rwkv_w0, rwkv_w2, rwkv_a0, rwkv_a2, rwkv_k_k,
           rwkv_k_a, rwkv_r_k, rwkv_ln_g, rwkv_ln_b, lru_conv_w, lru_conv_b, lru_w_a, lru_b_a,
           lru_w_x, lru_b_x, lru_lambda, w_out, final_g):
    b, s, d = x.shape
    depth = w_in.shape[0]
    g, h, dh, r = D_GROUP, N_HEADS, HEAD_DIM, RWKV_LORA
    t = b * s
    bh = b * h
    tm = _pick_tile(t, 512)
    tq = _pick_tile(s, 256)
    tt = _pick_tile(s, 512)
    row = lambda a: a.reshape(1, -1).astype(F32)

    bdm = _block_diag(jnp.ones((h, dh, dh), BF16))
    x2 = x.reshape(t, d)
    for l in range(depth):
        fqkv, sqkv, gates, rkv, lx, misc = _inproj(x2, row(norm_g[l]), _permute_w_in(w_in[l]), tm=tm)

        ff = misc[:, 2 * r:2 * r + h].reshape(b, s, h).transpose(0, 2, 1).reshape(bh, s // LANES, LANES)
        fbias = jnp.broadcast_to(jnp.tile(b_forget[l], b).reshape(bh, 1, 1), (bh, 1, LANES)).astype(F32)
        f_parts = [p.reshape(bh, s, 1) for p in _forget_cumsum(ff, fbias)]

        def heads(a3):
            a5 = a3.reshape(b, s, 3, h, dh).transpose(2, 0, 3, 1, 4).reshape(3, bh, s, dh)
            return a5[0], a5[1], a5[2].transpose(0, 2, 1)

        fq, fk, fvt = heads(fqkv)
        ones = jnp.ones((bh, s, 3), BF16)
        zpad = jnp.zeros((bh, s, QK_AUG - dh - 6), BF16)
        q_aug = jnp.concatenate([fq, ones] + f_parts + [zpad], axis=-1)
        k_aug = jnp.concatenate([fk] + [-p for p in f_parts] + [ones, zpad], axis=-1)
        y_fox = _fox_attention(q_aug, k_aug, fvt, tq=tq, tk=_pick_tile(s, 2 * tq))

        sq, sk, svt = heads(sqkv)
        y_sb = _sb_attention(sq, sk, svt, tq=tq)

        def tokens(yt):
            return yt.reshape(b, h, dh, s).transpose(0, 3, 1, 2).reshape(t, g)

        mu = rwkv_mu[l]
        pad_rows = lambda a, lo: jnp.zeros((LANES, g), F32).at[lo:lo + r].set(a)
        rw_prm = [row(mu[:N_RKV]),
                  jnp.zeros((1, LANES), F32).at[0, :2 * r].set(mu[N_RKV:]),
                  row(rwkv_w0[l]), pad_rows(rwkv_w2[l], 0), row(rwkv_a0[l]), pad_rows(rwkv_a2[l], r),
                  row(rwkv_k_k[l]), row(rwkv_k_a[l]), row(rwkv_r_k[l]), row(rwkv_ln_g[l]),
                  row(rwkv_ln_b[l])]
        y_rw = _rwkv(rkv.reshape(b, s, N_RKV), misc.reshape(b, s, LANES), rw_prm, bdm, tt=tt)

        lru_prm = [lru_conv_w[l].astype(F32), row(lru_conv_b[l]),
                   _block_diag(lru_w_a[l]).astype(BF16), row(lru_b_a[l]),
                   _block_diag(lru_w_x[l]).astype(BF16), row(lru_b_x[l]), row(lru_lambda[l])]
        y_lru = _lru(lx.reshape(b, s, g), lru_prm, tt=tt)

        ys = [tokens(y_fox), tokens(y_sb), y_rw.reshape(t, g), y_lru.reshape(t, g)]
        x2 = _outproj(x2, ys, gates, w_out[l].astype(BF16), row(final_g), tm=tm,
                      final=(l == depth - 1))
    return x2.reshape(b, s, d)
```

```python
import functools

import numpy as np
import jax
import jax.numpy as jnp
from jax import lax
from jax.experimental import pallas as pl
from jax.experimental.pallas import tpu as pltpu

F32 = jnp.float32
BF16 = jnp.bfloat16

D_MODEL = 1024
D_GROUP = 256
N_HEADS = 4
HEAD_DIM = 64
RWKV_LORA = 32
CONV_WIDTH = 4
LRU_C = 8.0
RMS_EPS = 1e-6
GN_EPS = 64e-5
N_RKV = 3 * D_GROUP

LANES = 128
SUBLANES = 8
VMEM_LIMIT_BYTES = 56 * 1024 * 1024

HEAD_PAD = LANES
QK_COLS = N_HEADS * HEAD_PAD
C_FQ = 0
C_FK = C_FQ + QK_COLS
C_SQ = C_FK + QK_COLS
C_SK = C_SQ + QK_COLS
C_FV = C_SK + QK_COLS
C_SV = C_FV + D_GROUP
C_GATES = C_SV + D_GROUP
C_RKV = C_GATES + 4 * D_GROUP
C_LX = C_RKV + N_RKV
C_MISC = C_LX + D_GROUP
N_PROJ = C_MISC + LANES
FF_LANE = 2 * RWKV_LORA
AUG_K_F = HEAD_DIM
AUG_K_ONE = HEAD_DIM + 3
AUG_Q_ONE = HEAD_DIM
AUG_Q_F = HEAD_DIM + 3

NEG_BIG = -1e30
SB_SKIP_LOG = -106.0
RWKV_CHUNK = 64
RWKV_UNROLL = 2


def _dot(a, b):
    return jnp.dot(a, b, preferred_element_type=F32)


def _dot_nt(a, b):
    return lax.dot_general(a, b, (((1,), (1,)), ((), ())), preferred_element_type=F32)


def _split2(x):
    hi = x.astype(BF16)
    lo = (x - hi.astype(F32)).astype(BF16)
    return hi, lo


def _split3(x):
    hi = x.astype(BF16)
    r1 = x - hi.astype(F32)
    mid = r1.astype(BF16)
    lo = (r1 - mid.astype(F32)).astype(BF16)
    return hi, mid, lo


def _dot3(a, b):
    ah, al = _split2(a)
    bh, bl = _split2(b)
    return _dot(ah, bh) + (_dot(al, bh) + _dot(ah, bl))


def _dot_sel_lhs(sel, x):
    hi, mid, lo = _split3(x)
    return _dot(sel, hi) + (_dot(sel, mid) + _dot(sel, lo))


def _dot_sel_rhs(x, sel):
    hi, lo = _split2(x)
    return _dot(hi, sel) + _dot(lo, sel)


def _softplus(x):
    return jnp.maximum(x, 0.0) + jnp.log1p(jnp.exp(-jnp.abs(x)))


def _log_sigmoid(x):
    return jnp.minimum(x, 0.0) - jnp.log1p(jnp.exp(-jnp.abs(x)))


def _sigmoid(x):
    return 1.0 / (1.0 + jnp.exp(-x))


def _iota(shape, dim):
    return lax.broadcasted_iota(jnp.int32, shape, dim)


def _full_spec(a):
    return pl.BlockSpec(a.shape, lambda *_: (0,) * a.ndim)


def _params(n_grid):
    return pltpu.CompilerParams(dimension_semantics=("arbitrary",) * n_grid,
                                vmem_limit_bytes=VMEM_LIMIT_BYTES)


def _inproj_kernel(x_ref, g_ref, w_ref, fb_ref, selq_ref, selk_ref, augq_ref, augk_ref,
                   fq_ref, fk_ref, fvt_ref, sq_ref, sk_ref, svt_ref, gates_ref, rkv_ref, lx_ref,
                   misc_ref, ftot, v_stage):
    @pl.when(pl.program_id(1) == 0)
    def _():
        ftot[...] = jnp.zeros_like(ftot)

    x = x_ref[0]
    tm = x.shape[0]
    ms = jnp.mean(x * x, axis=-1, keepdims=True)
    h = (x * lax.rsqrt(ms + RMS_EPS) * g_ref[...]).astype(BF16)
    scale = HEAD_DIM ** -0.5

    def proj(c0, width):
        return _dot(h, w_ref[:, c0:c0 + width])

    misc = proj(C_MISC, LANES)
    misc_ref[0] = misc
    gates_ref[0] = proj(C_GATES, 4 * D_GROUP)
    rkv_ref[0] = proj(C_RKV, N_RKV)
    lx_ref[0] = proj(C_LX, D_GROUP)

    lf = _log_sigmoid(misc + fb_ref[...])
    lower = (_iota((tm, tm), 1) <= _iota((tm, tm), 0)).astype(BF16)
    f = _dot_sel_lhs(lower, lf) + ftot[...]
    ftot[...] = f[tm - 1:tm, :]
    pieces = jnp.concatenate(_split3(f), axis=1)

    fq = proj(C_FQ, QK_COLS) * scale + _dot(pieces, selq_ref[...]) + augq_ref[...]
    fk = proj(C_FK, QK_COLS) + _dot(pieces, selk_ref[...]) + augk_ref[...]
    sq = proj(C_SQ, QK_COLS) * scale
    sk = proj(C_SK, QK_COLS)
    for hd in range(N_HEADS):
        cols = slice(hd * HEAD_PAD, (hd + 1) * HEAD_PAD)
        fq_ref[0, hd] = fq[:, cols].astype(BF16)
        fk_ref[0, hd] = fk[:, cols].astype(BF16)
        sq_ref[0, hd] = sq[:, cols].astype(BF16)
        sk_ref[0, hd] = sk[:, cols].astype(BF16)
    for c0, vt_ref in ((C_FV, fvt_ref), (C_SV, svt_ref)):
        v_stage[...] = proj(c0, D_GROUP)
        vt_ref[0] = v_stage[...].T.astype(BF16)


def _inproj(x, consts, *, tm):
    b, s, _ = x.shape
    seq = lambda width: pl.BlockSpec((1, tm, width), lambda bi, i: (bi, i, 0))
    heads = pl.BlockSpec((1, N_HEADS, tm, HEAD_PAD), lambda bi, i: (bi, 0, i, 0))
    chan = pl.BlockSpec((1, D_GROUP, tm), lambda bi, i: (bi, 0, i))
    heads_shape = jax.ShapeDtypeStruct((b, N_HEADS, s, HEAD_PAD), BF16)
    chan_shape = jax.ShapeDtypeStruct((b, D_GROUP, s), BF16)
    seq_shape = lambda width: jax.ShapeDtypeStruct((b, s, width), F32)
    return pl.pallas_call(
        _inproj_kernel,
        grid=(b, s // tm),
        in_specs=[seq(D_MODEL)] + [_full_spec(a) for a in consts],
        out_specs=[heads, heads, chan, heads, heads, chan,
                   seq(4 * D_GROUP), seq(N_RKV), seq(D_GROUP), seq(LANES)],
        out_shape=[heads_shape, heads_shape, chan_shape, heads_shape, heads_shape, chan_shape,
                   seq_shape(4 * D_GROUP), seq_shape(N_RKV), seq_shape(D_GROUP), seq_shape(LANES)],
        scratch_shapes=[pltpu.VMEM((1, LANES), F32), pltpu.VMEM((tm, D_GROUP), F32)],
        compiler_params=_params(2),
        name="inproj",
    )(x, *consts)


def _fox_kernel(q_ref, k_ref, vt_ref, o_ref, s_scr, p_scr, mx_scr, al_scr, m_scr, l_scr, acc_scr,
                *, tq, tk):
    i = pl.program_id(2)
    q = q_ref[0, 0]
    n_full = (i * tq) // tk

    def scores_to(j, slot):
        s = _dot_nt(k_ref[0, 0, pl.ds(pl.multiple_of(j * tk, tk), tk), :], q)
        s_scr[slot] = s
        mx_scr[slot] = jnp.max(s, axis=0, keepdims=True)

    def weighted_values(j, slot):
        return _dot(vt_ref[0, :, pl.ds(pl.multiple_of(j * tk, tk), tk)], p_scr[slot])

    def softmax_to(s, mx, slot):
        m_old = m_scr[...]
        m_new = jnp.maximum(m_old, mx)
        alpha = jnp.exp(m_old - m_new)
        p = jnp.exp(s - m_new)
        l_scr[...] = alpha * l_scr[...] + jnp.sum(p, axis=0, keepdims=True)
        m_scr[...] = m_new
        al_scr[slot] = alpha
        p_scr[slot] = p.astype(BF16)

    def stage(j, cur):
        nxt = 1 - cur
        pv_prev = weighted_values(jnp.maximum(j - 1, 0), nxt)
        scores_to(j + 1, nxt)
        softmax_to(s_scr[cur], mx_scr[cur], cur)
        acc_scr[...] = al_scr[nxt] * acc_scr[...] + pv_prev

    def tail(cur):
        nxt = 1 - cur
        pv_prev = weighted_values(jnp.maximum(n_full - 1, 0), nxt)
        kpos = n_full * tk + _iota((tk, tq), 0)
        s = jnp.where(kpos <= i * tq + _iota((tk, tq), 1), s_scr[cur], NEG_BIG)
        softmax_to(s, jnp.max(s, axis=0, keepdims=True), cur)
        acc = al_scr[nxt] * acc_scr[...] + pv_prev
        o_ref[0] = (al_scr[cur] * acc + weighted_values(n_full, cur)) / l_scr[...]

    m_scr[...] = jnp.full((1, tq), NEG_BIG, F32)
    l_scr[...] = jnp.zeros((1, tq), F32)
    acc_scr[...] = jnp.zeros((HEAD_DIM, tq), F32)
    p_scr[1] = jnp.zeros((tk, tq), BF16)
    al_scr[1] = jnp.ones((1, tq), F32)
    scores_to(0, 0)

    def pair(jj, carry):
        stage(2 * jj, 0)
        stage(2 * jj + 1, 1)
        return carry

    lax.fori_loop(0, n_full // 2, pair, 0)

    @pl.when(n_full % 2 == 1)
    def _():
        stage(n_full - 1, 0)
        tail(1)

    @pl.when(n_full % 2 == 0)
    def _():
        tail(0)


def _fox_scratch(tq, tk):
    return [pltpu.VMEM((2, tk, tq), F32), pltpu.VMEM((2, tk, tq), BF16),
            pltpu.VMEM((2, 1, tq), F32), pltpu.VMEM((2, 1, tq), F32),
            pltpu.VMEM((1, tq), F32), pltpu.VMEM((1, tq), F32), pltpu.VMEM((HEAD_DIM, tq), F32)]


def _attention_call(kernel_fn, name, q, k, vt, *, tq, scratch=()):
    b, h, s, _ = q.shape
    return pl.pallas_call(
        kernel_fn,
        grid=(b, h, s // tq),
        in_specs=[pl.BlockSpec((1, 1, tq, HEAD_PAD), lambda bi, hi, i: (bi, hi, i, 0)),
                  pl.BlockSpec((1, 1, s, HEAD_PAD), lambda bi, hi, i: (bi, hi, 0, 0)),
                  pl.BlockSpec((1, HEAD_DIM, s), lambda bi, hi, i: (bi, hi, 0))],
        out_specs=pl.BlockSpec((1, HEAD_DIM, tq), lambda bi, hi, i: (bi, hi, i)),
        scratch_shapes=list(scratch),
        out_shape=jax.ShapeDtypeStruct((b, h * HEAD_DIM, s), F32),
        compiler_params=_params(3),
        name=name,
    )(q, k, vt)


def _sb_kernel(q_ref, k_ref, vt_ref, o_ref, *, tq):
    i = pl.program_id(2)
    tk = tq
    q = q_ref[0, 0]
    later = (_iota((tk, tk), 0) < _iota((tk, tk), 1)).astype(BF16)

    def block(j, rest_q, acc, masked):
        ks = pl.multiple_of(j * tk, tk)
        z = _dot_nt(k_ref[0, 0, pl.ds(ks, tk), :], q)
        log_keep = _log_sigmoid(-z)
        if masked:
            mask = _iota((tk, tq), 0) < _iota((tk, tq), 1)
            log_keep = jnp.where(mask, log_keep, 0.0)
        hi, lo = _split2(log_keep)
        rest_in = _dot(later, hi) + _dot(later, lo)
        att = jnp.exp(z + log_keep + rest_in + rest_q)
        if masked:
            att = jnp.where(mask, att, 0.0)
        acc = acc + _dot(vt_ref[0, :, pl.ds(ks, tk)], att.astype(BF16))
        rest_q = rest_q + rest_in[0:1, :] + log_keep[0:1, :]
        return rest_q, acc

    rest_q, acc = block(i, jnp.zeros((1, tq), F32), jnp.zeros((HEAD_DIM, tq), F32), True)

    def cond(c):
        j, rest_q, _ = c
        return jnp.logical_and(j >= 0, jnp.max(rest_q) > SB_SKIP_LOG)

    def body(c):
        j, rest_q, acc = c
        rest_q, acc = block(j, rest_q, acc, False)
        return j - 1, rest_q, acc

    _, _, acc = lax.while_loop(cond, body, (i - 1, rest_q, acc))
    o_ref[0] = acc


def _rwkv_kernel(p_ref, m_ref, mu_rkv_ref, mu_misc_ref, w0_ref, w2_ref, a0_ref, a2_ref,
                 kk_ref, ka_ref, rk_ref, lng_ref, lnb_ref, bdm_ref, o_ref,
                 pad_rkv, pad_misc, r_s, k_s, v_s, kn_s, al_s, lw_s,
                 wt_s, u0_s, o0_s, mrb_s, rt_s, bh_s, kh_s, pc_s, y_s, ht_s, *, tt, nb):
    c = RWKV_CHUNK
    g = D_GROUP
    n_chunks = tt // c

    @pl.when(pl.program_id(0) == 0)
    def _():
        pad_rkv[:, 0:SUBLANES, :] = jnp.zeros((nb, SUBLANES, N_RKV), F32)
        pad_misc[:, 0:SUBLANES, :] = jnp.zeros((nb, SUBLANES, LANES), F32)
        ht_s[...] = jnp.zeros_like(ht_s)

    bdm = bdm_ref[...]
    bdm_f = bdm.astype(F32)

    for b in range(nb):
        p = p_ref[b]
        misc = m_ref[b]
        pad_rkv[b, SUBLANES:SUBLANES + tt, :] = p
        pad_misc[b, SUBLANES:SUBLANES + tt, :] = misc
        p_prev = pad_rkv[b, SUBLANES - 1:SUBLANES - 1 + tt, :]
        m_prev = pad_misc[b, SUBLANES - 1:SUBLANES - 1 + tt, :]
        pad_rkv[b, 0:SUBLANES, :] = p[tt - SUBLANES:tt, :]
        pad_misc[b, 0:SUBLANES, :] = misc[tt - SUBLANES:tt, :]
        p = p + (p_prev - p) * mu_rkv_ref[...]
        misc = misc + (m_prev - misc) * mu_misc_ref[...]
        k = p[:, g:2 * g]

        w = -_softplus(-(w0_ref[...] + _dot3(jnp.tanh(misc), w2_ref[...]))) - 0.5
        alpha = _sigmoid(a0_ref[...] + _dot3(misc, a2_ref[...]))
        kn = k * kk_ref[...]
        ss = _dot_sel_rhs(kn * kn, bdm)
        r_s[b] = p[:, 0:g]
        k_s[b] = k * (1.0 + (alpha - 1.0) * ka_ref[...])
        v_s[b] = p[:, 2 * g:3 * g]
        kn_s[b] = kn * lax.rsqrt(jnp.maximum(ss, 1e-12))
        al_s[b] = alpha
        lw_s[b] = -jnp.exp(w)

    row = _iota((c, g), 0)
    col = _iota((c, g), 1) % c
    strict = col < row
    incl = col <= row
    eye = (col == row).astype(F32)
    lower_c = (_iota((c, c), 1) <= _iota((c, c), 0)).astype(BF16)
    level_masks = []
    m = 1
    while m < c:
        level_masks.append(jnp.logical_and(
            strict, jnp.logical_and(row // (2 * m) == col // (2 * m), row // m != col // m)))
        m *= 2

    def bd(x):
        return jnp.concatenate([x.astype(BF16)] * N_HEADS, axis=0) * bdm

    def mm(a, b_bf16):
        return _dot(a.astype(BF16), b_bf16)

    def mm_nt(a, b_bf16):
        return _dot_nt(a.astype(BF16), b_bf16)

    def each(f, *xs):
        return [f(*a) for a in zip(*xs)]

    def phase_a(chains):
        sls = [pl.ds(pl.multiple_of(ci * c, c), c) for _, ci in chains]
        ld = lambda ref: [ref[b, sl, :] for (b, _), sl in zip(chains, sls)]
        r_c, k_c, v_c, kn_c, al_c, lw_c = (ld(s) for s in (r_s, k_s, v_s, kn_s, al_s, lw_s))
        cl = each(lambda x: _dot_sel_lhs(lower_c, x), lw_c)
        cl_last = each(lambda x: x[c - 1:c, :], cl)
        a_t = each(lambda kn, x, lw: -kn * jnp.exp(x - lw), kn_c, cl, lw_c)
        r_t = each(lambda r, x: r * jnp.exp(x), r_c, cl)
        q_inv = each(lambda x: jnp.exp(-x), cl)
        p_rem = each(lambda xl, x: jnp.exp(xl - x), cl_last, cl)
        kna = each(lambda kn, al: kn * al, kn_c, al_c)
        ar = each(lambda a, r: jnp.concatenate([a, r], axis=0), a_t, r_t)
        s_b = each(lambda x, kb, qi: mm_nt(x, bd(kb * qi)), ar, kna, q_inv)
        s_k = each(lambda x, kk, qi: mm_nt(x, bd(kk * qi)), ar, k_c, q_inv)
        n = each(lambda x: jnp.where(strict, x[0:c], 0.0), s_b)
        a_ak = each(lambda x: jnp.where(strict, x[0:c], 0.0), s_k)
        m_rb = each(lambda x: jnp.where(incl, x[c:2 * c], 0.0), s_b)
        m_rk = each(lambda x: jnp.where(incl, x[c:2 * c], 0.0), s_k)

        inv = each(lambda x: eye + jnp.where(level_masks[0], x, 0.0), n)
        for lm in level_masks[1:]:
            half = each(lambda d, x: mm(d, bd(jnp.where(lm, x, 0.0))), inv, n)
            inv = each(lambda d, hf: d + mm(hf, bd(d)), inv, half)

        v_bd = each(bd, v_c)
        akv = each(mm, a_ak, v_bd)
        wt = each(lambda d, a: mm(d, bd(a)), inv, a_t)
        u0 = each(lambda d, x: mm(d, bd(x)), inv, akv)
        o0 = each(mm, m_rk, v_bd)
        bh = each(lambda x, p: x * p, kna, p_rem)
        kh = each(lambda x, p: x * p, k_c, p_rem)
        for ref, vals in zip((wt_s, u0_s, o0_s, mrb_s, rt_s, bh_s, kh_s),
                             (wt, u0, o0, m_rb, r_t, bh, kh)):
            for (b, _), sl, val in zip(chains, sls, vals):
                ref[b, sl, :] = val
        for (b, ci), xl in zip(chains, cl_last):
            pc_s[b, pl.ds(pl.multiple_of(ci * SUBLANES, SUBLANES), SUBLANES), :] = (
                jnp.broadcast_to(jnp.exp(xl), (SUBLANES, g)))

    def phase_b(ci):
        sl = pl.ds(pl.multiple_of(ci * c, c), c)
        bs = list(range(nb))
        ht = [ht_s[b] for b in bs]
        wr = [jnp.concatenate([wt_s[b, sl, :], rt_s[b, sl, :]], axis=0) for b in bs]
        wrh = each(lambda x, hh: mm_nt(x, hh.astype(BF16)), wr, ht)
        u = [x[0:c] + u0_s[b, sl, :] for x, b in zip(wrh, bs)]
        uv_t = [jnp.concatenate([x, v_s[b, sl, :]], axis=0).T for x, b in zip(u, bs)]
        bk = [jnp.concatenate([bh_s[b, sl, :], kh_s[b, sl, :]], axis=0) for b in bs]
        upd = each(lambda x, y: mm(x, y.astype(BF16)), uv_t, bk)
        mu_ = [mm(mrb_s[b, sl, :], bd(x)) for x, b in zip(u, bs)]
        for b in bs:
            p_c = pc_s[b, pl.ds(pl.multiple_of(ci * SUBLANES, SUBLANES), 1), :]
            y_s[b, sl, :] = wrh[b][c:2 * c] + mu_[b] + o0_s[b, sl, :]
            ht_s[b] = ht[b] * p_c + bdm_f * upd[b]

    def loop_a(ci, carry):
        phase_a([(b, ci * RWKV_UNROLL + j) for j in range(RWKV_UNROLL) for b in range(nb)])
        return carry

    def loop_b(ci, carry):
        phase_b(ci)
        return carry

    lax.fori_loop(0, n_chunks // RWKV_UNROLL, loop_a, 0)
    lax.fori_loop(0, n_chunks, loop_b, 0)

    inv_n = 1.0 / HEAD_DIM
    for b in range(nb):
        y = y_s[b]
        mean = _dot_sel_rhs(y, bdm) * inv_n
        yc = y - mean
        var = _dot_sel_rhs(yc * yc, bdm) * inv_n
        yn = yc * lax.rsqrt(var + GN_EPS) * lng_ref[...] + lnb_ref[...]
        bonus = _dot_sel_rhs(r_s[b] * k_s[b] * rk_ref[...], bdm) * v_s[b]
        o_ref[b] = yn + bonus


def _rwkv(p_rkv, misc, prm, bdm, *, tt):
    b, s, _ = p_rkv.shape
    g = D_GROUP
    seq = lambda width: pl.BlockSpec((b, tt, width), lambda ti: (0, ti, 0))
    big = pltpu.VMEM((b, tt, g), F32)
    return pl.pallas_call(
        functools.partial(_rwkv_kernel, tt=tt, nb=b),
        grid=(s // tt,),
        in_specs=[seq(N_RKV), seq(LANES)] + [_full_spec(a) for a in prm] + [_full_spec(bdm)],
        out_specs=seq(g),
        out_shape=jax.ShapeDtypeStruct((b, s, g), F32),
        scratch_shapes=[pltpu.VMEM((b, tt + SUBLANES, N_RKV), F32),
                        pltpu.VMEM((b, tt + SUBLANES, LANES), F32)]
        + [big] * 13
        + [pltpu.VMEM((b, tt // RWKV_CHUNK * SUBLANES, g), F32), big,
           pltpu.VMEM((b, g, g), F32)],
        compiler_params=_params(1),
        name="rwkv7",
    )(p_rkv, misc, *prm, bdm)


def _lru_kernel(x_ref, cw_ref, cb_ref, wa_ref, ba_ref, wx_ref, bx_ref, lam_ref, o_ref,
                pad, a_s, u_s, h_s, *, tt):
    t_idx = pl.program_id(1)

    @pl.when(t_idx == 0)
    def _():
        pad[0:SUBLANES, :] = jnp.zeros((SUBLANES, D_GROUP), F32)
        h_s[...] = jnp.zeros((1, D_GROUP), F32)

    x = x_ref[0]
    pad[SUBLANES:SUBLANES + tt, :] = x
    xc = cw_ref[CONV_WIDTH - 1:CONV_WIDTH, :] * x + cb_ref[...]
    for d in range(1, CONV_WIDTH):
        tap = CONV_WIDTH - 1 - d
        xc = xc + cw_ref[tap:tap + 1, :] * pad[SUBLANES - d:SUBLANES - d + tt, :]
    pad[0:SUBLANES, :] = x[tt - SUBLANES:tt, :]

    xb = xc.astype(BF16)
    r = _sigmoid(_dot(xb, wa_ref[...]) + ba_ref[...])
    i = _sigmoid(_dot(xb, wx_ref[...]) + bx_ref[...])
    log_a = -LRU_C * r * _softplus(-lam_ref[...])
    a_s[...] = jnp.exp(log_a)
    th = jnp.tanh(log_a)
    u_s[...] = jnp.sqrt(-2.0 * th / (1.0 - th)) * (i * xc)

    def group(gi, h):
        base = pl.multiple_of(gi * SUBLANES, SUBLANES)
        a8 = a_s[pl.ds(base, SUBLANES), :]
        u8 = u_s[pl.ds(base, SUBLANES), :]
        rows = []
        for j in range(SUBLANES):
            h = a8[j:j + 1, :] * h + u8[j:j + 1, :]
            rows.append(h)
        o_ref[0, pl.ds(base, SUBLANES), :] = jnp.concatenate(rows, axis=0)
        return h

    h_s[...] = lax.fori_loop(0, tt // SUBLANES, group, h_s[...])


def _lru(x, prm, *, tt):
    b, s, g = x.shape
    seq = pl.BlockSpec((1, tt, g), lambda bi, ti: (bi, ti, 0))
    return pl.pallas_call(
        functools.partial(_lru_kernel, tt=tt),
        grid=(b, s // tt),
        in_specs=[seq] + [_full_spec(a) for a in prm],
        out_specs=seq,
        out_shape=jax.ShapeDtypeStruct((b, s, g), F32),
        scratch_shapes=[pltpu.VMEM((tt + SUBLANES, g), F32), pltpu.VMEM((tt, g), F32),
                        pltpu.VMEM((tt, g), F32), pltpu.VMEM((1, g), F32)],
        compiler_params=_params(2),
        name="rg_lru",
    )(x, *prm)


def _outproj_kernel(x_ref, yft_ref, yst_ref, yr_ref, yl_ref, gates_ref, w_ref, fg_ref, o_ref, *,
                    final):
    acc = x_ref[0]
    ys = (yft_ref[0].T, yst_ref[0].T, yr_ref[0], yl_ref[0])
    for gi, y in enumerate(ys):
        gate = gates_ref[0, :, gi * D_GROUP:(gi + 1) * D_GROUP]
        y = y * (gate * _sigmoid(gate))
        acc = acc + _dot(y.astype(BF16), w_ref[gi * D_GROUP:(gi + 1) * D_GROUP, :])
    if final:
        ms = jnp.mean(acc * acc, axis=-1, keepdims=True)
        acc = acc * lax.rsqrt(ms + RMS_EPS) * fg_ref[...]
    o_ref[0] = acc


def _outproj(x, y_fox_t, y_sb_t, y_rw, y_lru, gates, w, final_g, *, tm, final):
    b, s, _ = x.shape
    seq = lambda width: pl.BlockSpec((1, tm, width), lambda bi, i: (bi, i, 0))
    chan = pl.BlockSpec((1, D_GROUP, tm), lambda bi, i: (bi, 0, i))
    return pl.pallas_call(
        functools.partial(_outproj_kernel, final=final),
        grid=(b, s // tm),
        in_specs=[seq(D_MODEL), chan, chan, seq(D_GROUP), seq(D_GROUP), seq(4 * D_GROUP),
                  _full_spec(w), _full_spec(final_g)],
        out_specs=seq(D_MODEL),
        out_shape=jax.ShapeDtypeStruct((b, s, D_MODEL), F32),
        compiler_params=_params(2),
        name="outproj",
    )(x, y_fox_t, y_sb_t, y_rw, y_lru, gates, w, final_g)


def _permute_w_in(w):
    g, h, r, dh = D_GROUP, N_HEADS, RWKV_LORA, HEAD_DIM
    o_ff = 4 * g
    o_sb = o_ff + h
    o_rw = o_sb + 4 * g
    o_rg = o_rw + 3 * g + 2 * r
    o_lx = o_rg + g
    o_lg = o_lx + g

    def head_pad(cols):
        c3 = cols.reshape(cols.shape[0], h, dh)
        return jnp.pad(c3, ((0, 0), (0, 0), (0, HEAD_PAD - dh))).reshape(cols.shape[0], QK_COLS)

    cols = [head_pad(w[:, 0:g]), head_pad(w[:, g:2 * g]),
            head_pad(w[:, o_sb:o_sb + g]), head_pad(w[:, o_sb + g:o_sb + 2 * g]),
            w[:, 2 * g:3 * g], w[:, o_sb + 2 * g:o_sb + 3 * g],
            w[:, 3 * g:4 * g], w[:, o_sb + 3 * g:o_sb + 4 * g], w[:, o_rg:o_rg + g],
            w[:, o_lg:o_lg + g],
            w[:, o_rw:o_rw + 3 * g], w[:, o_lx:o_lx + g],
            w[:, o_rw + 3 * g:o_rw + 3 * g + 2 * r], w[:, o_ff:o_ff + h],
            jnp.zeros((w.shape[0], LANES - 2 * r - h), w.dtype)]
    return jnp.concatenate(cols, axis=1).astype(BF16)


def _forget_placement():
    selq = np.zeros((3 * LANES, QK_COLS), np.float32)
    selk = np.zeros((3 * LANES, QK_COLS), np.float32)
    augq = np.zeros((1, QK_COLS), np.float32)
    augk = np.zeros((1, QK_COLS), np.float32)
    for hd in range(N_HEADS):
        for piece in range(3):
            src = piece * LANES + FF_LANE + hd
            selq[src, hd * HEAD_PAD + AUG_Q_F + piece] = 1.0
            selk[src, hd * HEAD_PAD + AUG_K_F + piece] = -1.0
            augq[0, hd * HEAD_PAD + AUG_Q_ONE + piece] = 1.0
            augk[0, hd * HEAD_PAD + AUG_K_ONE + piece] = 1.0
    return (jnp.asarray(selq, BF16), jnp.asarray(selk, BF16), jnp.asarray(augq), jnp.asarray(augk))


def _block_diag(w):
    h, n, _ = w.shape
    eye = jnp.eye(h, dtype=w.dtype)
    return jnp.einsum('hij,hk->hikj', w, eye).reshape(h * n, h * n)


def _pick_tile(s, pref):
    t = pref
    while s % t:
        t //= 2
    return t


def kernel(x, norm_g, w_in, b_forget, rwkv_mu, rwkv_w0, rwkv_w2, rwkv_a0, rwkv_a2, rwkv_k_k,
           rwkv_k_a, rwkv_r_k, rwkv_ln_g, rwkv_ln_b, lru_conv_w, lru_conv_b, lru_w_a, lru_b_a,
           lru_w_x, lru_b_x, lru_lambda, w_out, final_g):
    b, s, d = x.shape
    depth = w_in.shape[0]
    g, h, dh, r = D_GROUP, N_HEADS, HEAD_DIM, RWKV_LORA
    tm = _pick_tile(s, 512)
    tq = _pick_tile(s, 256)
    tk_fox = _pick_tile(s, 2 * tq)
    tt = _pick_tile(s, 512)
    row = lambda a: a.reshape(1, -1).astype(F32)

    bdm = _block_diag(jnp.ones((h, dh, dh), BF16))
    selq, selk, augq, augk = _forget_placement()
    for l in range(depth):
        fbias = jnp.zeros((1, LANES), F32).at[0, FF_LANE:FF_LANE + h].set(b_forget[l])
        consts = [row(norm_g[l]), _permute_w_in(w_in[l]), fbias, selq, selk, augq, augk]
        fq, fk, fvt, sq, sk, svt, gates, rkv, lx, misc = _inproj(x, consts, tm=tm)

        y_fox_t = _attention_call(functools.partial(_fox_kernel, tq=tq, tk=tk_fox),
                                  "fox_attention", fq, fk, fvt, tq=tq,
                                  scratch=_fox_scratch(tq, tk_fox))
        y_sb_t = _attention_call(functools.partial(_sb_kernel, tq=tq),
                                 "sb_attention", sq, sk, svt, tq=tq)

        mu = rwkv_mu[l]
        pad_rows = lambda a, lo: jnp.zeros((LANES, g), F32).at[lo:lo + r].set(a)
        rw_prm = [row(mu[:N_RKV]),
                  jnp.zeros((1, LANES), F32).at[0, :2 * r].set(mu[N_RKV:]),
                  row(rwkv_w0[l]), pad_rows(rwkv_w2[l], 0), row(rwkv_a0[l]), pad_rows(rwkv_a2[l], r),
                  row(rwkv_k_k[l]), row(rwkv_k_a[l]), row(rwkv_r_k[l]), row(rwkv_ln_g[l]),
                  row(rwkv_ln_b[l])]
        y_rw = _rwkv(rkv, misc, rw_prm, bdm, tt=tt)

        lru_prm = [lru_conv_w[l].astype(F32), row(lru_conv_b[l]),
                   _block_diag(lru_w_a[l]).astype(BF16), row(lru_b_a[l]),
                   _block_diag(lru_w_x[l]).astype(BF16), row(lru_b_x[l]), row(lru_lambda[l])]
        y_lru = _lru(lx, lru_prm, tt=tt)

        x = _outproj(x, y_fox_t, y_sb_t, y_rw, y_lru, gates, w_out[l].astype(BF16), row(final_g),
                     tm=tm, final=(l == depth - 1))
    return x
```

```python
import functools

import numpy as np
import jax
import jax.numpy as jnp
from jax import lax
from jax.experimental import pallas as pl
from jax.experimental.pallas import tpu as pltpu

F32 = jnp.float32
BF16 = jnp.bfloat16

D_MODEL = 1024
D_GROUP = 256
N_HEADS = 4
HEAD_DIM = 64
RWKV_LORA = 32
CONV_WIDTH = 4
LRU_C = 8.0
RMS_EPS = 1e-6
GN_EPS = 64e-5
N_RKV = 3 * D_GROUP

LANES = 128
SUBLANES = 8
VMEM_LIMIT_BYTES = 56 * 1024 * 1024

HEAD_PAD = LANES
QK_COLS = N_HEADS * HEAD_PAD
C_FQ = 0
C_FK = C_FQ + QK_COLS
C_SQ = C_FK + QK_COLS
C_SK = C_SQ + QK_COLS
C_FV = C_SK + QK_COLS
C_SV = C_FV + D_GROUP
C_GATES = C_SV + D_GROUP
C_RKV = C_GATES + 4 * D_GROUP
C_LX = C_RKV + N_RKV
C_MISC = C_LX + D_GROUP
N_PROJ = C_MISC + LANES
FF_LANE = 2 * RWKV_LORA
AUG_K_F = HEAD_DIM
AUG_K_ONE = HEAD_DIM + 3
AUG_Q_ONE = HEAD_DIM
AUG_Q_F = HEAD_DIM + 3

V_AUG = HEAD_DIM + 16
LOG2E = 1.4426950408889634

NEG_BIG = -1e30
SB_SKIP_LOG = -106.0
RWKV_CHUNK = 64
RWKV_UNROLL = 2


def _dot(a, b):
    return jnp.dot(a, b, preferred_element_type=F32)


def _dot_nt(a, b):
    return lax.dot_general(a, b, (((1,), (1,)), ((), ())), preferred_element_type=F32)


def _split2(x):
    hi = x.astype(BF16)
    lo = (x - hi.astype(F32)).astype(BF16)
    return hi, lo


def _split3(x):
    hi = x.astype(BF16)
    r1 = x - hi.astype(F32)
    mid = r1.astype(BF16)
    lo = (r1 - mid.astype(F32)).astype(BF16)
    return hi, mid, lo


def _dot3(a, b):
    ah, al = _split2(a)
    bh, bl = _split2(b)
    return _dot(ah, bh) + (_dot(al, bh) + _dot(ah, bl))


def _dot_sel_lhs(sel, x):
    hi, mid, lo = _split3(x)
    return _dot(sel, hi) + (_dot(sel, mid) + _dot(sel, lo))


def _dot_sel_rhs(x, sel):
    hi, lo = _split2(x)
    return _dot(hi, sel) + _dot(lo, sel)


def _softplus(x):
    return jnp.maximum(x, 0.0) + jnp.log1p(jnp.exp(-jnp.abs(x)))


def _log_sigmoid(x):
    return jnp.minimum(x, 0.0) - jnp.log1p(jnp.exp(-jnp.abs(x)))


def _sigmoid(x):
    return 1.0 / (1.0 + jnp.exp(-x))


def _iota(shape, dim):
    return lax.broadcasted_iota(jnp.int32, shape, dim)


def _full_spec(a):
    return pl.BlockSpec(a.shape, lambda *_: (0,) * a.ndim)


def _params(n_grid):
    return pltpu.CompilerParams(dimension_semantics=("arbitrary",) * n_grid,
                                vmem_limit_bytes=VMEM_LIMIT_BYTES)


def _inproj_kernel(x_ref, g_ref, w_ref, fb_ref, selq_ref, selk_ref, augq_ref, augk_ref,
                   fq_ref, fk_ref, fvt_ref, sq_ref, sk_ref, svt_ref, gates_ref, rkv_ref, lx_ref,
                   misc_ref, ftot, v_stage):
    @pl.when(pl.program_id(1) == 0)
    def _():
        ftot[...] = jnp.zeros_like(ftot)

    x = x_ref[0]
    tm = x.shape[0]
    ms = jnp.mean(x * x, axis=-1, keepdims=True)
    h = (x * lax.rsqrt(ms + RMS_EPS) * g_ref[...]).astype(BF16)
    scale = HEAD_DIM ** -0.5

    def proj(c0, width):
        return _dot(h, w_ref[:, c0:c0 + width])

    misc = proj(C_MISC, LANES)
    misc_ref[0] = misc
    gates_ref[0] = proj(C_GATES, 4 * D_GROUP)
    rkv_ref[0] = proj(C_RKV, N_RKV)
    lx_ref[0] = proj(C_LX, D_GROUP)

    lf = _log_sigmoid(misc + fb_ref[...])
    lower = (_iota((tm, tm), 1) <= _iota((tm, tm), 0)).astype(BF16)
    f = _dot_sel_lhs(lower, lf) + ftot[...]
    ftot[...] = f[tm - 1:tm, :]
    pieces = jnp.concatenate(_split3(f * LOG2E), axis=1)

    fq = proj(C_FQ, QK_COLS) * (scale * LOG2E) + _dot(pieces, selq_ref[...]) + augq_ref[...]
    fk = proj(C_FK, QK_COLS) + _dot(pieces, selk_ref[...]) + augk_ref[...]
    sq = proj(C_SQ, QK_COLS) * scale
    sk = proj(C_SK, QK_COLS)
    for hd in range(N_HEADS):
        cols = slice(hd * HEAD_PAD, (hd + 1) * HEAD_PAD)
        fq_ref[0, hd] = fq[:, cols].astype(BF16)
        fk_ref[0, hd] = fk[:, cols].astype(BF16)
        sq_ref[0, hd] = sq[:, cols].astype(BF16)
        sk_ref[0, hd] = sk[:, cols].astype(BF16)
    v_stage[...] = proj(C_SV, D_GROUP)
    svt_ref[0] = v_stage[...].T.astype(BF16)
    v_stage[...] = proj(C_FV, D_GROUP)
    vt = v_stage[...].T.astype(BF16)
    ones_rows = (_iota((V_AUG - HEAD_DIM, tm), 0) == 0).astype(BF16)
    for hd in range(N_HEADS):
        fvt_ref[0, hd * V_AUG:hd * V_AUG + HEAD_DIM, :] = vt[hd * HEAD_DIM:(hd + 1) * HEAD_DIM, :]
        fvt_ref[0, hd * V_AUG + HEAD_DIM:(hd + 1) * V_AUG, :] = ones_rows


def _inproj(x, consts, *, tm):
    b, s, _ = x.shape
    seq = lambda width: pl.BlockSpec((1, tm, width), lambda bi, i: (bi, i, 0))
    heads = pl.BlockSpec((1, N_HEADS, tm, HEAD_PAD), lambda bi, i: (bi, 0, i, 0))
    chan = pl.BlockSpec((1, D_GROUP, tm), lambda bi, i: (bi, 0, i))
    heads_shape = jax.ShapeDtypeStruct((b, N_HEADS, s, HEAD_PAD), BF16)
    chan_shape = jax.ShapeDtypeStruct((b, D_GROUP, s), BF16)
    chan_aug = pl.BlockSpec((1, N_HEADS * V_AUG, tm), lambda bi, i: (bi, 0, i))
    chan_aug_shape = jax.ShapeDtypeStruct((b, N_HEADS * V_AUG, s), BF16)
    seq_shape = lambda width: jax.ShapeDtypeStruct((b, s, width), F32)
    return pl.pallas_call(
        _inproj_kernel,
        grid=(b, s // tm),
        in_specs=[seq(D_MODEL)] + [_full_spec(a) for a in consts],
        out_specs=[heads, heads, chan_aug, heads, heads, chan,
                   seq(4 * D_GROUP), seq(N_RKV), seq(D_GROUP), seq(LANES)],
        out_shape=[heads_shape, heads_shape, chan_aug_shape, heads_shape, heads_shape, chan_shape,
                   seq_shape(4 * D_GROUP), seq_shape(N_RKV), seq_shape(D_GROUP), seq_shape(LANES)],
        scratch_shapes=[pltpu.VMEM((1, LANES), F32), pltpu.VMEM((tm, D_GROUP), F32)],
        compiler_params=_params(2),
        name="inproj",
    )(x, *consts)


def _fox_kernel(q_ref, k_ref, vt_ref, o_ref, s_scr, p_scr, mx_scr, al_scr, m_scr, acc_scr, *, tq, tk):
    i = pl.program_id(2)
    q = q_ref[0, 0]
    n_full = (i * tq) // tk

    def scores_to(j, slot):
        s = _dot_nt(k_ref[0, 0, pl.ds(pl.multiple_of(j * tk, tk), tk), :], q)
        s_scr[slot] = s
        mx_scr[slot] = jnp.max(s, axis=0, keepdims=True)

    def weighted_values(j, slot):
        return _dot(vt_ref[0, :, pl.ds(pl.multiple_of(j * tk, tk), tk)], p_scr[slot])

    def softmax_to(s, mx, slot):
        m_old = m_scr[...]
        m_new = jnp.maximum(m_old, mx)
        m_scr[...] = m_new
        al_scr[slot] = jnp.exp2(m_old - m_new)
        p_scr[slot] = jnp.exp2(s - m_new).astype(BF16)

    def stage(j, cur):
        nxt = 1 - cur
        pv_prev = weighted_values(jnp.maximum(j - 1, 0), nxt)
        scores_to(j + 1, nxt)
        softmax_to(s_scr[cur], mx_scr[cur], cur)
        acc_scr[...] = al_scr[nxt] * acc_scr[...] + pv_prev

    def tail(cur):
        nxt = 1 - cur
        pv_prev = weighted_values(jnp.maximum(n_full - 1, 0), nxt)
        kpos = n_full * tk + _iota((tk, tq), 0)
        s = jnp.where(kpos <= i * tq + _iota((tk, tq), 1), s_scr[cur], NEG_BIG)
        softmax_to(s, jnp.max(s, axis=0, keepdims=True), cur)
        acc = al_scr[cur] * (al_scr[nxt] * acc_scr[...] + pv_prev) + weighted_values(n_full, cur)
        o_ref[0] = acc[0:HEAD_DIM] / acc[HEAD_DIM:HEAD_DIM + 1]

    m_scr[...] = jnp.full((1, tq), NEG_BIG, F32)
    acc_scr[...] = jnp.zeros((V_AUG, tq), F32)
    p_scr[1] = jnp.zeros((tk, tq), BF16)
    al_scr[1] = jnp.ones((1, tq), F32)
    scores_to(0, 0)

    def pair(jj, carry):
        stage(2 * jj, 0)
        stage(2 * jj + 1, 1)
        return carry

    lax.fori_loop(0, n_full // 2, pair, 0)

    @pl.when(n_full % 2 == 1)
    def _():
        stage(n_full - 1, 0)
        tail(1)

    @pl.when(n_full % 2 == 0)
    def _():
        tail(0)


def _fox_scratch(tq, tk):
    return [pltpu.VMEM((2, tk, tq), F32), pltpu.VMEM((2, tk, tq), BF16),
            pltpu.VMEM((2, 1, tq), F32), pltpu.VMEM((2, 1, tq), F32),
            pltpu.VMEM((1, tq), F32), pltpu.VMEM((V_AUG, tq), F32)]


def _attention_call(kernel_fn, name, q, k, vt, *, tq, scratch=()):
    b, h, s, _ = q.shape
    return pl.pallas_call(
        kernel_fn,
        grid=(b, h, s // tq),
        in_specs=[pl.BlockSpec((1, 1, tq, HEAD_PAD), lambda bi, hi, i: (bi, hi, i, 0)),
                  pl.BlockSpec((1, 1, s, HEAD_PAD), lambda bi, hi, i: (bi, hi, 0, 0)),
                  pl.BlockSpec((1, V_AUG, s), lambda bi, hi, i: (bi, hi, 0))],
        out_specs=pl.BlockSpec((1, HEAD_DIM, tq), lambda bi, hi, i: (bi, hi, i)),
        scratch_shapes=list(scratch),
        out_shape=jax.ShapeDtypeStruct((b, h * HEAD_DIM, s), F32),
        compiler_params=_params(3),
        name=name,
    )(q, k, vt)


def _sb_kernel(q_ref, k_ref, vt_ref, o_ref, *, tq):
    i = pl.program_id(1)
    tk = tq
    heads = list(range(N_HEADS))
    q = [q_ref[0, hd] for hd in heads]
    later = (_iota((tk, tk), 0) < _iota((tk, tk), 1)).astype(BF16)

    def each(f, *xs):
        return [f(*a) for a in zip(*xs)]

    def block(j, rest_q, acc, masked):
        ks = pl.multiple_of(j * tk, tk)
        z = [_dot_nt(k_ref[0, hd, pl.ds(ks, tk), :], q[hd]) for hd in heads]
        log_keep = each(lambda x: _log_sigmoid(-x), z)
        if masked:
            mask = _iota((tk, tq), 0) < _iota((tk, tq), 1)
            log_keep = each(lambda x: jnp.where(mask, x, 0.0), log_keep)
        split = each(_split2, log_keep)
        rest_in = each(lambda hl: _dot(later, hl[0]) + _dot(later, hl[1]), split)
        att = each(lambda x, lk, ri, rq: jnp.exp(x + lk + ri + rq), z, log_keep, rest_in, rest_q)
        if masked:
            att = each(lambda x: jnp.where(mask, x, 0.0), att)
        pv = [_dot(vt_ref[0, hd * HEAD_DIM:(hd + 1) * HEAD_DIM, pl.ds(ks, tk)],
                   att[hd].astype(BF16)) for hd in heads]
        acc = each(lambda a, x: a + x, acc, pv)
        rest_q = each(lambda rq, ri, lk: rq + ri[0:1, :] + lk[0:1, :], rest_q, rest_in, log_keep)
        return rest_q, acc

    rest_q, acc = block(i, [jnp.zeros((1, tq), F32)] * N_HEADS,
                        [jnp.zeros((HEAD_DIM, tq), F32)] * N_HEADS, True)

    def cond(c):
        j, rest_q, _ = c
        alive = functools.reduce(jnp.maximum, rest_q)
        return jnp.logical_and(j >= 0, jnp.max(alive) > SB_SKIP_LOG)

    def body(c):
        j, rest_q, acc = c
        rest_q, acc = block(j, list(rest_q), list(acc), False)
        return j - 1, tuple(rest_q), tuple(acc)

    _, _, acc = lax.while_loop(cond, body, (i - 1, tuple(rest_q), tuple(acc)))
    o_ref[0] = jnp.concatenate(list(acc), axis=0)


def _sb_attention(q, k, vt, *, tq):
    b, h, s, _ = q.shape
    return pl.pallas_call(
        functools.partial(_sb_kernel, tq=tq),
        grid=(b, s // tq),
        in_specs=[pl.BlockSpec((1, h, tq, HEAD_PAD), lambda bi, i: (bi, 0, i, 0)),
                  pl.BlockSpec((1, h, s, HEAD_PAD), lambda bi, i: (bi, 0, 0, 0)),
                  pl.BlockSpec((1, h * HEAD_DIM, s), lambda bi, i: (bi, 0, 0))],
        out_specs=pl.BlockSpec((1, h * HEAD_DIM, tq), lambda bi, i: (bi, 0, i)),
        out_shape=jax.ShapeDtypeStruct((b, h * HEAD_DIM, s), F32),
        compiler_params=_params(2),
        name="sb_attention",
    )(q, k, vt)


def _rwkv_kernel(p_ref, m_ref, mu_rkv_ref, mu_misc_ref, w0_ref, w2_ref, a0_ref, a2_ref,
                 kk_ref, ka_ref, rk_ref, lng_ref, lnb_ref, bdm_ref, o_ref,
                 pad_rkv, pad_misc, r_s, k_s, v_s, kn_s, al_s, lw_s,
                 wt_s, u0_s, o0_s, mrb_s, rt_s, bh_s, kh_s, pc_s, y_s, ht_s, *, tt, nb):
    c = RWKV_CHUNK
    g = D_GROUP
    n_chunks = tt // c

    @pl.when(pl.program_id(0) == 0)
    def _():
        pad_rkv[:, 0:SUBLANES, :] = jnp.zeros((nb, SUBLANES, N_RKV), F32)
        pad_misc[:, 0:SUBLANES, :] = jnp.zeros((nb, SUBLANES, LANES), F32)
        ht_s[...] = jnp.zeros_like(ht_s)

    bdm = bdm_ref[...]
    bdm_f = bdm.astype(F32)

    for b in range(nb):
        p = p_ref[b]
        misc = m_ref[b]
        pad_rkv[b, SUBLANES:SUBLANES + tt, :] = p
        pad_misc[b, SUBLANES:SUBLANES + tt, :] = misc
        p_prev = pad_rkv[b, SUBLANES - 1:SUBLANES - 1 + tt, :]
        m_prev = pad_misc[b, SUBLANES - 1:SUBLANES - 1 + tt, :]
        pad_rkv[b, 0:SUBLANES, :] = p[tt - SUBLANES:tt, :]
        pad_misc[b, 0:SUBLANES, :] = misc[tt - SUBLANES:tt, :]
        p = p + (p_prev - p) * mu_rkv_ref[...]
        misc = misc + (m_prev - misc) * mu_misc_ref[...]
        k = p[:, g:2 * g]

        w = -_softplus(-(w0_ref[...] + _dot3(jnp.tanh(misc), w2_ref[...]))) - 0.5
        alpha = _sigmoid(a0_ref[...] + _dot3(misc, a2_ref[...]))
        kn = k * kk_ref[...]
        ss = _dot_sel_rhs(kn * kn, bdm)
        r_s[b] = p[:, 0:g]
        k_s[b] = k * (1.0 + (alpha - 1.0) * ka_ref[...])
        v_s[b] = p[:, 2 * g:3 * g]
        kn_s[b] = kn * lax.rsqrt(jnp.maximum(ss, 1e-12))
        al_s[b] = alpha
        lw_s[b] = -jnp.exp(w)

    row = _iota((c, g), 0)
    col = _iota((c, g), 1) % c
    strict = col < row
    incl = col <= row
    eye = (col == row).astype(F32)
    lower_c = (_iota((c, c), 1) <= _iota((c, c), 0)).astype(BF16)
    level_masks = []
    m = 1
    while m < c:
        level_masks.append(jnp.logical_and(
            strict, jnp.logical_and(row // (2 * m) == col // (2 * m), row // m != col // m)))
        m *= 2

    def bd(x):
        return jnp.concatenate([x.astype(BF16)] * N_HEADS, axis=0) * bdm

    def mm(a, b_bf16):
        return _dot(a.astype(BF16), b_bf16)

    def mm_nt(a, b_bf16):
        return _dot_nt(a.astype(BF16), b_bf16)

    def each(f, *xs):
        return [f(*a) for a in zip(*xs)]

    def phase_a(chains):
        sls = [pl.ds(pl.multiple_of(ci * c, c), c) for _, ci in chains]
        ld = lambda ref: [ref[b, sl, :] for (b, _), sl in zip(chains, sls)]
        r_c, k_c, v_c, kn_c, al_c, lw_c = (ld(s) for s in (r_s, k_s, v_s, kn_s, al_s, lw_s))
        cl = each(lambda x: _dot_sel_lhs(lower_c, x), lw_c)
        cl_last = each(lambda x: x[c - 1:c, :], cl)
        a_t = each(lambda kn, x, lw: -kn * jnp.exp(x - lw), kn_c, cl, lw_c)
        r_t = each(lambda r, x: r * jnp.exp(x), r_c, cl)
        q_inv = each(lambda x: jnp.exp(-x), cl)
        p_rem = each(lambda xl, x: jnp.exp(xl - x), cl_last, cl)
        kna = each(lambda kn, al: kn * al, kn_c, al_c)
        ar = each(lambda a, r: jnp.concatenate([a, r], axis=0), a_t, r_t)
        s_b = each(lambda x, kb, qi: mm_nt(x, bd(kb * qi)), ar, kna, q_inv)
        s_k = each(lambda x, kk, qi: mm_nt(x, bd(kk * qi)), ar, k_c, q_inv)
        n = each(lambda x: jnp.where(strict, x[0:c], 0.0), s_b)
        a_ak = each(lambda x: jnp.where(strict, x[0:c], 0.0), s_k)
        m_rb = each(lambda x: jnp.where(incl, x[c:2 * c], 0.0), s_b)
        m_rk = each(lambda x: jnp.where(incl, x[c:2 * c], 0.0), s_k)

        inv = each(lambda x: eye + jnp.where(level_masks[0], x, 0.0), n)
        for lm in level_masks[1:]:
            half = each(lambda d, x: mm(d, bd(jnp.where(lm, x, 0.0))), inv, n)
            inv = each(lambda d, hf: d + mm(hf, bd(d)), inv, half)

        v_bd = each(bd, v_c)
        akv = each(mm, a_ak, v_bd)
        wt = each(lambda d, a: mm(d, bd(a)), inv, a_t)
        u0 = each(lambda d, x: mm(d, bd(x)), inv, akv)
        o0 = each(mm, m_rk, v_bd)
        bh = each(lambda x, p: x * p, kna, p_rem)
        kh = each(lambda x, p: x * p, k_c, p_rem)
        for ref, vals in zip((wt_s, u0_s, o0_s, mrb_s, rt_s, bh_s, kh_s),
                             (wt, u0, o0, m_rb, r_t, bh, kh)):
            for (b, _), sl, val in zip(chains, sls, vals):
                ref[b, sl, :] = val
        for (b, ci), xl in zip(chains, cl_last):
            pc_s[b, pl.ds(pl.multiple_of(ci * SUBLANES, SUBLANES), SUBLANES), :] = (
                jnp.broadcast_to(jnp.exp(xl), (SUBLANES, g)))

    def phase_b(ci):
        sl = pl.ds(pl.multiple_of(ci * c, c), c)
        bs = list(range(nb))
        ht = [ht_s[b] for b in bs]
        wr = [jnp.concatenate([wt_s[b, sl, :], rt_s[b, sl, :]], axis=0) for b in bs]
        wrh = each(lambda x, hh: mm_nt(x, hh.astype(BF16)), wr, ht)
        u = [x[0:c] + u0_s[b, sl, :] for x, b in zip(wrh, bs)]
        uv_t = [jnp.concatenate([x, v_s[b, sl, :]], axis=0).T for x, b in zip(u, bs)]
        bk = [jnp.concatenate([bh_s[b, sl, :], kh_s[b, sl, :]], axis=0) for b in bs]
        upd = each(lambda x, y: mm(x, y.astype(BF16)), uv_t, bk)
        mu_ = [mm(mrb_s[b, sl, :], bd(x)) for x, b in zip(u, bs)]
        for b in bs:
            p_c = pc_s[b, pl.ds(pl.multiple_of(ci * SUBLANES, SUBLANES), 1), :]
            y_s[b, sl, :] = wrh[b][c:2 * c] + mu_[b] + o0_s[b, sl, :]
            ht_s[b] = ht[b] * p_c + bdm_f * upd[b]

    def loop_a(ci, carry):
        phase_a([(b, ci * RWKV_UNROLL + j) for j in range(RWKV_UNROLL) for b in range(nb)])
        return carry

    def loop_b(ci, carry):
        phase_b(ci)
        return carry

    lax.fori_loop(0, n_chunks // RWKV_UNROLL, loop_a, 0)
    lax.fori_loop(0, n_chunks, loop_b, 0)

    inv_n = 1.0 / HEAD_DIM
    for b in range(nb):
        y = y_s[b]
        mean = _dot_sel_rhs(y, bdm) * inv_n
        yc = y - mean
        var = _dot_sel_rhs(yc * yc, bdm) * inv_n
        yn = yc * lax.rsqrt(var + GN_EPS) * lng_ref[...] + lnb_ref[...]
        bonus = _dot_sel_rhs(r_s[b] * k_s[b] * rk_ref[...], bdm) * v_s[b]
        o_ref[b] = yn + bonus


def _rwkv(p_rkv, misc, prm, bdm, *, tt):
    b, s, _ = p_rkv.shape
    g = D_GROUP
    seq = lambda width: pl.BlockSpec((b, tt, width), lambda ti: (0, ti, 0))
    big = pltpu.VMEM((b, tt, g), F32)
    return pl.pallas_call(
        functools.partial(_rwkv_kernel, tt=tt, nb=b),
        grid=(s // tt,),
        in_specs=[seq(N_RKV), seq(LANES)] + [_full_spec(a) for a in prm] + [_full_spec(bdm)],
        out_specs=seq(g),
        out_shape=jax.ShapeDtypeStruct((b, s, g), F32),
        scratch_shapes=[pltpu.VMEM((b, tt + SUBLANES, N_RKV), F32),
                        pltpu.VMEM((b, tt + SUBLANES, LANES), F32)]
        + [big] * 13
        + [pltpu.VMEM((b, tt // RWKV_CHUNK * SUBLANES, g), F32), big,
           pltpu.VMEM((b, g, g), F32)],
        compiler_params=_params(1),
        name="rwkv7",
    )(p_rkv, misc, *prm, bdm)


def _lru_kernel(x_ref, cw_ref, cb_ref, wa_ref, ba_ref, wx_ref, bx_ref, lam_ref, o_ref,
                pad, a_s, u_s, h_s, *, tt):
    t_idx = pl.program_id(1)

    @pl.when(t_idx == 0)
    def _():
        pad[0:SUBLANES, :] = jnp.zeros((SUBLANES, D_GROUP), F32)
        h_s[...] = jnp.zeros((1, D_GROUP), F32)

    x = x_ref[0]
    pad[SUBLANES:SUBLANES + tt, :] = x
    xc = cw_ref[CONV_WIDTH - 1:CONV_WIDTH, :] * x + cb_ref[...]
    for d in range(1, CONV_WIDTH):
        tap = CONV_WIDTH - 1 - d
        xc = xc + cw_ref[tap:tap + 1, :] * pad[SUBLANES - d:SUBLANES - d + tt, :]
    pad[0:SUBLANES, :] = x[tt - SUBLANES:tt, :]

    xb = xc.astype(BF16)
    r = _sigmoid(_dot(xb, wa_ref[...]) + ba_ref[...])
    i = _sigmoid(_dot(xb, wx_ref[...]) + bx_ref[...])
    log_a = -LRU_C * r * _softplus(-lam_ref[...])
    a_s[...] = jnp.exp(log_a)
    th = jnp.tanh(log_a)
    u_s[...] = jnp.sqrt(-2.0 * th / (1.0 - th)) * (i * xc)

    def group(gi, h):
        base = pl.multiple_of(gi * SUBLANES, SUBLANES)
        a8 = a_s[pl.ds(base, SUBLANES), :]
        u8 = u_s[pl.ds(base, SUBLANES), :]
        rows = []
        for j in range(SUBLANES):
            h = a8[j:j + 1, :] * h + u8[j:j + 1, :]
            rows.append(h)
        o_ref[0, pl.ds(base, SUBLANES), :] = jnp.concatenate(rows, axis=0)
        return h

    h_s[...] = lax.fori_loop(0, tt // SUBLANES, group, h_s[...])


def _lru(x, prm, *, tt):
    b, s, g = x.shape
    seq = pl.BlockSpec((1, tt, g), lambda bi, ti: (bi, ti, 0))
    return pl.pallas_call(
        functools.partial(_lru_kernel, tt=tt),
        grid=(b, s // tt),
        in_specs=[seq] + [_full_spec(a) for a in prm],
        out_specs=seq,
        out_shape=jax.ShapeDtypeStruct((b, s, g), F32),
        scratch_shapes=[pltpu.VMEM((tt + SUBLANES, g), F32), pltpu.VMEM((tt, g), F32),
                        pltpu.VMEM((tt, g), F32), pltpu.VMEM((1, g), F32)],
        compiler_params=_params(2),
        name="rg_lru",
    )(x, *prm)


def _outproj_kernel(x_ref, yft_ref, yst_ref, yr_ref, yl_ref, gates_ref, w_ref, fg_ref, o_ref, *,
                    final):
    acc = x_ref[0]
    ys = (yft_ref[0].T, yst_ref[0].T, yr_ref[0], yl_ref[0])
    for gi, y in enumerate(ys):
        gate = gates_ref[0, :, gi * D_GROUP:(gi + 1) * D_GROUP]
        y = y * (gate * _sigmoid(gate))
        acc = acc + _dot(y.astype(BF16), w_ref[gi * D_GROUP:(gi + 1) * D_GROUP, :])
    if final:
        ms = jnp.mean(acc * acc, axis=-1, keepdims=True)
        acc = acc * lax.rsqrt(ms + RMS_EPS) * fg_ref[...]
    o_ref[0] = acc


def _outproj(x, y_fox_t, y_sb_t, y_rw, y_lru, gates, w, final_g, *, tm, final):
    b, s, _ = x.shape
    seq = lambda width: pl.BlockSpec((1, tm, width), lambda bi, i: (bi, i, 0))
    chan = pl.BlockSpec((1, D_GROUP, tm), lambda bi, i: (bi, 0, i))
    return pl.pallas_call(
        functools.partial(_outproj_kernel, final=final),
        grid=(b, s // tm),
        in_specs=[seq(D_MODEL), chan, chan, seq(D_GROUP), seq(D_GROUP), seq(4 * D_GROUP),
                  _full_spec(w), _full_spec(final_g)],
        out_specs=seq(D_MODEL),
        out_shape=jax.ShapeDtypeStruct((b, s, D_MODEL), F32),
        compiler_params=_params(2),
        name="outproj",
    )(x, y_fox_t, y_sb_t, y_rw, y_lru, gates, w, final_g)


def _w_in_segments():
    g, h, r, dh = D_GROUP, N_HEADS, RWKV_LORA, HEAD_DIM
    o_ff = 4 * g
    o_sb = o_ff + h
    o_rw = o_sb + 4 * g
    o_rg = o_rw + 3 * g + 2 * r
    o_lx = o_rg + g
    o_lg = o_lx + g
    segs = []
    for src, dst in ((0, C_FQ), (g, C_FK), (o_sb, C_SQ), (o_sb + g, C_SK)):
        segs += [(src + hd * dh, dst + hd * HEAD_PAD, dh) for hd in range(h)]
    segs += [(2 * g, C_FV, g), (o_sb + 2 * g, C_SV, g),
             (3 * g, C_GATES, g), (o_sb + 3 * g, C_GATES + g, g), (o_rg, C_GATES + 2 * g, g),
             (o_lg, C_GATES + 3 * g, g),
             (o_rw, C_RKV, 3 * g), (o_lx, C_LX, g),
             (o_rw + 3 * g, C_MISC, 2 * r), (o_ff, C_MISC + FF_LANE, h)]
    return segs


def _w_in_kernel(w_ref, o_ref):
    o_ref[...] = jnp.zeros_like(o_ref)
    for src, dst, width in _w_in_segments():
        o_ref[0, :, dst:dst + width] = w_ref[0, :, src:src + width].astype(BF16)


def _permute_w_in(w_in, *, tr):
    depth, d, n_in = w_in.shape
    return pl.pallas_call(
        _w_in_kernel,
        grid=(depth, d // tr),
        in_specs=[pl.BlockSpec((1, tr, n_in), lambda l, i: (l, i, 0))],
        out_specs=pl.BlockSpec((1, tr, N_PROJ), lambda l, i: (l, i, 0)),
        out_shape=jax.ShapeDtypeStruct((depth, d, N_PROJ), BF16),
        compiler_params=_params(2),
        name="w_in_layout",
    )(w_in)


def _forget_placement():
    selq = np.zeros((3 * LANES, QK_COLS), np.float32)
    selk = np.zeros((3 * LANES, QK_COLS), np.float32)
    augq = np.zeros((1, QK_COLS), np.float32)
    augk = np.zeros((1, QK_COLS), np.float32)
    for hd in range(N_HEADS):
        for piece in range(3):
            src = piece * LANES + FF_LANE + hd
            selq[src, hd * HEAD_PAD + AUG_Q_F + piece] = 1.0
            selk[src, hd * HEAD_PAD + AUG_K_F + piece] = -1.0
            augq[0, hd * HEAD_PAD + AUG_Q_ONE + piece] = 1.0
            augk[0, hd * HEAD_PAD + AUG_K_ONE + piece] = 1.0
    return (jnp.asarray(selq, BF16), jnp.asarray(selk, BF16), jnp.asarray(augq), jnp.asarray(augk))


def _block_diag(w):
    h, n, _ = w.shape
    eye = jnp.eye(h, dtype=w.dtype)
    return jnp.einsum('hij,hk->hikj', w, eye).reshape(h * n, h * n)


def _pick_tile(s, pref):
    t = pref
    while s % t:
        t //= 2
    return t


def kernel(x, norm_g, w_in, b_forget, rwkv_mu, rwkv_w0, rwkv_w2, rwkv_a0, rwkv_a2, rwkv_k_k,
           rwkv_k_a, rwkv_r_k, rwkv_ln_g, rwkv_ln_b, lru_conv_w, lru_conv_b, lru_w_a, lru_b_a,
           lru_w_x, lru_b_x, lru_lambda, w_out, final_g):
    b, s, d = x.shape
    depth = w_in.shape[0]
    g, h, dh, r = D_GROUP, N_HEADS, HEAD_DIM, RWKV_LORA
    tm = _pick_tile(s, 512)
    tq = _pick_tile(s, 256)
    tk_fox = _pick_tile(s, 2 * tq)
    tt = _pick_tile(s, 512)
    row = lambda a: a.reshape(1, -1).astype(F32)

    bdm = _block_diag(jnp.ones((h, dh, dh), BF16))
    selq, selk, augq, augk = _forget_placement()
    w_in_k = _permute_w_in(w_in, tr=LANES)
    for l in range(depth):
        fbias = jnp.zeros((1, LANES), F32).at[0, FF_LANE:FF_LANE + h].set(b_forget[l])
        consts = [row(norm_g[l]), w_in_k[l], fbias, selq, selk, augq, augk]
        fq, fk, fvt, sq, sk, svt, gates, rkv, lx, misc = _inproj(x, consts, tm=tm)

        y_fox_t = _attention_call(functools.partial(_fox_kernel, tq=tq, tk=tk_fox),
                                  "fox_attention", fq, fk, fvt, tq=tq,
                                  scratch=_fox_scratch(tq, tk_fox))
        y_sb_t = _sb_attention(sq, sk, svt, tq=tq)

        mu = rwkv_mu[l]
        pad_rows = lambda a, lo: jnp.zeros((LANES, g), F32).at[lo:lo + r].set(a)
        rw_prm = [row(mu[:N_RKV]),
                  jnp.zeros((1, LANES), F32).at[0, :2 * r].set(mu[N_RKV:]),
                  row(rwkv_w0[l]), pad_rows(rwkv_w2[l], 0), row(rwkv_a0[l]), pad_rows(rwkv_a2[l], r),
                  row(rwkv_k_k[l]), row(rwkv_k_a[l]), row(rwkv_r_k[l]), row(rwkv_ln_g[l]),
                  row(rwkv_ln_b[l])]
        y_rw = _rwkv(rkv, misc, rw_prm, bdm, tt=tt)

        lru_prm = [lru_conv_w[l].astype(F32), row(lru_conv_b[l]),
                   _block_diag(lru_w_a[l]).astype(BF16), row(lru_b_a[l]),
                   _block_diag(lru_w_x[l]).astype(BF16), row(lru_b_x[l]), row(lru_lambda[l])]
        y_lru = _lru(lx, lru_prm, tt=tt)

        x = _outproj(x, y_fox_t, y_sb_t, y_rw, y_lru, gates, w_out[l].astype(BF16), row(final_g),
                     tm=tm, final=(l == depth - 1))
    return x
```

```python
import functools

import numpy as np
import jax
import jax.numpy as jnp
from jax import lax
from jax.experimental import pallas as pl
from jax.experimental.pallas import tpu as pltpu

F32 = jnp.float32
BF16 = jnp.bfloat16

D_MODEL = 1024
D_GROUP = 256
N_HEADS = 4
HEAD_DIM = 64
RWKV_LORA = 32
CONV_WIDTH = 4
LRU_C = 8.0
RMS_EPS = 1e-6
GN_EPS = 64e-5
N_RKV = 3 * D_GROUP

LANES = 128
SUBLANES = 8
VMEM_LIMIT_BYTES = 56 * 1024 * 1024

HEAD_PAD = LANES
QK_COLS = N_HEADS * HEAD_PAD
C_FQ = 0
C_FK = C_FQ + QK_COLS
C_SQ = C_FK + QK_COLS
C_SK = C_SQ + QK_COLS
C_FV = C_SK + QK_COLS
C_SV = C_FV + D_GROUP
C_GATES = C_SV + D_GROUP
C_RKV = C_GATES + 4 * D_GROUP
C_LX = C_RKV + N_RKV
C_MISC = C_LX + D_GROUP
N_PROJ = C_MISC + LANES
FF_LANE = 2 * RWKV_LORA
AUG_K_F = HEAD_DIM
AUG_K_ONE = HEAD_DIM + 3
AUG_Q_ONE = HEAD_DIM
AUG_Q_F = HEAD_DIM + 3

V_AUG = HEAD_DIM + 16
LOG2E = 1.4426950408889634

FOX_SKIP_LOG2 = -150.0
FOX_NORM_MARGIN = 1.02
FOX_F_SLACK = 1.0

NEG_BIG = -1e30
SB_SKIP_LOG = -106.0
RWKV_CHUNK = 64
RWKV_UNROLL = 2


def _dot(a, b):
    return jnp.dot(a, b, preferred_element_type=F32)


def _dot_nt(a, b):
    return lax.dot_general(a, b, (((1,), (1,)), ((), ())), preferred_element_type=F32)


def _split2(x):
    hi = x.astype(BF16)
    lo = (x - hi.astype(F32)).astype(BF16)
    return hi, lo


def _split3(x):
    hi = x.astype(BF16)
    r1 = x - hi.astype(F32)
    mid = r1.astype(BF16)
    lo = (r1 - mid.astype(F32)).astype(BF16)
    return hi, mid, lo


def _dot3(a, b):
    ah, al = _split2(a)
    bh, bl = _split2(b)
    return _dot(ah, bh) + (_dot(al, bh) + _dot(ah, bl))


def _dot_sel_lhs(sel, x):
    hi, mid, lo = _split3(x)
    return _dot(sel, hi) + (_dot(sel, mid) + _dot(sel, lo))


def _dot_sel_rhs(x, sel):
    hi, lo = _split2(x)
    return _dot(hi, sel) + _dot(lo, sel)


def _softplus(x):
    return jnp.maximum(x, 0.0) + jnp.log1p(jnp.exp(-jnp.abs(x)))


def _log_sigmoid(x):
    return jnp.minimum(x, 0.0) - jnp.log1p(jnp.exp(-jnp.abs(x)))


def _sigmoid(x):
    return 1.0 / (1.0 + jnp.exp(-x))


def _iota(shape, dim):
    return lax.broadcasted_iota(jnp.int32, shape, dim)


def _full_spec(a):
    return pl.BlockSpec(a.shape, lambda *_: (0,) * a.ndim)


def _params(n_grid):
    return pltpu.CompilerParams(dimension_semantics=("arbitrary",) * n_grid,
                                vmem_limit_bytes=VMEM_LIMIT_BYTES)


def _inproj_kernel(x_ref, g_ref, w_ref, fb_ref, selq_ref, selk_ref, augq_ref, augk_ref,
                   fq_ref, fk_ref, fvt_ref, sq_ref, sk_ref, svt_ref, gates_ref, rkv_ref, lx_ref,
                   misc_ref, f2_ref, norms_ref, ftot, v_stage):
    @pl.when(pl.program_id(1) == 0)
    def _():
        ftot[...] = jnp.zeros_like(ftot)

    x = x_ref[0]
    tm = x.shape[0]
    ms = jnp.mean(x * x, axis=-1, keepdims=True)
    h = (x * lax.rsqrt(ms + RMS_EPS) * g_ref[...]).astype(BF16)
    scale = HEAD_DIM ** -0.5

    def proj(c0, width):
        return _dot(h, w_ref[:, c0:c0 + width])

    misc = proj(C_MISC, LANES)
    misc_ref[0] = misc
    gates_ref[0] = proj(C_GATES, 4 * D_GROUP)
    rkv_ref[0] = proj(C_RKV, N_RKV)
    lx_ref[0] = proj(C_LX, D_GROUP)

    lf = _log_sigmoid(misc + fb_ref[...])
    lower = (_iota((tm, tm), 1) <= _iota((tm, tm), 0)).astype(BF16)
    f = _dot_sel_lhs(lower, lf) + ftot[...]
    ftot[...] = f[tm - 1:tm, :]
    f2 = f * LOG2E
    f2_ref[0] = f2
    pieces = jnp.concatenate(_split3(f2), axis=1)

    pq = proj(C_FQ, QK_COLS) * (scale * LOG2E)
    pk = proj(C_FK, QK_COLS)
    lane = _iota((SUBLANES, LANES), 1)
    norms = jnp.zeros((SUBLANES, LANES), F32)
    for side, pv in enumerate((pq, pk)):
        for hd in range(N_HEADS):
            blk = pv[:, hd * HEAD_PAD:(hd + 1) * HEAD_PAD]
            top = jnp.max(jnp.sum(blk * blk, axis=-1, keepdims=True), axis=0, keepdims=True)
            norms = jnp.where(lane == side * N_HEADS + hd, top, norms)
    norms_ref[0, 0] = norms
    fq = pq + _dot(pieces, selq_ref[...]) + augq_ref[...]
    fk = pk + _dot(pieces, selk_ref[...]) + augk_ref[...]
    sq = proj(C_SQ, QK_COLS) * scale
    sk = proj(C_SK, QK_COLS)
    for hd in range(N_HEADS):
        cols = slice(hd * HEAD_PAD, (hd + 1) * HEAD_PAD)
        fq_ref[0, hd] = fq[:, cols].astype(BF16)
        fk_ref[0, hd] = fk[:, cols].astype(BF16)
        sq_ref[0, hd] = sq[:, cols].astype(BF16)
        sk_ref[0, hd] = sk[:, cols].astype(BF16)
    v_stage[...] = proj(C_SV, D_GROUP)
    svt_ref[0] = v_stage[...].T.astype(BF16)
    v_stage[...] = proj(C_FV, D_GROUP)
    vt = v_stage[...].T.astype(BF16)
    ones_rows = (_iota((V_AUG - HEAD_DIM, tm), 0) == 0).astype(BF16)
    for hd in range(N_HEADS):
        fvt_ref[0, hd * V_AUG:hd * V_AUG + HEAD_DIM, :] = vt[hd * HEAD_DIM:(hd + 1) * HEAD_DIM, :]
        fvt_ref[0, hd * V_AUG + HEAD_DIM:(hd + 1) * V_AUG, :] = ones_rows


def _inproj(x, consts, *, tm):
    b, s, _ = x.shape
    seq = lambda width: pl.BlockSpec((1, tm, width), lambda bi, i: (bi, i, 0))
    heads = pl.BlockSpec((1, N_HEADS, tm, HEAD_PAD), lambda bi, i: (bi, 0, i, 0))
    chan = pl.BlockSpec((1, D_GROUP, tm), lambda bi, i: (bi, 0, i))
    heads_shape = jax.ShapeDtypeStruct((b, N_HEADS, s, HEAD_PAD), BF16)
    chan_shape = jax.ShapeDtypeStruct((b, D_GROUP, s), BF16)
    chan_aug = pl.BlockSpec((1, N_HEADS * V_AUG, tm), lambda bi, i: (bi, 0, i))
    chan_aug_shape = jax.ShapeDtypeStruct((b, N_HEADS * V_AUG, s), BF16)
    seq_shape = lambda width: jax.ShapeDtypeStruct((b, s, width), F32)
    return pl.pallas_call(
        _inproj_kernel,
        grid=(b, s // tm),
        in_specs=[seq(D_MODEL)] + [_full_spec(a) for a in consts],
        out_specs=[heads, heads, chan_aug, heads, heads, chan,
                   seq(4 * D_GROUP), seq(N_RKV), seq(D_GROUP), seq(LANES), seq(LANES),
                   pl.BlockSpec((1, 1, SUBLANES, LANES), lambda bi, i: (bi, i, 0, 0))],
        out_shape=[heads_shape, heads_shape, chan_aug_shape, heads_shape, heads_shape, chan_shape,
                   seq_shape(4 * D_GROUP), seq_shape(N_RKV), seq_shape(D_GROUP), seq_shape(LANES),
                   seq_shape(LANES), jax.ShapeDtypeStruct((b, s // tm, SUBLANES, LANES), F32)],
        scratch_shapes=[pltpu.VMEM((1, LANES), F32), pltpu.VMEM((tm, D_GROUP), F32)],
        compiler_params=_params(2),
        name="inproj",
    )(x, *consts)


def _fox_kernel(fend_ref, qmax_ref, kmax_ref, q_ref, k_ref, vt_ref, o_ref,
                s_scr, p_scr, mx_scr, al_scr, m_scr, acc_scr, *, tq, tk):
    i = pl.program_id(2)
    nq = pl.num_programs(2)
    bh = pl.program_id(0) * N_HEADS + pl.program_id(1)
    q = q_ref[0, 0]
    n_full = (i * tq) // tk

    qk_bound = FOX_NORM_MARGIN * 2.0 * qmax_ref[bh * nq + i] * kmax_ref[bh]
    f_tile = fend_ref[bh * nq + jnp.maximum(i - 1, 0)]

    def first_live(jj, first):
        j = n_full - 1 - jj
        f_end = fend_ref[bh * nq + (j + 1) * (tk // tq) - 1]
        live = qk_bound + (f_tile - f_end) + FOX_F_SLACK >= FOX_SKIP_LOG2
        return jnp.where(live, j, first)

    base = lax.fori_loop(0, n_full, first_live, n_full)
    n_eff = n_full - base

    def scores_to(j, slot):
        s = _dot_nt(k_ref[0, 0, pl.ds(pl.multiple_of(j * tk, tk), tk), :], q)
        s_scr[slot] = s
        mx_scr[slot] = jnp.max(s, axis=0, keepdims=True)

    def weighted_values(j, slot):
        return _dot(vt_ref[0, :, pl.ds(pl.multiple_of(j * tk, tk), tk)], p_scr[slot])

    def softmax_to(s, mx, slot):
        m_old = m_scr[...]
        m_new = jnp.maximum(m_old, mx)
        m_scr[...] = m_new
        al_scr[slot] = jnp.exp2(m_old - m_new)
        p_scr[slot] = jnp.exp2(s - m_new).astype(BF16)

    def stage(local, cur):
        j = base + local
        nxt = 1 - cur
        pv_prev = weighted_values(jnp.maximum(j - 1, 0), nxt)
        scores_to(j + 1, nxt)
        softmax_to(s_scr[cur], mx_scr[cur], cur)
        acc_scr[...] = al_scr[nxt] * acc_scr[...] + pv_prev

    def tail(cur):
        nxt = 1 - cur
        pv_prev = weighted_values(jnp.maximum(n_full - 1, 0), nxt)
        kpos = n_full * tk + _iota((tk, tq), 0)
        s = jnp.where(kpos <= i * tq + _iota((tk, tq), 1), s_scr[cur], NEG_BIG)
        softmax_to(s, jnp.max(s, axis=0, keepdims=True), cur)
        acc = al_scr[cur] * (al_scr[nxt] * acc_scr[...] + pv_prev) + weighted_values(n_full, cur)
        o_ref[0] = acc[0:HEAD_DIM] / acc[HEAD_DIM:HEAD_DIM + 1]

    m_scr[...] = jnp.full((1, tq), NEG_BIG, F32)
    acc_scr[...] = jnp.zeros((V_AUG, tq), F32)
    p_scr[1] = jnp.zeros((tk, tq), BF16)
    al_scr[1] = jnp.ones((1, tq), F32)
    scores_to(base, 0)

    def pair(jj, carry):
        stage(2 * jj, 0)
        stage(2 * jj + 1, 1)
        return carry

    lax.fori_loop(0, n_eff // 2, pair, 0)

    @pl.when(n_eff % 2 == 1)
    def _():
        stage(n_eff - 1, 0)
        tail(1)

    @pl.when(n_eff % 2 == 0)
    def _():
        tail(0)


def _fox_scratch(tq, tk):
    return [pltpu.VMEM((2, tk, tq), F32), pltpu.VMEM((2, tk, tq), BF16),
            pltpu.VMEM((2, 1, tq), F32), pltpu.VMEM((2, 1, tq), F32),
            pltpu.VMEM((1, tq), F32), pltpu.VMEM((V_AUG, tq), F32)]


def _fox_attention(fend, qmax, kmax, q, k, vt, *, tq, tk):
    b, h, s, _ = q.shape
    grid_spec = pltpu.PrefetchScalarGridSpec(
        num_scalar_prefetch=3,
        grid=(b, h, s // tq),
        in_specs=[pl.BlockSpec((1, 1, tq, HEAD_PAD), lambda bi, hi, i, *_: (bi, hi, i, 0)),
                  pl.BlockSpec((1, 1, s, HEAD_PAD), lambda bi, hi, i, *_: (bi, hi, 0, 0)),
                  pl.BlockSpec((1, V_AUG, s), lambda bi, hi, i, *_: (bi, hi, 0))],
        out_specs=pl.BlockSpec((1, HEAD_DIM, tq), lambda bi, hi, i, *_: (bi, hi, i)),
        scratch_shapes=_fox_scratch(tq, tk))
    return pl.pallas_call(
        functools.partial(_fox_kernel, tq=tq, tk=tk),
        grid_spec=grid_spec,
        out_shape=jax.ShapeDtypeStruct((b, h * HEAD_DIM, s), F32),
        compiler_params=_params(3),
        name="fox_attention",
    )(fend, qmax, kmax, q, k, vt)


def _sb_kernel(q_ref, k_ref, vt_ref, o_ref, *, tq):
    i = pl.program_id(1)
    tk = tq
    heads = list(range(N_HEADS))
    q = [q_ref[0, hd] for hd in heads]
    later = (_iota((tk, tk), 0) < _iota((tk, tk), 1)).astype(BF16)

    def each(f, *xs):
        return [f(*a) for a in zip(*xs)]

    def block(j, rest_q, acc, masked):
        ks = pl.multiple_of(j * tk, tk)
        z = [_dot_nt(k_ref[0, hd, pl.ds(ks, tk), :], q[hd]) for hd in heads]
        log_keep = each(lambda x: _log_sigmoid(-x), z)
        if masked:
            mask = _iota((tk, tq), 0) < _iota((tk, tq), 1)
            log_keep = each(lambda x: jnp.where(mask, x, 0.0), log_keep)
        split = each(_split2, log_keep)
        rest_in = each(lambda hl: _dot(later, hl[0]) + _dot(later, hl[1]), split)
        att = each(lambda x, lk, ri, rq: jnp.exp(x + lk + ri + rq), z, log_keep, rest_in, rest_q)
        if masked:
            att = each(lambda x: jnp.where(mask, x, 0.0), att)
        pv = [_dot(vt_ref[0, hd * HEAD_DIM:(hd + 1) * HEAD_DIM, pl.ds(ks, tk)],
                   att[hd].astype(BF16)) for hd in heads]
        acc = each(lambda a, x: a + x, acc, pv)
        rest_q = each(lambda rq, ri, lk: rq + ri[0:1, :] + lk[0:1, :], rest_q, rest_in, log_keep)
        return rest_q, acc

    rest_q, acc = block(i, [jnp.zeros((1, tq), F32)] * N_HEADS,
                        [jnp.zeros((HEAD_DIM, tq), F32)] * N_HEADS, True)

    def cond(c):
        j, rest_q, _ = c
        alive = functools.reduce(jnp.maximum, rest_q)
        return jnp.logical_and(j >= 0, jnp.max(alive) > SB_SKIP_LOG)

    def body(c):
        j, rest_q, acc = c
        rest_q, acc = block(j, list(rest_q), list(acc), False)
        return j - 1, tuple(rest_q), tuple(acc)

    _, _, acc = lax.while_loop(cond, body, (i - 1, tuple(rest_q), tuple(acc)))
    o_ref[0] = jnp.concatenate(list(acc), axis=0)


def _sb_attention(q, k, vt, *, tq):
    b, h, s, _ = q.shape
    return pl.pallas_call(
        functools.partial(_sb_kernel, tq=tq),
        grid=(b, s // tq),
        in_specs=[pl.BlockSpec((1, h, tq, HEAD_PAD), lambda bi, i: (bi, 0, i, 0)),
                  pl.BlockSpec((1, h, s, HEAD_PAD), lambda bi, i: (bi, 0, 0, 0)),
                  pl.BlockSpec((1, h * HEAD_DIM, s), lambda bi, i: (bi, 0, 0))],
        out_specs=pl.BlockSpec((1, h * HEAD_DIM, tq), lambda bi, i: (bi, 0, i)),
        out_shape=jax.ShapeDtypeStruct((b, h * HEAD_DIM, s), F32),
        compiler_params=_params(2),
        name="sb_attention",
    )(q, k, vt)


def _rwkv_kernel(p_ref, m_ref, mu_rkv_ref, mu_misc_ref, w0_ref, w2_ref, a0_ref, a2_ref,
                 kk_ref, ka_ref, rk_ref, lng_ref, lnb_ref, bdm_ref, o_ref,
                 pad_rkv, pad_misc, r_s, k_s, v_s, kn_s, al_s, lw_s,
                 wt_s, u0_s, o0_s, mrb_s, rt_s, bh_s, kh_s, pc_s, y_s, ht_s, *, tt, nb):
    c = RWKV_CHUNK
    g = D_GROUP
    n_chunks = tt // c

    @pl.when(pl.program_id(0) == 0)
    def _():
        pad_rkv[:, 0:SUBLANES, :] = jnp.zeros((nb, SUBLANES, N_RKV), F32)
        pad_misc[:, 0:SUBLANES, :] = jnp.zeros((nb, SUBLANES, LANES), F32)
        ht_s[...] = jnp.zeros_like(ht_s)

    bdm = bdm_ref[...]
    bdm_f = bdm.astype(F32)

    for b in range(nb):
        p = p_ref[b]
        misc = m_ref[b]
        pad_rkv[b, SUBLANES:SUBLANES + tt, :] = p
        pad_misc[b, SUBLANES:SUBLANES + tt, :] = misc
        p_prev = pad_rkv[b, SUBLANES - 1:SUBLANES - 1 + tt, :]
        m_prev = pad_misc[b, SUBLANES - 1:SUBLANES - 1 + tt, :]
        pad_rkv[b, 0:SUBLANES, :] = p[tt - SUBLANES:tt, :]
        pad_misc[b, 0:SUBLANES, :] = misc[tt - SUBLANES:tt, :]
        p = p + (p_prev - p) * mu_rkv_ref[...]
        misc = misc + (m_prev - misc) * mu_misc_ref[...]
        k = p[:, g:2 * g]

        w = -_softplus(-(w0_ref[...] + _dot3(jnp.tanh(misc), w2_ref[...]))) - 0.5
        alpha = _sigmoid(a0_ref[...] + _dot3(misc, a2_ref[...]))
        kn = k * kk_ref[...]
        ss = _dot_sel_rhs(kn * kn, bdm)
        r_s[b] = p[:, 0:g]
        k_s[b] = k * (1.0 + (alpha - 1.0) * ka_ref[...])
        v_s[b] = p[:, 2 * g:3 * g]
        kn_s[b] = kn * lax.rsqrt(jnp.maximum(ss, 1e-12))
        al_s[b] = alpha
        lw_s[b] = -jnp.exp(w)

    row = _iota((c, g), 0)
    col = _iota((c, g), 1) % c
    strict = col < row
    incl = col <= row
    eye = (col == row).astype(F32)
    lower_c = (_iota((c, c), 1) <= _iota((c, c), 0)).astype(BF16)
    level_masks = []
    m = 1
    while m < c:
        level_masks.append(jnp.logical_and(
            strict, jnp.logical_and(row // (2 * m) == col // (2 * m), row // m != col // m)))
        m *= 2

    def bd(x):
        return jnp.concatenate([x.astype(BF16)] * N_HEADS, axis=0) * bdm

    def mm(a, b_bf16):
        return _dot(a.astype(BF16), b_bf16)

    def mm_nt(a, b_bf16):
        return _dot_nt(a.astype(BF16), b_bf16)

    def each(f, *xs):
        return [f(*a) for a in zip(*xs)]

    def phase_a(chains):
        sls = [pl.ds(pl.multiple_of(ci * c, c), c) for _, ci in chains]
        ld = lambda ref: [ref[b, sl, :] for (b, _), sl in zip(chains, sls)]
        r_c, k_c, v_c, kn_c, al_c, lw_c = (ld(s) for s in (r_s, k_s, v_s, kn_s, al_s, lw_s))
        cl = each(lambda x: _dot_sel_lhs(lower_c, x), lw_c)
        cl_last = each(lambda x: x[c - 1:c, :], cl)
        a_t = each(lambda kn, x, lw: -kn * jnp.exp(x - lw), kn_c, cl, lw_c)
        r_t = each(lambda r, x: r * jnp.exp(x), r_c, cl)
        q_inv = each(lambda x: jnp.exp(-x), cl)
        p_rem = each(lambda xl, x: jnp.exp(xl - x), cl_last, cl)
        kna = each(lambda kn, al: kn * al, kn_c, al_c)
        ar = each(lambda a, r: jnp.concatenate([a, r], axis=0), a_t, r_t)
        s_b = each(lambda x, kb, qi: mm_nt(x, bd(kb * qi)), ar, kna, q_inv)
        s_k = each(lambda x, kk, qi: mm_nt(x, bd(kk * qi)), ar, k_c, q_inv)
        n = each(lambda x: jnp.where(strict, x[0:c], 0.0), s_b)
        a_ak = each(lambda x: jnp.where(strict, x[0:c], 0.0), s_k)
        m_rb = each(lambda x: jnp.where(incl, x[c:2 * c], 0.0), s_b)
        m_rk = each(lambda x: jnp.where(incl, x[c:2 * c], 0.0), s_k)

        inv = each(lambda x: eye + jnp.where(level_masks[0], x, 0.0), n)
        for lm in level_masks[1:]:
            half = each(lambda d, x: mm(d, bd(jnp.where(lm, x, 0.0))), inv, n)
            inv = each(lambda d, hf: d + mm(hf, bd(d)), inv, half)

        v_bd = each(bd, v_c)
        akv = each(mm, a_ak, v_bd)
        wt = each(lambda d, a: mm(d, bd(a)), inv, a_t)
        u0 = each(lambda d, x: mm(d, bd(x)), inv, akv)
        o0 = each(mm, m_rk, v_bd)
        bh = each(lambda x, p: x * p, kna, p_rem)
        kh = each(lambda x, p: x * p, k_c, p_rem)
        for ref, vals in zip((wt_s, u0_s, o0_s, mrb_s, rt_s, bh_s, kh_s),
                             (wt, u0, o0, m_rb, r_t, bh, kh)):
            for (b, _), sl, val in zip(chains, sls, vals):
                ref[b, sl, :] = val
        for (b, ci), xl in zip(chains, cl_last):
            pc_s[b, pl.ds(pl.multiple_of(ci * SUBLANES, SUBLANES), SUBLANES), :] = (
                jnp.broadcast_to(jnp.exp(xl), (SUBLANES, g)))

    def phase_b(ci):
        sl = pl.ds(pl.multiple_of(ci * c, c), c)
        bs = list(range(nb))
        ht = [ht_s[b] for b in bs]
        wr = [jnp.concatenate([wt_s[b, sl, :], rt_s[b, sl, :]], axis=0) for b in bs]
        wrh = each(lambda x, hh: mm_nt(x, hh.astype(BF16)), wr, ht)
        u = [x[0:c] + u0_s[b, sl, :] for x, b in zip(wrh, bs)]
        uv_t = [jnp.concatenate([x, v_s[b, sl, :]], axis=0).T for x, b in zip(u, bs)]
        bk = [jnp.concatenate([bh_s[b, sl, :], kh_s[b, sl, :]], axis=0) for b in bs]
        upd = each(lambda x, y: mm(x, y.astype(BF16)), uv_t, bk)
        mu_ = [mm(mrb_s[b, sl, :], bd(x)) for x, b in zip(u, bs)]
        for b in bs:
            p_c = pc_s[b, pl.ds(pl.multiple_of(ci * SUBLANES, SUBLANES), 1), :]
            y_s[b, sl, :] = wrh[b][c:2 * c] + mu_[b] + o0_s[b, sl, :]
            ht_s[b] = ht[b] * p_c + bdm_f * upd[b]

    def loop_a(ci, carry):
        phase_a([(b, ci * RWKV_UNROLL + j) for j in range(RWKV_UNROLL) for b in range(nb)])
        return carry

    def loop_b(ci, carry):
        phase_b(ci)
        return carry

    lax.fori_loop(0, n_chunks // RWKV_UNROLL, loop_a, 0)
    lax.fori_loop(0, n_chunks, loop_b, 0)

    inv_n = 1.0 / HEAD_DIM
    for b in range(nb):
        y = y_s[b]
        mean = _dot_sel_rhs(y, bdm) * inv_n
        yc = y - mean
        var = _dot_sel_rhs(yc * yc, bdm) * inv_n
        yn = yc * lax.rsqrt(var + GN_EPS) * lng_ref[...] + lnb_ref[...]
        bonus = _dot_sel_rhs(r_s[b] * k_s[b] * rk_ref[...], bdm) * v_s[b]
        o_ref[b] = yn + bonus


def _rwkv(p_rkv, misc, prm, bdm, *, tt):
    b, s, _ = p_rkv.shape
    g = D_GROUP
    seq = lambda width: pl.BlockSpec((b, tt, width), lambda ti: (0, ti, 0))
    big = pltpu.VMEM((b, tt, g), F32)
    return pl.pallas_call(
        functools.partial(_rwkv_kernel, tt=tt, nb=b),
        grid=(s // tt,),
        in_specs=[seq(N_RKV), seq(LANES)] + [_full_spec(a) for a in prm] + [_full_spec(bdm)],
        out_specs=seq(g),
        out_shape=jax.ShapeDtypeStruct((b, s, g), F32),
        scratch_shapes=[pltpu.VMEM((b, tt + SUBLANES, N_RKV), F32),
                        pltpu.VMEM((b, tt + SUBLANES, LANES), F32)]
        + [big] * 13
        + [pltpu.VMEM((b, tt // RWKV_CHUNK * SUBLANES, g), F32), big,
           pltpu.VMEM((b, g, g), F32)],
        compiler_params=_params(1),
        name="rwkv7",
    )(p_rkv, misc, *prm, bdm)


def _lru_kernel(x_ref, cw_ref, cb_ref, wa_ref, ba_ref, wx_ref, bx_ref, lam_ref, o_ref,
                pad, a_s, u_s, h_s, *, tt):
    t_idx = pl.program_id(1)

    @pl.when(t_idx == 0)
    def _():
        pad[0:SUBLANES, :] = jnp.zeros((SUBLANES, D_GROUP), F32)
        h_s[...] = jnp.zeros((1, D_GROUP), F32)

    x = x_ref[0]
    pad[SUBLANES:SUBLANES + tt, :] = x
    xc = cw_ref[CONV_WIDTH - 1:CONV_WIDTH, :] * x + cb_ref[...]
    for d in range(1, CONV_WIDTH):
        tap = CONV_WIDTH - 1 - d
        xc = xc + cw_ref[tap:tap + 1, :] * pad[SUBLANES - d:SUBLANES - d + tt, :]
    pad[0:SUBLANES, :] = x[tt - SUBLANES:tt, :]

    xb = xc.astype(BF16)
    r = _sigmoid(_dot(xb, wa_ref[...]) + ba_ref[...])
    i = _sigmoid(_dot(xb, wx_ref[...]) + bx_ref[...])
    log_a = -LRU_C * r * _softplus(-lam_ref[...])
    a_s[...] = jnp.exp(log_a)
    th = jnp.tanh(log_a)
    u_s[...] = jnp.sqrt(-2.0 * th / (1.0 - th)) * (i * xc)

    def group(gi, h):
        base = pl.multiple_of(gi * SUBLANES, SUBLANES)
        a8 = a_s[pl.ds(base, SUBLANES), :]
        u8 = u_s[pl.ds(base, SUBLANES), :]
        rows = []
        for j in range(SUBLANES):
            h = a8[j:j + 1, :] * h + u8[j:j + 1, :]
            rows.append(h)
        o_ref[0, pl.ds(base, SUBLANES), :] = jnp.concatenate(rows, axis=0)
        return h

    h_s[...] = lax.fori_loop(0, tt // SUBLANES, group, h_s[...])


def _lru(x, prm, *, tt):
    b, s, g = x.shape
    seq = pl.BlockSpec((1, tt, g), lambda bi, ti: (bi, ti, 0))
    return pl.pallas_call(
        functools.partial(_lru_kernel, tt=tt),
        grid=(b, s // tt),
        in_specs=[seq] + [_full_spec(a) for a in prm],
        out_specs=seq,
        out_shape=jax.ShapeDtypeStruct((b, s, g), F32),
        scratch_shapes=[pltpu.VMEM((tt + SUBLANES, g), F32), pltpu.VMEM((tt, g), F32),
                        pltpu.VMEM((tt, g), F32), pltpu.VMEM((1, g), F32)],
        compiler_params=_params(2),
        name="rg_lru",
    )(x, *prm)


def _outproj_kernel(x_ref, yft_ref, yst_ref, yr_ref, yl_ref, gates_ref, w_ref, fg_ref, o_ref, *,
                    final):
    acc = x_ref[0]
    ys = (yft_ref[0].T, yst_ref[0].T, yr_ref[0], yl_ref[0])
    for gi, y in enumerate(ys):
        gate = gates_ref[0, :, gi * D_GROUP:(gi + 1) * D_GROUP]
        y = y * (gate * _sigmoid(gate))
        acc = acc + _dot(y.astype(BF16), w_ref[gi * D_GROUP:(gi + 1) * D_GROUP, :])
    if final:
        ms = jnp.mean(acc * acc, axis=-1, keepdims=True)
        acc = acc * lax.rsqrt(ms + RMS_EPS) * fg_ref[...]
    o_ref[0] = acc


def _outproj(x, y_fox_t, y_sb_t, y_rw, y_lru, gates, w, final_g, *, tm, final):
    b, s, _ = x.shape
    seq = lambda width: pl.BlockSpec((1, tm, width), lambda bi, i: (bi, i, 0))
    chan = pl.BlockSpec((1, D_GROUP, tm), lambda bi, i: (bi, 0, i))
    return pl.pallas_call(
        functools.partial(_outproj_kernel, final=final),
        grid=(b, s // tm),
        in_specs=[seq(D_MODEL), chan, chan, seq(D_GROUP), seq(D_GROUP), seq(4 * D_GROUP),
                  _full_spec(w), _full_spec(final_g)],
        out_specs=seq(D_MODEL),
        out_shape=jax.ShapeDtypeStruct((b, s, D_MODEL), F32),
        compiler_params=_params(2),
        name="outproj",
    )(x, y_fox_t, y_sb_t, y_rw, y_lru, gates, w, final_g)


def _w_in_segments():
    g, h, r, dh = D_GROUP, N_HEADS, RWKV_LORA, HEAD_DIM
    o_ff = 4 * g
    o_sb = o_ff + h
    o_rw = o_sb + 4 * g
    o_rg = o_rw + 3 * g + 2 * r
    o_lx = o_rg + g
    o_lg = o_lx + g
    segs = []
    for src, dst in ((0, C_FQ), (g, C_FK), (o_sb, C_SQ), (o_sb + g, C_SK)):
        segs += [(src + hd * dh, dst + hd * HEAD_PAD, dh) for hd in range(h)]
    segs += [(2 * g, C_FV, g), (o_sb + 2 * g, C_SV, g),
             (3 * g, C_GATES, g), (o_sb + 3 * g, C_GATES + g, g), (o_rg, C_GATES + 2 * g, g),
             (o_lg, C_GATES + 3 * g, g),
             (o_rw, C_RKV, 3 * g), (o_lx, C_LX, g),
             (o_rw + 3 * g, C_MISC, 2 * r), (o_ff, C_MISC + FF_LANE, h)]
    return segs


def _w_in_kernel(w_ref, o_ref):
    o_ref[...] = jnp.zeros_like(o_ref)
    for src, dst, width in _w_in_segments():
        o_ref[0, :, dst:dst + width] = w_ref[0, :, src:src + width].astype(BF16)


def _permute_w_in(w_in, *, tr):
    depth, d, n_in = w_in.shape
    return pl.pallas_call(
        _w_in_kernel,
        grid=(depth, d // tr),
        in_specs=[pl.BlockSpec((1, tr, n_in), lambda l, i: (l, i, 0))],
        out_specs=pl.BlockSpec((1, tr, N_PROJ), lambda l, i: (l, i, 0)),
        out_shape=jax.ShapeDtypeStruct((depth, d, N_PROJ), BF16),
        compiler_params=_params(2),
        name="w_in_layout",
    )(w_in)


def _forget_placement():
    selq = np.zeros((3 * LANES, QK_COLS), np.float32)
    selk = np.zeros((3 * LANES, QK_COLS), np.float32)
    augq = np.zeros((1, QK_COLS), np.float32)
    augk = np.zeros((1, QK_COLS), np.float32)
    for hd in range(N_HEADS):
        for piece in range(3):
            src = piece * LANES + FF_LANE + hd
            selq[src, hd * HEAD_PAD + AUG_Q_F + piece] = 1.0
            selk[src, hd * HEAD_PAD + AUG_K_F + piece] = -1.0
            augq[0, hd * HEAD_PAD + AUG_Q_ONE + piece] = 1.0
            augk[0, hd * HEAD_PAD + AUG_K_ONE + piece] = 1.0
    return (jnp.asarray(selq, BF16), jnp.asarray(selk, BF16), jnp.asarray(augq), jnp.asarray(augk))


def _block_diag(w):
    h, n, _ = w.shape
    eye = jnp.eye(h, dtype=w.dtype)
    return jnp.einsum('hij,hk->hikj', w, eye).reshape(h * n, h * n)


def _pick_tile(s, pref):
    t = pref
    while s % t:
        t //= 2
    return t


def kernel(x, norm_g, w_in, b_forget, rwkv_mu, rwkv_w0, rwkv_w2, rwkv_a0, rwkv_a2, rwkv_k_k,
           rwkv_k_a, rwkv_r_k, rwkv_ln_g, rwkv_ln_b, lru_conv_w, lru_conv_b, lru_w_a, lru_b_a,
           lru_w_x, lru_b_x, lru_lambda, w_out, final_g):
    b, s, d = x.shape
    depth = w_in.shape[0]
    g, h, dh, r = D_GROUP, N_HEADS, HEAD_DIM, RWKV_LORA
    tm = _pick_tile(s, 512)
    tq = _pick_tile(s, 256)
    tk_fox = _pick_tile(s, 2 * tq)
    tt = _pick_tile(s, 512)
    row = lambda a: a.reshape(1, -1).astype(F32)

    bdm = _block_diag(jnp.ones((h, dh, dh), BF16))
    selq, selk, augq, augk = _forget_placement()
    w_in_k = _permute_w_in(w_in, tr=LANES)
    for l in range(depth):
        fbias = jnp.zeros((1, LANES), F32).at[0, FF_LANE:FF_LANE + h].set(b_forget[l])
        consts = [row(norm_g[l]), w_in_k[l], fbias, selq, selk, augq, augk]
        fq, fk, fvt, sq, sk, svt, gates, rkv, lx, misc, f2, norms = _inproj(x, consts, tm=tm)

        per_head = lambda a: a.transpose(0, 2, 1).reshape(-1)
        fend = per_head(f2[:, tq - 1::tq, FF_LANE:FF_LANE + h])
        qmax = per_head(jnp.repeat(jnp.sqrt(norms[:, :, 0, 0:h]), tm // tq, axis=1))
        kmax = jnp.sqrt(jnp.max(norms[:, :, 0, h:2 * h], axis=1)).reshape(-1)
        y_fox_t = _fox_attention(fend, qmax, kmax, fq, fk, fvt, tq=tq, tk=tk_fox)
        y_sb_t = _sb_attention(sq, sk, svt, tq=tq)

        mu = rwkv_mu[l]
        pad_rows = lambda a, lo: jnp.zeros((LANES, g), F32).at[lo:lo + r].set(a)
        rw_prm = [row(mu[:N_RKV]),
                  jnp.zeros((1, LANES), F32).at[0, :2 * r].set(mu[N_RKV:]),
                  row(rwkv_w0[l]), pad_rows(rwkv_w2[l], 0), row(rwkv_a0[l]), pad_rows(rwkv_a2[l], r),
                  row(rwkv_k_k[l]), row(rwkv_k_a[l]), row(rwkv_r_k[l]), row(rwkv_ln_g[l]),
                  row(rwkv_ln_b[l])]
        y_rw = _rwkv(rkv, misc, rw_prm, bdm, tt=tt)

        lru_prm = [lru_conv_w[l].astype(F32), row(lru_conv_b[l]),
                   _block_diag(lru_w_a[l]).astype(BF16), row(lru_b_a[l]),
                   _block_diag(lru_w_x[l]).astype(BF16), row(lru_b_x[l]), row(lru_lambda[l])]
        y_lru = _lru(lx, lru_prm, tt=tt)

        x = _outproj(x, y_fox_t, y_sb_t, y_rw, y_lru, gates, w_out[l].astype(BF16), row(final_g),
                     tm=tm, final=(l == depth - 1))
    return x
```

```python
import functools

import numpy as np
import jax
import jax.numpy as jnp
from jax import lax
from jax.experimental import pallas as pl
from jax.experimental.pallas import tpu as pltpu

F32 = jnp.float32
BF16 = jnp.bfloat16

D_MODEL = 1024
D_GROUP = 256
N_HEADS = 4
HEAD_DIM = 64
RWKV_LORA = 32
CONV_WIDTH = 4
LRU_C = 8.0
RMS_EPS = 1e-6
GN_EPS = 64e-5
N_RKV = 3 * D_GROUP

LANES = 128
SUBLANES = 8
VMEM_LIMIT_BYTES = 56 * 1024 * 1024

HEAD_PAD = LANES
QK_COLS = N_HEADS * HEAD_PAD
C_FQ = 0
C_FK = C_FQ + QK_COLS
C_SQ = C_FK + QK_COLS
C_SK = C_SQ + QK_COLS
C_FV = C_SK + QK_COLS
C_SV = C_FV + D_GROUP
C_GATES = C_SV + D_GROUP
C_RKV = C_GATES + 4 * D_GROUP
C_LX = C_RKV + N_RKV
C_MISC = C_LX + D_GROUP
N_PROJ = C_MISC + LANES
FF_LANE = 2 * RWKV_LORA
AUG_K_F = HEAD_DIM
AUG_K_ONE = HEAD_DIM + 3
AUG_Q_ONE = HEAD_DIM
AUG_Q_F = HEAD_DIM + 3

CUMSUM_BLOCK = LANES
V_AUG = HEAD_DIM + 16
LOG2E = 1.4426950408889634

FOX_SKIP_LOG2 = -150.0
FOX_NORM_MARGIN = 1.02
FOX_F_SLACK = 1.0

NEG_BIG = -1e30
SB_SKIP_LOG = -106.0
RWKV_CHUNK = 64
RWKV_UNROLL = 2


def _dot(a, b):
    return jnp.dot(a, b, preferred_element_type=F32)


def _dot_nt(a, b):
    return lax.dot_general(a, b, (((1,), (1,)), ((), ())), preferred_element_type=F32)


def _split2(x):
    hi = x.astype(BF16)
    lo = (x - hi.astype(F32)).astype(BF16)
    return hi, lo


def _split3(x):
    hi = x.astype(BF16)
    r1 = x - hi.astype(F32)
    mid = r1.astype(BF16)
    lo = (r1 - mid.astype(F32)).astype(BF16)
    return hi, mid, lo


def _dot3(a, b):
    ah, al = _split2(a)
    bh, bl = _split2(b)
    return _dot(ah, bh) + (_dot(al, bh) + _dot(ah, bl))


def _dot_sel_lhs(sel, x):
    hi, mid, lo = _split3(x)
    return _dot(sel, hi) + (_dot(sel, mid) + _dot(sel, lo))


def _dot_sel_rhs(x, sel):
    hi, lo = _split2(x)
    return _dot(hi, sel) + _dot(lo, sel)


def _softplus(x):
    return jnp.maximum(x, 0.0) + jnp.log1p(jnp.exp(-jnp.abs(x)))


def _log_sigmoid(x):
    return jnp.minimum(x, 0.0) - jnp.log1p(jnp.exp(-jnp.abs(x)))


def _sigmoid(x):
    return 1.0 / (1.0 + jnp.exp(-x))


def _iota(shape, dim):
    return lax.broadcasted_iota(jnp.int32, shape, dim)


def _full_spec(a):
    return pl.BlockSpec(a.shape, lambda *_: (0,) * a.ndim)


def _params(n_grid):
    return pltpu.CompilerParams(dimension_semantics=("arbitrary",) * n_grid,
                                vmem_limit_bytes=VMEM_LIMIT_BYTES)


def _inproj_kernel(x_ref, g_ref, w_ref, fb_ref, selq_ref, selk_ref, augq_ref, augk_ref,
                   fq_ref, fk_ref, fvt_ref, sq_ref, sk_ref, svt_ref, gates_ref, rkv_ref, lx_ref,
                   misc_ref, f2_ref, norms_ref, ftot, v_stage):
    @pl.when(pl.program_id(1) == 0)
    def _():
        ftot[...] = jnp.zeros_like(ftot)

    x = x_ref[0]
    tm = x.shape[0]
    ms = jnp.mean(x * x, axis=-1, keepdims=True)
    h = (x * lax.rsqrt(ms + RMS_EPS) * g_ref[...]).astype(BF16)
    scale = HEAD_DIM ** -0.5

    def proj(c0, width):
        return _dot(h, w_ref[0, :, c0:c0 + width])

    misc = proj(C_MISC, LANES)
    misc_ref[0] = misc
    gates_ref[0] = proj(C_GATES, 4 * D_GROUP)
    rkv_ref[0] = proj(C_RKV, N_RKV)
    lx_ref[0] = proj(C_LX, D_GROUP)

    lf = _log_sigmoid(misc + fb_ref[...])
    cb = CUMSUM_BLOCK
    lower = (_iota((cb, cb), 1) <= _iota((cb, cb), 0)).astype(BF16)
    run = ftot[...]
    blocks = []
    for r0 in range(0, tm, cb):
        blk = _dot_sel_lhs(lower, lf[r0:r0 + cb, :]) + run
        run = blk[cb - 1:cb, :]
        blocks.append(blk)
    ftot[...] = run
    f = jnp.concatenate(blocks, axis=0)
    f2 = f * LOG2E
    f2_ref[0] = f2
    pieces = jnp.concatenate(_split3(f2), axis=1)

    pq = proj(C_FQ, QK_COLS) * (scale * LOG2E)
    pk = proj(C_FK, QK_COLS)
    lane = _iota((SUBLANES, LANES), 1)
    norms = jnp.zeros((SUBLANES, LANES), F32)
    for side, pv in enumerate((pq, pk)):
        for hd in range(N_HEADS):
            blk = pv[:, hd * HEAD_PAD:(hd + 1) * HEAD_PAD]
            top = jnp.max(jnp.sum(blk * blk, axis=-1, keepdims=True), axis=0, keepdims=True)
            norms = jnp.where(lane == side * N_HEADS + hd, top, norms)
    norms_ref[0, 0] = norms
    fq = pq + _dot(pieces, selq_ref[...]) + augq_ref[...]
    fk = pk + _dot(pieces, selk_ref[...]) + augk_ref[...]
    sq = proj(C_SQ, QK_COLS) * scale
    sk = proj(C_SK, QK_COLS)
    for hd in range(N_HEADS):
        cols = slice(hd * HEAD_PAD, (hd + 1) * HEAD_PAD)
        fq_ref[0, hd] = fq[:, cols].astype(BF16)
        fk_ref[0, hd] = fk[:, cols].astype(BF16)
        sq_ref[0, hd] = sq[:, cols].astype(BF16)
        sk_ref[0, hd] = sk[:, cols].astype(BF16)
    v_stage[...] = proj(C_SV, D_GROUP)
    svt_ref[0] = v_stage[...].T.astype(BF16)
    v_stage[...] = proj(C_FV, D_GROUP)
    vt = v_stage[...].T.astype(BF16)
    ones_rows = (_iota((V_AUG - HEAD_DIM, tm), 0) == 0).astype(BF16)
    for hd in range(N_HEADS):
        fvt_ref[0, hd * V_AUG:hd * V_AUG + HEAD_DIM, :] = vt[hd * HEAD_DIM:(hd + 1) * HEAD_DIM, :]
        fvt_ref[0, hd * V_AUG + HEAD_DIM:(hd + 1) * V_AUG, :] = ones_rows


def _inproj(x, consts, layer, *, tm):
    b, s, _ = x.shape
    w_spec = pl.BlockSpec((1,) + consts[1].shape[1:], lambda bi, i: (layer, 0, 0))
    seq = lambda width: pl.BlockSpec((1, tm, width), lambda bi, i: (bi, i, 0))
    heads = pl.BlockSpec((1, N_HEADS, tm, HEAD_PAD), lambda bi, i: (bi, 0, i, 0))
    chan = pl.BlockSpec((1, D_GROUP, tm), lambda bi, i: (bi, 0, i))
    heads_shape = jax.ShapeDtypeStruct((b, N_HEADS, s, HEAD_PAD), BF16)
    chan_shape = jax.ShapeDtypeStruct((b, D_GROUP, s), BF16)
    chan_aug = pl.BlockSpec((1, N_HEADS * V_AUG, tm), lambda bi, i: (bi, 0, i))
    chan_aug_shape = jax.ShapeDtypeStruct((b, N_HEADS * V_AUG, s), BF16)
    seq_shape = lambda width: jax.ShapeDtypeStruct((b, s, width), F32)
    return pl.pallas_call(
        _inproj_kernel,
        grid=(b, s // tm),
        in_specs=[seq(D_MODEL), _full_spec(consts[0]), w_spec] + [_full_spec(a) for a in consts[2:]],
        out_specs=[heads, heads, chan_aug, heads, heads, chan,
                   seq(4 * D_GROUP), seq(N_RKV), seq(D_GROUP), seq(LANES), seq(LANES),
                   pl.BlockSpec((1, 1, SUBLANES, LANES), lambda bi, i: (bi, i, 0, 0))],
        out_shape=[heads_shape, heads_shape, chan_aug_shape, heads_shape, heads_shape, chan_shape,
                   seq_shape(4 * D_GROUP), seq_shape(N_RKV), seq_shape(D_GROUP), seq_shape(LANES),
                   seq_shape(LANES), jax.ShapeDtypeStruct((b, s // tm, SUBLANES, LANES), F32)],
        scratch_shapes=[pltpu.VMEM((1, LANES), F32), pltpu.VMEM((tm, D_GROUP), F32)],
        compiler_params=_params(2),
        name="inproj",
    )(x, *consts)


def _fox_kernel(fend_ref, qmax_ref, kmax_ref, q_ref, k_ref, vt_ref, o_ref,
                s_scr, p_scr, mx_scr, al_scr, m_scr, acc_scr, *, tq, tk):
    i = pl.program_id(1)
    nq = pl.num_programs(1)
    heads = list(range(N_HEADS))
    q = [q_ref[0, hd] for hd in heads]
    n_full = (i * tq) // tk

    def first_live_block(hd):
        bh = pl.program_id(0) * N_HEADS + hd
        qk_bound = FOX_NORM_MARGIN * 2.0 * qmax_ref[bh * nq + i] * kmax_ref[bh]
        f_tile = fend_ref[bh * nq + jnp.maximum(i - 1, 0)]

        def first_live(jj, first):
            j = n_full - 1 - jj
            f_end = fend_ref[bh * nq + (j + 1) * (tk // tq) - 1]
            live = qk_bound + (f_tile - f_end) + FOX_F_SLACK >= FOX_SKIP_LOG2
            return jnp.where(live, j, first)

        return lax.fori_loop(0, n_full, first_live, n_full)

    base = functools.reduce(jnp.minimum, [first_live_block(hd) for hd in heads])
    n_eff = n_full - base

    def key_rows(j):
        return pl.ds(pl.multiple_of(j * tk, tk), tk)

    def scores_to(j, slot):
        s = [_dot_nt(k_ref[0, hd, key_rows(j), :], q[hd]) for hd in heads]
        for hd in heads:
            s_scr[hd, slot] = s[hd]
            mx_scr[hd, slot] = jnp.max(s[hd], axis=0, keepdims=True)

    def weighted_values(j, slot):
        return [_dot(vt_ref[0, hd * V_AUG:(hd + 1) * V_AUG, key_rows(j)], p_scr[hd, slot])
                for hd in heads]

    def softmax_to(s, mx, slot):
        m_old = [m_scr[hd] for hd in heads]
        m_new = [jnp.maximum(a, b) for a, b in zip(m_old, mx)]
        p = [jnp.exp2(a - b).astype(BF16) for a, b in zip(s, m_new)]
        for hd in heads:
            m_scr[hd] = m_new[hd]
            al_scr[hd, slot] = jnp.exp2(m_old[hd] - m_new[hd])
            p_scr[hd, slot] = p[hd]

    def stage(local, cur):
        j = base + local
        nxt = 1 - cur
        pv_prev = weighted_values(jnp.maximum(j - 1, 0), nxt)
        scores_to(j + 1, nxt)
        softmax_to([s_scr[hd, cur] for hd in heads], [mx_scr[hd, cur] for hd in heads], cur)
        for hd in heads:
            acc_scr[hd] = al_scr[hd, nxt] * acc_scr[hd] + pv_prev[hd]

    def tail(cur):
        nxt = 1 - cur
        pv_prev = weighted_values(jnp.maximum(n_full - 1, 0), nxt)
        mask = n_full * tk + _iota((tk, tq), 0) <= i * tq + _iota((tk, tq), 1)
        s = [jnp.where(mask, s_scr[hd, cur], NEG_BIG) for hd in heads]
        softmax_to(s, [jnp.max(x, axis=0, keepdims=True) for x in s], cur)
        pv_last = weighted_values(n_full, cur)
        for hd in heads:
            acc = al_scr[hd, cur] * (al_scr[hd, nxt] * acc_scr[hd] + pv_prev[hd]) + pv_last[hd]
            o_ref[0, hd * HEAD_DIM:(hd + 1) * HEAD_DIM, :] = (
                acc[0:HEAD_DIM] / acc[HEAD_DIM:HEAD_DIM + 1])

    m_scr[...] = jnp.full(m_scr.shape, NEG_BIG, F32)
    acc_scr[...] = jnp.zeros_like(acc_scr)
    p_scr[:, 1] = jnp.zeros((N_HEADS, tk, tq), BF16)
    al_scr[:, 1] = jnp.ones((N_HEADS, 1, tq), F32)
    scores_to(base, 0)

    def pair(jj, carry):
        stage(2 * jj, 0)
        stage(2 * jj + 1, 1)
        return carry

    lax.fori_loop(0, n_eff // 2, pair, 0)

    @pl.when(n_eff % 2 == 1)
    def _():
        stage(n_eff - 1, 0)
        tail(1)

    @pl.when(n_eff % 2 == 0)
    def _():
        tail(0)


def _fox_attention(fend, qmax, kmax, q, k, vt, *, tq, tk):
    b, h, s, _ = q.shape
    grid_spec = pltpu.PrefetchScalarGridSpec(
        num_scalar_prefetch=3,
        grid=(b, s // tq),
        in_specs=[pl.BlockSpec((1, h, tq, HEAD_PAD), lambda bi, i, *_: (bi, 0, i, 0)),
                  pl.BlockSpec((1, h, s, HEAD_PAD), lambda bi, i, *_: (bi, 0, 0, 0)),
                  pl.BlockSpec((1, h * V_AUG, s), lambda bi, i, *_: (bi, 0, 0))],
        out_specs=pl.BlockSpec((1, h * HEAD_DIM, tq), lambda bi, i, *_: (bi, 0, i)),
        scratch_shapes=[pltpu.VMEM((h, 2, tk, tq), F32), pltpu.VMEM((h, 2, tk, tq), BF16),
                        pltpu.VMEM((h, 2, 1, tq), F32), pltpu.VMEM((h, 2, 1, tq), F32),
                        pltpu.VMEM((h, 1, tq), F32), pltpu.VMEM((h, V_AUG, tq), F32)])
    return pl.pallas_call(
        functools.partial(_fox_kernel, tq=tq, tk=tk),
        grid_spec=grid_spec,
        out_shape=jax.ShapeDtypeStruct((b, h * HEAD_DIM, s), F32),
        compiler_params=_params(2),
        name="fox_attention",
    )(fend, qmax, kmax, q, k, vt)


def _sb_kernel(q_ref, k_ref, vt_ref, o_ref, *, tq):
    i = pl.program_id(1)
    tk = tq
    heads = list(range(N_HEADS))
    q = [q_ref[0, hd] for hd in heads]
    later = (_iota((tk, tk), 0) < _iota((tk, tk), 1)).astype(BF16)

    def each(f, *xs):
        return [f(*a) for a in zip(*xs)]

    def block(j, rest_q, acc, masked):
        ks = pl.multiple_of(j * tk, tk)
        z = [_dot_nt(k_ref[0, hd, pl.ds(ks, tk), :], q[hd]) for hd in heads]
        log_keep = each(lambda x: _log_sigmoid(-x), z)
        if masked:
            mask = _iota((tk, tq), 0) < _iota((tk, tq), 1)
            log_keep = each(lambda x: jnp.where(mask, x, 0.0), log_keep)
        split = each(_split2, log_keep)
        rest_in = each(lambda hl: _dot(later, hl[0]) + _dot(later, hl[1]), split)
        att = each(lambda x, lk, ri, rq: jnp.exp(x + lk + ri + rq), z, log_keep, rest_in, rest_q)
        if masked:
            att = each(lambda x: jnp.where(mask, x, 0.0), att)
        pv = [_dot(vt_ref[0, hd * HEAD_DIM:(hd + 1) * HEAD_DIM, pl.ds(ks, tk)],
                   att[hd].astype(BF16)) for hd in heads]
        acc = each(lambda a, x: a + x, acc, pv)
        rest_q = each(lambda rq, ri, lk: rq + ri[0:1, :] + lk[0:1, :], rest_q, rest_in, log_keep)
        return rest_q, acc

    rest_q, acc = block(i, [jnp.zeros((1, tq), F32)] * N_HEADS,
                        [jnp.zeros((HEAD_DIM, tq), F32)] * N_HEADS, True)

    def cond(c):
        j, rest_q, _ = c
        alive = functools.reduce(jnp.maximum, rest_q)
        return jnp.logical_and(j >= 0, jnp.max(alive) > SB_SKIP_LOG)

    def body(c):
        j, rest_q, acc = c
        rest_q, acc = block(j, list(rest_q), list(acc), False)
        return j - 1, tuple(rest_q), tuple(acc)

    _, _, acc = lax.while_loop(cond, body, (i - 1, tuple(rest_q), tuple(acc)))
    o_ref[0] = jnp.concatenate(list(acc), axis=0)


def _sb_attention(q, k, vt, *, tq):
    b, h, s, _ = q.shape
    return pl.pallas_call(
        functools.partial(_sb_kernel, tq=tq),
        grid=(b, s // tq),
        in_specs=[pl.BlockSpec((1, h, tq, HEAD_PAD), lambda bi, i: (bi, 0, i, 0)),
                  pl.BlockSpec((1, h, s, HEAD_PAD), lambda bi, i: (bi, 0, 0, 0)),
                  pl.BlockSpec((1, h * HEAD_DIM, s), lambda bi, i: (bi, 0, 0))],
        out_specs=pl.BlockSpec((1, h * HEAD_DIM, tq), lambda bi, i: (bi, 0, i)),
        out_shape=jax.ShapeDtypeStruct((b, h * HEAD_DIM, s), F32),
        compiler_params=_params(2),
        name="sb_attention",
    )(q, k, vt)


def _rwkv_kernel(p_ref, m_ref, mu_rkv_ref, mu_misc_ref, w0_ref, w2_ref, a0_ref, a2_ref,
                 kk_ref, ka_ref, rk_ref, lng_ref, lnb_ref, bdm_ref, o_ref,
                 pad_rkv, pad_misc, r_s, k_s, v_s, kn_s, al_s, lw_s,
                 wt_s, u0_s, o0_s, mrb_s, rt_s, bh_s, kh_s, pc_s, y_s, ht_s, *, tt, nb):
    c = RWKV_CHUNK
    g = D_GROUP
    n_chunks = tt // c

    @pl.when(pl.program_id(0) == 0)
    def _():
        pad_rkv[:, 0:SUBLANES, :] = jnp.zeros((nb, SUBLANES, N_RKV), F32)
        pad_misc[:, 0:SUBLANES, :] = jnp.zeros((nb, SUBLANES, LANES), F32)
        ht_s[...] = jnp.zeros_like(ht_s)

    bdm = bdm_ref[...]
    bdm_f = bdm.astype(F32)

    for b in range(nb):
        p = p_ref[b]
        misc = m_ref[b]
        pad_rkv[b, SUBLANES:SUBLANES + tt, :] = p
        pad_misc[b, SUBLANES:SUBLANES + tt, :] = misc
        p_prev = pad_rkv[b, SUBLANES - 1:SUBLANES - 1 + tt, :]
        m_prev = pad_misc[b, SUBLANES - 1:SUBLANES - 1 + tt, :]
        pad_rkv[b, 0:SUBLANES, :] = p[tt - SUBLANES:tt, :]
        pad_misc[b, 0:SUBLANES, :] = misc[tt - SUBLANES:tt, :]
        p = p + (p_prev - p) * mu_rkv_ref[...]
        misc = misc + (m_prev - misc) * mu_misc_ref[...]
        k = p[:, g:2 * g]

        w = -_softplus(-(w0_ref[...] + _dot3(jnp.tanh(misc), w2_ref[...]))) - 0.5
        alpha = _sigmoid(a0_ref[...] + _dot3(misc, a2_ref[...]))
        kn = k * kk_ref[...]
        ss = _dot_sel_rhs(kn * kn, bdm)
        r_s[b] = p[:, 0:g]
        k_s[b] = k * (1.0 + (alpha - 1.0) * ka_ref[...])
        v_s[b] = p[:, 2 * g:3 * g]
        kn_s[b] = kn * lax.rsqrt(jnp.maximum(ss, 1e-12))
        al_s[b] = alpha
        lw_s[b] = -jnp.exp(w)

    row = _iota((c, g), 0)
    col = _iota((c, g), 1) % c
    strict = col < row
    incl = col <= row
    eye = (col == row).astype(F32)
    lower_c = (_iota((c, c), 1) <= _iota((c, c), 0)).astype(BF16)
    level_masks = []
    m = 1
    while m < c:
        level_masks.append(jnp.logical_and(
            strict, jnp.logical_and(row // (2 * m) == col // (2 * m), row // m != col // m)))
        m *= 2

    def bd(x):
        return jnp.concatenate([x.astype(BF16)] * N_HEADS, axis=0) * bdm

    def mm(a, b_bf16):
        return _dot(a.astype(BF16), b_bf16)

    def mm_nt(a, b_bf16):
        return _dot_nt(a.astype(BF16), b_bf16)

    def each(f, *xs):
        return [f(*a) for a in zip(*xs)]

    def phase_a(chains):
        sls = [pl.ds(pl.multiple_of(ci * c, c), c) for _, ci in chains]
        ld = lambda ref: [ref[b, sl, :] for (b, _), sl in zip(chains, sls)]
        r_c, k_c, v_c, kn_c, al_c, lw_c = (ld(s) for s in (r_s, k_s, v_s, kn_s, al_s, lw_s))
        cl = each(lambda x: _dot_sel_lhs(lower_c, x), lw_c)
        cl_last = each(lambda x: x[c - 1:c, :], cl)
        a_t = each(lambda kn, x, lw: -kn * jnp.exp(x - lw), kn_c, cl, lw_c)
        r_t = each(lambda r, x: r * jnp.exp(x), r_c, cl)
        q_inv = each(lambda x: jnp.exp(-x), cl)
        p_rem = each(lambda xl, x: jnp.exp(xl - x), cl_last, cl)
        kna = each(lambda kn, al: kn * al, kn_c, al_c)
        ar = each(lambda a, r: jnp.concatenate([a, r], axis=0), a_t, r_t)
        s_b = each(lambda x, kb, qi: mm_nt(x, bd(kb * qi)), ar, kna, q_inv)
        s_k = each(lambda x, kk, qi: mm_nt(x, bd(kk * qi)), ar, k_c, q_inv)
        n = each(lambda x: jnp.where(strict, x[0:c], 0.0), s_b)
        a_ak = each(lambda x: jnp.where(strict, x[0:c], 0.0), s_k)
        m_rb = each(lambda x: jnp.where(incl, x[c:2 * c], 0.0), s_b)
        m_rk = each(lambda x: jnp.where(incl, x[c:2 * c], 0.0), s_k)

        inv = each(lambda x: eye + jnp.where(level_masks[0], x, 0.0), n)
        for lm in level_masks[1:]:
            half = each(lambda d, x: mm(d, bd(jnp.where(lm, x, 0.0))), inv, n)
            inv = each(lambda d, hf: d + mm(hf, bd(d)), inv, half)

        v_bd = each(bd, v_c)
        akv = each(mm, a_ak, v_bd)
        wt = each(lambda d, a: mm(d, bd(a)), inv, a_t)
        u0 = each(lambda d, x: mm(d, bd(x)), inv, akv)
        o0 = each(mm, m_rk, v_bd)
        bh = each(lambda x, p: x * p, kna, p_rem)
        kh = each(lambda x, p: x * p, k_c, p_rem)
        for ref, vals in zip((wt_s, u0_s, o0_s, mrb_s, rt_s, bh_s, kh_s),
                             (wt, u0, o0, m_rb, r_t, bh, kh)):
            for (b, _), sl, val in zip(chains, sls, vals):
                ref[b, sl, :] = val
        for (b, ci), xl in zip(chains, cl_last):
            pc_s[b, pl.ds(pl.multiple_of(ci * SUBLANES, SUBLANES), SUBLANES), :] = (
                jnp.broadcast_to(jnp.exp(xl), (SUBLANES, g)))

    def phase_b(ci):
        sl = pl.ds(pl.multiple_of(ci * c, c), c)
        bs = list(range(nb))
        ht = [ht_s[b] for b in bs]
        wr = [jnp.concatenate([wt_s[b, sl, :], rt_s[b, sl, :]], axis=0) for b in bs]
        wrh = each(lambda x, hh: mm_nt(x, hh.astype(BF16)), wr, ht)
        u = [x[0:c] + u0_s[b, sl, :] for x, b in zip(wrh, bs)]
        uv_t = [jnp.concatenate([x, v_s[b, sl, :]], axis=0).T for x, b in zip(u, bs)]
        bk = [jnp.concatenate([bh_s[b, sl, :], kh_s[b, sl, :]], axis=0) for b in bs]
        upd = each(lambda x, y: mm(x, y.astype(BF16)), uv_t, bk)
        mu_ = [mm(mrb_s[b, sl, :], bd(x)) for x, b in zip(u, bs)]
        for b in bs:
            p_c = pc_s[b, pl.ds(pl.multiple_of(ci * SUBLANES, SUBLANES), 1), :]
            y_s[b, sl, :] = wrh[b][c:2 * c] + mu_[b] + o0_s[b, sl, :]
            ht_s[b] = ht[b] * p_c + bdm_f * upd[b]

    def loop_a(ci, carry):
        phase_a([(b, ci * RWKV_UNROLL + j) for j in range(RWKV_UNROLL) for b in range(nb)])
        return carry

    def loop_b(ci, carry):
        phase_b(ci)
        return carry

    lax.fori_loop(0, n_chunks // RWKV_UNROLL, loop_a, 0)
    lax.fori_loop(0, n_chunks, loop_b, 0)

    inv_n = 1.0 / HEAD_DIM
    for b in range(nb):
        y = y_s[b]
        mean = _dot_sel_rhs(y, bdm) * inv_n
        yc = y - mean
        var = _dot_sel_rhs(yc * yc, bdm) * inv_n
        yn = yc * lax.rsqrt(var + GN_EPS) * lng_ref[...] + lnb_ref[...]
        bonus = _dot_sel_rhs(r_s[b] * k_s[b] * rk_ref[...], bdm) * v_s[b]
        o_ref[b] = yn + bonus


def _rwkv(p_rkv, misc, prm, bdm, *, tt):
    b, s, _ = p_rkv.shape
    g = D_GROUP
    seq = lambda width: pl.BlockSpec((b, tt, width), lambda ti: (0, ti, 0))
    big = pltpu.VMEM((b, tt, g), F32)
    return pl.pallas_call(
        functools.partial(_rwkv_kernel, tt=tt, nb=b),
        grid=(s // tt,),
        in_specs=[seq(N_RKV), seq(LANES)] + [_full_spec(a) for a in prm] + [_full_spec(bdm)],
        out_specs=seq(g),
        out_shape=jax.ShapeDtypeStruct((b, s, g), F32),
        scratch_shapes=[pltpu.VMEM((b, tt + SUBLANES, N_RKV), F32),
                        pltpu.VMEM((b, tt + SUBLANES, LANES), F32)]
        + [big] * 13
        + [pltpu.VMEM((b, tt // RWKV_CHUNK * SUBLANES, g), F32), big,
           pltpu.VMEM((b, g, g), F32)],
        compiler_params=_params(1),
        name="rwkv7",
    )(p_rkv, misc, *prm, bdm)


def _lru_kernel(x_ref, cw_ref, cb_ref, wa_ref, ba_ref, wx_ref, bx_ref, lam_ref, o_ref,
                pad, a_s, u_s, h_s, *, tt, nb):
    @pl.when(pl.program_id(0) == 0)
    def _():
        pad[:, 0:SUBLANES, :] = jnp.zeros((nb, SUBLANES, D_GROUP), F32)
        h_s[...] = jnp.zeros_like(h_s)

    for b in range(nb):
        x = x_ref[b]
        pad[b, SUBLANES:SUBLANES + tt, :] = x
        xc = cw_ref[CONV_WIDTH - 1:CONV_WIDTH, :] * x + cb_ref[...]
        for d in range(1, CONV_WIDTH):
            tap = CONV_WIDTH - 1 - d
            xc = xc + cw_ref[tap:tap + 1, :] * pad[b, SUBLANES - d:SUBLANES - d + tt, :]
        pad[b, 0:SUBLANES, :] = x[tt - SUBLANES:tt, :]

        xb = xc.astype(BF16)
        r = _sigmoid(_dot(xb, wa_ref[...]) + ba_ref[...])
        i = _sigmoid(_dot(xb, wx_ref[...]) + bx_ref[...])
        log_a = -LRU_C * r * _softplus(-lam_ref[...])
        a_s[b] = jnp.exp(log_a)
        th = jnp.tanh(log_a)
        u_s[b] = jnp.sqrt(-2.0 * th / (1.0 - th)) * (i * xc)

    def group(gi, hs):
        base = pl.multiple_of(gi * SUBLANES, SUBLANES)
        a8 = [a_s[b, pl.ds(base, SUBLANES), :] for b in range(nb)]
        u8 = [u_s[b, pl.ds(base, SUBLANES), :] for b in range(nb)]
        hs = list(hs)
        rows = [[] for _ in range(nb)]
        for j in range(SUBLANES):
            for b in range(nb):
                hs[b] = a8[b][j:j + 1, :] * hs[b] + u8[b][j:j + 1, :]
                rows[b].append(hs[b])
        for b in range(nb):
            o_ref[b, pl.ds(base, SUBLANES), :] = jnp.concatenate(rows[b], axis=0)
        return tuple(hs)

    hs = lax.fori_loop(0, tt // SUBLANES, group, tuple(h_s[b] for b in range(nb)))
    for b in range(nb):
        h_s[b] = hs[b]


def _lru(x, prm, *, tt):
    b, s, g = x.shape
    seq = pl.BlockSpec((b, tt, g), lambda ti: (0, ti, 0))
    return pl.pallas_call(
        functools.partial(_lru_kernel, tt=tt, nb=b),
        grid=(s // tt,),
        in_specs=[seq] + [_full_spec(a) for a in prm],
        out_specs=seq,
        out_shape=jax.ShapeDtypeStruct((b, s, g), F32),
        scratch_shapes=[pltpu.VMEM((b, tt + SUBLANES, g), F32), pltpu.VMEM((b, tt, g), F32),
                        pltpu.VMEM((b, tt, g), F32), pltpu.VMEM((b, 1, g), F32)],
        compiler_params=_params(1),
        name="rg_lru",
    )(x, *prm)


def _outproj_kernel(x_ref, yft_ref, yst_ref, yr_ref, yl_ref, gates_ref, w_ref, fg_ref, o_ref, *,
                    final):
    acc = x_ref[0]
    ys = (yft_ref[0].T, yst_ref[0].T, yr_ref[0], yl_ref[0])
    for gi, y in enumerate(ys):
        gate = gates_ref[0, :, gi * D_GROUP:(gi + 1) * D_GROUP]
        y = y * (gate * _sigmoid(gate))
        acc = acc + _dot(y.astype(BF16), w_ref[gi * D_GROUP:(gi + 1) * D_GROUP, :])
    if final:
        ms = jnp.mean(acc * acc, axis=-1, keepdims=True)
        acc = acc * lax.rsqrt(ms + RMS_EPS) * fg_ref[...]
    o_ref[0] = acc


def _outproj(x, y_fox_t, y_sb_t, y_rw, y_lru, gates, w, final_g, *, tm, final):
    b, s, _ = x.shape
    seq = lambda width: pl.BlockSpec((1, tm, width), lambda bi, i: (bi, i, 0))
    chan = pl.BlockSpec((1, D_GROUP, tm), lambda bi, i: (bi, 0, i))
    return pl.pallas_call(
        functools.partial(_outproj_kernel, final=final),
        grid=(b, s // tm),
        in_specs=[seq(D_MODEL), chan, chan, seq(D_GROUP), seq(D_GROUP), seq(4 * D_GROUP),
                  _full_spec(w), _full_spec(final_g)],
        out_specs=seq(D_MODEL),
        out_shape=jax.ShapeDtypeStruct((b, s, D_MODEL), F32),
        compiler_params=_params(2),
        name="outproj",
    )(x, y_fox_t, y_sb_t, y_rw, y_lru, gates, w, final_g)


def _w_in_segments():
    g, h, r, dh = D_GROUP, N_HEADS, RWKV_LORA, HEAD_DIM
    o_ff = 4 * g
    o_sb = o_ff + h
    o_rw = o_sb + 4 * g
    o_rg = o_rw + 3 * g + 2 * r
    o_lx = o_rg + g
    o_lg = o_lx + g
    segs = []
    for src, dst in ((0, C_FQ), (g, C_FK), (o_sb, C_SQ), (o_sb + g, C_SK)):
        segs += [(src + hd * dh, dst + hd * HEAD_PAD, dh) for hd in range(h)]
    segs += [(2 * g, C_FV, g), (o_sb + 2 * g, C_SV, g),
             (3 * g, C_GATES, g), (o_sb + 3 * g, C_GATES + g, g), (o_rg, C_GATES + 2 * g, g),
             (o_lg, C_GATES + 3 * g, g),
             (o_rw, C_RKV, 3 * g), (o_lx, C_LX, g),
             (o_rw + 3 * g, C_MISC, 2 * r), (o_ff, C_MISC + FF_LANE, h)]
    return segs


def _w_in_kernel(w_ref, o_ref):
    o_ref[...] = jnp.zeros_like(o_ref)
    for src, dst, width in _w_in_segments():
        o_ref[0, :, dst:dst + width] = w_ref[0, :, src:src + width].astype(BF16)


def _permute_w_in(w_in, *, tr):
    depth, d, n_in = w_in.shape
    return pl.pallas_call(
        _w_in_kernel,
        grid=(depth, d // tr),
        in_specs=[pl.BlockSpec((1, tr, n_in), lambda l, i: (l, i, 0))],
        out_specs=pl.BlockSpec((1, tr, N_PROJ), lambda l, i: (l, i, 0)),
        out_shape=jax.ShapeDtypeStruct((depth, d, N_PROJ), BF16),
        compiler_params=_params(2),
        name="w_in_layout",
    )(w_in)


def _forget_placement():
    selq = np.zeros((3 * LANES, QK_COLS), np.float32)
    selk = np.zeros((3 * LANES, QK_COLS), np.float32)
    augq = np.zeros((1, QK_COLS), np.float32)
    augk = np.zeros((1, QK_COLS), np.float32)
    for hd in range(N_HEADS):
        for piece in range(3):
            src = piece * LANES + FF_LANE + hd
            selq[src, hd * HEAD_PAD + AUG_Q_F + piece] = 1.0
            selk[src, hd * HEAD_PAD + AUG_K_F + piece] = -1.0
            augq[0, hd * HEAD_PAD + AUG_Q_ONE + piece] = 1.0
            augk[0, hd * HEAD_PAD + AUG_K_ONE + piece] = 1.0
    return (jnp.asarray(selq, BF16), jnp.asarray(selk, BF16), jnp.asarray(augq), jnp.asarray(augk))


def _block_diag(w):
    h, n, _ = w.shape
    eye = jnp.eye(h, dtype=w.dtype)
    return jnp.einsum('hij,hk->hikj', w, eye).reshape(h * n, h * n)


def _pick_tile(s, pref):
    t = pref
    while s % t:
        t //= 2
    return t


def kernel(x, norm_g, w_in, b_forget, rwkv_mu, rwkv_w0, rwkv_w2, rwkv_a0, rwkv_a2, rwkv_k_k,
           rwkv_k_a, rwkv_r_k, rwkv_ln_g, rwkv_ln_b, lru_conv_w, lru_conv_b, lru_w_a, lru_b_a,
           lru_w_x, lru_b_x, lru_lambda, w_out, final_g):
    b, s, d = x.shape
    depth = w_in.shape[0]
    g, h, dh, r = D_GROUP, N_HEADS, HEAD_DIM, RWKV_LORA
    tm = _pick_tile(s, 512)
    tq = _pick_tile(s, 256)
    tk_fox = _pick_tile(s, 2 * tq)
    tt = _pick_tile(s, 512)
    row = lambda a: a.reshape(1, -1).astype(F32)

    bdm = _block_diag(jnp.ones((h, dh, dh), BF16))
    selq, selk, augq, augk = _forget_placement()
    w_in_k = _permute_w_in(w_in, tr=LANES)
    for l in range(depth):
        fbias = jnp.zeros((1, LANES), F32).at[0, FF_LANE:FF_LANE + h].set(b_forget[l])
        consts = [row(norm_g[l]), w_in_k, fbias, selq, selk, augq, augk]
        fq, fk, fvt, sq, sk, svt, gates, rkv, lx, misc, f2, norms = _inproj(x, consts, l, tm=tm)

        per_head = lambda a: a.transpose(0, 2, 1).reshape(-1)
        fend = per_head(f2[:, tq - 1::tq, FF_LANE:FF_LANE + h])
        qmax = per_head(jnp.repeat(jnp.sqrt(norms[:, :, 0, 0:h]), tm // tq, axis=1))
        kmax = jnp.sqrt(jnp.max(norms[:, :, 0, h:2 * h], axis=1)).reshape(-1)
        y_fox_t = _fox_attention(fend, qmax, kmax, fq, fk, fvt, tq=tq, tk=tk_fox)
        y_sb_t = _sb_attention(sq, sk, svt, tq=tq)

        mu = rwkv_mu[l]
        pad_rows = lambda a, lo: jnp.zeros((LANES, g), F32).at[lo:lo + r].set(a)
        rw_prm = [row(mu[:N_RKV]),
                  jnp.zeros((1, LANES), F32).at[0, :2 * r].set(mu[N_RKV:]),
                  row(rwkv_w0[l]), pad_rows(rwkv_w2[l], 0), row(rwkv_a0[l]), pad_rows(rwkv_a2[l], r),
                  row(rwkv_k_k[l]), row(rwkv_k_a[l]), row(rwkv_r_k[l]), row(rwkv_ln_g[l]),
                  row(rwkv_ln_b[l])]
        y_rw = _rwkv(rkv, misc, rw_prm, bdm, tt=tt)

        lru_prm = [lru_conv_w[l].astype(F32), row(lru_conv_b[l]),
                   _block_diag(lru_w_a[l]).astype(BF16), row(lru_b_a[l]),
                   _block_diag(lru_w_x[l]).astype(BF16), row(lru_b_x[l]), row(lru_lambda[l])]
        y_lru = _lru(lx, lru_prm, tt=tt)

        x = _outproj(x, y_fox_t, y_sb_t, y_rw, y_lru, gates, w_out[l].astype(BF16), row(final_g),
                     tm=tm, final=(l == depth - 1))
    return x
```

```python
import functools

import jax
import jax.numpy as jnp
from jax import lax
from jax.experimental import pallas as pl
from jax.experimental.pallas import tpu as pltpu

F32 = jnp.float32
BF16 = jnp.bfloat16

D_MODEL = 1024
D_GROUP = 256
N_HEADS = 4
HEAD_DIM = 64
RWKV_LORA = 32
CONV_WIDTH = 4
LRU_C = 8.0
RMS_EPS = 1e-6
GN_EPS = 64e-5
N_RKV = 3 * D_GROUP

LANES = 128
SUBLANES = 8
VMEM_LIMIT_BYTES = 56 * 1024 * 1024

HEAD_PAD = LANES
QK_COLS = N_HEADS * HEAD_PAD
C_FQ = 0
C_FK = C_FQ + QK_COLS
C_SQ = C_FK + QK_COLS
C_SK = C_SQ + QK_COLS
C_FV = C_SK + QK_COLS
C_SV = C_FV + D_GROUP
C_GATES = C_SV + D_GROUP
C_RKV = C_GATES + 4 * D_GROUP
C_LX = C_RKV + N_RKV
C_MISC = C_LX + D_GROUP
N_PROJ = C_MISC + LANES
FF_LANE = 2 * RWKV_LORA
AUG_K_F = HEAD_DIM
AUG_K_ONE = HEAD_DIM + 3
AUG_Q_ONE = HEAD_DIM
AUG_Q_F = HEAD_DIM + 3

CUMSUM_BLOCK = LANES
V_AUG = HEAD_DIM + 16
LOG2E = 1.4426950408889634

FOX_SKIP_LOG2 = -150.0
FOX_NORM_MARGIN = 1.02
FOX_F_SLACK = 1.0

NEG_BIG = -1e30
SB_SKIP_LOG = -106.0
RWKV_CHUNK = 64
RWKV_UNROLL = 4


def _dot(a, b):
    return jnp.dot(a, b, preferred_element_type=F32)


def _dot_nt(a, b):
    return lax.dot_general(a, b, (((1,), (1,)), ((), ())), preferred_element_type=F32)


def _split2(x):
    hi = x.astype(BF16)
    lo = (x - hi.astype(F32)).astype(BF16)
    return hi, lo


def _split3(x):
    hi = x.astype(BF16)
    r1 = x - hi.astype(F32)
    mid = r1.astype(BF16)
    lo = (r1 - mid.astype(F32)).astype(BF16)
    return hi, mid, lo


def _dot3(a, b):
    ah, al = _split2(a)
    bh, bl = _split2(b)
    return _dot(ah, bh) + (_dot(al, bh) + _dot(ah, bl))


def _dot_sel_lhs(sel, x):
    hi, mid, lo = _split3(x)
    return _dot(sel, hi) + (_dot(sel, mid) + _dot(sel, lo))


def _dot_sel_rhs(x, sel):
    hi, lo = _split2(x)
    return _dot(hi, sel) + _dot(lo, sel)


def _softplus(x):
    return jnp.maximum(x, 0.0) + jnp.log(1.0 + jnp.exp(-jnp.abs(x)))


def _log_sigmoid(x):
    return jnp.minimum(x, 0.0) - jnp.log(1.0 + jnp.exp(-jnp.abs(x)))


def _sigmoid(x):
    return 1.0 / (1.0 + jnp.exp(-x))


def _iota(shape, dim):
    return lax.broadcasted_iota(jnp.int32, shape, dim)


def _full_spec(a):
    return pl.BlockSpec(a.shape, lambda *_: (0,) * a.ndim)


def _params(n_grid):
    return pltpu.CompilerParams(dimension_semantics=("arbitrary",) * n_grid,
                                vmem_limit_bytes=VMEM_LIMIT_BYTES)


def _inproj_kernel(x_ref, g_ref, w_ref, fb_ref,
                   fq_ref, fk_ref, fvt_ref, sq_ref, sk_ref, svt_ref, gates_ref, rkv_ref, lx_ref,
                   misc_ref, f2_ref, norms_ref, ftot, v_stage):
    @pl.when(pl.program_id(1) == 0)
    def _():
        ftot[...] = jnp.zeros_like(ftot)

    x = x_ref[0]
    tm = x.shape[0]
    ms = jnp.mean(x * x, axis=-1, keepdims=True)
    h = (x * lax.rsqrt(ms + RMS_EPS) * g_ref[...]).astype(BF16)
    scale = HEAD_DIM ** -0.5

    def proj(c0, width):
        return _dot(h, w_ref[0, :, c0:c0 + width])

    misc = proj(C_MISC, LANES)
    misc_ref[0] = misc
    gates_ref[0] = proj(C_GATES, 4 * D_GROUP)
    rkv_ref[0] = proj(C_RKV, N_RKV)
    lx_ref[0] = proj(C_LX, D_GROUP)

    lf = _log_sigmoid(misc + fb_ref[...])
    cb = CUMSUM_BLOCK
    lower = (_iota((cb, cb), 1) <= _iota((cb, cb), 0)).astype(BF16)
    run = ftot[...]
    blocks = []
    for r0 in range(0, tm, cb):
        blk = _dot_sel_lhs(lower, lf[r0:r0 + cb, :]) + run
        run = blk[cb - 1:cb, :]
        blocks.append(blk)
    ftot[...] = run
    f = jnp.concatenate(blocks, axis=0)
    f2 = f * LOG2E
    f2_ref[0] = f2
    hi = f2.astype(BF16).astype(F32)
    mid = (f2 - hi).astype(BF16).astype(F32)
    lo = (f2 - hi - mid).astype(BF16).astype(F32)

    pq = proj(C_FQ, QK_COLS) * (scale * LOG2E)
    pk = proj(C_FK, QK_COLS)
    sq = proj(C_SQ, QK_COLS) * scale
    sk = proj(C_SK, QK_COLS)
    lane = _iota((tm, LANES), 1)
    ones_q = jnp.where(jnp.logical_and(lane >= AUG_Q_ONE, lane < AUG_Q_ONE + 3), 1.0, 0.0)
    ones_k = jnp.where(jnp.logical_and(lane >= AUG_K_ONE, lane < AUG_K_ONE + 3), 1.0, 0.0)
    nlane = _iota((SUBLANES, LANES), 1)
    norms = jnp.zeros((SUBLANES, LANES), F32)
    for hd in range(N_HEADS):
        cols = slice(hd * HEAD_PAD, (hd + 1) * HEAD_PAD)
        aug_q, aug_k = ones_q, ones_k
        for i3, piece in enumerate((hi, mid, lo)):
            col = jnp.broadcast_to(piece[:, FF_LANE + hd:FF_LANE + hd + 1], (tm, LANES))
            aug_q = jnp.where(lane == AUG_Q_F + i3, col, aug_q)
            aug_k = jnp.where(lane == AUG_K_F + i3, -col, aug_k)
        fq_ref[0, hd] = (pq[:, cols] + aug_q).astype(BF16)
        fk_ref[0, hd] = (pk[:, cols] + aug_k).astype(BF16)
        sq_ref[0, hd] = sq[:, cols].astype(BF16)
        sk_ref[0, hd] = sk[:, cols].astype(BF16)
        for side, pv in enumerate((pq, pk)):
            blk = pv[:, cols]
            top = jnp.max(jnp.sum(blk * blk, axis=-1, keepdims=True), axis=0, keepdims=True)
            norms = jnp.where(nlane == side * N_HEADS + hd, top, norms)
    norms_ref[0, 0] = norms
    v_stage[...] = proj(C_SV, D_GROUP)
    svt_ref[0] = v_stage[...].T.astype(BF16)
    v_stage[...] = proj(C_FV, D_GROUP)
    vt = v_stage[...].T.astype(BF16)
    ones_rows = (_iota((V_AUG - HEAD_DIM, tm), 0) == 0).astype(BF16)
    for hd in range(N_HEADS):
        fvt_ref[0, hd * V_AUG:hd * V_AUG + HEAD_DIM, :] = vt[hd * HEAD_DIM:(hd + 1) * HEAD_DIM, :]
        fvt_ref[0, hd * V_AUG + HEAD_DIM:(hd + 1) * V_AUG, :] = ones_rows


def _inproj(x, consts, layer, *, tm):
    b, s, _ = x.shape
    w_spec = pl.BlockSpec((1,) + consts[1].shape[1:], lambda bi, i: (layer, 0, 0))
    seq = lambda width: pl.BlockSpec((1, tm, width), lambda bi, i: (bi, i, 0))
    heads = pl.BlockSpec((1, N_HEADS, tm, HEAD_PAD), lambda bi, i: (bi, 0, i, 0))
    chan = pl.BlockSpec((1, D_GROUP, tm), lambda bi, i: (bi, 0, i))
    heads_shape = jax.ShapeDtypeStruct((b, N_HEADS, s, HEAD_PAD), BF16)
    chan_shape = jax.ShapeDtypeStruct((b, D_GROUP, s), BF16)
    chan_aug = pl.BlockSpec((1, N_HEADS * V_AUG, tm), lambda bi, i: (bi, 0, i))
    chan_aug_shape = jax.ShapeDtypeStruct((b, N_HEADS * V_AUG, s), BF16)
    seq_shape = lambda width: jax.ShapeDtypeStruct((b, s, width), F32)
    return pl.pallas_call(
        _inproj_kernel,
        grid=(b, s // tm),
        in_specs=[seq(D_MODEL), _full_spec(consts[0]), w_spec] + [_full_spec(a) for a in consts[2:]],
        out_specs=[heads, heads, chan_aug, heads, heads, chan,
                   seq(4 * D_GROUP), seq(N_RKV), seq(D_GROUP), seq(LANES), seq(LANES),
                   pl.BlockSpec((1, 1, SUBLANES, LANES), lambda bi, i: (bi, i, 0, 0))],
        out_shape=[heads_shape, heads_shape, chan_aug_shape, heads_shape, heads_shape, chan_shape,
                   seq_shape(4 * D_GROUP), seq_shape(N_RKV), seq_shape(D_GROUP), seq_shape(LANES),
                   seq_shape(LANES), jax.ShapeDtypeStruct((b, s // tm, SUBLANES, LANES), F32)],
        scratch_shapes=[pltpu.VMEM((1, LANES), F32), pltpu.VMEM((tm, D_GROUP), F32)],
        compiler_params=_params(2),
        name="inproj",
    )(x, *consts)


def _fox_kernel(fend_ref, qmax_ref, kmax_ref, q_ref, k_ref, vt_ref, o_ref,
                s_scr, p_scr, mx_scr, al_scr, m_scr, acc_scr, *, tq, tk):
    i = pl.program_id(1)
    nq = pl.num_programs(1)
    heads = list(range(N_HEADS))
    q = [q_ref[0, hd] for hd in heads]
    n_full = (i * tq) // tk

    def first_live_block(hd):
        bh = pl.program_id(0) * N_HEADS + hd
        qk_bound = FOX_NORM_MARGIN * 2.0 * qmax_ref[bh * nq + i] * kmax_ref[bh]
        f_tile = fend_ref[bh * nq + jnp.maximum(i - 1, 0)]

        def first_live(jj, first):
            j = n_full - 1 - jj
            f_end = fend_ref[bh * nq + (j + 1) * (tk // tq) - 1]
            live = qk_bound + (f_tile - f_end) + FOX_F_SLACK >= FOX_SKIP_LOG2
            return jnp.where(live, j, first)

        return lax.fori_loop(0, n_full, first_live, n_full)

    base = functools.reduce(jnp.minimum, [first_live_block(hd) for hd in heads])
    n_eff = n_full - base

    def key_rows(j):
        return pl.ds(pl.multiple_of(j * tk, tk), tk)

    def scores_to(j, slot):
        s = [_dot_nt(k_ref[0, hd, key_rows(j), :], q[hd]) for hd in heads]
        for hd in heads:
            s_scr[hd, slot] = s[hd]
            mx_scr[hd, slot] = jnp.max(s[hd], axis=0, keepdims=True)

    def weighted_values(j, slot):
        return [_dot(vt_ref[0, hd * V_AUG:(hd + 1) * V_AUG, key_rows(j)], p_scr[hd, slot])
                for hd in heads]

    def softmax_to(s, mx, slot):
        m_old = [m_scr[hd] for hd in heads]
        m_new = [jnp.maximum(a, b) for a, b in zip(m_old, mx)]
        p = [jnp.exp2(a - b).astype(BF16) for a, b in zip(s, m_new)]
        for hd in heads:
            m_scr[hd] = m_new[hd]
            al_scr[hd, slot] = jnp.exp2(m_old[hd] - m_new[hd])
            p_scr[hd, slot] = p[hd]

    def stage(local, cur):
        j = base + local
        nxt = 1 - cur
        pv_prev = weighted_values(jnp.maximum(j - 1, 0), nxt)
        scores_to(j + 1, nxt)
        softmax_to([s_scr[hd, cur] for hd in heads], [mx_scr[hd, cur] for hd in heads], cur)
        for hd in heads:
            acc_scr[hd] = al_scr[hd, nxt] * acc_scr[hd] + pv_prev[hd]

    def tail(cur):
        nxt = 1 - cur
        pv_prev = weighted_values(jnp.maximum(n_full - 1, 0), nxt)
        mask = n_full * tk + _iota((tk, tq), 0) <= i * tq + _iota((tk, tq), 1)
        s = [jnp.where(mask, s_scr[hd, cur], NEG_BIG) for hd in heads]
        softmax_to(s, [jnp.max(x, axis=0, keepdims=True) for x in s], cur)
        pv_last = weighted_values(n_full, cur)
        for hd in heads:
            acc = al_scr[hd, cur] * (al_scr[hd, nxt] * acc_scr[hd] + pv_prev[hd]) + pv_last[hd]
            o_ref[0, hd * HEAD_DIM:(hd + 1) * HEAD_DIM, :] = (
                acc[0:HEAD_DIM] / acc[HEAD_DIM:HEAD_DIM + 1])

    m_scr[...] = jnp.full(m_scr.shape, NEG_BIG, F32)
    acc_scr[...] = jnp.zeros_like(acc_scr)
    p_scr[:, 1] = jnp.zeros((N_HEADS, tk, tq), BF16)
    al_scr[:, 1] = jnp.ones((N_HEADS, 1, tq), F32)
    scores_to(base, 0)

    def pair(jj, carry):
        stage(2 * jj, 0)
        stage(2 * jj + 1, 1)
        return carry

    lax.fori_loop(0, n_eff // 2, pair, 0)

    @pl.when(n_eff % 2 == 1)
    def _():
        stage(n_eff - 1, 0)
        tail(1)

    @pl.when(n_eff % 2 == 0)
    def _():
        tail(0)


def _fox_attention(fend, qmax, kmax, q, k, vt, *, tq, tk):
    b, h, s, _ = q.shape
    grid_spec = pltpu.PrefetchScalarGridSpec(
        num_scalar_prefetch=3,
        grid=(b, s // tq),
        in_specs=[pl.BlockSpec((1, h, tq, HEAD_PAD), lambda bi, i, *_: (bi, 0, i, 0)),
                  pl.BlockSpec((1, h, s, HEAD_PAD), lambda bi, i, *_: (bi, 0, 0, 0)),
                  pl.BlockSpec((1, h * V_AUG, s), lambda bi, i, *_: (bi, 0, 0))],
        out_specs=pl.BlockSpec((1, h * HEAD_DIM, tq), lambda bi, i, *_: (bi, 0, i)),
        scratch_shapes=[pltpu.VMEM((h, 2, tk, tq), F32), pltpu.VMEM((h, 2, tk, tq), BF16),
                        pltpu.VMEM((h, 2, 1, tq), F32), pltpu.VMEM((h, 2, 1, tq), F32),
                        pltpu.VMEM((h, 1, tq), F32), pltpu.VMEM((h, V_AUG, tq), F32)])
    return pl.pallas_call(
        functools.partial(_fox_kernel, tq=tq, tk=tk),
        grid_spec=grid_spec,
        out_shape=jax.ShapeDtypeStruct((b, h * HEAD_DIM, s), F32),
        compiler_params=_params(2),
        name="fox_attention",
    )(fend, qmax, kmax, q, k, vt)


def _sb_kernel(q_ref, k_ref, vt_ref, o_ref, *, tq):
    i = pl.program_id(1)
    tk = tq
    heads = list(range(N_HEADS))
    q = [q_ref[0, hd] for hd in heads]
    later = (_iota((tk, tk), 0) < _iota((tk, tk), 1)).astype(BF16)

    def each(f, *xs):
        return [f(*a) for a in zip(*xs)]

    def block(j, rest_q, acc, masked):
        ks = pl.multiple_of(j * tk, tk)
        z = [_dot_nt(k_ref[0, hd, pl.ds(ks, tk), :], q[hd]) for hd in heads]
        log_keep = each(lambda x: _log_sigmoid(-x), z)
        if masked:
            mask = _iota((tk, tq), 0) < _iota((tk, tq), 1)
            log_keep = each(lambda x: jnp.where(mask, x, 0.0), log_keep)
        split = each(_split2, log_keep)
        rest_in = each(lambda hl: _dot(later, hl[0]) + _dot(later, hl[1]), split)
        att = each(lambda x, lk, ri, rq: jnp.exp(x + lk + ri + rq), z, log_keep, rest_in, rest_q)
        if masked:
            att = each(lambda x: jnp.where(mask, x, 0.0), att)
        pv = [_dot(vt_ref[0, hd * HEAD_DIM:(hd + 1) * HEAD_DIM, pl.ds(ks, tk)],
                   att[hd].astype(BF16)) for hd in heads]
        acc = each(lambda a, x: a + x, acc, pv)
        rest_q = each(lambda rq, ri, lk: rq + ri[0:1, :] + lk[0:1, :], rest_q, rest_in, log_keep)
        return rest_q, acc

    rest_q, acc = block(i, [jnp.zeros((1, tq), F32)] * N_HEADS,
                        [jnp.zeros((HEAD_DIM, tq), F32)] * N_HEADS, True)

    def cond(c):
        j, rest_q, _ = c
        alive = functools.reduce(jnp.maximum, rest_q)
        return jnp.logical_and(j >= 0, jnp.max(alive) > SB_SKIP_LOG)

    def body(c):
        j, rest_q, acc = c
        rest_q, acc = block(j, list(rest_q), list(acc), False)
        return j - 1, tuple(rest_q), tuple(acc)

    _, _, acc = lax.while_loop(cond, body, (i - 1, tuple(rest_q), tuple(acc)))
    o_ref[0] = jnp.concatenate(list(acc), axis=0)


def _sb_attention(q, k, vt, *, tq):
    b, h, s, _ = q.shape
    return pl.pallas_call(
        functools.partial(_sb_kernel, tq=tq),
        grid=(b, s // tq),
        in_specs=[pl.BlockSpec((1, h, tq, HEAD_PAD), lambda bi, i: (bi, 0, i, 0)),
                  pl.BlockSpec((1, h, s, HEAD_PAD), lambda bi, i: (bi, 0, 0, 0)),
                  pl.BlockSpec((1, h * HEAD_DIM, s), lambda bi, i: (bi, 0, 0))],
        out_specs=pl.BlockSpec((1, h * HEAD_DIM, tq), lambda bi, i: (bi, 0, i)),
        out_shape=jax.ShapeDtypeStruct((b, h * HEAD_DIM, s), F32),
        compiler_params=_params(2),
        name="sb_attention",
    )(q, k, vt)


def _rwkv_kernel(p_ref, m_ref, mu_rkv_ref, mu_misc_ref, w0_ref, w2_ref, a0_ref, a2_ref,
                 kk_ref, ka_ref, rk_ref, lng_ref, lnb_ref, bdm_ref, o_ref,
                 pad_rkv, pad_misc, r_s, k_s, v_s, kn_s, al_s, lw_s,
                 wt_s, u0_s, o0_s, mrb_s, rt_s, bh_s, kh_s, pc_s, y_s, ht_s, *, tt, nb):
    c = RWKV_CHUNK
    g = D_GROUP
    n_chunks = tt // c

    @pl.when(pl.program_id(0) == 0)
    def _():
        pad_rkv[:, 0:SUBLANES, :] = jnp.zeros((nb, SUBLANES, N_RKV), F32)
        pad_misc[:, 0:SUBLANES, :] = jnp.zeros((nb, SUBLANES, LANES), F32)
        ht_s[...] = jnp.zeros_like(ht_s)

    bdm = bdm_ref[...]
    bdm_f = bdm.astype(F32)

    for b in range(nb):
        p = p_ref[b]
        misc = m_ref[b]
        pad_rkv[b, SUBLANES:SUBLANES + tt, :] = p
        pad_misc[b, SUBLANES:SUBLANES + tt, :] = misc
        p_prev = pad_rkv[b, SUBLANES - 1:SUBLANES - 1 + tt, :]
        m_prev = pad_misc[b, SUBLANES - 1:SUBLANES - 1 + tt, :]
        pad_rkv[b, 0:SUBLANES, :] = p[tt - SUBLANES:tt, :]
        pad_misc[b, 0:SUBLANES, :] = misc[tt - SUBLANES:tt, :]
        p = p + (p_prev - p) * mu_rkv_ref[...]
        misc = misc + (m_prev - misc) * mu_misc_ref[...]
        k = p[:, g:2 * g]

        w = -_softplus(-(w0_ref[...] + _dot3(jnp.tanh(misc), w2_ref[...]))) - 0.5
        alpha = _sigmoid(a0_ref[...] + _dot3(misc, a2_ref[...]))
        kn = k * kk_ref[...]
        ss = _dot_sel_rhs(kn * kn, bdm)
        r_s[b] = p[:, 0:g]
        k_s[b] = k * (1.0 + (alpha - 1.0) * ka_ref[...])
        v_s[b] = p[:, 2 * g:3 * g]
        kn_s[b] = kn * lax.rsqrt(jnp.maximum(ss, 1e-12))
        al_s[b] = alpha
        lw_s[b] = -jnp.exp(w)

    row = _iota((c, g), 0)
    col = _iota((c, g), 1) % c
    strict = col < row
    incl = col <= row
    eye = (col == row).astype(F32)
    lower_c = (_iota((c, c), 1) <= _iota((c, c), 0)).astype(BF16)
    level_masks = []
    m = 1
    while m < c:
        level_masks.append(jnp.logical_and(
            strict, jnp.logical_and(row // (2 * m) == col // (2 * m), row // m != col // m)))
        m *= 2

    def bd(x):
        return jnp.concatenate([x.astype(BF16)] * N_HEADS, axis=0) * bdm

    def mm(a, b_bf16):
        return _dot(a.astype(BF16), b_bf16)

    def mm_nt(a, b_bf16):
        return _dot_nt(a.astype(BF16), b_bf16)

    def each(f, *xs):
        return [f(*a) for a in zip(*xs)]

    def phase_a(chains):
        sls = [pl.ds(pl.multiple_of(ci * c, c), c) for _, ci in chains]
        ld = lambda ref: [ref[b, sl, :] for (b, _), sl in zip(chains, sls)]
        r_c, k_c, v_c, kn_c, al_c, lw_c = (ld(s) for s in (r_s, k_s, v_s, kn_s, al_s, lw_s))
        cl = each(lambda x: _dot_sel_lhs(lower_c, x), lw_c)
        cl_last = each(lambda x: x[c - 1:c, :], cl)
        a_t = each(lambda kn, x, lw: -kn * jnp.exp(x - lw), kn_c, cl, lw_c)
        r_t = each(lambda r, x: r * jnp.exp(x), r_c, cl)
        q_inv = each(lambda x: jnp.exp(-x), cl)
        p_rem = each(lambda xl, x: jnp.exp(xl - x), cl_last, cl)
        kna = each(lambda kn, al: kn * al, kn_c, al_c)
        ar = each(lambda a, r: jnp.concatenate([a, r], axis=0), a_t, r_t)
        s_b = each(lambda x, kb, qi: mm_nt(x, bd(kb * qi)), ar, kna, q_inv)
        s_k = each(lambda x, kk, qi: mm_nt(x, bd(kk * qi)), ar, k_c, q_inv)
        n = each(lambda x: jnp.where(strict, x[0:c], 0.0), s_b)
        a_ak = each(lambda x: jnp.where(strict, x[0:c], 0.0), s_k)
        m_rb = each(lambda x: jnp.where(incl, x[c:2 * c], 0.0), s_b)
        m_rk = each(lambda x: jnp.where(incl, x[c:2 * c], 0.0), s_k)

        inv = each(lambda x: eye + jnp.where(level_masks[0], x, 0.0), n)
        for lm in level_masks[1:]:
            half = each(lambda d, x: mm(d, bd(jnp.where(lm, x, 0.0))), inv, n)
            inv = each(lambda d, hf: d + mm(hf, bd(d)), inv, half)

        v_bd = each(bd, v_c)
        akv = each(mm, a_ak, v_bd)
        wt = each(lambda d, a: mm(d, bd(a)), inv, a_t)
        u0 = each(lambda d, x: mm(d, bd(x)), inv, akv)
        o0 = each(mm, m_rk, v_bd)
        bh = each(lambda x, p: x * p, kna, p_rem)
        kh = each(lambda x, p: x * p, k_c, p_rem)
        for ref, vals in zip((wt_s, u0_s, o0_s, mrb_s, rt_s, bh_s, kh_s),
                             (wt, u0, o0, m_rb, r_t, bh, kh)):
            for (b, _), sl, val in zip(chains, sls, vals):
                ref[b, sl, :] = val
        for (b, ci), xl in zip(chains, cl_last):
            pc_s[b, pl.ds(pl.multiple_of(ci * SUBLANES, SUBLANES), SUBLANES), :] = (
                jnp.broadcast_to(jnp.exp(xl), (SUBLANES, g)))

    def phase_b(ci):
        sl = pl.ds(pl.multiple_of(ci * c, c), c)
        bs = list(range(nb))
        ht = [ht_s[b] for b in bs]
        wr = [jnp.concatenate([wt_s[b, sl, :], rt_s[b, sl, :]], axis=0) for b in bs]
        wrh = each(lambda x, hh: mm_nt(x, hh.astype(BF16)), wr, ht)
        u = [x[0:c] + u0_s[b, sl, :] for x, b in zip(wrh, bs)]
        uv_t = [jnp.concatenate([x, v_s[b, sl, :]], axis=0).T for x, b in zip(u, bs)]
        bk = [jnp.concatenate([bh_s[b, sl, :], kh_s[b, sl, :]], axis=0) for b in bs]
        upd = each(lambda x, y: mm(x, y.astype(BF16)), uv_t, bk)
        mu_ = [mm(mrb_s[b, sl, :], bd(x)) for x, b in zip(u, bs)]
        for b in bs:
            p_c = pc_s[b, pl.ds(pl.multiple_of(ci * SUBLANES, SUBLANES), 1), :]
            y_s[b, sl, :] = wrh[b][c:2 * c] + mu_[b] + o0_s[b, sl, :]
            ht_s[b] = ht[b] * p_c + bdm_f * upd[b]

    def loop_a(ci, carry):
        phase_a([(b, ci * RWKV_UNROLL + j) for j in range(RWKV_UNROLL) for b in range(nb)])
        return carry

    def loop_b(ci, carry):
        phase_b(ci)
        return carry

    lax.fori_loop(0, n_chunks // RWKV_UNROLL, loop_a, 0)
    lax.fori_loop(0, n_chunks, loop_b, 0)

    inv_n = 1.0 / HEAD_DIM
    for b in range(nb):
        y = y_s[b]
        mean = _dot_sel_rhs(y, bdm) * inv_n
        yc = y - mean
        var = _dot_sel_rhs(yc * yc, bdm) * inv_n
        yn = yc * lax.rsqrt(var + GN_EPS) * lng_ref[...] + lnb_ref[...]
        bonus = _dot_sel_rhs(r_s[b] * k_s[b] * rk_ref[...], bdm) * v_s[b]
        o_ref[b] = yn + bonus


def _rwkv(p_rkv, misc, prm, bdm, *, tt):
    b, s, _ = p_rkv.shape
    g = D_GROUP
    seq = lambda width: pl.BlockSpec((b, tt, width), lambda ti: (0, ti, 0))
    big = pltpu.VMEM((b, tt, g), F32)
    return pl.pallas_call(
        functools.partial(_rwkv_kernel, tt=tt, nb=b),
        grid=(s // tt,),
        in_specs=[seq(N_RKV), seq(LANES)] + [_full_spec(a) for a in prm] + [_full_spec(bdm)],
        out_specs=seq(g),
        out_shape=jax.ShapeDtypeStruct((b, s, g), F32),
        scratch_shapes=[pltpu.VMEM((b, tt + SUBLANES, N_RKV), F32),
                        pltpu.VMEM((b, tt + SUBLANES, LANES), F32)]
        + [big] * 13
        + [pltpu.VMEM((b, tt // RWKV_CHUNK * SUBLANES, g), F32), big,
           pltpu.VMEM((b, g, g), F32)],
        compiler_params=_params(1),
        name="rwkv7",
    )(p_rkv, misc, *prm, bdm)


def _lru_kernel(x_ref, cw_ref, cb_ref, wa_ref, ba_ref, wx_ref, bx_ref, lam_ref, o_ref,
                pad, a_s, u_s, h_s, *, tt, nb):
    @pl.when(pl.program_id(0) == 0)
    def _():
        pad[:, 0:SUBLANES, :] = jnp.zeros((nb, SUBLANES, D_GROUP), F32)
        h_s[...] = jnp.zeros_like(h_s)

    for b in range(nb):
        x = x_ref[b]
        pad[b, SUBLANES:SUBLANES + tt, :] = x
        xc = cw_ref[CONV_WIDTH - 1:CONV_WIDTH, :] * x + cb_ref[...]
        for d in range(1, CONV_WIDTH):
            tap = CONV_WIDTH - 1 - d
            xc = xc + cw_ref[tap:tap + 1, :] * pad[b, SUBLANES - d:SUBLANES - d + tt, :]
        pad[b, 0:SUBLANES, :] = x[tt - SUBLANES:tt, :]

        xb = xc.astype(BF16)
        r = _sigmoid(_dot(xb, wa_ref[...]) + ba_ref[...])
        i = _sigmoid(_dot(xb, wx_ref[...]) + bx_ref[...])
        log_a = -LRU_C * r * _softplus(-lam_ref[...])
        a_s[b] = jnp.exp(log_a)
        th = jnp.tanh(log_a)
        u_s[b] = jnp.sqrt(-2.0 * th / (1.0 - th)) * (i * xc)

    def group(gi, hs):
        base = pl.multiple_of(gi * SUBLANES, SUBLANES)
        a8 = [a_s[b, pl.ds(base, SUBLANES), :] for b in range(nb)]
        u8 = [u_s[b, pl.ds(base, SUBLANES), :] for b in range(nb)]
        hs = list(hs)
        rows = [[] for _ in range(nb)]
        for j in range(SUBLANES):
            for b in range(nb):
                hs[b] = a8[b][j:j + 1, :] * hs[b] + u8[b][j:j + 1, :]
                rows[b].append(hs[b])
        for b in range(nb):
            o_ref[b, pl.ds(base, SUBLANES), :] = jnp.concatenate(rows[b], axis=0)
        return tuple(hs)

    hs = lax.fori_loop(0, tt // SUBLANES, group, tuple(h_s[b] for b in range(nb)))
    for b in range(nb):
        h_s[b] = hs[b]


def _lru(x, prm, *, tt):
    b, s, g = x.shape
    seq = pl.BlockSpec((b, tt, g), lambda ti: (0, ti, 0))
    return pl.pallas_call(
        functools.partial(_lru_kernel, tt=tt, nb=b),
        grid=(s // tt,),
        in_specs=[seq] + [_full_spec(a) for a in prm],
        out_specs=seq,
        out_shape=jax.ShapeDtypeStruct((b, s, g), F32),
        scratch_shapes=[pltpu.VMEM((b, tt + SUBLANES, g), F32), pltpu.VMEM((b, tt, g), F32),
                        pltpu.VMEM((b, tt, g), F32), pltpu.VMEM((b, 1, g), F32)],
        compiler_params=_params(1),
        name="rg_lru",
    )(x, *prm)


def _outproj_kernel(x_ref, yft_ref, yst_ref, yr_ref, yl_ref, gates_ref, w_ref, fg_ref, o_ref, *,
                    final):
    acc = x_ref[0]
    ys = (yft_ref[0].T, yst_ref[0].T, yr_ref[0], yl_ref[0])
    for gi, y in enumerate(ys):
        gate = gates_ref[0, :, gi * D_GROUP:(gi + 1) * D_GROUP]
        y = y * (gate * _sigmoid(gate))
        acc = acc + _dot(y.astype(BF16), w_ref[gi * D_GROUP:(gi + 1) * D_GROUP, :])
    if final:
        ms = jnp.mean(acc * acc, axis=-1, keepdims=True)
        acc = acc * lax.rsqrt(ms + RMS_EPS) * fg_ref[...]
    o_ref[0] = acc


def _outproj(x, y_fox_t, y_sb_t, y_rw, y_lru, gates, w, final_g, *, tm, final):
    b, s, _ = x.shape
    seq = lambda width: pl.BlockSpec((1, tm, width), lambda bi, i: (bi, i, 0))
    chan = pl.BlockSpec((1, D_GROUP, tm), lambda bi, i: (bi, 0, i))
    return pl.pallas_call(
        functools.partial(_outproj_kernel, final=final),
        grid=(b, s // tm),
        in_specs=[seq(D_MODEL), chan, chan, seq(D_GROUP), seq(D_GROUP), seq(4 * D_GROUP),
                  _full_spec(w), _full_spec(final_g)],
        out_specs=seq(D_MODEL),
        out_shape=jax.ShapeDtypeStruct((b, s, D_MODEL), F32),
        compiler_params=_params(2),
        name="outproj",
    )(x, y_fox_t, y_sb_t, y_rw, y_lru, gates, w, final_g)


def _w_in_segments():
    g, h, r, dh = D_GROUP, N_HEADS, RWKV_LORA, HEAD_DIM
    o_ff = 4 * g
    o_sb = o_ff + h
    o_rw = o_sb + 4 * g
    o_rg = o_rw + 3 * g + 2 * r
    o_lx = o_rg + g
    o_lg = o_lx + g
    segs = []
    for src, dst in ((0, C_FQ), (g, C_FK), (o_sb, C_SQ), (o_sb + g, C_SK)):
        segs += [(src + hd * dh, dst + hd * HEAD_PAD, dh) for hd in range(h)]
    segs += [(2 * g, C_FV, g), (o_sb + 2 * g, C_SV, g),
             (3 * g, C_GATES, g), (o_sb + 3 * g, C_GATES + g, g), (o_rg, C_GATES + 2 * g, g),
             (o_lg, C_GATES + 3 * g, g),
             (o_rw, C_RKV, 3 * g), (o_lx, C_LX, g),
             (o_rw + 3 * g, C_MISC, 2 * r), (o_ff, C_MISC + FF_LANE, h)]
    return segs


def _w_in_kernel(w_ref, o_ref):
    o_ref[...] = jnp.zeros_like(o_ref)
    for src, dst, width in _w_in_segments():
        o_ref[0, :, dst:dst + width] = w_ref[0, :, src:src + width].astype(BF16)


def _permute_w_in(w_in, *, tr):
    depth, d, n_in = w_in.shape
    return pl.pallas_call(
        _w_in_kernel,
        grid=(depth, d // tr),
        in_specs=[pl.BlockSpec((1, tr, n_in), lambda l, i: (l, i, 0))],
        out_specs=pl.BlockSpec((1, tr, N_PROJ), lambda l, i: (l, i, 0)),
        out_shape=jax.ShapeDtypeStruct((depth, d, N_PROJ), BF16),
        compiler_params=_params(2),
        name="w_in_layout",
    )(w_in)


def _block_diag(w):
    h, n, _ = w.shape
    eye = jnp.eye(h, dtype=w.dtype)
    return jnp.einsum('hij,hk->hikj', w, eye).reshape(h * n, h * n)


def _pick_tile(s, pref):
    t = pref
    while s % t:
        t //= 2
    return t


def kernel(x, norm_g, w_in, b_forget, rwkv_mu, rwkv_w0, rwkv_w2, rwkv_a0, rwkv_a2, rwkv_k_k,
           rwkv_k_a, rwkv_r_k, rwkv_ln_g, rwkv_ln_b, lru_conv_w, lru_conv_b, lru_w_a, lru_b_a,
           lru_w_x, lru_b_x, lru_lambda, w_out, final_g):
    b, s, d = x.shape
    depth = w_in.shape[0]
    g, h, dh, r = D_GROUP, N_HEADS, HEAD_DIM, RWKV_LORA
    tm = _pick_tile(s, 512)
    tq = _pick_tile(s, 256)
    tk_fox = _pick_tile(s, 2 * tq)
    tt = _pick_tile(s, 512)
    row = lambda a: a.reshape(1, -1).astype(F32)

    bdm = _block_diag(jnp.ones((h, dh, dh), BF16))
    w_in_k = _permute_w_in(w_in, tr=LANES)
    for l in range(depth):
        fbias = jnp.zeros((1, LANES), F32).at[0, FF_LANE:FF_LANE + h].set(b_forget[l])
        consts = [row(norm_g[l]), w_in_k, fbias]
        fq, fk, fvt, sq, sk, svt, gates, rkv, lx, misc, f2, norms = _inproj(x, consts, l, tm=tm)

        per_head = lambda a: a.transpose(0, 2, 1).reshape(-1)
        fend = per_head(f2[:, tq - 1::tq, FF_LANE:FF_LANE + h])
        qmax = per_head(jnp.repeat(jnp.sqrt(norms[:, :, 0, 0:h]), tm // tq, axis=1))
        kmax = jnp.sqrt(jnp.max(norms[:, :, 0, h:2 * h], axis=1)).reshape(-1)
        y_fox_t = _fox_attention(fend, qmax, kmax, fq, fk, fvt, tq=tq, tk=tk_fox)
        y_sb_t = _sb_attention(sq, sk, svt, tq=tq)

        mu = rwkv_mu[l]
        pad_rows = lambda a, lo: jnp.zeros((LANES, g), F32).at[lo:lo + r].set(a)
        rw_prm = [row(mu[:N_RKV]),
                  jnp.zeros((1, LANES), F32).at[0, :2 * r].set(mu[N_RKV:]),
                  row(rwkv_w0[l]), pad_rows(rwkv_w2[l], 0), row(rwkv_a0[l]), pad_rows(rwkv_a2[l], r),
                  row(rwkv_k_k[l]), row(rwkv_k_a[l]), row(rwkv_r_k[l]), row(rwkv_ln_g[l]),
                  row(rwkv_ln_b[l])]
        y_rw = _rwkv(rkv, misc, rw_prm, bdm, tt=tt)

        lru_prm = [lru_conv_w[l].astype(F32), row(lru_conv_b[l]),
                   _block_diag(lru_w_a[l]).astype(BF16), row(lru_b_a[l]),
                   _block_diag(lru_w_x[l]).astype(BF16), row(lru_b_x[l]), row(lru_lambda[l])]
        y_lru = _lru(lx, lru_prm, tt=tt)

        x = _outproj(x, y_fox_t, y_sb_t, y_rw, y_lru, gates, w_out[l].astype(BF16), row(final_g),
                     tm=tm, final=(l == depth - 1))
    return x
```

```python
import functools

import jax
import jax.numpy as jnp
from jax import lax
from jax.experimental import pallas as pl
from jax.experimental.pallas import tpu as pltpu

F32 = jnp.float32
BF16 = jnp.bfloat16

D_MODEL = 1024
D_GROUP = 256
N_HEADS = 4
HEAD_DIM = 64
RWKV_LORA = 32
CONV_WIDTH = 4
LRU_C = 8.0
RMS_EPS = 1e-6
GN_EPS = 64e-5
N_RKV = 3 * D_GROUP

LANES = 128
SUBLANES = 8
VMEM_LIMIT_BYTES = 56 * 1024 * 1024

HEAD_PAD = LANES
QK_COLS = N_HEADS * HEAD_PAD
C_FQ = 0
C_FK = C_FQ + QK_COLS
C_SQ = C_FK + QK_COLS
C_SK = C_SQ + QK_COLS
C_FV = C_SK + QK_COLS
C_SV = C_FV + D_GROUP
C_GATES = C_SV + D_GROUP
C_RKV = C_GATES + 4 * D_GROUP
C_LX = C_RKV + N_RKV
C_MISC = C_LX + D_GROUP
N_PROJ = C_MISC + LANES
FF_LANE = 2 * RWKV_LORA
AUG_K_F = HEAD_DIM
AUG_K_ONE = HEAD_DIM + 3
AUG_Q_ONE = HEAD_DIM
AUG_Q_F = HEAD_DIM + 3

CUMSUM_BLOCK = LANES
V_AUG = HEAD_DIM + 16
LOG2E = 1.4426950408889634

FOX_SKIP_LOG2 = -150.0
FOX_NORM_MARGIN = 1.02
FOX_F_SLACK = 1.0

NEG_BIG = -1e30
SB_SKIP_LOG2 = -150.0
RWKV_CHUNK = 64
RWKV_UNROLL = 4


def _dot(a, b):
    return jnp.dot(a, b, preferred_element_type=F32)


def _dot_nt(a, b):
    return lax.dot_general(a, b, (((1,), (1,)), ((), ())), preferred_element_type=F32)


def _split2(x):
    hi = x.astype(BF16)
    lo = (x - hi.astype(F32)).astype(BF16)
    return hi, lo


def _split3(x):
    hi = x.astype(BF16)
    r1 = x - hi.astype(F32)
    mid = r1.astype(BF16)
    lo = (r1 - mid.astype(F32)).astype(BF16)
    return hi, mid, lo


def _dot3(a, b):
    ah, al = _split2(a)
    bh, bl = _split2(b)
    return _dot(ah, bh) + (_dot(al, bh) + _dot(ah, bl))


def _dot_sel_lhs(sel, x):
    hi, mid, lo = _split3(x)
    return _dot(sel, hi) + (_dot(sel, mid) + _dot(sel, lo))


def _dot_sel_rhs(x, sel):
    hi, lo = _split2(x)
    return _dot(hi, sel) + _dot(lo, sel)


def _softplus(x):
    return jnp.maximum(x, 0.0) + jnp.log(1.0 + jnp.exp(-jnp.abs(x)))


def _log_sigmoid(x):
    return jnp.minimum(x, 0.0) - jnp.log(1.0 + jnp.exp(-jnp.abs(x)))


def _log2_sigmoid_of_log2(x2):
    return jnp.minimum(x2, 0.0) - jnp.log2(1.0 + jnp.exp2(-jnp.abs(x2)))


def _sigmoid(x):
    return 1.0 / (1.0 + jnp.exp(-x))


def _iota(shape, dim):
    return lax.broadcasted_iota(jnp.int32, shape, dim)


def _full_spec(a):
    return pl.BlockSpec(a.shape, lambda *_: (0,) * a.ndim)


def _params(n_grid):
    return pltpu.CompilerParams(dimension_semantics=("arbitrary",) * n_grid,
                                vmem_limit_bytes=VMEM_LIMIT_BYTES)


def _inproj_kernel(x_ref, g_ref, w_ref, fb_ref, mu_rkv_ref, mu_misc_ref,
                   fq_ref, fk_ref, fvt_ref, sq_ref, sk_ref, svt_ref, gates_ref, rkv_ref, lx_ref,
                   misc_ref, f2_ref, norms_ref, ftot, v_stage, pad_rkv, pad_misc):
    @pl.when(pl.program_id(1) == 0)
    def _():
        ftot[...] = jnp.zeros_like(ftot)
        pad_rkv[0:SUBLANES, :] = jnp.zeros((SUBLANES, N_RKV), F32)
        pad_misc[0:SUBLANES, :] = jnp.zeros((SUBLANES, LANES), F32)

    x = x_ref[0]
    tm = x.shape[0]
    ms = jnp.mean(x * x, axis=-1, keepdims=True)
    h = (x * lax.rsqrt(ms + RMS_EPS) * g_ref[...]).astype(BF16)
    scale = HEAD_DIM ** -0.5

    def proj(c0, width):
        return _dot(h, w_ref[0, :, c0:c0 + width])

    misc = proj(C_MISC, LANES)
    gates_ref[0] = proj(C_GATES, 4 * D_GROUP)
    lx_ref[0] = proj(C_LX, D_GROUP)

    for val, pad, mu_ref, out_ref in ((proj(C_RKV, N_RKV), pad_rkv, mu_rkv_ref, rkv_ref),
                                      (misc, pad_misc, mu_misc_ref, misc_ref)):
        pad[SUBLANES:SUBLANES + tm, :] = val
        prev = pad[SUBLANES - 1:SUBLANES - 1 + tm, :]
        pad[0:SUBLANES, :] = val[tm - SUBLANES:tm, :]
        out_ref[0] = val + (prev - val) * mu_ref[...]

    lf = _log_sigmoid(misc + fb_ref[...])
    cb = CUMSUM_BLOCK
    lower = (_iota((cb, cb), 1) <= _iota((cb, cb), 0)).astype(BF16)
    run = ftot[...]
    blocks = []
    for r0 in range(0, tm, cb):
        blk = _dot_sel_lhs(lower, lf[r0:r0 + cb, :]) + run
        run = blk[cb - 1:cb, :]
        blocks.append(blk)
    ftot[...] = run
    f = jnp.concatenate(blocks, axis=0)
    f2 = f * LOG2E
    f2_ref[0] = f2
    hi = f2.astype(BF16).astype(F32)
    mid = (f2 - hi).astype(BF16).astype(F32)
    lo = (f2 - hi - mid).astype(BF16).astype(F32)

    pq = proj(C_FQ, QK_COLS) * (scale * LOG2E)
    pk = proj(C_FK, QK_COLS)
    sq = proj(C_SQ, QK_COLS) * (scale * LOG2E)
    sk = proj(C_SK, QK_COLS)
    lane = _iota((tm, LANES), 1)
    ones_q = jnp.where(jnp.logical_and(lane >= AUG_Q_ONE, lane < AUG_Q_ONE + 3), 1.0, 0.0)
    ones_k = jnp.where(jnp.logical_and(lane >= AUG_K_ONE, lane < AUG_K_ONE + 3), 1.0, 0.0)
    nlane = _iota((SUBLANES, LANES), 1)
    norms = jnp.zeros((SUBLANES, LANES), F32)
    for hd in range(N_HEADS):
        cols = slice(hd * HEAD_PAD, (hd + 1) * HEAD_PAD)
        aug_q, aug_k = ones_q, ones_k
        for i3, piece in enumerate((hi, mid, lo)):
            col = jnp.broadcast_to(piece[:, FF_LANE + hd:FF_LANE + hd + 1], (tm, LANES))
            aug_q = jnp.where(lane == AUG_Q_F + i3, col, aug_q)
            aug_k = jnp.where(lane == AUG_K_F + i3, -col, aug_k)
        fq_ref[0, hd] = (pq[:, cols] + aug_q).astype(BF16)
        fk_ref[0, hd] = (pk[:, cols] + aug_k).astype(BF16)
        sq_ref[0, hd] = sq[:, cols].astype(BF16)
        sk_ref[0, hd] = sk[:, cols].astype(BF16)
        for side, pv in enumerate((pq, pk)):
            blk = pv[:, cols]
            top = jnp.max(jnp.sum(blk * blk, axis=-1, keepdims=True), axis=0, keepdims=True)
            norms = jnp.where(nlane == side * N_HEADS + hd, top, norms)
    norms_ref[0, 0] = norms
    v_stage[...] = proj(C_SV, D_GROUP)
    svt_ref[0] = v_stage[...].T.astype(BF16)
    v_stage[...] = proj(C_FV, D_GROUP)
    vt = v_stage[...].T.astype(BF16)
    ones_rows = (_iota((V_AUG - HEAD_DIM, tm), 0) == 0).astype(BF16)
    for hd in range(N_HEADS):
        fvt_ref[0, hd * V_AUG:hd * V_AUG + HEAD_DIM, :] = vt[hd * HEAD_DIM:(hd + 1) * HEAD_DIM, :]
        fvt_ref[0, hd * V_AUG + HEAD_DIM:(hd + 1) * V_AUG, :] = ones_rows


def _inproj(x, consts, layer, *, tm):
    b, s, _ = x.shape
    w_spec = pl.BlockSpec((1,) + consts[1].shape[1:], lambda bi, i: (layer, 0, 0))
    seq = lambda width: pl.BlockSpec((1, tm, width), lambda bi, i: (bi, i, 0))
    heads = pl.BlockSpec((1, N_HEADS, tm, HEAD_PAD), lambda bi, i: (bi, 0, i, 0))
    chan = pl.BlockSpec((1, D_GROUP, tm), lambda bi, i: (bi, 0, i))
    heads_shape = jax.ShapeDtypeStruct((b, N_HEADS, s, HEAD_PAD), BF16)
    chan_shape = jax.ShapeDtypeStruct((b, D_GROUP, s), BF16)
    chan_aug = pl.BlockSpec((1, N_HEADS * V_AUG, tm), lambda bi, i: (bi, 0, i))
    chan_aug_shape = jax.ShapeDtypeStruct((b, N_HEADS * V_AUG, s), BF16)
    seq_shape = lambda width: jax.ShapeDtypeStruct((b, s, width), F32)
    return pl.pallas_call(
        _inproj_kernel,
        grid=(b, s // tm),
        in_specs=[seq(D_MODEL), _full_spec(consts[0]), w_spec] + [_full_spec(a) for a in consts[2:]],
        out_specs=[heads, heads, chan_aug, heads, heads, chan,
                   seq(4 * D_GROUP), seq(N_RKV), seq(D_GROUP), seq(LANES), seq(LANES),
                   pl.BlockSpec((1, 1, SUBLANES, LANES), lambda bi, i: (bi, i, 0, 0))],
        out_shape=[heads_shape, heads_shape, chan_aug_shape, heads_shape, heads_shape, chan_shape,
                   seq_shape(4 * D_GROUP), seq_shape(N_RKV), seq_shape(D_GROUP), seq_shape(LANES),
                   seq_shape(LANES), jax.ShapeDtypeStruct((b, s // tm, SUBLANES, LANES), F32)],
        scratch_shapes=[pltpu.VMEM((1, LANES), F32), pltpu.VMEM((tm, D_GROUP), F32),
                        pltpu.VMEM((tm + SUBLANES, N_RKV), F32),
                        pltpu.VMEM((tm + SUBLANES, LANES), F32)],
        compiler_params=_params(2),
        name="inproj",
    )(x, *consts)


def _fox_kernel(fend_ref, qmax_ref, kmax_ref, q_ref, k_ref, vt_ref, o_ref,
                s_scr, p_scr, mx_scr, al_scr, m_scr, acc_scr, *, tq, tk):
    i = pl.program_id(1)
    nq = pl.num_programs(1)
    heads = list(range(N_HEADS))
    q = [q_ref[0, hd] for hd in heads]
    n_full = (i * tq) // tk

    def first_live_block(hd):
        bh = pl.program_id(0) * N_HEADS + hd
        qk_bound = FOX_NORM_MARGIN * 2.0 * qmax_ref[bh * nq + i] * kmax_ref[bh]
        f_tile = fend_ref[bh * nq + jnp.maximum(i - 1, 0)]

        def first_live(jj, first):
            j = n_full - 1 - jj
            f_end = fend_ref[bh * nq + (j + 1) * (tk // tq) - 1]
            live = qk_bound + (f_tile - f_end) + FOX_F_SLACK >= FOX_SKIP_LOG2
            return jnp.where(live, j, first)

        return lax.fori_loop(0, n_full, first_live, n_full)

    base = functools.reduce(jnp.minimum, [first_live_block(hd) for hd in heads])
    n_eff = n_full - base

    def key_rows(j):
        return pl.ds(pl.multiple_of(j * tk, tk), tk)

    def scores_to(j, slot):
        s = [_dot_nt(k_ref[0, hd, key_rows(j), :], q[hd]) for hd in heads]
        for hd in heads:
            s_scr[hd, slot] = s[hd]
            mx_scr[hd, slot] = jnp.max(s[hd], axis=0, keepdims=True)

    def weighted_values(j, slot):
        return [_dot(vt_ref[0, hd * V_AUG:(hd + 1) * V_AUG, key_rows(j)], p_scr[hd, slot])
                for hd in heads]

    def softmax_to(s, mx, slot):
        m_old = [m_scr[hd] for hd in heads]
        m_new = [jnp.maximum(a, b) for a, b in zip(m_old, mx)]
        p = [jnp.exp2(a - b).astype(BF16) for a, b in zip(s, m_new)]
        for hd in heads:
            m_scr[hd] = m_new[hd]
            al_scr[hd, slot] = jnp.exp2(m_old[hd] - m_new[hd])
            p_scr[hd, slot] = p[hd]

    def stage(local, cur):
        j = base + local
        nxt = 1 - cur
        pv_prev = weighted_values(jnp.maximum(j - 1, 0), nxt)
        scores_to(j + 1, nxt)
        softmax_to([s_scr[hd, cur] for hd in heads], [mx_scr[hd, cur] for hd in heads], cur)
        for hd in heads:
            acc_scr[hd] = al_scr[hd, nxt] * acc_scr[hd] + pv_prev[hd]

    def tail(cur):
        nxt = 1 - cur
        pv_prev = weighted_values(jnp.maximum(n_full - 1, 0), nxt)
        mask = n_full * tk + _iota((tk, tq), 0) <= i * tq + _iota((tk, tq), 1)
        s = [jnp.where(mask, s_scr[hd, cur], NEG_BIG) for hd in heads]
        softmax_to(s, [jnp.max(x, axis=0, keepdims=True) for x in s], cur)
        pv_last = weighted_values(n_full, cur)
        for hd in heads:
            acc = al_scr[hd, cur] * (al_scr[hd, nxt] * acc_scr[hd] + pv_prev[hd]) + pv_last[hd]
            o_ref[0, hd * HEAD_DIM:(hd + 1) * HEAD_DIM, :] = (
                acc[0:HEAD_DIM] / acc[HEAD_DIM:HEAD_DIM + 1])

    m_scr[...] = jnp.full(m_scr.shape, NEG_BIG, F32)
    acc_scr[...] = jnp.zeros_like(acc_scr)
    p_scr[:, 1] = jnp.zeros((N_HEADS, tk, tq), BF16)
    al_scr[:, 1] = jnp.ones((N_HEADS, 1, tq), F32)
    scores_to(base, 0)

    def pair(jj, carry):
        stage(2 * jj, 0)
        stage(2 * jj + 1, 1)
        return carry

    lax.fori_loop(0, n_eff // 2, pair, 0)

    @pl.when(n_eff % 2 == 1)
    def _():
        stage(n_eff - 1, 0)
        tail(1)

    @pl.when(n_eff % 2 == 0)
    def _():
        tail(0)


def _fox_attention(fend, qmax, kmax, q, k, vt, *, tq, tk):
    b, h, s, _ = q.shape
    grid_spec = pltpu.PrefetchScalarGridSpec(
        num_scalar_prefetch=3,
        grid=(b, s // tq),
        in_specs=[pl.BlockSpec((1, h, tq, HEAD_PAD), lambda bi, i, *_: (bi, 0, i, 0)),
                  pl.BlockSpec((1, h, s, HEAD_PAD), lambda bi, i, *_: (bi, 0, 0, 0)),
                  pl.BlockSpec((1, h * V_AUG, s), lambda bi, i, *_: (bi, 0, 0))],
        out_specs=pl.BlockSpec((1, h * HEAD_DIM, tq), lambda bi, i, *_: (bi, 0, i)),
        scratch_shapes=[pltpu.VMEM((h, 2, tk, tq), F32), pltpu.VMEM((h, 2, tk, tq), BF16),
                        pltpu.VMEM((h, 2, 1, tq), F32), pltpu.VMEM((h, 2, 1, tq), F32),
                        pltpu.VMEM((h, 1, tq), F32), pltpu.VMEM((h, V_AUG, tq), F32)])
    return pl.pallas_call(
        functools.partial(_fox_kernel, tq=tq, tk=tk),
        grid_spec=grid_spec,
        out_shape=jax.ShapeDtypeStruct((b, h * HEAD_DIM, s), F32),
        compiler_params=_params(2),
        name="fox_attention",
    )(fend, qmax, kmax, q, k, vt)


def _sb_kernel(q_ref, k_ref, vt_ref, o_ref, *, tq):
    i = pl.program_id(1)
    tk = tq
    heads = list(range(N_HEADS))
    q = [q_ref[0, hd] for hd in heads]
    later = (_iota((tk, tk), 0) < _iota((tk, tk), 1)).astype(BF16)

    def each(f, *xs):
        return [f(*a) for a in zip(*xs)]

    def block(j, rest_q, acc, masked):
        ks = pl.multiple_of(j * tk, tk)
        z = [_dot_nt(k_ref[0, hd, pl.ds(ks, tk), :], q[hd]) for hd in heads]
        log_keep = each(lambda x: _log2_sigmoid_of_log2(-x), z)
        if masked:
            mask = _iota((tk, tq), 0) < _iota((tk, tq), 1)
            log_keep = each(lambda x: jnp.where(mask, x, 0.0), log_keep)
        split = each(_split2, log_keep)
        rest_in = each(lambda hl: _dot(later, hl[0]) + _dot(later, hl[1]), split)
        att = each(lambda x, lk, ri, rq: jnp.exp2(x + lk + ri + rq), z, log_keep, rest_in, rest_q)
        if masked:
            att = each(lambda x: jnp.where(mask, x, 0.0), att)
        pv = [_dot(vt_ref[0, hd * HEAD_DIM:(hd + 1) * HEAD_DIM, pl.ds(ks, tk)],
                   att[hd].astype(BF16)) for hd in heads]
        acc = each(lambda a, x: a + x, acc, pv)
        rest_q = each(lambda rq, ri, lk: rq + ri[0:1, :] + lk[0:1, :], rest_q, rest_in, log_keep)
        return rest_q, acc

    rest_q, acc = block(i, [jnp.zeros((1, tq), F32)] * N_HEADS,
                        [jnp.zeros((HEAD_DIM, tq), F32)] * N_HEADS, True)

    def cond(c):
        j, rest_q, _ = c
        alive = functools.reduce(jnp.maximum, rest_q)
        return jnp.logical_and(j >= 0, jnp.max(alive) > SB_SKIP_LOG2)

    def body(c):
        j, rest_q, acc = c
        rest_q, acc = block(j, list(rest_q), list(acc), False)
        return j - 1, tuple(rest_q), tuple(acc)

    _, _, acc = lax.while_loop(cond, body, (i - 1, tuple(rest_q), tuple(acc)))
    o_ref[0] = jnp.concatenate(list(acc), axis=0)


def _sb_attention(q, k, vt, *, tq):
    b, h, s, _ = q.shape
    return pl.pallas_call(
        functools.partial(_sb_kernel, tq=tq),
        grid=(b, s // tq),
        in_specs=[pl.BlockSpec((1, h, tq, HEAD_PAD), lambda bi, i: (bi, 0, i, 0)),
                  pl.BlockSpec((1, h, s, HEAD_PAD), lambda bi, i: (bi, 0, 0, 0)),
                  pl.BlockSpec((1, h * HEAD_DIM, s), lambda bi, i: (bi, 0, 0))],
        out_specs=pl.BlockSpec((1, h * HEAD_DIM, tq), lambda bi, i: (bi, 0, i)),
        out_shape=jax.ShapeDtypeStruct((b, h * HEAD_DIM, s), F32),
        compiler_params=_params(2),
        name="sb_attention",
    )(q, k, vt)


def _rwkv_kernel(p_ref, m_ref, w0_ref, w2_ref, a0_ref, a2_ref,
                 kk_ref, ka_ref, rk_ref, lng_ref, lnb_ref, bdm_ref, o_ref,
                 r_s, k_s, v_s, kn_s, al_s, lw_s,
                 wt_s, u0_s, o0_s, mrb_s, rt_s, bh_s, kh_s, pc_s, y_s, ht_s, *, tt, nb):
    c = RWKV_CHUNK
    g = D_GROUP
    n_chunks = tt // c

    @pl.when(pl.program_id(0) == 0)
    def _():
        ht_s[...] = jnp.zeros_like(ht_s)

    bdm = bdm_ref[...]
    bdm_f = bdm.astype(F32)

    for b in range(nb):
        p = p_ref[b]
        misc = m_ref[b]
        k = p[:, g:2 * g]

        w = -_softplus(-(w0_ref[...] + _dot3(jnp.tanh(misc), w2_ref[...]))) - 0.5
        alpha = _sigmoid(a0_ref[...] + _dot3(misc, a2_ref[...]))
        kn = k * kk_ref[...]
        ss = _dot_sel_rhs(kn * kn, bdm)
        r_s[b] = p[:, 0:g]
        k_s[b] = k * (1.0 + (alpha - 1.0) * ka_ref[...])
        v_s[b] = p[:, 2 * g:3 * g]
        kn_s[b] = kn * lax.rsqrt(jnp.maximum(ss, 1e-12))
        al_s[b] = alpha
        lw_s[b] = -jnp.exp(w)

    row = _iota((c, g), 0)
    col = _iota((c, g), 1) % c
    strict = col < row
    incl = col <= row
    eye = (col == row).astype(F32)
    lower_c = (_iota((c, c), 1) <= _iota((c, c), 0)).astype(BF16)
    level_masks = []
    m = 1
    while m < c:
        level_masks.append(jnp.logical_and(
            strict, jnp.logical_and(row // (2 * m) == col // (2 * m), row // m != col // m)))
        m *= 2

    def bd(x):
        return jnp.concatenate([x.astype(BF16)] * N_HEADS, axis=0) * bdm

    def mm(a, b_bf16):
        return _dot(a.astype(BF16), b_bf16)

    def mm_nt(a, b_bf16):
        return _dot_nt(a.astype(BF16), b_bf16)

    def each(f, *xs):
        return [f(*a) for a in zip(*xs)]

    def phase_a(chains):
        sls = [pl.ds(pl.multiple_of(ci * c, c), c) for _, ci in chains]
        ld = lambda ref: [ref[b, sl, :] for (b, _), sl in zip(chains, sls)]
        r_c, k_c, v_c, kn_c, al_c, lw_c = (ld(s) for s in (r_s, k_s, v_s, kn_s, al_s, lw_s))
        cl = each(lambda x: _dot_sel_lhs(lower_c, x), lw_c)
        cl_last = each(lambda x: x[c - 1:c, :], cl)
        a_t = each(lambda kn, x, lw: -kn * jnp.exp(x - lw), kn_c, cl, lw_c)
        r_t = each(lambda r, x: r * jnp.exp(x), r_c, cl)
        q_inv = each(lambda x: jnp.exp(-x), cl)
        p_rem = each(lambda xl, x: jnp.exp(xl - x), cl_last, cl)
        kna = each(lambda kn, al: kn * al, kn_c, al_c)
        ar = each(lambda a, r: jnp.concatenate([a, r], axis=0), a_t, r_t)
        s_b = each(lambda x, kb, qi: mm_nt(x, bd(kb * qi)), ar, kna, q_inv)
        s_k = each(lambda x, kk, qi: mm_nt(x, bd(kk * qi)), ar, k_c, q_inv)
        n = each(lambda x: jnp.where(strict, x[0:c], 0.0), s_b)
        a_ak = each(lambda x: jnp.where(strict, x[0:c], 0.0), s_k)
        m_rb = each(lambda x: jnp.where(incl, x[c:2 * c], 0.0), s_b)
        m_rk = each(lambda x: jnp.where(incl, x[c:2 * c], 0.0), s_k)

        inv = each(lambda x: eye + jnp.where(level_masks[0], x, 0.0), n)
        for lm in level_masks[1:]:
            half = each(lambda d, x: mm(d, bd(jnp.where(lm, x, 0.0))), inv, n)
            inv = each(lambda d, hf: d + mm(hf, bd(d)), inv, half)

        v_bd = each(bd, v_c)
        akv = each(mm, a_ak, v_bd)
        wt = each(lambda d, a: mm(d, bd(a)), inv, a_t)
        u0 = each(lambda d, x: mm(d, bd(x)), inv, akv)
        o0 = each(mm, m_rk, v_bd)
        bh = each(lambda x, p: x * p, kna, p_rem)
        kh = each(lambda x, p: x * p, k_c, p_rem)
        for ref, vals in zip((wt_s, u0_s, o0_s, mrb_s, rt_s, bh_s, kh_s),
                             (wt, u0, o0, m_rb, r_t, bh, kh)):
            for (b, _), sl, val in zip(chains, sls, vals):
                ref[b, sl, :] = val
        for (b, ci), xl in zip(chains, cl_last):
            pc_s[b, pl.ds(pl.multiple_of(ci * SUBLANES, SUBLANES), SUBLANES), :] = (
                jnp.broadcast_to(jnp.exp(xl), (SUBLANES, g)))

    def phase_b(ci):
        sl = pl.ds(pl.multiple_of(ci * c, c), c)
        bs = list(range(nb))
        ht = [ht_s[b] for b in bs]
        wr = [jnp.concatenate([wt_s[b, sl, :], rt_s[b, sl, :]], axis=0) for b in bs]
        wrh = each(lambda x, hh: mm_nt(x, hh.astype(BF16)), wr, ht)
        u = [x[0:c] + u0_s[b, sl, :] for x, b in zip(wrh, bs)]
        uv_t = [jnp.concatenate([x, v_s[b, sl, :]], axis=0).T for x, b in zip(u, bs)]
        bk = [jnp.concatenate([bh_s[b, sl, :], kh_s[b, sl, :]], axis=0) for b in bs]
        upd = each(lambda x, y: mm(x, y.astype(BF16)), uv_t, bk)
        mu_ = [mm(mrb_s[b, sl, :], bd(x)) for x, b in zip(u, bs)]
        for b in bs:
            p_c = pc_s[b, pl.ds(pl.multiple_of(ci * SUBLANES, SUBLANES), 1), :]
            y_s[b, sl, :] = wrh[b][c:2 * c] + mu_[b] + o0_s[b, sl, :]
            ht_s[b] = ht[b] * p_c + bdm_f * upd[b]

    def loop_a(ci, carry):
        phase_a([(b, ci * RWKV_UNROLL + j) for j in range(RWKV_UNROLL) for b in range(nb)])
        return carry

    def loop_b(ci, carry):
        phase_b(ci)
        return carry

    lax.fori_loop(0, n_chunks // RWKV_UNROLL, loop_a, 0)
    lax.fori_loop(0, n_chunks, loop_b, 0)

    inv_n = 1.0 / HEAD_DIM
    for b in range(nb):
        y = y_s[b]
        mean = _dot_sel_rhs(y, bdm) * inv_n
        yc = y - mean
        var = _dot_sel_rhs(yc * yc, bdm) * inv_n
        yn = yc * lax.rsqrt(var + GN_EPS) * lng_ref[...] + lnb_ref[...]
        bonus = _dot_sel_rhs(r_s[b] * k_s[b] * rk_ref[...], bdm) * v_s[b]
        o_ref[b] = yn + bonus


def _rwkv(p_rkv, misc, prm, bdm, *, tt):
    b, s, _ = p_rkv.shape
    g = D_GROUP
    seq = lambda width: pl.BlockSpec((b, tt, width), lambda ti: (0, ti, 0))
    big = pltpu.VMEM((b, tt, g), F32)
    return pl.pallas_call(
        functools.partial(_rwkv_kernel, tt=tt, nb=b),
        grid=(s // tt,),
        in_specs=[seq(N_RKV), seq(LANES)] + [_full_spec(a) for a in prm] + [_full_spec(bdm)],
        out_specs=seq(g),
        out_shape=jax.ShapeDtypeStruct((b, s, g), F32),
        scratch_shapes=[big] * 13
        + [pltpu.VMEM((b, tt // RWKV_CHUNK * SUBLANES, g), F32), big,
           pltpu.VMEM((b, g, g), F32)],
        compiler_params=_params(1),
        name="rwkv7",
    )(p_rkv, misc, *prm, bdm)


def _lru_tile(x_ref, cw_ref, cb_ref, wa_ref, ba_ref, wx_ref, bx_ref, lam_ref, o_ref,
              pad, a_s, u_s, h_s, *, tt, nb):
    @pl.when(pl.program_id(0) == 0)
    def _():
        pad[:, 0:SUBLANES, :] = jnp.zeros((nb, SUBLANES, D_GROUP), F32)
        h_s[...] = jnp.zeros_like(h_s)

    for b in range(nb):
        x = x_ref[b]
        pad[b, SUBLANES:SUBLANES + tt, :] = x
        xc = cw_ref[CONV_WIDTH - 1:CONV_WIDTH, :] * x + cb_ref[...]
        for d in range(1, CONV_WIDTH):
            tap = CONV_WIDTH - 1 - d
            xc = xc + cw_ref[tap:tap + 1, :] * pad[b, SUBLANES - d:SUBLANES - d + tt, :]
        pad[b, 0:SUBLANES, :] = x[tt - SUBLANES:tt, :]

        xb = xc.astype(BF16)
        r = _sigmoid(_dot(xb, wa_ref[...]) + ba_ref[...])
        i = _sigmoid(_dot(xb, wx_ref[...]) + bx_ref[...])
        log_a = -LRU_C * r * _softplus(-lam_ref[...])
        a_s[b] = jnp.exp(log_a)
        th = jnp.tanh(log_a)
        u_s[b] = jnp.sqrt(-2.0 * th / (1.0 - th)) * (i * xc)

    def group(gi, hs):
        base = pl.multiple_of(gi * SUBLANES, SUBLANES)
        a8 = [a_s[b, pl.ds(base, SUBLANES), :] for b in range(nb)]
        u8 = [u_s[b, pl.ds(base, SUBLANES), :] for b in range(nb)]
        hs = list(hs)
        rows = [[] for _ in range(nb)]
        for j in range(SUBLANES):
            for b in range(nb):
                hs[b] = a8[b][j:j + 1, :] * hs[b] + u8[b][j:j + 1, :]
                rows[b].append(hs[b])
        for b in range(nb):
            o_ref[b, pl.ds(base, SUBLANES), :] = jnp.concatenate(rows[b], axis=0)
        return tuple(hs)

    hs = lax.fori_loop(0, tt // SUBLANES, group, tuple(h_s[b] for b in range(nb)))
    for b in range(nb):
        h_s[b] = hs[b]


def _outproj_kernel(x_ref, yft_ref, yst_ref, yr_ref, lx_ref, gates_ref, w_ref, fg_ref,
                    cw_ref, cb_ref, wa_ref, ba_ref, wx_ref, bx_ref, lam_ref, o_ref,
                    pad, a_s, u_s, yl_s, h_s, *, final, tm, nb):
    _lru_tile(lx_ref, cw_ref, cb_ref, wa_ref, ba_ref, wx_ref, bx_ref, lam_ref, yl_s,
              pad, a_s, u_s, h_s, tt=tm, nb=nb)
    for b in range(nb):
        acc = x_ref[b]
        ys = (yft_ref[b].T, yst_ref[b].T, yr_ref[b], yl_s[b])
        for gi, y in enumerate(ys):
            gate = gates_ref[b, :, gi * D_GROUP:(gi + 1) * D_GROUP]
            y = y * (gate * _sigmoid(gate))
            acc = acc + _dot(y.astype(BF16), w_ref[gi * D_GROUP:(gi + 1) * D_GROUP, :])
        if final:
            ms = jnp.mean(acc * acc, axis=-1, keepdims=True)
            acc = acc * lax.rsqrt(ms + RMS_EPS) * fg_ref[...]
        o_ref[b] = acc


def _outproj(x, y_fox_t, y_sb_t, y_rw, lx, gates, w, final_g, lru_prm, *, tm, final):
    b, s, _ = x.shape
    g = D_GROUP
    seq = lambda width: pl.BlockSpec((b, tm, width), lambda i: (0, i, 0))
    chan = pl.BlockSpec((b, g, tm), lambda i: (0, 0, i))
    big = pltpu.VMEM((b, tm, g), F32)
    return pl.pallas_call(
        functools.partial(_outproj_kernel, final=final, tm=tm, nb=b),
        grid=(s // tm,),
        in_specs=[seq(D_MODEL), chan, chan, seq(g), seq(g), seq(4 * g),
                  _full_spec(w), _full_spec(final_g)] + [_full_spec(a) for a in lru_prm],
        out_specs=seq(D_MODEL),
        out_shape=jax.ShapeDtypeStruct((b, s, D_MODEL), F32),
        scratch_shapes=[pltpu.VMEM((b, tm + SUBLANES, g), F32), big, big, big,
                        pltpu.VMEM((b, 1, g), F32)],
        compiler_params=_params(1),
        name="outproj",
    )(x, y_fox_t, y_sb_t, y_rw, lx, gates, w, final_g, *lru_prm)


def _w_in_segments():
    g, h, r, dh = D_GROUP, N_HEADS, RWKV_LORA, HEAD_DIM
    o_ff = 4 * g
    o_sb = o_ff + h
    o_rw = o_sb + 4 * g
    o_rg = o_rw + 3 * g + 2 * r
    o_lx = o_rg + g
    o_lg = o_lx + g
    segs = []
    for src, dst in ((0, C_FQ), (g, C_FK), (o_sb, C_SQ), (o_sb + g, C_SK)):
        segs += [(src + hd * dh, dst + hd * HEAD_PAD, dh) for hd in range(h)]
    segs += [(2 * g, C_FV, g), (o_sb + 2 * g, C_SV, g),
             (3 * g, C_GATES, g), (o_sb + 3 * g, C_GATES + g, g), (o_rg, C_GATES + 2 * g, g),
             (o_lg, C_GATES + 3 * g, g),
             (o_rw, C_RKV, 3 * g), (o_lx, C_LX, g),
             (o_rw + 3 * g, C_MISC, 2 * r), (o_ff, C_MISC + FF_LANE, h)]
    return segs


def _w_in_kernel(w_ref, o_ref):
    o_ref[...] = jnp.zeros_like(o_ref)
    for src, dst, width in _w_in_segments():
        o_ref[0, :, dst:dst + width] = w_ref[0, :, src:src + width].astype(BF16)


def _permute_w_in(w_in, *, tr):
    depth, d, n_in = w_in.shape
    return pl.pallas_call(
        _w_in_kernel,
        grid=(depth, d // tr),
        in_specs=[pl.BlockSpec((1, tr, n_in), lambda l, i: (l, i, 0))],
        out_specs=pl.BlockSpec((1, tr, N_PROJ), lambda l, i: (l, i, 0)),
        out_shape=jax.ShapeDtypeStruct((depth, d, N_PROJ), BF16),
        compiler_params=_params(2),
        name="w_in_layout",
    )(w_in)


def _block_diag(w):
    h, n, _ = w.shape
    eye = jnp.eye(h, dtype=w.dtype)
    return jnp.einsum('hij,hk->hikj', w, eye).reshape(h * n, h * n)


def _pick_tile(s, pref):
    t = pref
    while s % t:
        t //= 2
    return t


def kernel(x, norm_g, w_in, b_forget, rwkv_mu, rwkv_w0, rwkv_w2, rwkv_a0, rwkv_a2, rwkv_k_k,
           rwkv_k_a, rwkv_r_k, rwkv_ln_g, rwkv_ln_b, lru_conv_w, lru_conv_b, lru_w_a, lru_b_a,
           lru_w_x, lru_b_x, lru_lambda, w_out, final_g):
    b, s, d = x.shape
    depth = w_in.shape[0]
    g, h, dh, r = D_GROUP, N_HEADS, HEAD_DIM, RWKV_LORA
    tm = _pick_tile(s, 512)
    tq = _pick_tile(s, 256)
    tk_fox = _pick_tile(s, 2 * tq)
    tt = _pick_tile(s, 512)
    row = lambda a: a.reshape(1, -1).astype(F32)

    bdm = _block_diag(jnp.ones((h, dh, dh), BF16))
    w_in_k = _permute_w_in(w_in, tr=LANES)
    for l in range(depth):
        fbias = jnp.zeros((1, LANES), F32).at[0, FF_LANE:FF_LANE + h].set(b_forget[l])
        mu = rwkv_mu[l]
        consts = [row(norm_g[l]), w_in_k, fbias, row(mu[:N_RKV]),
                  jnp.zeros((1, LANES), F32).at[0, :2 * r].set(mu[N_RKV:])]
        fq, fk, fvt, sq, sk, svt, gates, rkv, lx, misc, f2, norms = _inproj(x, consts, l, tm=tm)

        per_head = lambda a: a.transpose(0, 2, 1).reshape(-1)
        fend = per_head(f2[:, tq - 1::tq, FF_LANE:FF_LANE + h])
        qmax = per_head(jnp.repeat(jnp.sqrt(norms[:, :, 0, 0:h]), tm // tq, axis=1))
        kmax = jnp.sqrt(jnp.max(norms[:, :, 0, h:2 * h], axis=1)).reshape(-1)
        y_fox_t = _fox_attention(fend, qmax, kmax, fq, fk, fvt, tq=tq, tk=tk_fox)
        y_sb_t = _sb_attention(sq, sk, svt, tq=tq)

        pad_rows = lambda a, lo: jnp.zeros((LANES, g), F32).at[lo:lo + r].set(a)
        rw_prm = [row(rwkv_w0[l]), pad_rows(rwkv_w2[l], 0), row(rwkv_a0[l]), pad_rows(rwkv_a2[l], r),
                  row(rwkv_k_k[l]), row(rwkv_k_a[l]), row(rwkv_r_k[l]), row(rwkv_ln_g[l]),
                  row(rwkv_ln_b[l])]
        y_rw = _rwkv(rkv, misc, rw_prm, bdm, tt=tt)

        lru_prm = [lru_conv_w[l].astype(F32), row(lru_conv_b[l]),
                   _block_diag(lru_w_a[l]).astype(BF16), row(lru_b_a[l]),
                   _block_diag(lru_w_x[l]).astype(BF16), row(lru_b_x[l]), row(lru_lambda[l])]
        x = _outproj(x, y_fox_t, y_sb_t, y_rw, lx, gates, w_out[l].astype(BF16), row(final_g),
                     lru_prm, tm=tm, final=(l == depth - 1))
    return x
```

```python
import functools

import jax
import jax.numpy as jnp
from jax import lax
from jax.experimental import pallas as pl
from jax.experimental.pallas import tpu as pltpu

F32 = jnp.float32
BF16 = jnp.bfloat16

D_MODEL = 1024
D_GROUP = 256
N_HEADS = 4
HEAD_DIM = 64
RWKV_LORA = 32
CONV_WIDTH = 4
LRU_C = 8.0
RMS_EPS = 1e-6
GN_EPS = 64e-5
N_RKV = 3 * D_GROUP

LANES = 128
SUBLANES = 8
VMEM_LIMIT_BYTES = 56 * 1024 * 1024

C_FQ = 0
C_FK = C_FQ + D_GROUP
C_SQ = C_FK + D_GROUP
C_SK = C_SQ + D_GROUP
C_FV = C_SK + D_GROUP
C_SV = C_FV + D_GROUP
C_GATES = C_SV + D_GROUP
C_RKV = C_GATES + 4 * D_GROUP
C_LX = C_RKV + N_RKV
C_MISC = C_LX + D_GROUP
N_PROJ = C_MISC + LANES
FF_LANE = 2 * RWKV_LORA
HEADS_PER_PAIR = LANES // HEAD_DIM
N_PAIRS = N_HEADS // HEADS_PER_PAIR
AUG_STRIDE = 8
AUG_K_F = 0
AUG_K_ONE = 3
AUG_Q_ONE = 0
AUG_Q_F = 3

CUMSUM_BLOCK = LANES
V_AUG = HEAD_DIM + 16
LOG2E = 1.4426950408889634

FOX_SKIP_LOG2 = -150.0
FOX_NORM_MARGIN = 1.02
FOX_F_SLACK = 1.0

NEG_BIG = -1e30
SB_SKIP_LOG2 = -150.0
RWKV_CHUNK = 64
RWKV_UNROLL = 4


def _dot(a, b):
    return jnp.dot(a, b, preferred_element_type=F32)


def _dot_nt(a, b):
    return lax.dot_general(a, b, (((1,), (1,)), ((), ())), preferred_element_type=F32)


def _split2(x):
    hi = x.astype(BF16)
    lo = (x - hi.astype(F32)).astype(BF16)
    return hi, lo


def _split3(x):
    hi = x.astype(BF16)
    r1 = x - hi.astype(F32)
    mid = r1.astype(BF16)
    lo = (r1 - mid.astype(F32)).astype(BF16)
    return hi, mid, lo


def _dot_sel_lhs(sel, x):
    hi, mid, lo = _split3(x)
    return _dot(sel, hi) + (_dot(sel, mid) + _dot(sel, lo))


def _dot_sel_rhs(x, sel):
    hi, lo = _split2(x)
    return _dot(hi, sel) + _dot(lo, sel)


def _softplus(x):
    return jnp.maximum(x, 0.0) + jnp.log(1.0 + jnp.exp(-jnp.abs(x)))


def _log_sigmoid(x):
    return jnp.minimum(x, 0.0) - jnp.log(1.0 + jnp.exp(-jnp.abs(x)))


def _log2_sigmoid_of_log2(x2):
    return jnp.minimum(x2, 0.0) - jnp.log2(1.0 + jnp.exp2(-jnp.abs(x2)))


def _sigmoid(x):
    return 1.0 / (1.0 + jnp.exp(-x))


def _iota(shape, dim):
    return lax.broadcasted_iota(jnp.int32, shape, dim)


def _full_spec(a):
    return pl.BlockSpec(a.shape, lambda *_: (0,) * a.ndim)


def _params(n_grid):
    return pltpu.CompilerParams(dimension_semantics=("arbitrary",) * n_grid,
                                vmem_limit_bytes=VMEM_LIMIT_BYTES)


def _inproj_kernel(x_ref, g_ref, w_ref, fb_ref, mu_rkv_ref, mu_misc_ref,
                   fq_ref, fk_ref, qaug_ref, kaug_ref, fvt_ref, sq_ref, sk_ref, svt_ref, gates_ref,
                   rkv_ref, lx_ref, misc_ref, f2_ref, norms_ref, ftot, v_stage, pad_rkv, pad_misc):
    @pl.when(pl.program_id(1) == 0)
    def _():
        ftot[...] = jnp.zeros_like(ftot)
        pad_rkv[0:SUBLANES, :] = jnp.zeros((SUBLANES, N_RKV), F32)
        pad_misc[0:SUBLANES, :] = jnp.zeros((SUBLANES, LANES), F32)

    x = x_ref[0]
    tm = x.shape[0]
    ms = jnp.mean(x * x, axis=-1, keepdims=True)
    h = (x * lax.rsqrt(ms + RMS_EPS) * g_ref[...]).astype(BF16)
    scale = HEAD_DIM ** -0.5

    def proj(c0, width):
        return _dot(h, w_ref[0, :, c0:c0 + width])

    misc = proj(C_MISC, LANES)
    gates_ref[0] = proj(C_GATES, 4 * D_GROUP)
    lx_ref[0] = proj(C_LX, D_GROUP)

    for val, pad, mu_ref, out_ref in ((proj(C_RKV, N_RKV), pad_rkv, mu_rkv_ref, rkv_ref),
                                      (misc, pad_misc, mu_misc_ref, misc_ref)):
        pad[SUBLANES:SUBLANES + tm, :] = val
        prev = pad[SUBLANES - 1:SUBLANES - 1 + tm, :]
        pad[0:SUBLANES, :] = val[tm - SUBLANES:tm, :]
        out_ref[0] = val + (prev - val) * mu_ref[...]

    lf = _log_sigmoid(misc + fb_ref[...])
    cb = CUMSUM_BLOCK
    lower = (_iota((cb, cb), 1) <= _iota((cb, cb), 0)).astype(BF16)
    run = ftot[...]
    blocks = []
    for r0 in range(0, tm, cb):
        blk = _dot_sel_lhs(lower, lf[r0:r0 + cb, :]) + run
        run = blk[cb - 1:cb, :]
        blocks.append(blk)
    ftot[...] = run
    f = jnp.concatenate(blocks, axis=0)
    f2 = f * LOG2E
    f2_ref[0] = f2
    hi = f2.astype(BF16).astype(F32)
    mid = (f2 - hi).astype(BF16).astype(F32)
    lo = (f2 - hi - mid).astype(BF16).astype(F32)

    pq = proj(C_FQ, D_GROUP) * (scale * LOG2E)
    pk = proj(C_FK, D_GROUP)
    sq = proj(C_SQ, D_GROUP) * (scale * LOG2E)
    sk = proj(C_SK, D_GROUP)
    for pr in range(N_PAIRS):
        cols = slice(pr * LANES, (pr + 1) * LANES)
        fq_ref[0, pr] = pq[:, cols].astype(BF16)
        fk_ref[0, pr] = pk[:, cols].astype(BF16)
        sq_ref[0, pr] = sq[:, cols].astype(BF16)
        sk_ref[0, pr] = sk[:, cols].astype(BF16)

    lane = _iota((tm, LANES), 1)
    in_group = lane % AUG_STRIDE
    valid = lane < AUG_STRIDE * N_HEADS
    aug_q = jnp.where(jnp.logical_and(valid, jnp.logical_and(in_group >= AUG_Q_ONE,
                                                             in_group < AUG_Q_ONE + 3)), 1.0, 0.0)
    aug_k = jnp.where(jnp.logical_and(valid, jnp.logical_and(in_group >= AUG_K_ONE,
                                                             in_group < AUG_K_ONE + 3)), 1.0, 0.0)
    for hd in range(N_HEADS):
        for i3, piece in enumerate((hi, mid, lo)):
            col = jnp.broadcast_to(piece[:, FF_LANE + hd:FF_LANE + hd + 1], (tm, LANES))
            aug_q = jnp.where(lane == AUG_STRIDE * hd + AUG_Q_F + i3, col, aug_q)
            aug_k = jnp.where(lane == AUG_STRIDE * hd + AUG_K_F + i3, -col, aug_k)
    qaug_ref[0] = aug_q.astype(BF16)
    kaug_ref[0] = aug_k.astype(BF16)

    head_of_row = _iota((D_GROUP, LANES), 0) // HEAD_DIM
    nlane = _iota((SUBLANES, LANES), 1)
    norms = jnp.zeros((SUBLANES, LANES), F32)
    for side, pv in enumerate((pq, pk)):
        sel = (head_of_row + side * N_HEADS == _iota((D_GROUP, LANES), 1)).astype(BF16)
        top = jnp.max(_dot((pv * pv).astype(BF16), sel), axis=0, keepdims=True)
        keep = jnp.logical_and(nlane >= side * N_HEADS, nlane < (side + 1) * N_HEADS)
        norms = jnp.where(keep, top, norms)
    norms_ref[0, 0] = norms
    v_stage[...] = proj(C_SV, D_GROUP)
    svt_ref[0] = v_stage[...].T.astype(BF16)
    v_stage[...] = proj(C_FV, D_GROUP)
    vt = v_stage[...].T.astype(BF16)
    ones_rows = (_iota((V_AUG - HEAD_DIM, tm), 0) == 0).astype(BF16)
    for hd in range(N_HEADS):
        fvt_ref[0, hd * V_AUG:hd * V_AUG + HEAD_DIM, :] = vt[hd * HEAD_DIM:(hd + 1) * HEAD_DIM, :]
        fvt_ref[0, hd * V_AUG + HEAD_DIM:(hd + 1) * V_AUG, :] = ones_rows


def _inproj(x, consts, layer, *, tm):
    b, s, _ = x.shape
    w_spec = pl.BlockSpec((1,) + consts[1].shape[1:], lambda bi, i: (layer, 0, 0))
    seq = lambda width: pl.BlockSpec((1, tm, width), lambda bi, i: (bi, i, 0))
    pairs = pl.BlockSpec((1, N_PAIRS, tm, LANES), lambda bi, i: (bi, 0, i, 0))
    pairs_shape = jax.ShapeDtypeStruct((b, N_PAIRS, s, LANES), BF16)
    chan = lambda rows: pl.BlockSpec((1, rows, tm), lambda bi, i: (bi, 0, i))
    chan_shape = lambda rows: jax.ShapeDtypeStruct((b, rows, s), BF16)
    seq_shape = lambda width, dtype=F32: jax.ShapeDtypeStruct((b, s, width), dtype)
    return pl.pallas_call(
        _inproj_kernel,
        grid=(b, s // tm),
        in_specs=[seq(D_MODEL), _full_spec(consts[0]), w_spec] + [_full_spec(a) for a in consts[2:]],
        out_specs=[pairs, pairs, seq(LANES), seq(LANES), chan(N_HEADS * V_AUG),
                   pairs, pairs, chan(D_GROUP),
                   seq(4 * D_GROUP), seq(N_RKV), seq(D_GROUP), seq(LANES), seq(LANES),
                   pl.BlockSpec((1, 1, SUBLANES, LANES), lambda bi, i: (bi, i, 0, 0))],
        out_shape=[pairs_shape, pairs_shape, seq_shape(LANES, BF16), seq_shape(LANES, BF16),
                   chan_shape(N_HEADS * V_AUG), pairs_shape, pairs_shape, chan_shape(D_GROUP),
                   seq_shape(4 * D_GROUP), seq_shape(N_RKV), seq_shape(D_GROUP), seq_shape(LANES),
                   seq_shape(LANES), jax.ShapeDtypeStruct((b, s // tm, SUBLANES, LANES), F32)],
        scratch_shapes=[pltpu.VMEM((1, LANES), F32), pltpu.VMEM((tm, D_GROUP), F32),
                        pltpu.VMEM((tm + SUBLANES, N_RKV), F32),
                        pltpu.VMEM((tm + SUBLANES, LANES), F32)],
        compiler_params=_params(2),
        name="inproj",
    )(x, *consts)


def _fox_kernel(fend_ref, qmax_ref, kmax_ref, q_ref, qaug_ref, k_ref, kaug_ref, vt_ref, o_ref,
                s_scr, p_scr, mx_scr, al_scr, m_scr, acc_scr, *, tq, tk):
    i = pl.program_id(1)
    nq = pl.num_programs(1)
    heads = list(range(N_HEADS))
    lane = _iota((tq, LANES), 1)
    q = []
    for hd in heads:
        own_half = lane // HEAD_DIM == hd % HEADS_PER_PAIR
        own_aug = lane // AUG_STRIDE == hd
        q.append(jnp.concatenate(
            [jnp.where(own_half, q_ref[0, hd // HEADS_PER_PAIR], jnp.zeros((), BF16)),
             jnp.where(own_aug, qaug_ref[0], jnp.zeros((), BF16))], axis=1))
    n_full = (i * tq) // tk

    def first_live_block(hd):
        bh = pl.program_id(0) * N_HEADS + hd
        qk_bound = FOX_NORM_MARGIN * 2.0 * qmax_ref[bh * nq + i] * kmax_ref[bh]
        f_tile = fend_ref[bh * nq + jnp.maximum(i - 1, 0)]

        def first_live(jj, first):
            j = n_full - 1 - jj
            f_end = fend_ref[bh * nq + (j + 1) * (tk // tq) - 1]
            live = qk_bound + (f_tile - f_end) + FOX_F_SLACK >= FOX_SKIP_LOG2
            return jnp.where(live, j, first)

        return lax.fori_loop(0, n_full, first_live, n_full)

    base = functools.reduce(jnp.minimum, [first_live_block(hd) for hd in heads])
    n_eff = n_full - base

    def key_rows(j):
        return pl.ds(pl.multiple_of(j * tk, tk), tk)

    def scores_to(j, slot):
        kaug = kaug_ref[0, key_rows(j), :]
        kp = [jnp.concatenate([k_ref[0, pr, key_rows(j), :], kaug], axis=1) for pr in range(N_PAIRS)]
        s = [_dot_nt(kp[hd // HEADS_PER_PAIR], q[hd]) for hd in heads]
        for hd in heads:
            s_scr[hd, slot] = s[hd]
            mx_scr[hd, slot] = jnp.max(s[hd], axis=0, keepdims=True)

    def weighted_values(j, slot):
        return [_dot(vt_ref[0, hd * V_AUG:(hd + 1) * V_AUG, key_rows(j)], p_scr[hd, slot])
                for hd in heads]

    def softmax_to(s, mx, slot):
        m_old = [m_scr[hd] for hd in heads]
        m_new = [jnp.maximum(a, b) for a, b in zip(m_old, mx)]
        p = [jnp.exp2(a - b).astype(BF16) for a, b in zip(s, m_new)]
        for hd in heads:
            m_scr[hd] = m_new[hd]
            al_scr[hd, slot] = jnp.exp2(m_old[hd] - m_new[hd])
            p_scr[hd, slot] = p[hd]

    def stage(local, cur):
        j = base + local
        nxt = 1 - cur
        pv_prev = weighted_values(jnp.maximum(j - 1, 0), nxt)
        scores_to(j + 1, nxt)
        softmax_to([s_scr[hd, cur] for hd in heads], [mx_scr[hd, cur] for hd in heads], cur)
        for hd in heads:
            acc_scr[hd] = al_scr[hd, nxt] * acc_scr[hd] + pv_prev[hd]

    def tail(cur):
        nxt = 1 - cur
        pv_prev = weighted_values(jnp.maximum(n_full - 1, 0), nxt)
        mask = n_full * tk + _iota((tk, tq), 0) <= i * tq + _iota((tk, tq), 1)
        s = [jnp.where(mask, s_scr[hd, cur], NEG_BIG) for hd in heads]
        softmax_to(s, [jnp.max(x, axis=0, keepdims=True) for x in s], cur)
        pv_last = weighted_values(n_full, cur)
        for hd in heads:
            acc = al_scr[hd, cur] * (al_scr[hd, nxt] * acc_scr[hd] + pv_prev[hd]) + pv_last[hd]
            o_ref[0, hd * HEAD_DIM:(hd + 1) * HEAD_DIM, :] = (
                acc[0:HEAD_DIM] / acc[HEAD_DIM:HEAD_DIM + 1])

    m_scr[...] = jnp.full(m_scr.shape, NEG_BIG, F32)
    acc_scr[...] = jnp.zeros_like(acc_scr)
    p_scr[:, 1] = jnp.zeros((N_HEADS, tk, tq), BF16)
    al_scr[:, 1] = jnp.ones((N_HEADS, 1, tq), F32)
    scores_to(base, 0)

    def pair(jj, carry):
        stage(2 * jj, 0)
        stage(2 * jj + 1, 1)
        return carry

    lax.fori_loop(0, n_eff // 2, pair, 0)

    @pl.when(n_eff % 2 == 1)
    def _():
        stage(n_eff - 1, 0)
        tail(1)

    @pl.when(n_eff % 2 == 0)
    def _():
        tail(0)


def _fox_attention(fend, qmax, kmax, q, qaug, k, kaug, vt, *, tq, tk):
    b, _, s, _ = q.shape
    h = N_HEADS
    grid_spec = pltpu.PrefetchScalarGridSpec(
        num_scalar_prefetch=3,
        grid=(b, s // tq),
        in_specs=[pl.BlockSpec((1, N_PAIRS, tq, LANES), lambda bi, i, *_: (bi, 0, i, 0)),
                  pl.BlockSpec((1, tq, LANES), lambda bi, i, *_: (bi, i, 0)),
                  pl.BlockSpec((1, N_PAIRS, s, LANES), lambda bi, i, *_: (bi, 0, 0, 0)),
                  pl.BlockSpec((1, s, LANES), lambda bi, i, *_: (bi, 0, 0)),
                  pl.BlockSpec((1, h * V_AUG, s), lambda bi, i, *_: (bi, 0, 0))],
        out_specs=pl.BlockSpec((1, h * HEAD_DIM, tq), lambda bi, i, *_: (bi, 0, i)),
        scratch_shapes=[pltpu.VMEM((h, 2, tk, tq), F32), pltpu.VMEM((h, 2, tk, tq), BF16),
                        pltpu.VMEM((h, 2, 1, tq), F32), pltpu.VMEM((h, 2, 1, tq), F32),
                        pltpu.VMEM((h, 1, tq), F32), pltpu.VMEM((h, V_AUG, tq), F32)])
    return pl.pallas_call(
        functools.partial(_fox_kernel, tq=tq, tk=tk),
        grid_spec=grid_spec,
        out_shape=jax.ShapeDtypeStruct((b, h * HEAD_DIM, s), F32),
        compiler_params=_params(2),
        name="fox_attention",
    )(fend, qmax, kmax, q, qaug, k, kaug, vt)


def _sb_kernel(q_ref, k_ref, vt_ref, o_ref, *, tq):
    i = pl.program_id(1)
    tk = tq
    heads = list(range(N_HEADS))
    lane = _iota((tq, LANES), 1)
    q = [jnp.where(lane // HEAD_DIM == hd % HEADS_PER_PAIR, q_ref[0, hd // HEADS_PER_PAIR],
                   jnp.zeros((), BF16)) for hd in heads]
    later = (_iota((tk, tk), 0) < _iota((tk, tk), 1)).astype(BF16)

    def each(f, *xs):
        return [f(*a) for a in zip(*xs)]

    def block(j, rest_q, acc, masked):
        ks = pl.multiple_of(j * tk, tk)
        kp = [k_ref[0, pr, pl.ds(ks, tk), :] for pr in range(N_PAIRS)]
        z = [_dot_nt(kp[hd // HEADS_PER_PAIR], q[hd]) for hd in heads]
        log_keep = each(lambda x: _log2_sigmoid_of_log2(-x), z)
        if masked:
            mask = _iota((tk, tq), 0) < _iota((tk, tq), 1)
            log_keep = each(lambda x: jnp.where(mask, x, 0.0), log_keep)
        split = each(_split2, log_keep)
        rest_in = each(lambda hl: _dot(later, hl[0]) + _dot(later, hl[1]), split)
        att = each(lambda x, lk, ri, rq: jnp.exp2(x + lk + ri + rq), z, log_keep, rest_in, rest_q)
        if masked:
            att = each(lambda x: jnp.where(mask, x, 0.0), att)
        pv = [_dot(vt_ref[0, hd * HEAD_DIM:(hd + 1) * HEAD_DIM, pl.ds(ks, tk)],
                   att[hd].astype(BF16)) for hd in heads]
        acc = each(lambda a, x: a + x, acc, pv)
        rest_q = each(lambda rq, ri, lk: rq + ri[0:1, :] + lk[0:1, :], rest_q, rest_in, log_keep)
        return rest_q, acc

    rest_q, acc = block(i, [jnp.zeros((1, tq), F32)] * N_HEADS,
                        [jnp.zeros((HEAD_DIM, tq), F32)] * N_HEADS, True)

    def cond(c):
        j, rest_q, _ = c
        alive = functools.reduce(jnp.maximum, rest_q)
        return jnp.logical_and(j >= 0, jnp.max(alive) > SB_SKIP_LOG2)

    def body(c):
        j, rest_q, acc = c
        rest_q, acc = block(j, list(rest_q), list(acc), False)
        return j - 1, tuple(rest_q), tuple(acc)

    _, _, acc = lax.while_loop(cond, body, (i - 1, tuple(rest_q), tuple(acc)))
    o_ref[0] = jnp.concatenate(list(acc), axis=0)


def _sb_attention(q, k, vt, *, tq):
    b, _, s, _ = q.shape
    h = N_HEADS
    return pl.pallas_call(
        functools.partial(_sb_kernel, tq=tq),
        grid=(b, s // tq),
        in_specs=[pl.BlockSpec((1, N_PAIRS, tq, LANES), lambda bi, i: (bi, 0, i, 0)),
                  pl.BlockSpec((1, N_PAIRS, s, LANES), lambda bi, i: (bi, 0, 0, 0)),
                  pl.BlockSpec((1, h * HEAD_DIM, s), lambda bi, i: (bi, 0, 0))],
        out_specs=pl.BlockSpec((1, h * HEAD_DIM, tq), lambda bi, i: (bi, 0, i)),
        out_shape=jax.ShapeDtypeStruct((b, h * HEAD_DIM, s), F32),
        compiler_params=_params(2),
        name="sb_attention",
    )(q, k, vt)


def _rwkv_kernel(p_ref, m_ref, w0_ref, w2_ref, a0_ref, a2_ref,
                 kk_ref, ka_ref, rk_ref, lng_ref, lnb_ref, bdm_ref, o_ref,
                 r_s, k_s, v_s, kn_s, al_s, lw_s,
                 wt_s, u0_s, o0_s, mrb_s, rt_s, bh_s, kh_s, pc_s, y_s, ht_s, *, tt, nb):
    c = RWKV_CHUNK
    g = D_GROUP
    n_chunks = tt // c

    @pl.when(pl.program_id(0) == 0)
    def _():
        ht_s[...] = jnp.zeros_like(ht_s)

    bdm = bdm_ref[...]
    bdm_f = bdm.astype(F32)

    for b in range(nb):
        p = p_ref[b]
        misc = m_ref[b]
        k = p[:, g:2 * g]

        w = -_softplus(-(w0_ref[...] + _dot(jnp.tanh(misc).astype(BF16), w2_ref[...]))) - 0.5
        alpha = _sigmoid(a0_ref[...] + _dot(misc.astype(BF16), a2_ref[...]))
        kn = k * kk_ref[...]
        ss = _dot_sel_rhs(kn * kn, bdm)
        r_s[b] = p[:, 0:g]
        k_s[b] = k * (1.0 + (alpha - 1.0) * ka_ref[...])
        v_s[b] = p[:, 2 * g:3 * g]
        kn_s[b] = kn * lax.rsqrt(jnp.maximum(ss, 1e-12))
        al_s[b] = alpha
        lw_s[b] = -jnp.exp(w)

    row = _iota((c, g), 0)
    col = _iota((c, g), 1) % c
    strict = col < row
    incl = col <= row
    eye = (col == row).astype(F32)
    lower_c = (_iota((c, c), 1) <= _iota((c, c), 0)).astype(BF16)
    level_masks = []
    m = 1
    while m < c:
        level_masks.append(jnp.logical_and(
            strict, jnp.logical_and(row // (2 * m) == col // (2 * m), row // m != col // m)))
        m *= 2

    def bd(x):
        return jnp.concatenate([x.astype(BF16)] * N_HEADS, axis=0) * bdm

    def mm(a, b_bf16):
        return _dot(a.astype(BF16), b_bf16)

    def mm_nt(a, b_bf16):
        return _dot_nt(a.astype(BF16), b_bf16)

    def each(f, *xs):
        return [f(*a) for a in zip(*xs)]

    def phase_a(chains):
        sls = [pl.ds(pl.multiple_of(ci * c, c), c) for _, ci in chains]
        ld = lambda ref: [ref[b, sl, :] for (b, _), sl in zip(chains, sls)]
        r_c, k_c, v_c, kn_c, al_c, lw_c = (ld(s) for s in (r_s, k_s, v_s, kn_s, al_s, lw_s))
        cl = each(lambda x: _dot_sel_lhs(lower_c, x), lw_c)
        cl_last = each(lambda x: x[c - 1:c, :], cl)
        a_t = each(lambda kn, x, lw: -kn * jnp.exp(x - lw), kn_c, cl, lw_c)
        r_t = each(lambda r, x: r * jnp.exp(x), r_c, cl)
        q_inv = each(lambda x: jnp.exp(-x), cl)
        p_rem = each(lambda xl, x: jnp.exp(xl - x), cl_last, cl)
        kna = each(lambda kn, al: kn * al, kn_c, al_c)
        ar = each(lambda a, r: jnp.concatenate([a, r], axis=0), a_t, r_t)
        s_b = each(lambda x, kb, qi: mm_nt(x, bd(kb * qi)), ar, kna, q_inv)
        s_k = each(lambda x, kk, qi: mm_nt(x, bd(kk * qi)), ar, k_c, q_inv)
        n = each(lambda x: jnp.where(strict, x[0:c], 0.0), s_b)
        a_ak = each(lambda x: jnp.where(strict, x[0:c], 0.0), s_k)
        m_rb = each(lambda x: jnp.where(incl, x[c:2 * c], 0.0), s_b)
        m_rk = each(lambda x: jnp.where(incl, x[c:2 * c], 0.0), s_k)

        inv = each(lambda x: eye + jnp.where(level_masks[0], x, 0.0), n)
        for lm in level_masks[1:]:
            half = each(lambda d, x: mm(d, bd(jnp.where(lm, x, 0.0))), inv, n)
            inv = each(lambda d, hf: d + mm(hf, bd(d)), inv, half)

        v_bd = each(bd, v_c)
        akv = each(mm, a_ak, v_bd)
        wt = each(lambda d, a: mm(d, bd(a)), inv, a_t)
        u0 = each(lambda d, x: mm(d, bd(x)), inv, akv)
        o0 = each(mm, m_rk, v_bd)
        bh = each(lambda x, p: x * p, kna, p_rem)
        kh = each(lambda x, p: x * p, k_c, p_rem)
        for ref, vals in zip((wt_s, u0_s, o0_s, mrb_s, rt_s, bh_s, kh_s),
                             (wt, u0, o0, m_rb, r_t, bh, kh)):
            for (b, _), sl, val in zip(chains, sls, vals):
                ref[b, sl, :] = val
        for (b, ci), xl in zip(chains, cl_last):
            pc_s[b, pl.ds(pl.multiple_of(ci * SUBLANES, SUBLANES), SUBLANES), :] = (
                jnp.broadcast_to(jnp.exp(xl), (SUBLANES, g)))

    def phase_b(ci):
        sl = pl.ds(pl.multiple_of(ci * c, c), c)
        bs = list(range(nb))
        ht = [ht_s[b] for b in bs]
        wr = [jnp.concatenate([wt_s[b, sl, :], rt_s[b, sl, :]], axis=0) for b in bs]
        wrh = each(lambda x, hh: mm_nt(x, hh.astype(BF16)), wr, ht)
        u = [x[0:c] + u0_s[b, sl, :] for x, b in zip(wrh, bs)]
        uv_t = [jnp.concatenate([x, v_s[b, sl, :]], axis=0).T for x, b in zip(u, bs)]
        bk = [jnp.concatenate([bh_s[b, sl, :], kh_s[b, sl, :]], axis=0) for b in bs]
        upd = each(lambda x, y: mm(x, y.astype(BF16)), uv_t, bk)
        mu_ = [mm(mrb_s[b, sl, :], bd(x)) for x, b in zip(u, bs)]
        for b in bs:
            p_c = pc_s[b, pl.ds(pl.multiple_of(ci * SUBLANES, SUBLANES), 1), :]
            y_s[b, sl, :] = wrh[b][c:2 * c] + mu_[b] + o0_s[b, sl, :]
            ht_s[b] = ht[b] * p_c + bdm_f * upd[b]

    def loop_a(ci, carry):
        phase_a([(b, ci * RWKV_UNROLL + j) for j in range(RWKV_UNROLL) for b in range(nb)])
        return carry

    def loop_b(ci, carry):
        phase_b(ci)
        return carry

    lax.fori_loop(0, n_chunks // RWKV_UNROLL, loop_a, 0)
    lax.fori_loop(0, n_chunks, loop_b, 0)

    inv_n = 1.0 / HEAD_DIM
    for b in range(nb):
        y = y_s[b]
        mean = _dot_sel_rhs(y, bdm) * inv_n
        yc = y - mean
        var = _dot_sel_rhs(yc * yc, bdm) * inv_n
        yn = yc * lax.rsqrt(var + GN_EPS) * lng_ref[...] + lnb_ref[...]
        bonus = _dot_sel_rhs(r_s[b] * k_s[b] * rk_ref[...], bdm) * v_s[b]
        o_ref[b] = yn + bonus


def _rwkv(p_rkv, misc, prm, bdm, *, tt):
    b, s, _ = p_rkv.shape
    g = D_GROUP
    seq = lambda width: pl.BlockSpec((b, tt, width), lambda ti: (0, ti, 0))
    big = pltpu.VMEM((b, tt, g), F32)
    return pl.pallas_call(
        functools.partial(_rwkv_kernel, tt=tt, nb=b),
        grid=(s // tt,),
        in_specs=[seq(N_RKV), seq(LANES)] + [_full_spec(a) for a in prm] + [_full_spec(bdm)],
        out_specs=seq(g),
        out_shape=jax.ShapeDtypeStruct((b, s, g), F32),
        scratch_shapes=[big] * 13
        + [pltpu.VMEM((b, tt // RWKV_CHUNK * SUBLANES, g), F32), big,
           pltpu.VMEM((b, g, g), F32)],
        compiler_params=_params(1),
        name="rwkv7",
    )(p_rkv, misc, *prm, bdm)


def _lru_tile(x_ref, cw_ref, cb_ref, wa_ref, ba_ref, wx_ref, bx_ref, lam_ref, o_ref,
              pad, a_s, u_s, h_s, *, tt, nb):
    @pl.when(pl.program_id(0) == 0)
    def _():
        pad[:, 0:SUBLANES, :] = jnp.zeros((nb, SUBLANES, D_GROUP), F32)
        h_s[...] = jnp.zeros_like(h_s)

    for b in range(nb):
        x = x_ref[b]
        pad[b, SUBLANES:SUBLANES + tt, :] = x
        xc = cw_ref[CONV_WIDTH - 1:CONV_WIDTH, :] * x + cb_ref[...]
        for d in range(1, CONV_WIDTH):
            tap = CONV_WIDTH - 1 - d
            xc = xc + cw_ref[tap:tap + 1, :] * pad[b, SUBLANES - d:SUBLANES - d + tt, :]
        pad[b, 0:SUBLANES, :] = x[tt - SUBLANES:tt, :]

        xb = xc.astype(BF16)
        r = _sigmoid(_dot(xb, wa_ref[...]) + ba_ref[...])
        i = _sigmoid(_dot(xb, wx_ref[...]) + bx_ref[...])
        log_a = -LRU_C * r * _softplus(-lam_ref[...])
        a_s[b] = jnp.exp(log_a)
        th = jnp.tanh(log_a)
        u_s[b] = jnp.sqrt(-2.0 * th / (1.0 - th)) * (i * xc)

    def group(gi, hs):
        base = pl.multiple_of(gi * SUBLANES, SUBLANES)
        a8 = [a_s[b, pl.ds(base, SUBLANES), :] for b in range(nb)]
        u8 = [u_s[b, pl.ds(base, SUBLANES), :] for b in range(nb)]
        hs = list(hs)
        rows = [[] for _ in range(nb)]
        for j in range(SUBLANES):
            for b in range(nb):
                hs[b] = a8[b][j:j + 1, :] * hs[b] + u8[b][j:j + 1, :]
                rows[b].append(hs[b])
        for b in range(nb):
            o_ref[b, pl.ds(base, SUBLANES), :] = jnp.concatenate(rows[b], axis=0)
        return tuple(hs)

    hs = lax.fori_loop(0, tt // SUBLANES, group, tuple(h_s[b] for b in range(nb)))
    for b in range(nb):
        h_s[b] = hs[b]


def _outproj_kernel(x_ref, yft_ref, yst_ref, yr_ref, lx_ref, gates_ref, w_ref, fg_ref,
                    cw_ref, cb_ref, wa_ref, ba_ref, wx_ref, bx_ref, lam_ref, o_ref,
                    pad, a_s, u_s, yl_s, h_s, *, final, tm, nb):
    _lru_tile(lx_ref, cw_ref, cb_ref, wa_ref, ba_ref, wx_ref, bx_ref, lam_ref, yl_s,
              pad, a_s, u_s, h_s, tt=tm, nb=nb)
    for b in range(nb):
        acc = x_ref[b]
        ys = (yft_ref[b].T, yst_ref[b].T, yr_ref[b], yl_s[b])
        for gi, y in enumerate(ys):
            gate = gates_ref[b, :, gi * D_GROUP:(gi + 1) * D_GROUP]
            y = y * (gate * _sigmoid(gate))
            acc = acc + _dot(y.astype(BF16), w_ref[gi * D_GROUP:(gi + 1) * D_GROUP, :])
        if final:
            ms = jnp.mean(acc * acc, axis=-1, keepdims=True)
            acc = acc * lax.rsqrt(ms + RMS_EPS) * fg_ref[...]
        o_ref[b] = acc


def _outproj(x, y_fox_t, y_sb_t, y_rw, lx, gates, w, final_g, lru_prm, *, tm, final):
    b, s, _ = x.shape
    g = D_GROUP
    seq = lambda width: pl.BlockSpec((b, tm, width), lambda i: (0, i, 0))
    chan = pl.BlockSpec((b, g, tm), lambda i: (0, 0, i))
    big = pltpu.VMEM((b, tm, g), F32)
    return pl.pallas_call(
        functools.partial(_outproj_kernel, final=final, tm=tm, nb=b),
        grid=(s // tm,),
        in_specs=[seq(D_MODEL), chan, chan, seq(g), seq(g), seq(4 * g),
                  _full_spec(w), _full_spec(final_g)] + [_full_spec(a) for a in lru_prm],
        out_specs=seq(D_MODEL),
        out_shape=jax.ShapeDtypeStruct((b, s, D_MODEL), F32),
        scratch_shapes=[pltpu.VMEM((b, tm + SUBLANES, g), F32), big, big, big,
                        pltpu.VMEM((b, 1, g), F32)],
        compiler_params=_params(1),
        name="outproj",
    )(x, y_fox_t, y_sb_t, y_rw, lx, gates, w, final_g, *lru_prm)


def _w_in_segments():
    g, h, r = D_GROUP, N_HEADS, RWKV_LORA
    o_ff = 4 * g
    o_sb = o_ff + h
    o_rw = o_sb + 4 * g
    o_rg = o_rw + 3 * g + 2 * r
    o_lx = o_rg + g
    o_lg = o_lx + g
    segs = [(0, C_FQ, 2 * g), (o_sb, C_SQ, 2 * g), (2 * g, C_FV, g), (o_sb + 2 * g, C_SV, g),
             (3 * g, C_GATES, g), (o_sb + 3 * g, C_GATES + g, g), (o_rg, C_GATES + 2 * g, g),
             (o_lg, C_GATES + 3 * g, g),
             (o_rw, C_RKV, 3 * g), (o_lx, C_LX, g),
             (o_rw + 3 * g, C_MISC, 2 * r), (o_ff, C_MISC + FF_LANE, h)]
    return segs


def _w_in_kernel(w_ref, o_ref):
    o_ref[...] = jnp.zeros_like(o_ref)
    for src, dst, width in _w_in_segments():
        o_ref[0, :, dst:dst + width] = w_ref[0, :, src:src + width].astype(BF16)


def _permute_w_in(w_in, *, tr):
    depth, d, n_in = w_in.shape
    return pl.pallas_call(
        _w_in_kernel,
        grid=(depth, d // tr),
        in_specs=[pl.BlockSpec((1, tr, n_in), lambda l, i: (l, i, 0))],
        out_specs=pl.BlockSpec((1, tr, N_PROJ), lambda l, i: (l, i, 0)),
        out_shape=jax.ShapeDtypeStruct((depth, d, N_PROJ), BF16),
        compiler_params=_params(2),
        name="w_in_layout",
    )(w_in)


def _block_diag(w):
    h, n, _ = w.shape
    eye = jnp.eye(h, dtype=w.dtype)
    return jnp.einsum('hij,hk->hikj', w, eye).reshape(h * n, h * n)


def _pick_tile(s, pref):
    t = pref
    while s % t:
        t //= 2
    return t


def kernel(x, norm_g, w_in, b_forget, rwkv_mu, rwkv_w0, rwkv_w2, rwkv_a0, rwkv_a2, rwkv_k_k,
           rwkv_k_a, rwkv_r_k, rwkv_ln_g, rwkv_ln_b, lru_conv_w, lru_conv_b, lru_w_a, lru_b_a,
           lru_w_x, lru_b_x, lru_lambda, w_out, final_g):
    b, s, d = x.shape
    depth = w_in.shape[0]
    g, h, dh, r = D_GROUP, N_HEADS, HEAD_DIM, RWKV_LORA
    tm = _pick_tile(s, 512)
    tq = _pick_tile(s, 256)
    tk_fox = _pick_tile(s, 2 * tq)
    tt = _pick_tile(s, 512)
    row = lambda a: a.reshape(1, -1).astype(F32)

    bdm = _block_diag(jnp.ones((h, dh, dh), BF16))
    w_in_k = _permute_w_in(w_in, tr=LANES)
    for l in range(depth):
        fbias = jnp.zeros((1, LANES), F32).at[0, FF_LANE:FF_LANE + h].set(b_forget[l])
        mu = rwkv_mu[l]
        consts = [row(norm_g[l]), w_in_k, fbias, row(mu[:N_RKV]),
                  jnp.zeros((1, LANES), F32).at[0, :2 * r].set(mu[N_RKV:])]
        (fq, fk, qaug, kaug, fvt, sq, sk, svt, gates, rkv, lx, misc, f2,
         norms) = _inproj(x, consts, l, tm=tm)

        per_head = lambda a: a.transpose(0, 2, 1).reshape(-1)
        fend = per_head(f2[:, tq - 1::tq, FF_LANE:FF_LANE + h])
        qmax = per_head(jnp.repeat(jnp.sqrt(norms[:, :, 0, 0:h]), tm // tq, axis=1))
        kmax = jnp.sqrt(jnp.max(norms[:, :, 0, h:2 * h], axis=1)).reshape(-1)
        y_fox_t = _fox_attention(fend, qmax, kmax, fq, qaug, fk, kaug, fvt, tq=tq, tk=tk_fox)
        y_sb_t = _sb_attention(sq, sk, svt, tq=tq)

        pad_rows = lambda a, lo: jnp.zeros((LANES, g), F32).at[lo:lo + r].set(a).astype(BF16)
        rw_prm = [row(rwkv_w0[l]), pad_rows(rwkv_w2[l], 0), row(rwkv_a0[l]), pad_rows(rwkv_a2[l], r),
                  row(rwkv_k_k[l]), row(rwkv_k_a[l]), row(rwkv_r_k[l]), row(rwkv_ln_g[l]),
                  row(rwkv_ln_b[l])]
        y_rw = _rwkv(rkv, misc, rw_prm, bdm, tt=tt)

        lru_prm = [lru_conv_w[l].astype(F32), row(lru_conv_b[l]),
                   _block_diag(lru_w_a[l]).astype(BF16), row(lru_b_a[l]),
                   _block_diag(lru_w_x[l]).astype(BF16), row(lru_b_x[l]), row(lru_lambda[l])]
        x = _outproj(x, y_fox_t, y_sb_t, y_rw, lx, gates, w_out[l].astype(BF16), row(final_g),
                     lru_prm, tm=tm, final=(l == depth - 1))
    return x
```

```python
import functools

import jax
import jax.numpy as jnp
from jax import lax
from jax.experimental import pallas as pl
from jax.experimental.pallas import tpu as pltpu

F32 = jnp.float32
BF16 = jnp.bfloat16

D_MODEL = 1024
D_GROUP = 256
N_HEADS = 4
HEAD_DIM = 64
RWKV_LORA = 32
CONV_WIDTH = 4
LRU_C = 8.0
RMS_EPS = 1e-6
GN_EPS = 64e-5
N_RKV = 3 * D_GROUP

LANES = 128
SUBLANES = 8
VMEM_LIMIT_BYTES = 56 * 1024 * 1024

C_FQ = 0
C_FK = C_FQ + D_GROUP
C_SQ = C_FK + D_GROUP
C_SK = C_SQ + D_GROUP
C_FV = C_SK + D_GROUP
C_SV = C_FV + D_GROUP
C_GATES = C_SV + D_GROUP
C_RKV = C_GATES + 4 * D_GROUP
C_LX = C_RKV + N_RKV
C_MISC = C_LX + D_GROUP
N_PROJ = C_MISC + LANES
FF_LANE = 2 * RWKV_LORA
HEADS_PER_PAIR = LANES // HEAD_DIM
N_PAIRS = N_HEADS // HEADS_PER_PAIR
AUG_STRIDE = 8
AUG_K_F = 0
AUG_K_ONE = 3
AUG_Q_ONE = 0
AUG_Q_F = 3

CUMSUM_BLOCK = LANES
V_AUG = HEAD_DIM + 16
LOG2E = 1.4426950408889634

FOX_SKIP_LOG2 = -150.0
FOX_NORM_MARGIN = 1.02
FOX_F_SLACK = 1.0

NEG_BIG = -1e30
SB_SKIP_LOG2 = -150.0
RWKV_CHUNK = 64
RWKV_UNROLL = 4


def _dot(a, b):
    return jnp.dot(a, b, preferred_element_type=F32)


def _dot_nt(a, b):
    return lax.dot_general(a, b, (((1,), (1,)), ((), ())), preferred_element_type=F32)


def _split2(x):
    hi = x.astype(BF16)
    lo = (x - hi.astype(F32)).astype(BF16)
    return hi, lo


def _split3(x):
    hi = x.astype(BF16)
    r1 = x - hi.astype(F32)
    mid = r1.astype(BF16)
    lo = (r1 - mid.astype(F32)).astype(BF16)
    return hi, mid, lo


def _dot_sel_lhs(sel, x):
    hi, mid, lo = _split3(x)
    return _dot(sel, hi) + (_dot(sel, mid) + _dot(sel, lo))


def _dot_sel_rhs(x, sel):
    hi, lo = _split2(x)
    return _dot(hi, sel) + _dot(lo, sel)


def _softplus(x):
    return jnp.maximum(x, 0.0) + jnp.log(1.0 + jnp.exp(-jnp.abs(x)))


def _log_sigmoid(x):
    return jnp.minimum(x, 0.0) - jnp.log(1.0 + jnp.exp(-jnp.abs(x)))


def _log2_sigmoid_of_log2(x2):
    return jnp.minimum(x2, 0.0) - jnp.log2(1.0 + jnp.exp2(-jnp.abs(x2)))


def _sigmoid(x):
    return 1.0 / (1.0 + jnp.exp(-x))


def _iota(shape, dim):
    return lax.broadcasted_iota(jnp.int32, shape, dim)


def _full_spec(a):
    return pl.BlockSpec(a.shape, lambda *_: (0,) * a.ndim)


def _params(n_grid):
    return pltpu.CompilerParams(dimension_semantics=("arbitrary",) * n_grid,
                                vmem_limit_bytes=VMEM_LIMIT_BYTES)


def _inproj_kernel(x_ref, g_ref, w_ref, fb_ref, mu_rkv_ref, mu_misc_ref,
                   fq_ref, fk_ref, qaug_ref, kaug_ref, fvt_ref, sq_ref, sk_ref, svt_ref, gates_ref,
                   rkv_ref, lx_ref, misc_ref, f2_ref, norms_ref, ftot, v_stage, pad_rkv, pad_misc):
    @pl.when(pl.program_id(1) == 0)
    def _():
        ftot[...] = jnp.zeros_like(ftot)
        pad_rkv[0:SUBLANES, :] = jnp.zeros((SUBLANES, N_RKV), F32)
        pad_misc[0:SUBLANES, :] = jnp.zeros((SUBLANES, LANES), F32)

    x = x_ref[0]
    tm = x.shape[0]
    ms = jnp.mean(x * x, axis=-1, keepdims=True)
    h = (x * lax.rsqrt(ms + RMS_EPS) * g_ref[...]).astype(BF16)
    scale = HEAD_DIM ** -0.5

    def proj(c0, width):
        return _dot(h, w_ref[0, :, c0:c0 + width])

    misc = proj(C_MISC, LANES)
    gates_ref[0] = proj(C_GATES, 4 * D_GROUP)
    lx_ref[0] = proj(C_LX, D_GROUP)

    for val, pad, mu_ref, out_ref in ((proj(C_RKV, N_RKV), pad_rkv, mu_rkv_ref, rkv_ref),
                                      (misc, pad_misc, mu_misc_ref, misc_ref)):
        pad[SUBLANES:SUBLANES + tm, :] = val
        prev = pad[SUBLANES - 1:SUBLANES - 1 + tm, :]
        pad[0:SUBLANES, :] = val[tm - SUBLANES:tm, :]
        out_ref[0] = val + (prev - val) * mu_ref[...]

    lf = _log_sigmoid(misc + fb_ref[...])
    cb = CUMSUM_BLOCK
    lower = (_iota((cb, cb), 1) <= _iota((cb, cb), 0)).astype(BF16)
    run = ftot[...]
    blocks = []
    for r0 in range(0, tm, cb):
        blk = _dot_sel_lhs(lower, lf[r0:r0 + cb, :]) + run
        run = blk[cb - 1:cb, :]
        blocks.append(blk)
    ftot[...] = run
    f = jnp.concatenate(blocks, axis=0)
    f2 = f * LOG2E
    f2_ref[0] = f2
    hi = f2.astype(BF16).astype(F32)
    mid = (f2 - hi).astype(BF16).astype(F32)
    lo = (f2 - hi - mid).astype(BF16).astype(F32)

    pq = proj(C_FQ, D_GROUP) * (scale * LOG2E)
    pk = proj(C_FK, D_GROUP)
    sq = proj(C_SQ, D_GROUP) * (scale * LOG2E)
    sk = proj(C_SK, D_GROUP)
    for pr in range(N_PAIRS):
        cols = slice(pr * LANES, (pr + 1) * LANES)
        fq_ref[0, pr] = pq[:, cols].astype(BF16)
        fk_ref[0, pr] = pk[:, cols].astype(BF16)
        sq_ref[0, pr] = sq[:, cols].astype(BF16)
        sk_ref[0, pr] = sk[:, cols].astype(BF16)

    lane = _iota((tm, LANES), 1)
    in_group = lane % AUG_STRIDE
    valid = lane < AUG_STRIDE * N_HEADS
    aug_q = jnp.where(jnp.logical_and(valid, jnp.logical_and(in_group >= AUG_Q_ONE,
                                                             in_group < AUG_Q_ONE + 3)), 1.0, 0.0)
    aug_k = jnp.where(jnp.logical_and(valid, jnp.logical_and(in_group >= AUG_K_ONE,
                                                             in_group < AUG_K_ONE + 3)), 1.0, 0.0)
    for hd in range(N_HEADS):
        for i3, piece in enumerate((hi, mid, lo)):
            col = jnp.broadcast_to(piece[:, FF_LANE + hd:FF_LANE + hd + 1], (tm, LANES))
            aug_q = jnp.where(lane == AUG_STRIDE * hd + AUG_Q_F + i3, col, aug_q)
            aug_k = jnp.where(lane == AUG_STRIDE * hd + AUG_K_F + i3, -col, aug_k)
    qaug_ref[0] = aug_q.astype(BF16)
    kaug_ref[0] = aug_k.astype(BF16)

    head_of_row = _iota((D_GROUP, LANES), 0) // HEAD_DIM
    nlane = _iota((SUBLANES, LANES), 1)
    norms = jnp.zeros((SUBLANES, LANES), F32)
    for side, pv in enumerate((pq, pk)):
        sel = (head_of_row + side * N_HEADS == _iota((D_GROUP, LANES), 1)).astype(BF16)
        top = jnp.max(_dot((pv * pv).astype(BF16), sel), axis=0, keepdims=True)
        keep = jnp.logical_and(nlane >= side * N_HEADS, nlane < (side + 1) * N_HEADS)
        norms = jnp.where(keep, top, norms)
    norms_ref[0, 0] = norms
    v_stage[...] = proj(C_SV, D_GROUP)
    svt_ref[0] = v_stage[...].T.astype(BF16)
    v_stage[...] = proj(C_FV, D_GROUP)
    vt = v_stage[...].T.astype(BF16)
    ones_rows = (_iota((V_AUG - HEAD_DIM, tm), 0) == 0).astype(BF16)
    for hd in range(N_HEADS):
        fvt_ref[0, hd * V_AUG:hd * V_AUG + HEAD_DIM, :] = vt[hd * HEAD_DIM:(hd + 1) * HEAD_DIM, :]
        fvt_ref[0, hd * V_AUG + HEAD_DIM:(hd + 1) * V_AUG, :] = ones_rows


def _inproj(x, consts, layer, *, tm):
    b, s, _ = x.shape
    w_spec = pl.BlockSpec((1,) + consts[1].shape[1:], lambda bi, i: (layer, 0, 0))
    seq = lambda width: pl.BlockSpec((1, tm, width), lambda bi, i: (bi, i, 0))
    pairs = pl.BlockSpec((1, N_PAIRS, tm, LANES), lambda bi, i: (bi, 0, i, 0))
    pairs_shape = jax.ShapeDtypeStruct((b, N_PAIRS, s, LANES), BF16)
    chan = lambda rows: pl.BlockSpec((1, rows, tm), lambda bi, i: (bi, 0, i))
    chan_shape = lambda rows: jax.ShapeDtypeStruct((b, rows, s), BF16)
    seq_shape = lambda width, dtype=F32: jax.ShapeDtypeStruct((b, s, width), dtype)
    return pl.pallas_call(
        _inproj_kernel,
        grid=(b, s // tm),
        in_specs=[seq(D_MODEL), _full_spec(consts[0]), w_spec] + [_full_spec(a) for a in consts[2:]],
        out_specs=[pairs, pairs, seq(LANES), seq(LANES), chan(N_HEADS * V_AUG),
                   pairs, pairs, chan(D_GROUP),
                   seq(4 * D_GROUP), seq(N_RKV), seq(D_GROUP), seq(LANES), seq(LANES),
                   pl.BlockSpec((1, 1, SUBLANES, LANES), lambda bi, i: (bi, i, 0, 0))],
        out_shape=[pairs_shape, pairs_shape, seq_shape(LANES, BF16), seq_shape(LANES, BF16),
                   chan_shape(N_HEADS * V_AUG), pairs_shape, pairs_shape, chan_shape(D_GROUP),
                   seq_shape(4 * D_GROUP), seq_shape(N_RKV), seq_shape(D_GROUP), seq_shape(LANES),
                   seq_shape(LANES), jax.ShapeDtypeStruct((b, s // tm, SUBLANES, LANES), F32)],
        scratch_shapes=[pltpu.VMEM((1, LANES), F32), pltpu.VMEM((tm, D_GROUP), F32),
                        pltpu.VMEM((tm + SUBLANES, N_RKV), F32),
                        pltpu.VMEM((tm + SUBLANES, LANES), F32)],
        compiler_params=_params(2),
        name="inproj",
    )(x, *consts)


def _fox_kernel(fend_ref, qmax_ref, kmax_ref, q_ref, qaug_ref, k_ref, kaug_ref, vt_ref, o_ref,
                s_scr, p_scr, mx_scr, al_scr, m_scr, acc_scr, *, tq, tk):
    i = pl.program_id(1)
    nq = pl.num_programs(1)
    heads = list(range(N_HEADS))
    lane = _iota((tq, LANES), 1)
    q = []
    for hd in heads:
        own_half = lane // HEAD_DIM == hd % HEADS_PER_PAIR
        own_aug = lane // AUG_STRIDE == hd
        q.append(jnp.concatenate(
            [jnp.where(own_half, q_ref[0, hd // HEADS_PER_PAIR], jnp.zeros((), BF16)),
             jnp.where(own_aug, qaug_ref[0], jnp.zeros((), BF16))], axis=1))
    n_full = (i * tq) // tk

    def first_live_block(hd):
        bh = pl.program_id(0) * N_HEADS + hd
        qk_bound = FOX_NORM_MARGIN * 2.0 * qmax_ref[bh * nq + i] * kmax_ref[bh]
        f_tile = fend_ref[bh * nq + jnp.maximum(i - 1, 0)]

        def first_live(jj, first):
            j = n_full - 1 - jj
            f_end = fend_ref[bh * nq + (j + 1) * (tk // tq) - 1]
            live = qk_bound + (f_tile - f_end) + FOX_F_SLACK >= FOX_SKIP_LOG2
            return jnp.where(live, j, first)

        return lax.fori_loop(0, n_full, first_live, n_full)

    base = functools.reduce(jnp.minimum, [first_live_block(hd) for hd in heads])
    n_eff = n_full - base

    def key_rows(j):
        return pl.ds(pl.multiple_of(j * tk, tk), tk)

    def scores_to(j, slot):
        kaug = kaug_ref[0, key_rows(j), :]
        kp = [jnp.concatenate([k_ref[0, pr, key_rows(j), :], kaug], axis=1) for pr in range(N_PAIRS)]
        s = [_dot_nt(kp[hd // HEADS_PER_PAIR], q[hd]) for hd in heads]
        for hd in heads:
            s_scr[hd, slot] = s[hd]
            mx_scr[hd, slot] = jnp.max(s[hd], axis=0, keepdims=True)

    def weighted_values(j, slot):
        return [_dot(vt_ref[0, hd * V_AUG:(hd + 1) * V_AUG, key_rows(j)], p_scr[hd, slot])
                for hd in heads]

    def softmax_to(s, mx, slot):
        m_old = [m_scr[hd] for hd in heads]
        m_new = [jnp.maximum(a, b) for a, b in zip(m_old, mx)]
        p = [jnp.exp2(a - b).astype(BF16) for a, b in zip(s, m_new)]
        for hd in heads:
            m_scr[hd] = m_new[hd]
            al_scr[hd, slot] = jnp.exp2(m_old[hd] - m_new[hd])
            p_scr[hd, slot] = p[hd]

    def stage(local, cur):
        j = base + local
        nxt = 1 - cur
        pv_prev = weighted_values(jnp.maximum(j - 1, 0), nxt)
        scores_to(j + 1, nxt)
        softmax_to([s_scr[hd, cur] for hd in heads], [mx_scr[hd, cur] for hd in heads], cur)
        for hd in heads:
            acc_scr[hd] = al_scr[hd, nxt] * acc_scr[hd] + pv_prev[hd]

    def tail(cur):
        nxt = 1 - cur
        pv_prev = weighted_values(jnp.maximum(n_full - 1, 0), nxt)
        mask = n_full * tk + _iota((tk, tq), 0) <= i * tq + _iota((tk, tq), 1)
        s = [jnp.where(mask, s_scr[hd, cur], NEG_BIG) for hd in heads]
        softmax_to(s, [jnp.max(x, axis=0, keepdims=True) for x in s], cur)
        pv_last = weighted_values(n_full, cur)
        for hd in heads:
            acc = al_scr[hd, cur] * (al_scr[hd, nxt] * acc_scr[hd] + pv_prev[hd]) + pv_last[hd]
            o_ref[0, hd * HEAD_DIM:(hd + 1) * HEAD_DIM, :] = (
                acc[0:HEAD_DIM] / acc[HEAD_DIM:HEAD_DIM + 1])

    m_scr[...] = jnp.full(m_scr.shape, NEG_BIG, F32)
    acc_scr[...] = jnp.zeros_like(acc_scr)
    p_scr[:, 1] = jnp.zeros((N_HEADS, tk, tq), BF16)
    al_scr[:, 1] = jnp.ones((N_HEADS, 1, tq), F32)
    scores_to(base, 0)

    def pair(jj, carry):
        stage(2 * jj, 0)
        stage(2 * jj + 1, 1)
        return carry

    lax.fori_loop(0, n_eff // 2, pair, 0)

    @pl.when(n_eff % 2 == 1)
    def _():
        stage(n_eff - 1, 0)
        tail(1)

    @pl.when(n_eff % 2 == 0)
    def _():
        tail(0)


def _fox_attention(fend, qmax, kmax, q, qaug, k, kaug, vt, *, tq, tk):
    b, _, s, _ = q.shape
    h = N_HEADS
    grid_spec = pltpu.PrefetchScalarGridSpec(
        num_scalar_prefetch=3,
        grid=(b, s // tq),
        in_specs=[pl.BlockSpec((1, N_PAIRS, tq, LANES), lambda bi, i, *_: (bi, 0, i, 0)),
                  pl.BlockSpec((1, tq, LANES), lambda bi, i, *_: (bi, i, 0)),
                  pl.BlockSpec((1, N_PAIRS, s, LANES), lambda bi, i, *_: (bi, 0, 0, 0)),
                  pl.BlockSpec((1, s, LANES), lambda bi, i, *_: (bi, 0, 0)),
                  pl.BlockSpec((1, h * V_AUG, s), lambda bi, i, *_: (bi, 0, 0))],
        out_specs=pl.BlockSpec((1, h * HEAD_DIM, tq), lambda bi, i, *_: (bi, 0, i)),
        scratch_shapes=[pltpu.VMEM((h, 2, tk, tq), F32), pltpu.VMEM((h, 2, tk, tq), BF16),
                        pltpu.VMEM((h, 2, 1, tq), F32), pltpu.VMEM((h, 2, 1, tq), F32),
                        pltpu.VMEM((h, 1, tq), F32), pltpu.VMEM((h, V_AUG, tq), F32)])
    return pl.pallas_call(
        functools.partial(_fox_kernel, tq=tq, tk=tk),
        grid_spec=grid_spec,
        out_shape=jax.ShapeDtypeStruct((b, h * HEAD_DIM, s), F32),
        compiler_params=_params(2),
        name="fox_attention",
    )(fend, qmax, kmax, q, qaug, k, kaug, vt)


def _sb_kernel(q_ref, k_ref, vt_ref, o_ref, *, tq):
    i = pl.program_id(1)
    tk = tq
    heads = list(range(N_HEADS))
    lane = _iota((tq, LANES), 1)
    q = [jnp.where(lane // HEAD_DIM == hd % HEADS_PER_PAIR, q_ref[0, hd // HEADS_PER_PAIR],
                   jnp.zeros((), BF16)) for hd in heads]
    later = (_iota((tk, tk), 0) < _iota((tk, tk), 1)).astype(BF16)

    def each(f, *xs):
        return [f(*a) for a in zip(*xs)]

    def block(j, rest_q, acc, masked):
        ks = pl.multiple_of(j * tk, tk)
        kp = [k_ref[0, pr, pl.ds(ks, tk), :] for pr in range(N_PAIRS)]
        z = [_dot_nt(kp[hd // HEADS_PER_PAIR], q[hd]) for hd in heads]
        log_keep = each(lambda x: _log2_sigmoid_of_log2(-x), z)
        if masked:
            mask = _iota((tk, tq), 0) < _iota((tk, tq), 1)
            log_keep = each(lambda x: jnp.where(mask, x, 0.0), log_keep)
        split = each(_split2, log_keep)
        rest_in = each(lambda hl: _dot(later, hl[0]) + _dot(later, hl[1]), split)
        att = each(lambda x, lk, ri, rq: jnp.exp2(x + lk + ri + rq), z, log_keep, rest_in, rest_q)
        if masked:
            att = each(lambda x: jnp.where(mask, x, 0.0), att)
        pv = [_dot(vt_ref[0, hd * HEAD_DIM:(hd + 1) * HEAD_DIM, pl.ds(ks, tk)],
                   att[hd].astype(BF16)) for hd in heads]
        acc = each(lambda a, x: a + x, acc, pv)
        rest_q = each(lambda rq, ri, lk: rq + ri[0:1, :] + lk[0:1, :], rest_q, rest_in, log_keep)
        return rest_q, acc

    rest_q, acc = block(i, [jnp.zeros((1, tq), F32)] * N_HEADS,
                        [jnp.zeros((HEAD_DIM, tq), F32)] * N_HEADS, True)

    def cond(c):
        j, rest_q, _ = c
        alive = functools.reduce(jnp.maximum, rest_q)
        return jnp.logical_and(j >= 0, jnp.max(alive) > SB_SKIP_LOG2)

    def body(c):
        j, rest_q, acc = c
        rest_q, acc = block(j, list(rest_q), list(acc), False)
        return j - 1, tuple(rest_q), tuple(acc)

    _, _, acc = lax.while_loop(cond, body, (i - 1, tuple(rest_q), tuple(acc)))
    o_ref[0] = jnp.concatenate(list(acc), axis=0)


def _sb_attention(q, k, vt, *, tq):
    b, _, s, _ = q.shape
    h = N_HEADS
    return pl.pallas_call(
        functools.partial(_sb_kernel, tq=tq),
        grid=(b, s // tq),
        in_specs=[pl.BlockSpec((1, N_PAIRS, tq, LANES), lambda bi, i: (bi, 0, i, 0)),
                  pl.BlockSpec((1, N_PAIRS, s, LANES), lambda bi, i: (bi, 0, 0, 0)),
                  pl.BlockSpec((1, h * HEAD_DIM, s), lambda bi, i: (bi, 0, 0))],
        out_specs=pl.BlockSpec((1, h * HEAD_DIM, tq), lambda bi, i: (bi, 0, i)),
        out_shape=jax.ShapeDtypeStruct((b, h * HEAD_DIM, s), F32),
        compiler_params=_params(2),
        name="sb_attention",
    )(q, k, vt)


def _rwkv_kernel(p_ref, m_ref, w0_ref, w2_ref, a0_ref, a2_ref,
                 kk_ref, ka_ref, rk_ref, lng_ref, lnb_ref, bdm_ref, o_ref,
                 r_s, k_s, v_s, kn_s, al_s, lw_s,
                 wt_s, u0_s, o0_s, mrb_s, rt_s, bh_s, kh_s, pc_s, y_s, ht_s, *, tt, nb):
    c = RWKV_CHUNK
    g = D_GROUP
    n_chunks = tt // c

    @pl.when(pl.program_id(0) == 0)
    def _():
        ht_s[...] = jnp.zeros_like(ht_s)

    bdm = bdm_ref[...]
    bdm_f = bdm.astype(F32)

    for b in range(nb):
        p = p_ref[b]
        misc = m_ref[b]
        k = p[:, g:2 * g]

        w = -_softplus(-(w0_ref[...] + _dot(jnp.tanh(misc).astype(BF16), w2_ref[...]))) - 0.5
        alpha = _sigmoid(a0_ref[...] + _dot(misc.astype(BF16), a2_ref[...]))
        kn = k * kk_ref[...]
        ss = _dot_sel_rhs(kn * kn, bdm)
        r_s[b] = p[:, 0:g]
        k_s[b] = k * (1.0 + (alpha - 1.0) * ka_ref[...])
        v_s[b] = p[:, 2 * g:3 * g]
        kn_s[b] = kn * lax.rsqrt(jnp.maximum(ss, 1e-12))
        al_s[b] = alpha
        lw_s[b] = -jnp.exp(w)

    row = _iota((c, g), 0)
    col = _iota((c, g), 1) % c
    strict = col < row
    incl = col <= row
    eye = (col == row).astype(F32)
    lower_c = (_iota((c, c), 1) <= _iota((c, c), 0)).astype(BF16)
    level_masks = []
    m = 1
    while m < c:
        level_masks.append(jnp.logical_and(
            strict, jnp.logical_and(row // (2 * m) == col // (2 * m), row // m != col // m)))
        m *= 2

    def bd(x):
        return jnp.concatenate([x.astype(BF16)] * N_HEADS, axis=0) * bdm

    def mm(a, b_bf16):
        return _dot(a.astype(BF16), b_bf16)

    def mm_nt(a, b_bf16):
        return _dot_nt(a.astype(BF16), b_bf16)

    def each(f, *xs):
        return [f(*a) for a in zip(*xs)]

    def phase_a(chains):
        sls = [pl.ds(pl.multiple_of(ci * c, c), c) for _, ci in chains]
        ld = lambda ref: [ref[b, sl, :] for (b, _), sl in zip(chains, sls)]
        r_c, k_c, v_c, kn_c, al_c, lw_c = (ld(s) for s in (r_s, k_s, v_s, kn_s, al_s, lw_s))
        cl = each(lambda x: _dot_sel_lhs(lower_c, x), lw_c)
        cl_last = each(lambda x: x[c - 1:c, :], cl)
        a_t = each(lambda kn, x, lw: -kn * jnp.exp(x - lw), kn_c, cl, lw_c)
        r_t = each(lambda r, x: r * jnp.exp(x), r_c, cl)
        q_inv = each(lambda x: jnp.exp(-x), cl)
        p_rem = each(lambda xl, x: jnp.exp(xl - x), cl_last, cl)
        kna = each(lambda kn, al: kn * al, kn_c, al_c)
        ar = each(lambda a, r: jnp.concatenate([a, r], axis=0), a_t, r_t)
        s_b = each(lambda x, kb, qi: mm_nt(x, bd(kb * qi)), ar, kna, q_inv)
        s_k = each(lambda x, kk, qi: mm_nt(x, bd(kk * qi)), ar, k_c, q_inv)
        n = each(lambda x: jnp.where(strict, x[0:c], 0.0), s_b)
        a_ak = each(lambda x: jnp.where(strict, x[0:c], 0.0), s_k)
        m_rb = each(lambda x: jnp.where(incl, x[c:2 * c], 0.0), s_b)
        m_rk = each(lambda x: jnp.where(incl, x[c:2 * c], 0.0), s_k)

        inv = each(lambda x: eye + jnp.where(level_masks[0], x, 0.0), n)
        for lm in level_masks[1:]:
            half = each(lambda d, x: mm(d, bd(jnp.where(lm, x, 0.0))), inv, n)
            inv = each(lambda d, hf: d + mm(hf, bd(d)), inv, half)

        v_bd = each(bd, v_c)
        akv = each(mm, a_ak, v_bd)
        wt = each(lambda d, a: mm(d, bd(a)), inv, a_t)
        u0 = each(lambda d, x: mm(d, bd(x)), inv, akv)
        o0 = each(mm, m_rk, v_bd)
        bh = each(lambda x, p: x * p, kna, p_rem)
        kh = each(lambda x, p: x * p, k_c, p_rem)
        for ref, vals in zip((wt_s, u0_s, o0_s, mrb_s, rt_s, bh_s, kh_s),
                             (wt, u0, o0, m_rb, r_t, bh, kh)):
            for (b, _), sl, val in zip(chains, sls, vals):
                ref[b, sl, :] = val
        for (b, ci), xl in zip(chains, cl_last):
            pc_s[b, pl.ds(pl.multiple_of(ci * SUBLANES, SUBLANES), SUBLANES), :] = (
                jnp.broadcast_to(jnp.exp(xl), (SUBLANES, g)))

    def phase_b(ci):
        sl = pl.ds(pl.multiple_of(ci * c, c), c)
        bs = list(range(nb))
        ht = [ht_s[b] for b in bs]
        wr = [jnp.concatenate([wt_s[b, sl, :], rt_s[b, sl, :]], axis=0) for b in bs]
        wrh = each(lambda x, hh: mm_nt(x, hh.astype(BF16)), wr, ht)
        u = [x[0:c] + u0_s[b, sl, :] for x, b in zip(wrh, bs)]
        uv_t = [jnp.concatenate([x, v_s[b, sl, :]], axis=0).T for x, b in zip(u, bs)]
        bk = [jnp.concatenate([bh_s[b, sl, :], kh_s[b, sl, :]], axis=0) for b in bs]
        upd = each(lambda x, y: mm(x, y.astype(BF16)), uv_t, bk)
        mu_ = [mm(mrb_s[b, sl, :], bd(x)) for x, b in zip(u, bs)]
        for b in bs:
            p_c = pc_s[b, pl.ds(pl.multiple_of(ci * SUBLANES, SUBLANES), 1), :]
            y_s[b, sl, :] = wrh[b][c:2 * c] + mu_[b] + o0_s[b, sl, :]
            ht_s[b] = ht[b] * p_c + bdm_f * upd[b]

    def loop_a(ci, carry):
        phase_a([(b, ci * RWKV_UNROLL + j) for j in range(RWKV_UNROLL) for b in range(nb)])
        return carry

    def loop_b(ci, carry):
        phase_b(ci)
        return carry

    lax.fori_loop(0, n_chunks // RWKV_UNROLL, loop_a, 0)
    lax.fori_loop(0, n_chunks, loop_b, 0)

    inv_n = 1.0 / HEAD_DIM
    for b in range(nb):
        y = y_s[b]
        mean = _dot_sel_rhs(y, bdm) * inv_n
        yc = y - mean
        var = _dot_sel_rhs(yc * yc, bdm) * inv_n
        yn = yc * lax.rsqrt(var + GN_EPS) * lng_ref[...] + lnb_ref[...]
        bonus = _dot_sel_rhs(r_s[b] * k_s[b] * rk_ref[...], bdm) * v_s[b]
        o_ref[b] = yn + bonus


def _rwkv(p_rkv, misc, prm, bdm, *, tt):
    b, s, _ = p_rkv.shape
    g = D_GROUP
    seq = lambda width: pl.BlockSpec((b, tt, width), lambda ti: (0, ti, 0))
    big = pltpu.VMEM((b, tt, g), F32)
    return pl.pallas_call(
        functools.partial(_rwkv_kernel, tt=tt, nb=b),
        grid=(s // tt,),
        in_specs=[seq(N_RKV), seq(LANES)] + [_full_spec(a) for a in prm] + [_full_spec(bdm)],
        out_specs=seq(g),
        out_shape=jax.ShapeDtypeStruct((b, s, g), F32),
        scratch_shapes=[big] * 13
        + [pltpu.VMEM((b, tt // RWKV_CHUNK * SUBLANES, g), F32), big,
           pltpu.VMEM((b, g, g), F32)],
        compiler_params=_params(1),
        name="rwkv7",
    )(p_rkv, misc, *prm, bdm)


def _lru_tile(x_ref, cw_ref, cb_ref, wa_ref, ba_ref, wx_ref, bx_ref, lam_ref, o_ref,
              pad, a_s, u_s, h_s, *, tt, nb):
    @pl.when(pl.program_id(0) == 0)
    def _():
        pad[:, 0:SUBLANES, :] = jnp.zeros((nb, SUBLANES, D_GROUP), F32)
        h_s[...] = jnp.zeros_like(h_s)

    for b in range(nb):
        x = x_ref[b]
        pad[b, SUBLANES:SUBLANES + tt, :] = x
        xc = cw_ref[CONV_WIDTH - 1:CONV_WIDTH, :] * x + cb_ref[...]
        for d in range(1, CONV_WIDTH):
            tap = CONV_WIDTH - 1 - d
            xc = xc + cw_ref[tap:tap + 1, :] * pad[b, SUBLANES - d:SUBLANES - d + tt, :]
        pad[b, 0:SUBLANES, :] = x[tt - SUBLANES:tt, :]

        xb = xc.astype(BF16)
        r = _sigmoid(_dot(xb, wa_ref[...]) + ba_ref[...])
        i = _sigmoid(_dot(xb, wx_ref[...]) + bx_ref[...])
        log_a = -LRU_C * r * _softplus(-lam_ref[...])
        a_s[b] = jnp.exp(log_a)
        th = jnp.tanh(log_a)
        u_s[b] = jnp.sqrt(-2.0 * th / (1.0 - th)) * (i * xc)

    row = _iota((SUBLANES, D_GROUP), 0)

    def rows_above(x, k, fill):
        return jnp.where(row >= k, pltpu.roll(x, k, axis=0), fill)

    def group(gi, hs):
        base = pl.multiple_of(gi * SUBLANES, SUBLANES)
        carry = []
        for b in range(nb):
            a = a_s[b, pl.ds(base, SUBLANES), :]
            u = u_s[b, pl.ds(base, SUBLANES), :]
            k = 1
            while k < SUBLANES:
                u = a * rows_above(u, k, 0.0) + u
                a = a * rows_above(a, k, 1.0)
                k *= 2
            h = a * hs[b] + u
            o_ref[b, pl.ds(base, SUBLANES), :] = h
            carry.append(h[SUBLANES - 1:SUBLANES, :])
        return tuple(carry)

    hs = lax.fori_loop(0, tt // SUBLANES, group, tuple(h_s[b] for b in range(nb)))
    for b in range(nb):
        h_s[b] = hs[b]


def _outproj_kernel(x_ref, yft_ref, yst_ref, yr_ref, lx_ref, gates_ref, w_ref, fg_ref,
                    cw_ref, cb_ref, wa_ref, ba_ref, wx_ref, bx_ref, lam_ref, o_ref,
                    pad, a_s, u_s, yl_s, h_s, *, final, tm, nb):
    _lru_tile(lx_ref, cw_ref, cb_ref, wa_ref, ba_ref, wx_ref, bx_ref, lam_ref, yl_s,
              pad, a_s, u_s, h_s, tt=tm, nb=nb)
    for b in range(nb):
        acc = x_ref[b]
        ys = (yft_ref[b].T, yst_ref[b].T, yr_ref[b], yl_s[b])
        for gi, y in enumerate(ys):
            gate = gates_ref[b, :, gi * D_GROUP:(gi + 1) * D_GROUP]
            y = y * (gate * _sigmoid(gate))
            acc = acc + _dot(y.astype(BF16), w_ref[gi * D_GROUP:(gi + 1) * D_GROUP, :])
        if final:
            ms = jnp.mean(acc * acc, axis=-1, keepdims=True)
            acc = acc * lax.rsqrt(ms + RMS_EPS) * fg_ref[...]
        o_ref[b] = acc


def _outproj(x, y_fox_t, y_sb_t, y_rw, lx, gates, w, final_g, lru_prm, *, tm, final):
    b, s, _ = x.shape
    g = D_GROUP
    seq = lambda width: pl.BlockSpec((b, tm, width), lambda i: (0, i, 0))
    chan = pl.BlockSpec((b, g, tm), lambda i: (0, 0, i))
    big = pltpu.VMEM((b, tm, g), F32)
    return pl.pallas_call(
        functools.partial(_outproj_kernel, final=final, tm=tm, nb=b),
        grid=(s // tm,),
        in_specs=[seq(D_MODEL), chan, chan, seq(g), seq(g), seq(4 * g),
                  _full_spec(w), _full_spec(final_g)] + [_full_spec(a) for a in lru_prm],
        out_specs=seq(D_MODEL),
        out_shape=jax.ShapeDtypeStruct((b, s, D_MODEL), F32),
        scratch_shapes=[pltpu.VMEM((b, tm + SUBLANES, g), F32), big, big, big,
                        pltpu.VMEM((b, 1, g), F32)],
        compiler_params=_params(1),
        name="outproj",
    )(x, y_fox_t, y_sb_t, y_rw, lx, gates, w, final_g, *lru_prm)


def _w_in_segments():
    g, h, r = D_GROUP, N_HEADS, RWKV_LORA
    o_ff = 4 * g
    o_sb = o_ff + h
    o_rw = o_sb + 4 * g
    o_rg = o_rw + 3 * g + 2 * r
    o_lx = o_rg + g
    o_lg = o_lx + g
    segs = [(0, C_FQ, 2 * g), (o_sb, C_SQ, 2 * g), (2 * g, C_FV, g), (o_sb + 2 * g, C_SV, g),
             (3 * g, C_GATES, g), (o_sb + 3 * g, C_GATES + g, g), (o_rg, C_GATES + 2 * g, g),
             (o_lg, C_GATES + 3 * g, g),
             (o_rw, C_RKV, 3 * g), (o_lx, C_LX, g),
             (o_rw + 3 * g, C_MISC, 2 * r), (o_ff, C_MISC + FF_LANE, h)]
    return segs


def _w_in_kernel(w_ref, o_ref):
    o_ref[...] = jnp.zeros_like(o_ref)
    for src, dst, width in _w_in_segments():
        o_ref[0, :, dst:dst + width] = w_ref[0, :, src:src + width].astype(BF16)


def _permute_w_in(w_in, *, tr):
    depth, d, n_in = w_in.shape
    return pl.pallas_call(
        _w_in_kernel,
        grid=(depth, d // tr),
        in_specs=[pl.BlockSpec((1, tr, n_in), lambda l, i: (l, i, 0))],
        out_specs=pl.BlockSpec((1, tr, N_PROJ), lambda l, i: (l, i, 0)),
        out_shape=jax.ShapeDtypeStruct((depth, d, N_PROJ), BF16),
        compiler_params=_params(2),
        name="w_in_layout",
    )(w_in)


def _block_diag(w):
    h, n, _ = w.shape
    eye = jnp.eye(h, dtype=w.dtype)
    return jnp.einsum('hij,hk->hikj', w, eye).reshape(h * n, h * n)


def _pick_tile(s, pref):
    t = pref
    while s % t:
        t //= 2
    return t


def kernel(x, norm_g, w_in, b_forget, rwkv_mu, rwkv_w0, rwkv_w2, rwkv_a0, rwkv_a2, rwkv_k_k,
           rwkv_k_a, rwkv_r_k, rwkv_ln_g, rwkv_ln_b, lru_conv_w, lru_conv_b, lru_w_a, lru_b_a,
           lru_w_x, lru_b_x, lru_lambda, w_out, final_g):
    b, s, d = x.shape
    depth = w_in.shape[0]
    g, h, dh, r = D_GROUP, N_HEADS, HEAD_DIM, RWKV_LORA
    tm = _pick_tile(s, 512)
    tq = _pick_tile(s, 256)
    tk_fox = tq
    tt = _pick_tile(s, 512)
    row = lambda a: a.reshape(1, -1).astype(F32)

    bdm = _block_diag(jnp.ones((h, dh, dh), BF16))
    w_in_k = _permute_w_in(w_in, tr=LANES)
    for l in range(depth):
        fbias = jnp.zeros((1, LANES), F32).at[0, FF_LANE:FF_LANE + h].set(b_forget[l])
        mu = rwkv_mu[l]
        consts = [row(norm_g[l]), w_in_k, fbias, row(mu[:N_RKV]),
                  jnp.zeros((1, LANES), F32).at[0, :2 * r].set(mu[N_RKV:])]
        (fq, fk, qaug, kaug, fvt, sq, sk, svt, gates, rkv, lx, misc, f2,
         norms) = _inproj(x, consts, l, tm=tm)

        per_head = lambda a: a.transpose(0, 2, 1).reshape(-1)
        fend = per_head(f2[:, tq - 1::tq, FF_LANE:FF_LANE + h])
        qmax = per_head(jnp.repeat(jnp.sqrt(norms[:, :, 0, 0:h]), tm // tq, axis=1))
        kmax = jnp.sqrt(jnp.max(norms[:, :, 0, h:2 * h], axis=1)).reshape(-1)
        y_fox_t = _fox_attention(fend, qmax, kmax, fq, qaug, fk, kaug, fvt, tq=tq, tk=tk_fox)
        y_sb_t = _sb_attention(sq, sk, svt, tq=tq)

        pad_rows = lambda a, lo: jnp.zeros((LANES, g), F32).at[lo:lo + r].set(a).astype(BF16)
        rw_prm = [row(rwkv_w0[l]), pad_rows(rwkv_w2[l], 0), row(rwkv_a0[l]), pad_rows(rwkv_a2[l], r),
                  row(rwkv_k_k[l]), row(rwkv_k_a[l]), row(rwkv_r_k[l]), row(rwkv_ln_g[l]),
                  row(rwkv_ln_b[l])]
        y_rw = _rwkv(rkv, misc, rw_prm, bdm, tt=tt)

        lru_prm = [lru_conv_w[l].astype(F32), row(lru_conv_b[l]),
                   _block_diag(lru_w_a[l]).astype(BF16), row(lru_b_a[l]),
                   _block_diag(lru_w_x[l]).astype(BF16), row(lru_b_x[l]), row(lru_lambda[l])]
        x = _outproj(x, y_fox_t, y_sb_t, y_rw, lx, gates, w_out[l].astype(BF16), row(final_g),
                     lru_prm, tm=tm, final=(l == depth - 1))
    return x
```

```python
import functools

import jax
import jax.numpy as jnp
from jax import lax
from jax.experimental import pallas as pl
from jax.experimental.pallas import tpu as pltpu

F32 = jnp.float32
BF16 = jnp.bfloat16

D_MODEL = 1024
D_GROUP = 256
N_HEADS = 4
HEAD_DIM = 64
RWKV_LORA = 32
CONV_WIDTH = 4
LRU_C = 8.0
RMS_EPS = 1e-6
GN_EPS = 64e-5
N_RKV = 3 * D_GROUP

LANES = 128
SUBLANES = 8
VMEM_LIMIT_BYTES = 56 * 1024 * 1024

C_FQ = 0
C_FK = C_FQ + D_GROUP
C_SQ = C_FK + D_GROUP
C_SK = C_SQ + D_GROUP
C_FV = C_SK + D_GROUP
C_SV = C_FV + D_GROUP
C_GATES = C_SV + D_GROUP
C_RKV = C_GATES + 4 * D_GROUP
C_LX = C_RKV + N_RKV
C_MISC = C_LX + D_GROUP
N_PROJ = C_MISC + LANES
FF_LANE = 2 * RWKV_LORA
HEADS_PER_PAIR = LANES // HEAD_DIM
N_PAIRS = N_HEADS // HEADS_PER_PAIR
AUG_STRIDE = 8
AUG_K_F = 0
AUG_K_ONE = 3
AUG_Q_ONE = 0
AUG_Q_F = 3

CUMSUM_BLOCK = LANES
V_AUG = HEAD_DIM + 16
LOG2E = 1.4426950408889634

FOX_SKIP_LOG2 = -150.0
FOX_NORM_MARGIN = 1.02
FOX_F_SLACK = 1.0

NEG_BIG = -1e30
SB_SKIP_LOG2 = -150.0
RWKV_CHUNK = 64
RWKV_UNROLL = 4


def _dot(a, b):
    return jnp.dot(a, b, preferred_element_type=F32)


def _dot_nt(a, b):
    return lax.dot_general(a, b, (((1,), (1,)), ((), ())), preferred_element_type=F32)


def _split2(x):
    hi = x.astype(BF16)
    lo = (x - hi.astype(F32)).astype(BF16)
    return hi, lo


def _split3(x):
    hi = x.astype(BF16)
    r1 = x - hi.astype(F32)
    mid = r1.astype(BF16)
    lo = (r1 - mid.astype(F32)).astype(BF16)
    return hi, mid, lo


def _dot_sel_lhs(sel, x):
    hi, mid, lo = _split3(x)
    return _dot(sel, hi) + (_dot(sel, mid) + _dot(sel, lo))


def _dot_sel_rhs(x, sel):
    hi, lo = _split2(x)
    return _dot(hi, sel) + _dot(lo, sel)


def _softplus(x):
    return jnp.maximum(x, 0.0) + jnp.log(1.0 + jnp.exp(-jnp.abs(x)))


def _log_sigmoid(x):
    return jnp.minimum(x, 0.0) - jnp.log(1.0 + jnp.exp(-jnp.abs(x)))


def _log2_sigmoid_of_log2(x2):
    return jnp.minimum(x2, 0.0) - jnp.log2(1.0 + jnp.exp2(-jnp.abs(x2)))


def _sigmoid(x):
    return 1.0 / (1.0 + jnp.exp(-x))


def _iota(shape, dim):
    return lax.broadcasted_iota(jnp.int32, shape, dim)


def _full_spec(a):
    return pl.BlockSpec(a.shape, lambda *_: (0,) * a.ndim)


def _params(n_grid):
    return pltpu.CompilerParams(dimension_semantics=("arbitrary",) * n_grid,
                                vmem_limit_bytes=VMEM_LIMIT_BYTES)


def _inproj_kernel(x_ref, g_ref, w_ref, fb_ref, mu_rkv_ref, mu_misc_ref,
                   fq_ref, fk_ref, qaug_ref, kaug_ref, fvt_ref, sq_ref, sk_ref, svt_ref, gates_ref,
                   rkv_ref, lx_ref, misc_ref, f2_ref, norms_ref, ftot, v_stage, pad_rkv, pad_misc):
    @pl.when(pl.program_id(1) == 0)
    def _():
        ftot[...] = jnp.zeros_like(ftot)
        pad_rkv[0:SUBLANES, :] = jnp.zeros((SUBLANES, N_RKV), F32)
        pad_misc[0:SUBLANES, :] = jnp.zeros((SUBLANES, LANES), F32)

    x = x_ref[0]
    tm = x.shape[0]
    ms = jnp.mean(x * x, axis=-1, keepdims=True)
    h = (x * lax.rsqrt(ms + RMS_EPS) * g_ref[...]).astype(BF16)
    scale = HEAD_DIM ** -0.5

    def proj(c0, width):
        return _dot(h, w_ref[0, :, c0:c0 + width])

    misc = proj(C_MISC, LANES)
    gates_ref[0] = proj(C_GATES, 4 * D_GROUP)
    lx_ref[0] = proj(C_LX, D_GROUP)

    for val, pad, mu_ref, out_ref in ((proj(C_RKV, N_RKV), pad_rkv, mu_rkv_ref, rkv_ref),
                                      (misc, pad_misc, mu_misc_ref, misc_ref)):
        pad[SUBLANES:SUBLANES + tm, :] = val
        prev = pad[SUBLANES - 1:SUBLANES - 1 + tm, :]
        pad[0:SUBLANES, :] = val[tm - SUBLANES:tm, :]
        out_ref[0] = val + (prev - val) * mu_ref[...]

    lf = _log_sigmoid(misc + fb_ref[...])
    cb = CUMSUM_BLOCK
    lower = (_iota((cb, cb), 1) <= _iota((cb, cb), 0)).astype(BF16)
    run = ftot[...]
    blocks = []
    for r0 in range(0, tm, cb):
        blk = _dot_sel_lhs(lower, lf[r0:r0 + cb, :]) + run
        run = blk[cb - 1:cb, :]
        blocks.append(blk)
    ftot[...] = run
    f = jnp.concatenate(blocks, axis=0)
    f2 = f * LOG2E
    f2_ref[0] = f2
    hi = f2.astype(BF16).astype(F32)
    mid = (f2 - hi).astype(BF16).astype(F32)
    lo = (f2 - hi - mid).astype(BF16).astype(F32)

    pq = proj(C_FQ, D_GROUP) * (scale * LOG2E)
    pk = proj(C_FK, D_GROUP)
    sq = proj(C_SQ, D_GROUP) * (scale * LOG2E)
    sk = proj(C_SK, D_GROUP)
    for pr in range(N_PAIRS):
        cols = slice(pr * LANES, (pr + 1) * LANES)
        fq_ref[0, pr] = pq[:, cols].astype(BF16)
        fk_ref[0, pr] = pk[:, cols].astype(BF16)
        sq_ref[0, pr] = sq[:, cols].astype(BF16)
        sk_ref[0, pr] = sk[:, cols].astype(BF16)

    lane = _iota((tm, LANES), 1)
    in_group = lane % AUG_STRIDE
    valid = lane < AUG_STRIDE * N_HEADS
    aug_q = jnp.where(jnp.logical_and(valid, jnp.logical_and(in_group >= AUG_Q_ONE,
                                                             in_group < AUG_Q_ONE + 3)), 1.0, 0.0)
    aug_k = jnp.where(jnp.logical_and(valid, jnp.logical_and(in_group >= AUG_K_ONE,
                                                             in_group < AUG_K_ONE + 3)), 1.0, 0.0)
    for hd in range(N_HEADS):
        for i3, piece in enumerate((hi, mid, lo)):
            col = jnp.broadcast_to(piece[:, FF_LANE + hd:FF_LANE + hd + 1], (tm, LANES))
            aug_q = jnp.where(lane == AUG_STRIDE * hd + AUG_Q_F + i3, col, aug_q)
            aug_k = jnp.where(lane == AUG_STRIDE * hd + AUG_K_F + i3, -col, aug_k)
    qaug_ref[0] = aug_q.astype(BF16)
    kaug_ref[0] = aug_k.astype(BF16)

    head_of_row = _iota((D_GROUP, LANES), 0) // HEAD_DIM
    nlane = _iota((SUBLANES, LANES), 1)
    norms = jnp.zeros((SUBLANES, LANES), F32)
    for side, pv in enumerate((pq, pk)):
        sel = (head_of_row + side * N_HEADS == _iota((D_GROUP, LANES), 1)).astype(BF16)
        top = jnp.max(_dot((pv * pv).astype(BF16), sel), axis=0, keepdims=True)
        keep = jnp.logical_and(nlane >= side * N_HEADS, nlane < (side + 1) * N_HEADS)
        norms = jnp.where(keep, top, norms)
    norms_ref[0, 0] = norms
    v_stage[...] = proj(C_SV, D_GROUP)
    svt_ref[0] = v_stage[...].T.astype(BF16)
    v_stage[...] = proj(C_FV, D_GROUP)
    vt = v_stage[...].T.astype(BF16)
    ones_rows = (_iota((V_AUG - HEAD_DIM, tm), 0) == 0).astype(BF16)
    for hd in range(N_HEADS):
        fvt_ref[0, hd * V_AUG:hd * V_AUG + HEAD_DIM, :] = vt[hd * HEAD_DIM:(hd + 1) * HEAD_DIM, :]
        fvt_ref[0, hd * V_AUG + HEAD_DIM:(hd + 1) * V_AUG, :] = ones_rows


def _inproj(x, consts, layer, *, tm):
    b, s, _ = x.shape
    w_spec = pl.BlockSpec((1,) + consts[1].shape[1:], lambda bi, i: (layer, 0, 0))
    seq = lambda width: pl.BlockSpec((1, tm, width), lambda bi, i: (bi, i, 0))
    pairs = pl.BlockSpec((1, N_PAIRS, tm, LANES), lambda bi, i: (bi, 0, i, 0))
    pairs_shape = jax.ShapeDtypeStruct((b, N_PAIRS, s, LANES), BF16)
    chan = lambda rows: pl.BlockSpec((1, rows, tm), lambda bi, i: (bi, 0, i))
    chan_shape = lambda rows: jax.ShapeDtypeStruct((b, rows, s), BF16)
    seq_shape = lambda width, dtype=F32: jax.ShapeDtypeStruct((b, s, width), dtype)
    return pl.pallas_call(
        _inproj_kernel,
        grid=(b, s // tm),
        in_specs=[seq(D_MODEL), _full_spec(consts[0]), w_spec] + [_full_spec(a) for a in consts[2:]],
        out_specs=[pairs, pairs, seq(LANES), seq(LANES), chan(N_HEADS * V_AUG),
                   pairs, pairs, chan(D_GROUP),
                   seq(4 * D_GROUP), seq(N_RKV), seq(D_GROUP), seq(LANES), seq(LANES),
                   pl.BlockSpec((1, 1, SUBLANES, LANES), lambda bi, i: (bi, i, 0, 0))],
        out_shape=[pairs_shape, pairs_shape, seq_shape(LANES, BF16), seq_shape(LANES, BF16),
                   chan_shape(N_HEADS * V_AUG), pairs_shape, pairs_shape, chan_shape(D_GROUP),
                   seq_shape(4 * D_GROUP), seq_shape(N_RKV), seq_shape(D_GROUP), seq_shape(LANES),
                   seq_shape(LANES), jax.ShapeDtypeStruct((b, s // tm, SUBLANES, LANES), F32)],
        scratch_shapes=[pltpu.VMEM((1, LANES), F32), pltpu.VMEM((tm, D_GROUP), F32),
                        pltpu.VMEM((tm + SUBLANES, N_RKV), F32),
                        pltpu.VMEM((tm + SUBLANES, LANES), F32)],
        compiler_params=_params(2),
        name="inproj",
    )(x, *consts)


def _fox_kernel(fend_ref, qmax_ref, kmax_ref, q_ref, qaug_ref, k_ref, kaug_ref, vt_ref, o_ref,
                s_scr, p_scr, mx_scr, al_scr, m_scr, acc_scr, *, tq, tk):
    i = pl.program_id(1)
    nq = pl.num_programs(1)
    heads = list(range(N_HEADS))
    lane = _iota((tq, LANES), 1)
    q = []
    for hd in heads:
        own_half = lane // HEAD_DIM == hd % HEADS_PER_PAIR
        own_aug = lane // AUG_STRIDE == hd
        q.append(jnp.concatenate(
            [jnp.where(own_half, q_ref[0, hd // HEADS_PER_PAIR], jnp.zeros((), BF16)),
             jnp.where(own_aug, qaug_ref[0], jnp.zeros((), BF16))], axis=1))
    n_full = (i * tq) // tk

    def first_live_block(hd):
        bh = pl.program_id(0) * N_HEADS + hd
        qk_bound = FOX_NORM_MARGIN * 2.0 * qmax_ref[bh * nq + i] * kmax_ref[bh]
        f_tile = fend_ref[bh * nq + jnp.maximum(i - 1, 0)]

        def first_live(jj, first):
            j = n_full - 1 - jj
            f_end = fend_ref[bh * nq + (j + 1) * (tk // tq) - 1]
            live = qk_bound + (f_tile - f_end) + FOX_F_SLACK >= FOX_SKIP_LOG2
            return jnp.where(live, j, first)

        return lax.fori_loop(0, n_full, first_live, n_full)

    base = functools.reduce(jnp.minimum, [first_live_block(hd) for hd in heads])
    n_eff = n_full - base

    def key_rows(j):
        return pl.ds(pl.multiple_of(j * tk, tk), tk)

    def scores_to(j, slot):
        kaug = kaug_ref[0, key_rows(j), :]
        kp = [jnp.concatenate([k_ref[0, pr, key_rows(j), :], kaug], axis=1) for pr in range(N_PAIRS)]
        s = [_dot_nt(kp[hd // HEADS_PER_PAIR], q[hd]) for hd in heads]
        for hd in heads:
            s_scr[hd, slot] = s[hd]
            mx_scr[hd, slot] = jnp.max(s[hd], axis=0, keepdims=True)

    def weighted_values(j, slot):
        return [_dot(vt_ref[0, hd * V_AUG:(hd + 1) * V_AUG, key_rows(j)], p_scr[hd, slot])
                for hd in heads]

    def softmax_to(s, mx, slot):
        m_old = [m_scr[hd] for hd in heads]
        m_new = [jnp.maximum(a, b) for a, b in zip(m_old, mx)]
        p = [jnp.exp2(a - b).astype(BF16) for a, b in zip(s, m_new)]
        for hd in heads:
            m_scr[hd] = m_new[hd]
            al_scr[hd, slot] = jnp.exp2(m_old[hd] - m_new[hd])
            p_scr[hd, slot] = p[hd]

    def stage(local, cur):
        j = base + local
        nxt = 1 - cur
        pv_prev = weighted_values(jnp.maximum(j - 1, 0), nxt)
        scores_to(j + 1, nxt)
        softmax_to([s_scr[hd, cur] for hd in heads], [mx_scr[hd, cur] for hd in heads], cur)
        for hd in heads:
            acc_scr[hd] = al_scr[hd, nxt] * acc_scr[hd] + pv_prev[hd]

    def tail(cur):
        nxt = 1 - cur
        pv_prev = weighted_values(jnp.maximum(n_full - 1, 0), nxt)
        mask = n_full * tk + _iota((tk, tq), 0) <= i * tq + _iota((tk, tq), 1)
        s = [jnp.where(mask, s_scr[hd, cur], NEG_BIG) for hd in heads]
        softmax_to(s, [jnp.max(x, axis=0, keepdims=True) for x in s], cur)
        pv_last = weighted_values(n_full, cur)
        for hd in heads:
            acc = al_scr[hd, cur] * (al_scr[hd, nxt] * acc_scr[hd] + pv_prev[hd]) + pv_last[hd]
            o_ref[0, hd * HEAD_DIM:(hd + 1) * HEAD_DIM, :] = (
                acc[0:HEAD_DIM] / acc[HEAD_DIM:HEAD_DIM + 1])

    m_scr[...] = jnp.full(m_scr.shape, NEG_BIG, F32)
    acc_scr[...] = jnp.zeros_like(acc_scr)
    p_scr[:, 1] = jnp.zeros((N_HEADS, tk, tq), BF16)
    al_scr[:, 1] = jnp.ones((N_HEADS, 1, tq), F32)
    scores_to(base, 0)

    def pair(jj, carry):
        stage(2 * jj, 0)
        stage(2 * jj + 1, 1)
        return carry

    lax.fori_loop(0, n_eff // 2, pair, 0)

    @pl.when(n_eff % 2 == 1)
    def _():
        stage(n_eff - 1, 0)
        tail(1)

    @pl.when(n_eff % 2 == 0)
    def _():
        tail(0)


def _fox_attention(fend, qmax, kmax, q, qaug, k, kaug, vt, *, tq, tk):
    b, _, s, _ = q.shape
    h = N_HEADS
    grid_spec = pltpu.PrefetchScalarGridSpec(
        num_scalar_prefetch=3,
        grid=(b, s // tq),
        in_specs=[pl.BlockSpec((1, N_PAIRS, tq, LANES), lambda bi, i, *_: (bi, 0, i, 0)),
                  pl.BlockSpec((1, tq, LANES), lambda bi, i, *_: (bi, i, 0)),
                  pl.BlockSpec((1, N_PAIRS, s, LANES), lambda bi, i, *_: (bi, 0, 0, 0)),
                  pl.BlockSpec((1, s, LANES), lambda bi, i, *_: (bi, 0, 0)),
                  pl.BlockSpec((1, h * V_AUG, s), lambda bi, i, *_: (bi, 0, 0))],
        out_specs=pl.BlockSpec((1, h * HEAD_DIM, tq), lambda bi, i, *_: (bi, 0, i)),
        scratch_shapes=[pltpu.VMEM((h, 2, tk, tq), F32), pltpu.VMEM((h, 2, tk, tq), BF16),
                        pltpu.VMEM((h, 2, 1, tq), F32), pltpu.VMEM((h, 2, 1, tq), F32),
                        pltpu.VMEM((h, 1, tq), F32), pltpu.VMEM((h, V_AUG, tq), F32)])
    return pl.pallas_call(
        functools.partial(_fox_kernel, tq=tq, tk=tk),
        grid_spec=grid_spec,
        out_shape=jax.ShapeDtypeStruct((b, h * HEAD_DIM, s), F32),
        compiler_params=_params(2),
        name="fox_attention",
    )(fend, qmax, kmax, q, qaug, k, kaug, vt)


def _sb_kernel(q_ref, k_ref, vt_ref, o_ref, *, tq):
    i = pl.program_id(1)
    tk = tq
    heads = list(range(N_HEADS))
    lane = _iota((tq, LANES), 1)
    q = [jnp.where(lane // HEAD_DIM == hd % HEADS_PER_PAIR, q_ref[0, hd // HEADS_PER_PAIR],
                   jnp.zeros((), BF16)) for hd in heads]
    later = (_iota((tk, tk), 0) < _iota((tk, tk), 1)).astype(BF16)

    def each(f, *xs):
        return [f(*a) for a in zip(*xs)]

    def block(j, rest_q, acc, masked):
        ks = pl.multiple_of(j * tk, tk)
        kp = [k_ref[0, pr, pl.ds(ks, tk), :] for pr in range(N_PAIRS)]
        z = [_dot_nt(kp[hd // HEADS_PER_PAIR], q[hd]) for hd in heads]
        log_keep = each(lambda x: _log2_sigmoid_of_log2(-x), z)
        if masked:
            mask = _iota((tk, tq), 0) < _iota((tk, tq), 1)
            log_keep = each(lambda x: jnp.where(mask, x, 0.0), log_keep)
        split = each(_split2, log_keep)
        rest_in = each(lambda hl: _dot(later, hl[0]) + _dot(later, hl[1]), split)
        att = each(lambda x, lk, ri, rq: jnp.exp2(x + lk + ri + rq), z, log_keep, rest_in, rest_q)
        if masked:
            att = each(lambda x: jnp.where(mask, x, 0.0), att)
        pv = [_dot(vt_ref[0, hd * HEAD_DIM:(hd + 1) * HEAD_DIM, pl.ds(ks, tk)],
                   att[hd].astype(BF16)) for hd in heads]
        acc = each(lambda a, x: a + x, acc, pv)
        rest_q = each(lambda rq, ri, lk: rq + ri[0:1, :] + lk[0:1, :], rest_q, rest_in, log_keep)
        return rest_q, acc

    rest_q, acc = block(i, [jnp.zeros((1, tq), F32)] * N_HEADS,
                        [jnp.zeros((HEAD_DIM, tq), F32)] * N_HEADS, True)

    def cond(c):
        j, rest_q, _ = c
        alive = functools.reduce(jnp.maximum, rest_q)
        return jnp.logical_and(j >= 0, jnp.max(alive) > SB_SKIP_LOG2)

    def body(c):
        j, rest_q, acc = c
        rest_q, acc = block(j, list(rest_q), list(acc), False)
        return j - 1, tuple(rest_q), tuple(acc)

    _, _, acc = lax.while_loop(cond, body, (i - 1, tuple(rest_q), tuple(acc)))
    o_ref[0] = jnp.concatenate(list(acc), axis=0)


def _sb_attention(q, k, vt, *, tq):
    b, _, s, _ = q.shape
    h = N_HEADS
    return pl.pallas_call(
        functools.partial(_sb_kernel, tq=tq),
        grid=(b, s // tq),
        in_specs=[pl.BlockSpec((1, N_PAIRS, tq, LANES), lambda bi, i: (bi, 0, i, 0)),
                  pl.BlockSpec((1, N_PAIRS, s, LANES), lambda bi, i: (bi, 0, 0, 0)),
                  pl.BlockSpec((1, h * HEAD_DIM, s), lambda bi, i: (bi, 0, 0))],
        out_specs=pl.BlockSpec((1, h * HEAD_DIM, tq), lambda bi, i: (bi, 0, i)),
        out_shape=jax.ShapeDtypeStruct((b, h * HEAD_DIM, s), F32),
        compiler_params=_params(2),
        name="sb_attention",
    )(q, k, vt)


def _rwkv_kernel(p_ref, m_ref, w0_ref, w2_ref, a0_ref, a2_ref,
                 kk_ref, ka_ref, rk_ref, lng_ref, lnb_ref, bdm_ref, o_ref,
                 r_s, k_s, v_s, kn_s, al_s, lw_s,
                 wt_s, u0_s, o0_s, mrb_s, rt_s, bh_s, kh_s, pc_s, y_s, ht_s, *, tt, nb):
    c = RWKV_CHUNK
    g = D_GROUP
    n_chunks = tt // c

    @pl.when(pl.program_id(0) == 0)
    def _():
        ht_s[...] = jnp.zeros_like(ht_s)

    bdm = bdm_ref[...]
    bdm_f = bdm.astype(F32)

    for b in range(nb):
        p = p_ref[b]
        misc = m_ref[b]
        k = p[:, g:2 * g]

        w = -_softplus(-(w0_ref[...] + _dot(jnp.tanh(misc).astype(BF16), w2_ref[...]))) - 0.5
        alpha = _sigmoid(a0_ref[...] + _dot(misc.astype(BF16), a2_ref[...]))
        kn = k * kk_ref[...]
        ss = _dot_sel_rhs(kn * kn, bdm)
        r_s[b] = p[:, 0:g]
        k_s[b] = k * (1.0 + (alpha - 1.0) * ka_ref[...])
        v_s[b] = p[:, 2 * g:3 * g]
        kn_s[b] = kn * lax.rsqrt(jnp.maximum(ss, 1e-12))
        al_s[b] = alpha
        lw_s[b] = -jnp.exp(w)

    row = _iota((c, g), 0)
    col = _iota((c, g), 1) % c
    strict = col < row
    incl = col <= row
    eye = (col == row).astype(F32)
    lower_c = (_iota((c, c), 1) <= _iota((c, c), 0)).astype(BF16)
    level_masks = []
    m = 1
    while m < c:
        level_masks.append(jnp.logical_and(
            strict, jnp.logical_and(row // (2 * m) == col // (2 * m), row // m != col // m)))
        m *= 2

    def bd(x):
        return jnp.concatenate([x.astype(BF16)] * N_HEADS, axis=0) * bdm

    def mm(a, b_bf16):
        return _dot(a.astype(BF16), b_bf16)

    def mm_nt(a, b_bf16):
        return _dot_nt(a.astype(BF16), b_bf16)

    def each(f, *xs):
        return [f(*a) for a in zip(*xs)]

    def phase_a(chains, side):
        def tick():
            next(side, None)

        sls = [pl.ds(ci * c, c) for _, ci in chains]
        ld = lambda ref: [ref[b, sl, :] for (b, _), sl in zip(chains, sls)]
        r_c, k_c, v_c, kn_c, al_c, lw_c = (ld(s) for s in (r_s, k_s, v_s, kn_s, al_s, lw_s))
        cl = each(lambda x: _dot_sel_lhs(lower_c, x), lw_c)
        tick()
        cl_last = each(lambda x: x[c - 1:c, :], cl)
        a_t = each(lambda kn, x, lw: -kn * jnp.exp(x - lw), kn_c, cl, lw_c)
        r_t = each(lambda r, x: r * jnp.exp(x), r_c, cl)
        q_inv = each(lambda x: jnp.exp(-x), cl)
        p_rem = each(lambda xl, x: jnp.exp(xl - x), cl_last, cl)
        kna = each(lambda kn, al: kn * al, kn_c, al_c)
        ar = each(lambda a, r: jnp.concatenate([a, r], axis=0), a_t, r_t)
        s_b = each(lambda x, kb, qi: mm_nt(x, bd(kb * qi)), ar, kna, q_inv)
        tick()
        s_k = each(lambda x, kk, qi: mm_nt(x, bd(kk * qi)), ar, k_c, q_inv)
        tick()
        n = each(lambda x: jnp.where(strict, x[0:c], 0.0), s_b)
        a_ak = each(lambda x: jnp.where(strict, x[0:c], 0.0), s_k)
        m_rb = each(lambda x: jnp.where(incl, x[c:2 * c], 0.0), s_b)
        m_rk = each(lambda x: jnp.where(incl, x[c:2 * c], 0.0), s_k)

        inv = each(lambda x: eye + jnp.where(level_masks[0], x, 0.0), n)
        for lm in level_masks[1:]:
            half = each(lambda d, x: mm(d, bd(jnp.where(lm, x, 0.0))), inv, n)
            tick()
            inv = each(lambda d, hf: d + mm(hf, bd(d)), inv, half)
            tick()

        v_bd = each(bd, v_c)
        akv = each(mm, a_ak, v_bd)
        tick()
        wt = each(lambda d, a: mm(d, bd(a)), inv, a_t)
        tick()
        u0 = each(lambda d, x: mm(d, bd(x)), inv, akv)
        tick()
        o0 = each(mm, m_rk, v_bd)
        bh = each(lambda x, p: x * p, kna, p_rem)
        kh = each(lambda x, p: x * p, k_c, p_rem)
        for _ in side:
            pass
        for ref, vals in zip((wt_s, u0_s, o0_s, mrb_s, rt_s, bh_s, kh_s),
                             (wt, u0, o0, m_rb, r_t, bh, kh)):
            for (b, _), sl, val in zip(chains, sls, vals):
                ref[b, sl, :] = val
        for (b, ci), xl in zip(chains, cl_last):
            pc_s[b, pl.ds(ci * SUBLANES, SUBLANES), :] = jnp.broadcast_to(jnp.exp(xl), (SUBLANES, g))

    def phase_b_steps(chunk_ids):
        bs = list(range(nb))
        for ci in chunk_ids:
            sl = pl.ds(ci * c, c)
            ht = [ht_s[b] for b in bs]
            wr = [jnp.concatenate([wt_s[b, sl, :], rt_s[b, sl, :]], axis=0) for b in bs]
            wrh = each(lambda x, hh: mm_nt(x, hh.astype(BF16)), wr, ht)
            yield
            u = [x[0:c] + u0_s[b, sl, :] for x, b in zip(wrh, bs)]
            uv_t = [jnp.concatenate([x, v_s[b, sl, :]], axis=0).T for x, b in zip(u, bs)]
            bk = [jnp.concatenate([bh_s[b, sl, :], kh_s[b, sl, :]], axis=0) for b in bs]
            upd = each(lambda x, y: mm(x, y.astype(BF16)), uv_t, bk)
            mu_ = [mm(mrb_s[b, sl, :], bd(x)) for x, b in zip(u, bs)]
            yield
            for b in bs:
                p_c = pc_s[b, pl.ds(ci * SUBLANES, 1), :]
                y_s[b, sl, :] = wrh[b][c:2 * c] + mu_[b] + o0_s[b, sl, :]
                ht_s[b] = ht[b] * p_c + bdm_f * upd[b]
            yield

    pending = iter(())
    for gi in range(n_chunks // RWKV_UNROLL):
        chunk_ids = range(gi * RWKV_UNROLL, (gi + 1) * RWKV_UNROLL)
        phase_a([(b, ci) for ci in chunk_ids for b in range(nb)], pending)
        pending = phase_b_steps(chunk_ids)
    for _ in pending:
        pass

    inv_n = 1.0 / HEAD_DIM
    for b in range(nb):
        y = y_s[b]
        mean = _dot_sel_rhs(y, bdm) * inv_n
        yc = y - mean
        var = _dot_sel_rhs(yc * yc, bdm) * inv_n
        yn = yc * lax.rsqrt(var + GN_EPS) * lng_ref[...] + lnb_ref[...]
        bonus = _dot_sel_rhs(r_s[b] * k_s[b] * rk_ref[...], bdm) * v_s[b]
        o_ref[b] = yn + bonus


def _rwkv(p_rkv, misc, prm, bdm, *, tt):
    b, s, _ = p_rkv.shape
    g = D_GROUP
    seq = lambda width: pl.BlockSpec((b, tt, width), lambda ti: (0, ti, 0))
    big = pltpu.VMEM((b, tt, g), F32)
    return pl.pallas_call(
        functools.partial(_rwkv_kernel, tt=tt, nb=b),
        grid=(s // tt,),
        in_specs=[seq(N_RKV), seq(LANES)] + [_full_spec(a) for a in prm] + [_full_spec(bdm)],
        out_specs=seq(g),
        out_shape=jax.ShapeDtypeStruct((b, s, g), F32),
        scratch_shapes=[big] * 13
        + [pltpu.VMEM((b, tt // RWKV_CHUNK * SUBLANES, g), F32), big,
           pltpu.VMEM((b, g, g), F32)],
        compiler_params=_params(1),
        name="rwkv7",
    )(p_rkv, misc, *prm, bdm)


def _lru_tile(x_ref, cw_ref, cb_ref, wa_ref, ba_ref, wx_ref, bx_ref, lam_ref, o_ref,
              pad, a_s, u_s, h_s, *, tt, nb):
    @pl.when(pl.program_id(0) == 0)
    def _():
        pad[:, 0:SUBLANES, :] = jnp.zeros((nb, SUBLANES, D_GROUP), F32)
        h_s[...] = jnp.zeros_like(h_s)

    for b in range(nb):
        x = x_ref[b]
        pad[b, SUBLANES:SUBLANES + tt, :] = x
        xc = cw_ref[CONV_WIDTH - 1:CONV_WIDTH, :] * x + cb_ref[...]
        for d in range(1, CONV_WIDTH):
            tap = CONV_WIDTH - 1 - d
            xc = xc + cw_ref[tap:tap + 1, :] * pad[b, SUBLANES - d:SUBLANES - d + tt, :]
        pad[b, 0:SUBLANES, :] = x[tt - SUBLANES:tt, :]

        xb = xc.astype(BF16)
        r = _sigmoid(_dot(xb, wa_ref[...]) + ba_ref[...])
        i = _sigmoid(_dot(xb, wx_ref[...]) + bx_ref[...])
        log_a = -LRU_C * r * _softplus(-lam_ref[...])
        a_s[b] = jnp.exp(log_a)
        th = jnp.tanh(log_a)
        u_s[b] = jnp.sqrt(-2.0 * th / (1.0 - th)) * (i * xc)

    row = _iota((SUBLANES, D_GROUP), 0)

    def rows_above(x, k, fill):
        return jnp.where(row >= k, pltpu.roll(x, k, axis=0), fill)

    def group(gi, hs):
        base = pl.multiple_of(gi * SUBLANES, SUBLANES)
        carry = []
        for b in range(nb):
            a = a_s[b, pl.ds(base, SUBLANES), :]
            u = u_s[b, pl.ds(base, SUBLANES), :]
            k = 1
            while k < SUBLANES:
                u = a * rows_above(u, k, 0.0) + u
                a = a * rows_above(a, k, 1.0)
                k *= 2
            h = a * hs[b] + u
            o_ref[b, pl.ds(base, SUBLANES), :] = h
            carry.append(h[SUBLANES - 1:SUBLANES, :])
        return tuple(carry)

    hs = lax.fori_loop(0, tt // SUBLANES, group, tuple(h_s[b] for b in range(nb)))
    for b in range(nb):
        h_s[b] = hs[b]


def _outproj_kernel(x_ref, yft_ref, yst_ref, yr_ref, lx_ref, gates_ref, w_ref, fg_ref,
                    cw_ref, cb_ref, wa_ref, ba_ref, wx_ref, bx_ref, lam_ref, o_ref,
                    pad, a_s, u_s, yl_s, h_s, *, final, tm, nb):
    _lru_tile(lx_ref, cw_ref, cb_ref, wa_ref, ba_ref, wx_ref, bx_ref, lam_ref, yl_s,
              pad, a_s, u_s, h_s, tt=tm, nb=nb)
    for b in range(nb):
        acc = x_ref[b]
        ys = (yft_ref[b].T, yst_ref[b].T, yr_ref[b], yl_s[b])
        for gi, y in enumerate(ys):
            gate = gates_ref[b, :, gi * D_GROUP:(gi + 1) * D_GROUP]
            y = y * (gate * _sigmoid(gate))
            acc = acc + _dot(y.astype(BF16), w_ref[gi * D_GROUP:(gi + 1) * D_GROUP, :])
        if final:
            ms = jnp.mean(acc * acc, axis=-1, keepdims=True)
            acc = acc * lax.rsqrt(ms + RMS_EPS) * fg_ref[...]
        o_ref[b] = acc


def _outproj(x, y_fox_t, y_sb_t, y_rw, lx, gates, w, final_g, lru_prm, *, tm, final):
    b, s, _ = x.shape
    g = D_GROUP
    seq = lambda width: pl.BlockSpec((b, tm, width), lambda i: (0, i, 0))
    chan = pl.BlockSpec((b, g, tm), lambda i: (0, 0, i))
    big = pltpu.VMEM((b, tm, g), F32)
    return pl.pallas_call(
        functools.partial(_outproj_kernel, final=final, tm=tm, nb=b),
        grid=(s // tm,),
        in_specs=[seq(D_MODEL), chan, chan, seq(g), seq(g), seq(4 * g),
                  _full_spec(w), _full_spec(final_g)] + [_full_spec(a) for a in lru_prm],
        out_specs=seq(D_MODEL),
        out_shape=jax.ShapeDtypeStruct((b, s, D_MODEL), F32),
        scratch_shapes=[pltpu.VMEM((b, tm + SUBLANES, g), F32), big, big, big,
                        pltpu.VMEM((b, 1, g), F32)],
        compiler_params=_params(1),
        name="outproj",
    )(x, y_fox_t, y_sb_t, y_rw, lx, gates, w, final_g, *lru_prm)


def _w_in_segments():
    g, h, r = D_GROUP, N_HEADS, RWKV_LORA
    o_ff = 4 * g
    o_sb = o_ff + h
    o_rw = o_sb + 4 * g
    o_rg = o_rw + 3 * g + 2 * r
    o_lx = o_rg + g
    o_lg = o_lx + g
    segs = [(0, C_FQ, 2 * g), (o_sb, C_SQ, 2 * g), (2 * g, C_FV, g), (o_sb + 2 * g, C_SV, g),
             (3 * g, C_GATES, g), (o_sb + 3 * g, C_GATES + g, g), (o_rg, C_GATES + 2 * g, g),
             (o_lg, C_GATES + 3 * g, g),
             (o_rw, C_RKV, 3 * g), (o_lx, C_LX, g),
             (o_rw + 3 * g, C_MISC, 2 * r), (o_ff, C_MISC + FF_LANE, h)]
    return segs


def _w_in_kernel(w_ref, o_ref):
    o_ref[...] = jnp.zeros_like(o_ref)
    for src, dst, width in _w_in_segments():
        o_ref[0, :, dst:dst + width] = w_ref[0, :, src:src + width].astype(BF16)


def _permute_w_in(w_in, *, tr):
    depth, d, n_in = w_in.shape
    return pl.pallas_call(
        _w_in_kernel,
        grid=(depth, d // tr),
        in_specs=[pl.BlockSpec((1, tr, n_in), lambda l, i: (l, i, 0))],
        out_specs=pl.BlockSpec((1, tr, N_PROJ), lambda l, i: (l, i, 0)),
        out_shape=jax.ShapeDtypeStruct((depth, d, N_PROJ), BF16),
        compiler_params=_params(2),
        name="w_in_layout",
    )(w_in)


def _block_diag(w):
    h, n, _ = w.shape
    eye = jnp.eye(h, dtype=w.dtype)
    return jnp.einsum('hij,hk->hikj', w, eye).reshape(h * n, h * n)


def _pick_tile(s, pref):
    t = pref
    while s % t:
        t //= 2
    return t


def kernel(x, norm_g, w_in, b_forget, rwkv_mu, rwkv_w0, rwkv_w2, rwkv_a0, rwkv_a2, rwkv_k_k,
           rwkv_k_a, rwkv_r_k, rwkv_ln_g, rwkv_ln_b, lru_conv_w, lru_conv_b, lru_w_a, lru_b_a,
           lru_w_x, lru_b_x, lru_lambda, w_out, final_g):
    b, s, d = x.shape
    depth = w_in.shape[0]
    g, h, dh, r = D_GROUP, N_HEADS, HEAD_DIM, RWKV_LORA
    tm = _pick_tile(s, 512)
    tq = _pick_tile(s, 256)
    tk_fox = tq
    tt = _pick_tile(s, 512)
    row = lambda a: a.reshape(1, -1).astype(F32)

    bdm = _block_diag(jnp.ones((h, dh, dh), BF16))
    w_in_k = _permute_w_in(w_in, tr=LANES)
    for l in range(depth):
        fbias = jnp.zeros((1, LANES), F32).at[0, FF_LANE:FF_LANE + h].set(b_forget[l])
        mu = rwkv_mu[l]
        consts = [row(norm_g[l]), w_in_k, fbias, row(mu[:N_RKV]),
                  jnp.zeros((1, LANES), F32).at[0, :2 * r].set(mu[N_RKV:])]
        (fq, fk, qaug, kaug, fvt, sq, sk, svt, gates, rkv, lx, misc, f2,
         norms) = _inproj(x, consts, l, tm=tm)

        per_head = lambda a: a.transpose(0, 2, 1).reshape(-1)
        fend = per_head(f2[:, tq - 1::tq, FF_LANE:FF_LANE + h])
        qmax = per_head(jnp.repeat(jnp.sqrt(norms[:, :, 0, 0:h]), tm // tq, axis=1))
        kmax = jnp.sqrt(jnp.max(norms[:, :, 0, h:2 * h], axis=1)).reshape(-1)
        y_fox_t = _fox_attention(fend, qmax, kmax, fq, qaug, fk, kaug, fvt, tq=tq, tk=tk_fox)
        y_sb_t = _sb_attention(sq, sk, svt, tq=tq)

        pad_rows = lambda a, lo: jnp.zeros((LANES, g), F32).at[lo:lo + r].set(a).astype(BF16)
        rw_prm = [row(rwkv_w0[l]), pad_rows(rwkv_w2[l], 0), row(rwkv_a0[l]), pad_rows(rwkv_a2[l], r),
                  row(rwkv_k_k[l]), row(rwkv_k_a[l]), row(rwkv_r_k[l]), row(rwkv_ln_g[l]),
                  row(rwkv_ln_b[l])]
        y_rw = _rwkv(rkv, misc, rw_prm, bdm, tt=tt)

        lru_prm = [lru_conv_w[l].astype(F32), row(lru_conv_b[l]),
                   _block_diag(lru_w_a[l]).astype(BF16), row(lru_b_a[l]),
                   _block_diag(lru_w_x[l]).astype(BF16), row(lru_b_x[l]), row(lru_lambda[l])]
        x = _outproj(x, y_fox_t, y_sb_t, y_rw, lx, gates, w_out[l].astype(BF16), row(final_g),
                     lru_prm, tm=tm, final=(l == depth - 1))
    return x
```

```python
import functools

import jax
import jax.numpy as jnp
from jax import lax
from jax.experimental import pallas as pl
from jax.experimental.pallas import tpu as pltpu

F32 = jnp.float32
BF16 = jnp.bfloat16

D_MODEL = 1024
D_GROUP = 256
N_HEADS = 4
HEAD_DIM = 64
RWKV_LORA = 32
CONV_WIDTH = 4
LRU_C = 8.0
RMS_EPS = 1e-6
GN_EPS = 64e-5
N_RKV = 3 * D_GROUP

LANES = 128
SUBLANES = 8
VMEM_LIMIT_BYTES = 56 * 1024 * 1024

C_FQ = 0
C_FK = C_FQ + D_GROUP
C_SQ = C_FK + D_GROUP
C_SK = C_SQ + D_GROUP
C_FV = C_SK + D_GROUP
C_SV = C_FV + D_GROUP
C_GATES = C_SV + D_GROUP
C_RKV = C_GATES + 4 * D_GROUP
C_LX = C_RKV + N_RKV
C_MISC = C_LX + D_GROUP
N_PROJ = C_MISC + LANES
FF_LANE = 2 * RWKV_LORA
HEADS_PER_PAIR = LANES // HEAD_DIM
N_PAIRS = N_HEADS // HEADS_PER_PAIR
AUG_STRIDE = 8
AUG_K_F = 0
AUG_K_ONE = 3
AUG_Q_ONE = 0
AUG_Q_F = 3

CUMSUM_BLOCK = LANES
V_AUG = HEAD_DIM + 16
LOG2E = 1.4426950408889634

FOX_SKIP_LOG2 = -150.0
FOX_NORM_MARGIN = 1.02
FOX_F_SLACK = 1.0

NEG_BIG = -1e30
SB_SKIP_LOG2 = -150.0
RWKV_CHUNK = 64
RWKV_UNROLL = 4


def _dot(a, b):
    return jnp.dot(a, b, preferred_element_type=F32)


def _dot_nt(a, b):
    return lax.dot_general(a, b, (((1,), (1,)), ((), ())), preferred_element_type=F32)


def _split2(x):
    hi = x.astype(BF16)
    lo = (x - hi.astype(F32)).astype(BF16)
    return hi, lo


def _split3(x):
    hi = x.astype(BF16)
    r1 = x - hi.astype(F32)
    mid = r1.astype(BF16)
    lo = (r1 - mid.astype(F32)).astype(BF16)
    return hi, mid, lo


def _dot_sel_lhs(sel, x):
    hi, mid, lo = _split3(x)
    return _dot(sel, hi) + (_dot(sel, mid) + _dot(sel, lo))


def _dot_sel_rhs(x, sel):
    hi, lo = _split2(x)
    return _dot(hi, sel) + _dot(lo, sel)


def _softplus(x):
    return jnp.maximum(x, 0.0) + jnp.log(1.0 + jnp.exp(-jnp.abs(x)))


def _log_sigmoid(x):
    return jnp.minimum(x, 0.0) - jnp.log(1.0 + jnp.exp(-jnp.abs(x)))


def _log2_sigmoid_of_log2(x2):
    return jnp.minimum(x2, 0.0) - jnp.log2(1.0 + jnp.exp2(-jnp.abs(x2)))


def _sigmoid(x):
    return 1.0 / (1.0 + jnp.exp(-x))


def _iota(shape, dim):
    return lax.broadcasted_iota(jnp.int32, shape, dim)


def _full_spec(a):
    return pl.BlockSpec(a.shape, lambda *_: (0,) * a.ndim)


def _params(n_grid):
    return pltpu.CompilerParams(dimension_semantics=("arbitrary",) * n_grid,
                                vmem_limit_bytes=VMEM_LIMIT_BYTES)


def _inproj_kernel(x_ref, g_ref, w_ref, fb_ref, mu_rkv_ref, mu_misc_ref,
                   fq_ref, fk_ref, qaug_ref, kaug_ref, fvt_ref, sq_ref, sk_ref, svt_ref, gates_ref,
                   rkv_ref, lx_ref, misc_ref, f2_ref, norms_ref, ftot, v_stage, pad_rkv, pad_misc):
    @pl.when(pl.program_id(1) == 0)
    def _():
        ftot[...] = jnp.zeros_like(ftot)
        pad_rkv[0:SUBLANES, :] = jnp.zeros((SUBLANES, N_RKV), F32)
        pad_misc[0:SUBLANES, :] = jnp.zeros((SUBLANES, LANES), F32)

    x = x_ref[0]
    tm = x.shape[0]
    ms = jnp.mean(x * x, axis=-1, keepdims=True)
    h = (x * lax.rsqrt(ms + RMS_EPS) * g_ref[...]).astype(BF16)
    scale = HEAD_DIM ** -0.5

    def proj(c0, width):
        return _dot(h, w_ref[0, :, c0:c0 + width])

    misc = proj(C_MISC, LANES)
    gates_ref[0] = proj(C_GATES, 4 * D_GROUP)
    lx_ref[0] = proj(C_LX, D_GROUP)

    for val, pad, mu_ref, out_ref in ((proj(C_RKV, N_RKV), pad_rkv, mu_rkv_ref, rkv_ref),
                                      (misc, pad_misc, mu_misc_ref, misc_ref)):
        pad[SUBLANES:SUBLANES + tm, :] = val
        prev = pad[SUBLANES - 1:SUBLANES - 1 + tm, :]
        pad[0:SUBLANES, :] = val[tm - SUBLANES:tm, :]
        out_ref[0] = val + (prev - val) * mu_ref[...]

    lf = _log_sigmoid(misc + fb_ref[...])
    cb = CUMSUM_BLOCK
    lower = (_iota((cb, cb), 1) <= _iota((cb, cb), 0)).astype(BF16)
    run = ftot[...]
    blocks = []
    for r0 in range(0, tm, cb):
        blk = _dot_sel_lhs(lower, lf[r0:r0 + cb, :]) + run
        run = blk[cb - 1:cb, :]
        blocks.append(blk)
    ftot[...] = run
    f = jnp.concatenate(blocks, axis=0)
    f2 = f * LOG2E
    f2_ref[0] = f2
    hi = f2.astype(BF16).astype(F32)
    mid = (f2 - hi).astype(BF16).astype(F32)
    lo = (f2 - hi - mid).astype(BF16).astype(F32)

    pq = proj(C_FQ, D_GROUP) * (scale * LOG2E)
    pk = proj(C_FK, D_GROUP)
    sq = proj(C_SQ, D_GROUP) * (scale * LOG2E)
    sk = proj(C_SK, D_GROUP)
    for pr in range(N_PAIRS):
        cols = slice(pr * LANES, (pr + 1) * LANES)
        fq_ref[0, pr] = pq[:, cols].astype(BF16)
        fk_ref[0, pr] = pk[:, cols].astype(BF16)
        sq_ref[0, pr] = sq[:, cols].astype(BF16)
        sk_ref[0, pr] = sk[:, cols].astype(BF16)

    lane = _iota((tm, LANES), 1)
    in_group = lane % AUG_STRIDE
    valid = lane < AUG_STRIDE * N_HEADS
    aug_q = jnp.where(jnp.logical_and(valid, jnp.logical_and(in_group >= AUG_Q_ONE,
                                                             in_group < AUG_Q_ONE + 3)), 1.0, 0.0)
    aug_k = jnp.where(jnp.logical_and(valid, jnp.logical_and(in_group >= AUG_K_ONE,
                                                             in_group < AUG_K_ONE + 3)), 1.0, 0.0)
    for hd in range(N_HEADS):
        for i3, piece in enumerate((hi, mid, lo)):
            col = jnp.broadcast_to(piece[:, FF_LANE + hd:FF_LANE + hd + 1], (tm, LANES))
            aug_q = jnp.where(lane == AUG_STRIDE * hd + AUG_Q_F + i3, col, aug_q)
            aug_k = jnp.where(lane == AUG_STRIDE * hd + AUG_K_F + i3, -col, aug_k)
    qaug_ref[0] = aug_q.astype(BF16)
    kaug_ref[0] = aug_k.astype(BF16)

    head_of_row = _iota((D_GROUP, LANES), 0) // HEAD_DIM
    nlane = _iota((SUBLANES, LANES), 1)
    norms = jnp.zeros((SUBLANES, LANES), F32)
    for side, pv in enumerate((pq, pk)):
        sel = (head_of_row + side * N_HEADS == _iota((D_GROUP, LANES), 1)).astype(BF16)
        top = jnp.max(_dot((pv * pv).astype(BF16), sel), axis=0, keepdims=True)
        keep = jnp.logical_and(nlane >= side * N_HEADS, nlane < (side + 1) * N_HEADS)
        norms = jnp.where(keep, top, norms)
    norms_ref[0, 0] = norms
    v_stage[...] = proj(C_SV, D_GROUP)
    svt_ref[0] = v_stage[...].T.astype(BF16)
    v_stage[...] = proj(C_FV, D_GROUP)
    vt = v_stage[...].T.astype(BF16)
    ones_rows = (_iota((V_AUG - HEAD_DIM, tm), 0) == 0).astype(BF16)
    for hd in range(N_HEADS):
        fvt_ref[0, hd * V_AUG:hd * V_AUG + HEAD_DIM, :] = vt[hd * HEAD_DIM:(hd + 1) * HEAD_DIM, :]
        fvt_ref[0, hd * V_AUG + HEAD_DIM:(hd + 1) * V_AUG, :] = ones_rows


def _inproj(x, consts, layer, *, tm):
    b, s, _ = x.shape
    w_spec = pl.BlockSpec((1,) + consts[1].shape[1:], lambda bi, i: (layer, 0, 0))
    seq = lambda width: pl.BlockSpec((1, tm, width), lambda bi, i: (bi, i, 0))
    pairs = pl.BlockSpec((1, N_PAIRS, tm, LANES), lambda bi, i: (bi, 0, i, 0))
    pairs_shape = jax.ShapeDtypeStruct((b, N_PAIRS, s, LANES), BF16)
    chan = lambda rows: pl.BlockSpec((1, rows, tm), lambda bi, i: (bi, 0, i))
    chan_shape = lambda rows: jax.ShapeDtypeStruct((b, rows, s), BF16)
    seq_shape = lambda width, dtype=F32: jax.ShapeDtypeStruct((b, s, width), dtype)
    return pl.pallas_call(
        _inproj_kernel,
        grid=(b, s // tm),
        in_specs=[seq(D_MODEL), _full_spec(consts[0]), w_spec] + [_full_spec(a) for a in consts[2:]],
        out_specs=[pairs, pairs, seq(LANES), seq(LANES), chan(N_HEADS * V_AUG),
                   pairs, pairs, chan(D_GROUP),
                   seq(4 * D_GROUP), seq(N_RKV), seq(D_GROUP), seq(LANES), seq(LANES),
                   pl.BlockSpec((1, 1, SUBLANES, LANES), lambda bi, i: (bi, i, 0, 0))],
        out_shape=[pairs_shape, pairs_shape, seq_shape(LANES, BF16), seq_shape(LANES, BF16),
                   chan_shape(N_HEADS * V_AUG), pairs_shape, pairs_shape, chan_shape(D_GROUP),
                   seq_shape(4 * D_GROUP), seq_shape(N_RKV), seq_shape(D_GROUP), seq_shape(LANES),
                   seq_shape(LANES), jax.ShapeDtypeStruct((b, s // tm, SUBLANES, LANES), F32)],
        scratch_shapes=[pltpu.VMEM((1, LANES), F32), pltpu.VMEM((tm, D_GROUP), F32),
                        pltpu.VMEM((tm + SUBLANES, N_RKV), F32),
                        pltpu.VMEM((tm + SUBLANES, LANES), F32)],
        compiler_params=_params(2),
        name="inproj",
    )(x, *consts)


def _fox_kernel(fend_ref, qmax_ref, kmax_ref, q_ref, qaug_ref, k_ref, kaug_ref, vt_ref, o_ref,
                s_scr, p_scr, mx_scr, al_scr, m_scr, acc_scr, *, tq, tk, nb):
    pair = pl.program_id(0)
    i = pl.program_id(1)
    nq = pl.num_programs(1)
    chains = [(b, hh) for b in range(nb) for hh in range(HEADS_PER_PAIR)]
    ids = list(range(len(chains)))
    lane = _iota((tq, LANES), 1)
    q = []
    for b, hh in chains:
        own_half = lane // HEAD_DIM == hh
        own_aug = lane // AUG_STRIDE == pair * HEADS_PER_PAIR + hh
        q.append(jnp.concatenate(
            [jnp.where(own_half, q_ref[b, 0], jnp.zeros((), BF16)),
             jnp.where(own_aug, qaug_ref[b], jnp.zeros((), BF16))], axis=1))
    n_full = (i * tq) // tk

    def first_live_block(b, hh):
        bh = b * N_HEADS + pair * HEADS_PER_PAIR + hh
        qk_bound = FOX_NORM_MARGIN * 2.0 * qmax_ref[bh * nq + i] * kmax_ref[bh]
        f_tile = fend_ref[bh * nq + jnp.maximum(i - 1, 0)]

        def first_live(jj, first):
            j = n_full - 1 - jj
            f_end = fend_ref[bh * nq + (j + 1) * (tk // tq) - 1]
            live = qk_bound + (f_tile - f_end) + FOX_F_SLACK >= FOX_SKIP_LOG2
            return jnp.where(live, j, first)

        return lax.fori_loop(0, n_full, first_live, n_full)

    base = functools.reduce(jnp.minimum, [first_live_block(b, hh) for b, hh in chains])
    n_eff = n_full - base

    def key_rows(j):
        return pl.ds(pl.multiple_of(j * tk, tk), tk)

    def scores_to(j, slot):
        kp = [jnp.concatenate([k_ref[b, 0, key_rows(j), :], kaug_ref[b, key_rows(j), :]], axis=1)
              for b in range(nb)]
        s = [_dot_nt(kp[b], q[c]) for c, (b, _) in zip(ids, chains)]
        for c in ids:
            s_scr[c, slot] = s[c]
            mx_scr[c, slot] = jnp.max(s[c], axis=0, keepdims=True)

    def weighted_values(j, slot):
        return [_dot(vt_ref[b, hh * V_AUG:(hh + 1) * V_AUG, key_rows(j)], p_scr[c, slot])
                for c, (b, hh) in zip(ids, chains)]

    def softmax_to(s, mx, slot):
        m_old = [m_scr[c] for c in ids]
        m_new = [jnp.maximum(a, b) for a, b in zip(m_old, mx)]
        p = [jnp.exp2(a - b).astype(BF16) for a, b in zip(s, m_new)]
        for c in ids:
            m_scr[c] = m_new[c]
            al_scr[c, slot] = jnp.exp2(m_old[c] - m_new[c])
            p_scr[c, slot] = p[c]

    def stage(local, cur):
        j = base + local
        nxt = 1 - cur
        pv_prev = weighted_values(jnp.maximum(j - 1, 0), nxt)
        scores_to(j + 1, nxt)
        softmax_to([s_scr[c, cur] for c in ids], [mx_scr[c, cur] for c in ids], cur)
        for c in ids:
            acc_scr[c] = al_scr[c, nxt] * acc_scr[c] + pv_prev[c]

    def tail(cur):
        nxt = 1 - cur
        pv_prev = weighted_values(jnp.maximum(n_full - 1, 0), nxt)
        mask = n_full * tk + _iota((tk, tq), 0) <= i * tq + _iota((tk, tq), 1)
        s = [jnp.where(mask, s_scr[c, cur], NEG_BIG) for c in ids]
        softmax_to(s, [jnp.max(x, axis=0, keepdims=True) for x in s], cur)
        pv_last = weighted_values(n_full, cur)
        for c, (b, hh) in zip(ids, chains):
            acc = al_scr[c, cur] * (al_scr[c, nxt] * acc_scr[c] + pv_prev[c]) + pv_last[c]
            o_ref[b, hh * HEAD_DIM:(hh + 1) * HEAD_DIM, :] = (
                acc[0:HEAD_DIM] / acc[HEAD_DIM:HEAD_DIM + 1])

    m_scr[...] = jnp.full(m_scr.shape, NEG_BIG, F32)
    acc_scr[...] = jnp.zeros_like(acc_scr)
    p_scr[:, 1] = jnp.zeros((len(chains), tk, tq), BF16)
    al_scr[:, 1] = jnp.ones((len(chains), 1, tq), F32)
    scores_to(base, 0)

    def pair_of_stages(jj, carry):
        stage(2 * jj, 0)
        stage(2 * jj + 1, 1)
        return carry

    lax.fori_loop(0, n_eff // 2, pair_of_stages, 0)

    @pl.when(n_eff % 2 == 1)
    def _():
        stage(n_eff - 1, 0)
        tail(1)

    @pl.when(n_eff % 2 == 0)
    def _():
        tail(0)


def _fox_attention(fend, qmax, kmax, q, qaug, k, kaug, vt, *, tq, tk):
    b, _, s, _ = q.shape
    n_chains = b * HEADS_PER_PAIR
    pair_rows = HEADS_PER_PAIR * V_AUG
    grid_spec = pltpu.PrefetchScalarGridSpec(
        num_scalar_prefetch=3,
        grid=(N_PAIRS, s // tq),
        in_specs=[pl.BlockSpec((b, 1, tq, LANES), lambda p, i, *_: (0, p, i, 0)),
                  pl.BlockSpec((b, tq, LANES), lambda p, i, *_: (0, i, 0)),
                  pl.BlockSpec((b, 1, s, LANES), lambda p, i, *_: (0, p, 0, 0)),
                  pl.BlockSpec((b, s, LANES), lambda p, i, *_: (0, 0, 0)),
                  pl.BlockSpec((b, pair_rows, s), lambda p, i, *_: (0, p, 0))],
        out_specs=pl.BlockSpec((b, HEADS_PER_PAIR * HEAD_DIM, tq), lambda p, i, *_: (0, p, i)),
        scratch_shapes=[pltpu.VMEM((n_chains, 2, tk, tq), F32), pltpu.VMEM((n_chains, 2, tk, tq), BF16),
                        pltpu.VMEM((n_chains, 2, 1, tq), F32), pltpu.VMEM((n_chains, 2, 1, tq), F32),
                        pltpu.VMEM((n_chains, 1, tq), F32), pltpu.VMEM((n_chains, V_AUG, tq), F32)])
    return pl.pallas_call(
        functools.partial(_fox_kernel, tq=tq, tk=tk, nb=b),
        grid_spec=grid_spec,
        out_shape=jax.ShapeDtypeStruct((b, N_HEADS * HEAD_DIM, s), F32),
        compiler_params=_params(2),
        name="fox_attention",
    )(fend, qmax, kmax, q, qaug, k, kaug, vt)


def _sb_kernel(q_ref, k_ref, vt_ref, o_ref, *, tq):
    i = pl.program_id(1)
    tk = tq
    heads = list(range(N_HEADS))
    lane = _iota((tq, LANES), 1)
    q = [jnp.where(lane // HEAD_DIM == hd % HEADS_PER_PAIR, q_ref[0, hd // HEADS_PER_PAIR],
                   jnp.zeros((), BF16)) for hd in heads]
    later = (_iota((tk, tk), 0) < _iota((tk, tk), 1)).astype(BF16)

    def each(f, *xs):
        return [f(*a) for a in zip(*xs)]

    def block(j, rest_q, acc, masked):
        ks = pl.multiple_of(j * tk, tk)
        kp = [k_ref[0, pr, pl.ds(ks, tk), :] for pr in range(N_PAIRS)]
        z = [_dot_nt(kp[hd // HEADS_PER_PAIR], q[hd]) for hd in heads]
        log_keep = each(lambda x: _log2_sigmoid_of_log2(-x), z)
        if masked:
            mask = _iota((tk, tq), 0) < _iota((tk, tq), 1)
            log_keep = each(lambda x: jnp.where(mask, x, 0.0), log_keep)
        split = each(_split2, log_keep)
        rest_in = each(lambda hl: _dot(later, hl[0]) + _dot(later, hl[1]), split)
        att = each(lambda x, lk, ri, rq: jnp.exp2(x + lk + ri + rq), z, log_keep, rest_in, rest_q)
        if masked:
            att = each(lambda x: jnp.where(mask, x, 0.0), att)
        pv = [_dot(vt_ref[0, hd * HEAD_DIM:(hd + 1) * HEAD_DIM, pl.ds(ks, tk)],
                   att[hd].astype(BF16)) for hd in heads]
        acc = each(lambda a, x: a + x, acc, pv)
        rest_q = each(lambda rq, ri, lk: rq + ri[0:1, :] + lk[0:1, :], rest_q, rest_in, log_keep)
        return rest_q, acc

    rest_q, acc = block(i, [jnp.zeros((1, tq), F32)] * N_HEADS,
                        [jnp.zeros((HEAD_DIM, tq), F32)] * N_HEADS, True)

    def cond(c):
        j, rest_q, _ = c
        alive = functools.reduce(jnp.maximum, rest_q)
        return jnp.logical_and(j >= 0, jnp.max(alive) > SB_SKIP_LOG2)

    def body(c):
        j, rest_q, acc = c
        rest_q, acc = block(j, list(rest_q), list(acc), False)
        return j - 1, tuple(rest_q), tuple(acc)

    _, _, acc = lax.while_loop(cond, body, (i - 1, tuple(rest_q), tuple(acc)))
    o_ref[0] = jnp.concatenate(list(acc), axis=0)


def _sb_attention(q, k, vt, *, tq):
    b, _, s, _ = q.shape
    h = N_HEADS
    return pl.pallas_call(
        functools.partial(_sb_kernel, tq=tq),
        grid=(b, s // tq),
        in_specs=[pl.BlockSpec((1, N_PAIRS, tq, LANES), lambda bi, i: (bi, 0, i, 0)),
                  pl.BlockSpec((1, N_PAIRS, s, LANES), lambda bi, i: (bi, 0, 0, 0)),
                  pl.BlockSpec((1, h * HEAD_DIM, s), lambda bi, i: (bi, 0, 0))],
        out_specs=pl.BlockSpec((1, h * HEAD_DIM, tq), lambda bi, i: (bi, 0, i)),
        out_shape=jax.ShapeDtypeStruct((b, h * HEAD_DIM, s), F32),
        compiler_params=_params(2),
        name="sb_attention",
    )(q, k, vt)


def _rwkv_kernel(p_ref, m_ref, w0_ref, w2_ref, a0_ref, a2_ref,
                 kk_ref, ka_ref, rk_ref, lng_ref, lnb_ref, bdm_ref, o_ref,
                 r_s, k_s, v_s, kn_s, al_s, lw_s,
                 wt_s, u0_s, o0_s, mrb_s, rt_s, bh_s, kh_s, pc_s, y_s, ht_s, *, tt, nb):
    c = RWKV_CHUNK
    g = D_GROUP
    n_chunks = tt // c

    @pl.when(pl.program_id(0) == 0)
    def _():
        ht_s[...] = jnp.zeros_like(ht_s)

    bdm = bdm_ref[...]
    bdm_f = bdm.astype(F32)

    for b in range(nb):
        p = p_ref[b]
        misc = m_ref[b]
        k = p[:, g:2 * g]

        w = -_softplus(-(w0_ref[...] + _dot(jnp.tanh(misc).astype(BF16), w2_ref[...]))) - 0.5
        alpha = _sigmoid(a0_ref[...] + _dot(misc.astype(BF16), a2_ref[...]))
        kn = k * kk_ref[...]
        ss = _dot_sel_rhs(kn * kn, bdm)
        r_s[b] = p[:, 0:g]
        k_s[b] = k * (1.0 + (alpha - 1.0) * ka_ref[...])
        v_s[b] = p[:, 2 * g:3 * g]
        kn_s[b] = kn * lax.rsqrt(jnp.maximum(ss, 1e-12))
        al_s[b] = alpha
        lw_s[b] = -jnp.exp(w)

    row = _iota((c, g), 0)
    col = _iota((c, g), 1) % c
    strict = col < row
    incl = col <= row
    eye = (col == row).astype(F32)
    lower_c = (_iota((c, c), 1) <= _iota((c, c), 0)).astype(BF16)
    level_masks = []
    m = 1
    while m < c:
        level_masks.append(jnp.logical_and(
            strict, jnp.logical_and(row // (2 * m) == col // (2 * m), row // m != col // m)))
        m *= 2

    def bd(x):
        return jnp.concatenate([x.astype(BF16)] * N_HEADS, axis=0) * bdm

    def mm(a, b_bf16):
        return _dot(a.astype(BF16), b_bf16)

    def mm_nt(a, b_bf16):
        return _dot_nt(a.astype(BF16), b_bf16)

    def each(f, *xs):
        return [f(*a) for a in zip(*xs)]

    def phase_a(chains, side):
        def tick():
            next(side, None)

        sls = [pl.ds(ci * c, c) for _, ci in chains]
        ld = lambda ref: [ref[b, sl, :] for (b, _), sl in zip(chains, sls)]
        r_c, k_c, v_c, kn_c, al_c, lw_c = (ld(s) for s in (r_s, k_s, v_s, kn_s, al_s, lw_s))
        cl = each(lambda x: _dot_sel_lhs(lower_c, x), lw_c)
        tick()
        cl_last = each(lambda x: x[c - 1:c, :], cl)
        a_t = each(lambda kn, x, lw: -kn * jnp.exp(x - lw), kn_c, cl, lw_c)
        r_t = each(lambda r, x: r * jnp.exp(x), r_c, cl)
        q_inv = each(lambda x: jnp.exp(-x), cl)
        p_rem = each(lambda xl, x: jnp.exp(xl - x), cl_last, cl)
        kna = each(lambda kn, al: kn * al, kn_c, al_c)
        ar = each(lambda a, r: jnp.concatenate([a, r], axis=0), a_t, r_t)
        s_b = each(lambda x, kb, qi: mm_nt(x, bd(kb * qi)), ar, kna, q_inv)
        tick()
        s_k = each(lambda x, kk, qi: mm_nt(x, bd(kk * qi)), ar, k_c, q_inv)
        tick()
        n = each(lambda x: jnp.where(strict, x[0:c], 0.0), s_b)
        a_ak = each(lambda x: jnp.where(strict, x[0:c], 0.0), s_k)
        m_rb = each(lambda x: jnp.where(incl, x[c:2 * c], 0.0), s_b)
        m_rk = each(lambda x: jnp.where(incl, x[c:2 * c], 0.0), s_k)

        inv = each(lambda x: eye + jnp.where(level_masks[0], x, 0.0), n)
        for lm in level_masks[1:]:
            half = each(lambda d, x: mm(d, bd(jnp.where(lm, x, 0.0))), inv, n)
            tick()
            inv = each(lambda d, hf: d + mm(hf, bd(d)), inv, half)
            tick()

        v_bd = each(bd, v_c)
        akv = each(mm, a_ak, v_bd)
        tick()
        wt = each(lambda d, a: mm(d, bd(a)), inv, a_t)
        tick()
        u0 = each(lambda d, x: mm(d, bd(x)), inv, akv)
        tick()
        o0 = each(mm, m_rk, v_bd)
        bh = each(lambda x, p: x * p, kna, p_rem)
        kh = each(lambda x, p: x * p, k_c, p_rem)
        for _ in side:
            pass
        for ref, vals in zip((wt_s, u0_s, o0_s, mrb_s, rt_s, bh_s, kh_s),
                             (wt, u0, o0, m_rb, r_t, bh, kh)):
            for (b, _), sl, val in zip(chains, sls, vals):
                ref[b, sl, :] = val
        for (b, ci), xl in zip(chains, cl_last):
            pc_s[b, pl.ds(ci * SUBLANES, SUBLANES), :] = jnp.broadcast_to(jnp.exp(xl), (SUBLANES, g))

    def phase_b_steps(chunk_ids):
        bs = list(range(nb))
        for ci in chunk_ids:
            sl = pl.ds(ci * c, c)
            ht = [ht_s[b] for b in bs]
            wr = [jnp.concatenate([wt_s[b, sl, :], rt_s[b, sl, :]], axis=0) for b in bs]
            wrh = each(lambda x, hh: mm_nt(x, hh.astype(BF16)), wr, ht)
            yield
            u = [x[0:c] + u0_s[b, sl, :] for x, b in zip(wrh, bs)]
            uv_t = [jnp.concatenate([x, v_s[b, sl, :]], axis=0).T for x, b in zip(u, bs)]
            bk = [jnp.concatenate([bh_s[b, sl, :], kh_s[b, sl, :]], axis=0) for b in bs]
            upd = each(lambda x, y: mm(x, y.astype(BF16)), uv_t, bk)
            mu_ = [mm(mrb_s[b, sl, :], bd(x)) for x, b in zip(u, bs)]
            yield
            for b in bs:
                p_c = pc_s[b, pl.ds(ci * SUBLANES, 1), :]
                y_s[b, sl, :] = wrh[b][c:2 * c] + mu_[b] + o0_s[b, sl, :]
                ht_s[b] = ht[b] * p_c + bdm_f * upd[b]
            yield

    pending = iter(())
    for gi in range(n_chunks // RWKV_UNROLL):
        chunk_ids = range(gi * RWKV_UNROLL, (gi + 1) * RWKV_UNROLL)
        phase_a([(b, ci) for ci in chunk_ids for b in range(nb)], pending)
        pending = phase_b_steps(chunk_ids)
    for _ in pending:
        pass

    inv_n = 1.0 / HEAD_DIM
    for b in range(nb):
        y = y_s[b]
        mean = _dot_sel_rhs(y, bdm) * inv_n
        yc = y - mean
        var = _dot_sel_rhs(yc * yc, bdm) * inv_n
        yn = yc * lax.rsqrt(var + GN_EPS) * lng_ref[...] + lnb_ref[...]
        bonus = _dot_sel_rhs(r_s[b] * k_s[b] * rk_ref[...], bdm) * v_s[b]
        o_ref[b] = yn + bonus


def _rwkv(p_rkv, misc, prm, bdm, *, tt):
    b, s, _ = p_rkv.shape
    g = D_GROUP
    seq = lambda width: pl.BlockSpec((b, tt, width), lambda ti: (0, ti, 0))
    big = pltpu.VMEM((b, tt, g), F32)
    return pl.pallas_call(
        functools.partial(_rwkv_kernel, tt=tt, nb=b),
        grid=(s // tt,),
        in_specs=[seq(N_RKV), seq(LANES)] + [_full_spec(a) for a in prm] + [_full_spec(bdm)],
        out_specs=seq(g),
        out_shape=jax.ShapeDtypeStruct((b, s, g), F32),
        scratch_shapes=[big] * 13
        + [pltpu.VMEM((b, tt // RWKV_CHUNK * SUBLANES, g), F32), big,
           pltpu.VMEM((b, g, g), F32)],
        compiler_params=_params(1),
        name="rwkv7",
    )(p_rkv, misc, *prm, bdm)


def _lru_tile(x_ref, cw_ref, cb_ref, wa_ref, ba_ref, wx_ref, bx_ref, lam_ref, o_ref,
              pad, a_s, u_s, h_s, *, tt, nb):
    @pl.when(pl.program_id(0) == 0)
    def _():
        pad[:, 0:SUBLANES, :] = jnp.zeros((nb, SUBLANES, D_GROUP), F32)
        h_s[...] = jnp.zeros_like(h_s)

    for b in range(nb):
        x = x_ref[b]
        pad[b, SUBLANES:SUBLANES + tt, :] = x
        xc = cw_ref[CONV_WIDTH - 1:CONV_WIDTH, :] * x + cb_ref[...]
        for d in range(1, CONV_WIDTH):
            tap = CONV_WIDTH - 1 - d
            xc = xc + cw_ref[tap:tap + 1, :] * pad[b, SUBLANES - d:SUBLANES - d + tt, :]
        pad[b, 0:SUBLANES, :] = x[tt - SUBLANES:tt, :]

        xb = xc.astype(BF16)
        r = _sigmoid(_dot(xb, wa_ref[...]) + ba_ref[...])
        i = _sigmoid(_dot(xb, wx_ref[...]) + bx_ref[...])
        log_a = -LRU_C * r * _softplus(-lam_ref[...])
        a_s[b] = jnp.exp(log_a)
        th = jnp.tanh(log_a)
        u_s[b] = jnp.sqrt(-2.0 * th / (1.0 - th)) * (i * xc)

    row = _iota((SUBLANES, D_GROUP), 0)

    def rows_above(x, k, fill):
        return jnp.where(row >= k, pltpu.roll(x, k, axis=0), fill)

    def group(gi, hs):
        base = pl.multiple_of(gi * SUBLANES, SUBLANES)
        carry = []
        for b in range(nb):
            a = a_s[b, pl.ds(base, SUBLANES), :]
            u = u_s[b, pl.ds(base, SUBLANES), :]
            k = 1
            while k < SUBLANES:
                u = a * rows_above(u, k, 0.0) + u
                a = a * rows_above(a, k, 1.0)
                k *= 2
            h = a * hs[b] + u
            o_ref[b, pl.ds(base, SUBLANES), :] = h
            carry.append(h[SUBLANES - 1:SUBLANES, :])
        return tuple(carry)

    hs = lax.fori_loop(0, tt // SUBLANES, group, tuple(h_s[b] for b in range(nb)))
    for b in range(nb):
        h_s[b] = hs[b]


def _outproj_kernel(x_ref, yft_ref, yst_ref, yr_ref, lx_ref, gates_ref, w_ref, fg_ref,
                    cw_ref, cb_ref, wa_ref, ba_ref, wx_ref, bx_ref, lam_ref, o_ref,
                    pad, a_s, u_s, yl_s, h_s, *, final, tm, nb):
    _lru_tile(lx_ref, cw_ref, cb_ref, wa_ref, ba_ref, wx_ref, bx_ref, lam_ref, yl_s,
              pad, a_s, u_s, h_s, tt=tm, nb=nb)
    for b in range(nb):
        acc = x_ref[b]
        ys = (yft_ref[b].T, yst_ref[b].T, yr_ref[b], yl_s[b])
        for gi, y in enumerate(ys):
            gate = gates_ref[b, :, gi * D_GROUP:(gi + 1) * D_GROUP]
            y = y * (gate * _sigmoid(gate))
            acc = acc + _dot(y.astype(BF16), w_ref[gi * D_GROUP:(gi + 1) * D_GROUP, :])
        if final:
            ms = jnp.mean(acc * acc, axis=-1, keepdims=True)
            acc = acc * lax.rsqrt(ms + RMS_EPS) * fg_ref[...]
        o_ref[b] = acc


def _outproj(x, y_fox_t, y_sb_t, y_rw, lx, gates, w, final_g, lru_prm, *, tm, final):
    b, s, _ = x.shape
    g = D_GROUP
    seq = lambda width: pl.BlockSpec((b, tm, width), lambda i: (0, i, 0))
    chan = pl.BlockSpec((b, g, tm), lambda i: (0, 0, i))
    big = pltpu.VMEM((b, tm, g), F32)
    return pl.pallas_call(
        functools.partial(_outproj_kernel, final=final, tm=tm, nb=b),
        grid=(s // tm,),
        in_specs=[seq(D_MODEL), chan, chan, seq(g), seq(g), seq(4 * g),
                  _full_spec(w), _full_spec(final_g)] + [_full_spec(a) for a in lru_prm],
        out_specs=seq(D_MODEL),
        out_shape=jax.ShapeDtypeStruct((b, s, D_MODEL), F32),
        scratch_shapes=[pltpu.VMEM((b, tm + SUBLANES, g), F32), big, big, big,
                        pltpu.VMEM((b, 1, g), F32)],
        compiler_params=_params(1),
        name="outproj",
    )(x, y_fox_t, y_sb_t, y_rw, lx, gates, w, final_g, *lru_prm)


def _w_in_segments():
    g, h, r = D_GROUP, N_HEADS, RWKV_LORA
    o_ff = 4 * g
    o_sb = o_ff + h
    o_rw = o_sb + 4 * g
    o_rg = o_rw + 3 * g + 2 * r
    o_lx = o_rg + g
    o_lg = o_lx + g
    segs = [(0, C_FQ, 2 * g), (o_sb, C_SQ, 2 * g), (2 * g, C_FV, g), (o_sb + 2 * g, C_SV, g),
             (3 * g, C_GATES, g), (o_sb + 3 * g, C_GATES + g, g), (o_rg, C_GATES + 2 * g, g),
             (o_lg, C_GATES + 3 * g, g),
             (o_rw, C_RKV, 3 * g), (o_lx, C_LX, g),
             (o_rw + 3 * g, C_MISC, 2 * r), (o_ff, C_MISC + FF_LANE, h)]
    return segs


def _w_in_kernel(w_ref, o_ref):
    o_ref[...] = jnp.zeros_like(o_ref)
    for src, dst, width in _w_in_segments():
        o_ref[0, :, dst:dst + width] = w_ref[0, :, src:src + width].astype(BF16)


def _permute_w_in(w_in, *, tr):
    depth, d, n_in = w_in.shape
    return pl.pallas_call(
        _w_in_kernel,
        grid=(depth, d // tr),
        in_specs=[pl.BlockSpec((1, tr, n_in), lambda l, i: (l, i, 0))],
        out_specs=pl.BlockSpec((1, tr, N_PROJ), lambda l, i: (l, i, 0)),
        out_shape=jax.ShapeDtypeStruct((depth, d, N_PROJ), BF16),
        compiler_params=_params(2),
        name="w_in_layout",
    )(w_in)


def _block_diag(w):
    h, n, _ = w.shape
    eye = jnp.eye(h, dtype=w.dtype)
    return jnp.einsum('hij,hk->hikj', w, eye).reshape(h * n, h * n)


def _pick_tile(s, pref):
    t = pref
    while s % t:
        t //= 2
    return t


def kernel(x, norm_g, w_in, b_forget, rwkv_mu, rwkv_w0, rwkv_w2, rwkv_a0, rwkv_a2, rwkv_k_k,
           rwkv_k_a, rwkv_r_k, rwkv_ln_g, rwkv_ln_b, lru_conv_w, lru_conv_b, lru_w_a, lru_b_a,
           lru_w_x, lru_b_x, lru_lambda, w_out, final_g):
    b, s, d = x.shape
    depth = w_in.shape[0]
    g, h, dh, r = D_GROUP, N_HEADS, HEAD_DIM, RWKV_LORA
    tm = _pick_tile(s, 512)
    tq = _pick_tile(s, 256)
    tk_fox = tq
    tt = _pick_tile(s, 512)
    row = lambda a: a.reshape(1, -1).astype(F32)

    bdm = _block_diag(jnp.ones((h, dh, dh), BF16))
    w_in_k = _permute_w_in(w_in, tr=LANES)
    for l in range(depth):
        fbias = jnp.zeros((1, LANES), F32).at[0, FF_LANE:FF_LANE + h].set(b_forget[l])
        mu = rwkv_mu[l]
        consts = [row(norm_g[l]), w_in_k, fbias, row(mu[:N_RKV]),
                  jnp.zeros((1, LANES), F32).at[0, :2 * r].set(mu[N_RKV:])]
        (fq, fk, qaug, kaug, fvt, sq, sk, svt, gates, rkv, lx, misc, f2,
         norms) = _inproj(x, consts, l, tm=tm)

        per_head = lambda a: a.transpose(0, 2, 1).reshape(-1)
        fend = per_head(f2[:, tq - 1::tq, FF_LANE:FF_LANE + h])
        qmax = per_head(jnp.repeat(jnp.sqrt(norms[:, :, 0, 0:h]), tm // tq, axis=1))
        kmax = jnp.sqrt(jnp.max(norms[:, :, 0, h:2 * h], axis=1)).reshape(-1)
        y_fox_t = _fox_attention(fend, qmax, kmax, fq, qaug, fk, kaug, fvt, tq=tq, tk=tk_fox)
        y_sb_t = _sb_attention(sq, sk, svt, tq=tq)

        pad_rows = lambda a, lo: jnp.zeros((LANES, g), F32).at[lo:lo + r].set(a).astype(BF16)
        rw_prm = [row(rwkv_w0[l]), pad_rows(rwkv_w2[l], 0), row(rwkv_a0[l]), pad_rows(rwkv_a2[l], r),
                  row(rwkv_k_k[l]), row(rwkv_k_a[l]), row(rwkv_r_k[l]), row(rwkv_ln_g[l]),
                  row(rwkv_ln_b[l])]
        y_rw = _rwkv(rkv, misc, rw_prm, bdm, tt=tt)

        lru_prm = [lru_conv_w[l].astype(F32), row(lru_conv_b[l]),
                   _block_diag(lru_w_a[l]).astype(BF16), row(lru_b_a[l]),
                   _block_diag(lru_w_x[l]).astype(BF16), row(lru_b_x[l]), row(lru_lambda[l])]
        x = _outproj(x, y_fox_t, y_sb_t, y_rw, lx, gates, w_out[l].astype(BF16), row(final_g),
                     lru_prm, tm=tm, final=(l == depth - 1))
    return x
```

```python
import functools

import jax
import jax.numpy as jnp
from jax import lax
from jax.experimental import pallas as pl
from jax.experimental.pallas import tpu as pltpu

F32 = jnp.float32
BF16 = jnp.bfloat16

D_MODEL = 1024
D_GROUP = 256
N_HEADS = 4
HEAD_DIM = 64
RWKV_LORA = 32
CONV_WIDTH = 4
LRU_C = 8.0
RMS_EPS = 1e-6
GN_EPS = 64e-5
N_RKV = 3 * D_GROUP

LANES = 128
SUBLANES = 8
VMEM_LIMIT_BYTES = 56 * 1024 * 1024

C_FQ = 0
C_FK = C_FQ + D_GROUP
C_SQ = C_FK + D_GROUP
C_SK = C_SQ + D_GROUP
C_FV = C_SK + D_GROUP
C_SV = C_FV + D_GROUP
C_GATES = C_SV + D_GROUP
C_RKV = C_GATES + 4 * D_GROUP
C_LX = C_RKV + N_RKV
C_MISC = C_LX + D_GROUP
N_PROJ = C_MISC + LANES
FF_LANE = 2 * RWKV_LORA
HEADS_PER_PAIR = LANES // HEAD_DIM
N_PAIRS = N_HEADS // HEADS_PER_PAIR
AUG_STRIDE = 8
AUG_K_F = 0
AUG_K_ONE = 3
AUG_Q_ONE = 0
AUG_Q_F = 3

CUMSUM_BLOCK = LANES
V_AUG = HEAD_DIM + 16
LOG2E = 1.4426950408889634

FOX_SKIP_LOG2 = -150.0
FOX_NORM_MARGIN = 1.02
FOX_F_SLACK = 1.0

NEG_BIG = -1e30
SB_SKIP_LOG2 = -150.0
RWKV_CHUNK = 64
RWKV_UNROLL = 4


def _dot(a, b):
    return jnp.dot(a, b, preferred_element_type=F32)


def _dot_nt(a, b):
    return lax.dot_general(a, b, (((1,), (1,)), ((), ())), preferred_element_type=F32)


def _split2(x):
    hi = x.astype(BF16)
    lo = (x - hi.astype(F32)).astype(BF16)
    return hi, lo


def _split3(x):
    hi = x.astype(BF16)
    r1 = x - hi.astype(F32)
    mid = r1.astype(BF16)
    lo = (r1 - mid.astype(F32)).astype(BF16)
    return hi, mid, lo


def _dot_sel_lhs(sel, x):
    hi, mid, lo = _split3(x)
    return _dot(sel, hi) + (_dot(sel, mid) + _dot(sel, lo))


def _dot_sel_rhs(x, sel):
    return _dot(x.astype(BF16), sel)


def _softplus(x):
    return jnp.maximum(x, 0.0) + jnp.log(1.0 + jnp.exp(-jnp.abs(x)))


def _log_sigmoid(x):
    return jnp.minimum(x, 0.0) - jnp.log(1.0 + jnp.exp(-jnp.abs(x)))


def _log2_sigmoid_of_log2(x2):
    return jnp.minimum(x2, 0.0) - jnp.log2(1.0 + jnp.exp2(-jnp.abs(x2)))


def _sigmoid(x):
    return 1.0 / (1.0 + jnp.exp(-x))


def _iota(shape, dim):
    return lax.broadcasted_iota(jnp.int32, shape, dim)


def _full_spec(a):
    return pl.BlockSpec(a.shape, lambda *_: (0,) * a.ndim)


def _params(n_grid):
    return pltpu.CompilerParams(dimension_semantics=("arbitrary",) * n_grid,
                                vmem_limit_bytes=VMEM_LIMIT_BYTES)


def _inproj_kernel(x_ref, g_ref, w_ref, fb_ref, mu_rkv_ref, mu_misc_ref,
                   fq_ref, fk_ref, qaug_ref, kaug_ref, fvt_ref, sq_ref, sk_ref, svt_ref, gates_ref,
                   rkv_ref, lx_ref, misc_ref, f2_ref, norms_ref, ftot, v_stage, pad_rkv, pad_misc):
    @pl.when(pl.program_id(1) == 0)
    def _():
        ftot[...] = jnp.zeros_like(ftot)
        pad_rkv[0:SUBLANES, :] = jnp.zeros((SUBLANES, N_RKV), F32)
        pad_misc[0:SUBLANES, :] = jnp.zeros((SUBLANES, LANES), F32)

    x = x_ref[0]
    tm = x.shape[0]
    ms = jnp.mean(x * x, axis=-1, keepdims=True)
    h = (x * lax.rsqrt(ms + RMS_EPS) * g_ref[...]).astype(BF16)
    scale = HEAD_DIM ** -0.5

    def proj(c0, width):
        return _dot(h, w_ref[0, :, c0:c0 + width])

    misc = proj(C_MISC, LANES)
    gates_ref[0] = proj(C_GATES, 4 * D_GROUP)
    lx_ref[0] = proj(C_LX, D_GROUP)

    for val, pad, mu_ref, out_ref in ((proj(C_RKV, N_RKV), pad_rkv, mu_rkv_ref, rkv_ref),
                                      (misc, pad_misc, mu_misc_ref, misc_ref)):
        pad[SUBLANES:SUBLANES + tm, :] = val
        prev = pad[SUBLANES - 1:SUBLANES - 1 + tm, :]
        pad[0:SUBLANES, :] = val[tm - SUBLANES:tm, :]
        out_ref[0] = val + (prev - val) * mu_ref[...]

    lf = _log_sigmoid(misc + fb_ref[...])
    cb = CUMSUM_BLOCK
    lower = (_iota((cb, cb), 1) <= _iota((cb, cb), 0)).astype(BF16)
    run = ftot[...]
    blocks = []
    for r0 in range(0, tm, cb):
        blk = _dot_sel_lhs(lower, lf[r0:r0 + cb, :]) + run
        run = blk[cb - 1:cb, :]
        blocks.append(blk)
    ftot[...] = run
    f = jnp.concatenate(blocks, axis=0)
    f2 = f * LOG2E
    f2_ref[0] = f2
    hi = f2.astype(BF16).astype(F32)
    mid = (f2 - hi).astype(BF16).astype(F32)
    lo = (f2 - hi - mid).astype(BF16).astype(F32)

    pq = proj(C_FQ, D_GROUP) * (scale * LOG2E)
    pk = proj(C_FK, D_GROUP)
    sq = proj(C_SQ, D_GROUP) * (scale * LOG2E)
    sk = proj(C_SK, D_GROUP)
    for pr in range(N_PAIRS):
        cols = slice(pr * LANES, (pr + 1) * LANES)
        fq_ref[0, pr] = pq[:, cols].astype(BF16)
        fk_ref[0, pr] = pk[:, cols].astype(BF16)
        sq_ref[0, pr] = sq[:, cols].astype(BF16)
        sk_ref[0, pr] = sk[:, cols].astype(BF16)

    lane = _iota((tm, LANES), 1)
    in_group = lane % AUG_STRIDE
    valid = lane < AUG_STRIDE * N_HEADS
    aug_q = jnp.where(jnp.logical_and(valid, jnp.logical_and(in_group >= AUG_Q_ONE,
                                                             in_group < AUG_Q_ONE + 3)), 1.0, 0.0)
    aug_k = jnp.where(jnp.logical_and(valid, jnp.logical_and(in_group >= AUG_K_ONE,
                                                             in_group < AUG_K_ONE + 3)), 1.0, 0.0)
    for hd in range(N_HEADS):
        for i3, piece in enumerate((hi, mid, lo)):
            col = jnp.broadcast_to(piece[:, FF_LANE + hd:FF_LANE + hd + 1], (tm, LANES))
            aug_q = jnp.where(lane == AUG_STRIDE * hd + AUG_Q_F + i3, col, aug_q)
            aug_k = jnp.where(lane == AUG_STRIDE * hd + AUG_K_F + i3, -col, aug_k)
    qaug_ref[0] = aug_q.astype(BF16)
    kaug_ref[0] = aug_k.astype(BF16)

    head_of_row = _iota((D_GROUP, LANES), 0) // HEAD_DIM
    nlane = _iota((SUBLANES, LANES), 1)
    norms = jnp.zeros((SUBLANES, LANES), F32)
    for side, pv in enumerate((pq, pk)):
        sel = (head_of_row + side * N_HEADS == _iota((D_GROUP, LANES), 1)).astype(BF16)
        top = jnp.max(_dot((pv * pv).astype(BF16), sel), axis=0, keepdims=True)
        keep = jnp.logical_and(nlane >= side * N_HEADS, nlane < (side + 1) * N_HEADS)
        norms = jnp.where(keep, top, norms)
    norms_ref[0, 0] = norms
    v_stage[...] = proj(C_SV, D_GROUP)
    svt_ref[0] = v_stage[...].T.astype(BF16)
    v_stage[...] = proj(C_FV, D_GROUP)
    vt = v_stage[...].T.astype(BF16)
    ones_rows = (_iota((V_AUG - HEAD_DIM, tm), 0) == 0).astype(BF16)
    for hd in range(N_HEADS):
        fvt_ref[0, hd * V_AUG:hd * V_AUG + HEAD_DIM, :] = vt[hd * HEAD_DIM:(hd + 1) * HEAD_DIM, :]
        fvt_ref[0, hd * V_AUG + HEAD_DIM:(hd + 1) * V_AUG, :] = ones_rows


def _inproj(x, consts, layer, *, tm):
    b, s, _ = x.shape
    w_spec = pl.BlockSpec((1,) + consts[1].shape[1:], lambda bi, i: (layer, 0, 0))
    seq = lambda width: pl.BlockSpec((1, tm, width), lambda bi, i: (bi, i, 0))
    pairs = pl.BlockSpec((1, N_PAIRS, tm, LANES), lambda bi, i: (bi, 0, i, 0))
    pairs_shape = jax.ShapeDtypeStruct((b, N_PAIRS, s, LANES), BF16)
    chan = lambda rows: pl.BlockSpec((1, rows, tm), lambda bi, i: (bi, 0, i))
    chan_shape = lambda rows: jax.ShapeDtypeStruct((b, rows, s), BF16)
    seq_shape = lambda width, dtype=F32: jax.ShapeDtypeStruct((b, s, width), dtype)
    return pl.pallas_call(
        _inproj_kernel,
        grid=(b, s // tm),
        in_specs=[seq(D_MODEL), _full_spec(consts[0]), w_spec] + [_full_spec(a) for a in consts[2:]],
        out_specs=[pairs, pairs, seq(LANES), seq(LANES), chan(N_HEADS * V_AUG),
                   pairs, pairs, chan(D_GROUP),
                   seq(4 * D_GROUP), seq(N_RKV), seq(D_GROUP), seq(LANES), seq(LANES),
                   pl.BlockSpec((1, 1, SUBLANES, LANES), lambda bi, i: (bi, i, 0, 0))],
        out_shape=[pairs_shape, pairs_shape, seq_shape(LANES, BF16), seq_shape(LANES, BF16),
                   chan_shape(N_HEADS * V_AUG), pairs_shape, pairs_shape, chan_shape(D_GROUP),
                   seq_shape(4 * D_GROUP), seq_shape(N_RKV), seq_shape(D_GROUP), seq_shape(LANES),
                   seq_shape(LANES), jax.ShapeDtypeStruct((b, s // tm, SUBLANES, LANES), F32)],
        scratch_shapes=[pltpu.VMEM((1, LANES), F32), pltpu.VMEM((tm, D_GROUP), F32),
                        pltpu.VMEM((tm + SUBLANES, N_RKV), F32),
                        pltpu.VMEM((tm + SUBLANES, LANES), F32)],
        compiler_params=_params(2),
        name="inproj",
    )(x, *consts)


def _fox_kernel(fend_ref, qmax_ref, kmax_ref, q_ref, qaug_ref, k_ref, kaug_ref, vt_ref, o_ref,
                s_scr, p_scr, mx_scr, al_scr, m_scr, acc_scr, *, tq, tk, nb):
    pair = pl.program_id(0)
    i = pl.program_id(1)
    nq = pl.num_programs(1)
    chains = [(b, hh) for b in range(nb) for hh in range(HEADS_PER_PAIR)]
    ids = list(range(len(chains)))
    lane = _iota((tq, LANES), 1)
    q = []
    for b, hh in chains:
        own_half = lane // HEAD_DIM == hh
        own_aug = lane // AUG_STRIDE == pair * HEADS_PER_PAIR + hh
        q.append(jnp.concatenate(
            [jnp.where(own_half, q_ref[b, 0], jnp.zeros((), BF16)),
             jnp.where(own_aug, qaug_ref[b], jnp.zeros((), BF16))], axis=1))
    n_full = (i * tq) // tk

    bhs = [b * N_HEADS + pair * HEADS_PER_PAIR + hh for b, hh in chains]
    rows = [bh * nq for bh in bhs]
    slack = [FOX_NORM_MARGIN * 2.0 * qmax_ref[r + i] * kmax_ref[bh]
             + fend_ref[r + jnp.maximum(i - 1, 0)] + FOX_F_SLACK - FOX_SKIP_LOG2
             for bh, r in zip(bhs, rows)]

    def some_chain_live(j):
        last_key = (jnp.maximum(j, 0) + 1) * (tk // tq) - 1
        live = [sl >= fend_ref[r + last_key] for sl, r in zip(slack, rows)]
        return jnp.logical_and(j >= 0, functools.reduce(jnp.logical_or, live))

    base = lax.while_loop(some_chain_live, lambda j: j - 1, n_full - 1) + 1
    n_eff = n_full - base

    def key_rows(j):
        return pl.ds(pl.multiple_of(j * tk, tk), tk)

    def scores_to(j, slot):
        kp = [jnp.concatenate([k_ref[b, 0, key_rows(j), :], kaug_ref[b, key_rows(j), :]], axis=1)
              for b in range(nb)]
        s = [_dot_nt(kp[b], q[c]) for c, (b, _) in zip(ids, chains)]
        for c in ids:
            s_scr[c, slot] = s[c]
            mx_scr[c, slot] = jnp.max(s[c], axis=0, keepdims=True)

    def weighted_values(j, slot):
        return [_dot(vt_ref[b, hh * V_AUG:(hh + 1) * V_AUG, key_rows(j)], p_scr[c, slot])
                for c, (b, hh) in zip(ids, chains)]

    def softmax_to(s, mx, slot):
        m_old = [m_scr[c] for c in ids]
        m_new = [jnp.maximum(a, b) for a, b in zip(m_old, mx)]
        p = [jnp.exp2(a - b).astype(BF16) for a, b in zip(s, m_new)]
        for c in ids:
            m_scr[c] = m_new[c]
            al_scr[c, slot] = jnp.exp2(m_old[c] - m_new[c])
            p_scr[c, slot] = p[c]

    def stage(local, cur):
        j = base + local
        nxt = 1 - cur
        pv_prev = weighted_values(jnp.maximum(j - 1, 0), nxt)
        scores_to(j + 1, nxt)
        softmax_to([s_scr[c, cur] for c in ids], [mx_scr[c, cur] for c in ids], cur)
        for c in ids:
            acc_scr[c] = al_scr[c, nxt] * acc_scr[c] + pv_prev[c]

    def tail(cur):
        nxt = 1 - cur
        pv_prev = weighted_values(jnp.maximum(n_full - 1, 0), nxt)
        mask = n_full * tk + _iota((tk, tq), 0) <= i * tq + _iota((tk, tq), 1)
        s = [jnp.where(mask, s_scr[c, cur], NEG_BIG) for c in ids]
        softmax_to(s, [jnp.max(x, axis=0, keepdims=True) for x in s], cur)
        pv_last = weighted_values(n_full, cur)
        for c, (b, hh) in zip(ids, chains):
            acc = al_scr[c, cur] * (al_scr[c, nxt] * acc_scr[c] + pv_prev[c]) + pv_last[c]
            o_ref[b, hh * HEAD_DIM:(hh + 1) * HEAD_DIM, :] = (
                acc[0:HEAD_DIM] / acc[HEAD_DIM:HEAD_DIM + 1])

    m_scr[...] = jnp.full(m_scr.shape, NEG_BIG, F32)
    acc_scr[...] = jnp.zeros_like(acc_scr)
    p_scr[:, 1] = jnp.zeros((len(chains), tk, tq), BF16)
    al_scr[:, 1] = jnp.ones((len(chains), 1, tq), F32)
    scores_to(base, 0)

    def pair_of_stages(jj, carry):
        stage(2 * jj, 0)
        stage(2 * jj + 1, 1)
        return carry

    lax.fori_loop(0, n_eff // 2, pair_of_stages, 0)

    @pl.when(n_eff % 2 == 1)
    def _():
        stage(n_eff - 1, 0)
        tail(1)

    @pl.when(n_eff % 2 == 0)
    def _():
        tail(0)


def _fox_attention(fend, qmax, kmax, q, qaug, k, kaug, vt, *, tq, tk):
    b, _, s, _ = q.shape
    n_chains = b * HEADS_PER_PAIR
    pair_rows = HEADS_PER_PAIR * V_AUG
    grid_spec = pltpu.PrefetchScalarGridSpec(
        num_scalar_prefetch=3,
        grid=(N_PAIRS, s // tq),
        in_specs=[pl.BlockSpec((b, 1, tq, LANES), lambda p, i, *_: (0, p, i, 0)),
                  pl.BlockSpec((b, tq, LANES), lambda p, i, *_: (0, i, 0)),
                  pl.BlockSpec((b, 1, s, LANES), lambda p, i, *_: (0, p, 0, 0)),
                  pl.BlockSpec((b, s, LANES), lambda p, i, *_: (0, 0, 0)),
                  pl.BlockSpec((b, pair_rows, s), lambda p, i, *_: (0, p, 0))],
        out_specs=pl.BlockSpec((b, HEADS_PER_PAIR * HEAD_DIM, tq), lambda p, i, *_: (0, p, i)),
        scratch_shapes=[pltpu.VMEM((n_chains, 2, tk, tq), F32), pltpu.VMEM((n_chains, 2, tk, tq), BF16),
                        pltpu.VMEM((n_chains, 2, 1, tq), F32), pltpu.VMEM((n_chains, 2, 1, tq), F32),
                        pltpu.VMEM((n_chains, 1, tq), F32), pltpu.VMEM((n_chains, V_AUG, tq), F32)])
    return pl.pallas_call(
        functools.partial(_fox_kernel, tq=tq, tk=tk, nb=b),
        grid_spec=grid_spec,
        out_shape=jax.ShapeDtypeStruct((b, N_HEADS * HEAD_DIM, s), F32),
        compiler_params=_params(2),
        name="fox_attention",
    )(fend, qmax, kmax, q, qaug, k, kaug, vt)


def _sb_kernel(q_ref, k_ref, vt_ref, o_ref, *, tq):
    i = pl.program_id(1)
    tk = tq
    heads = list(range(N_HEADS))
    lane = _iota((tq, LANES), 1)
    q = [jnp.where(lane // HEAD_DIM == hd % HEADS_PER_PAIR, q_ref[0, hd // HEADS_PER_PAIR],
                   jnp.zeros((), BF16)) for hd in heads]
    later = (_iota((tk, tk), 0) < _iota((tk, tk), 1)).astype(BF16)

    def each(f, *xs):
        return [f(*a) for a in zip(*xs)]

    def block(j, rest_q, acc, masked):
        ks = pl.multiple_of(j * tk, tk)
        kp = [k_ref[0, pr, pl.ds(ks, tk), :] for pr in range(N_PAIRS)]
        z = [_dot_nt(kp[hd // HEADS_PER_PAIR], q[hd]) for hd in heads]
        log_keep = each(lambda x: _log2_sigmoid_of_log2(-x), z)
        if masked:
            mask = _iota((tk, tq), 0) < _iota((tk, tq), 1)
            log_keep = each(lambda x: jnp.where(mask, x, 0.0), log_keep)
        split = each(_split2, log_keep)
        rest_in = each(lambda hl: _dot(later, hl[0]) + _dot(later, hl[1]), split)
        att = each(lambda x, lk, ri, rq: jnp.exp2(x + lk + ri + rq), z, log_keep, rest_in, rest_q)
        if masked:
            att = each(lambda x: jnp.where(mask, x, 0.0), att)
        pv = [_dot(vt_ref[0, hd * HEAD_DIM:(hd + 1) * HEAD_DIM, pl.ds(ks, tk)],
                   att[hd].astype(BF16)) for hd in heads]
        acc = each(lambda a, x: a + x, acc, pv)
        rest_q = each(lambda rq, ri, lk: rq + ri[0:1, :] + lk[0:1, :], rest_q, rest_in, log_keep)
        return rest_q, acc

    rest_q, acc = block(i, [jnp.zeros((1, tq), F32)] * N_HEADS,
                        [jnp.zeros((HEAD_DIM, tq), F32)] * N_HEADS, True)

    def cond(c):
        j, rest_q, _ = c
        alive = functools.reduce(jnp.maximum, rest_q)
        return jnp.logical_and(j >= 0, jnp.max(alive) > SB_SKIP_LOG2)

    def body(c):
        j, rest_q, acc = c
        rest_q, acc = block(j, list(rest_q), list(acc), False)
        return j - 1, tuple(rest_q), tuple(acc)

    _, _, acc = lax.while_loop(cond, body, (i - 1, tuple(rest_q), tuple(acc)))
    o_ref[0] = jnp.concatenate(list(acc), axis=0)


def _sb_attention(q, k, vt, *, tq):
    b, _, s, _ = q.shape
    h = N_HEADS
    return pl.pallas_call(
        functools.partial(_sb_kernel, tq=tq),
        grid=(b, s // tq),
        in_specs=[pl.BlockSpec((1, N_PAIRS, tq, LANES), lambda bi, i: (bi, 0, i, 0)),
                  pl.BlockSpec((1, N_PAIRS, s, LANES), lambda bi, i: (bi, 0, 0, 0)),
                  pl.BlockSpec((1, h * HEAD_DIM, s), lambda bi, i: (bi, 0, 0))],
        out_specs=pl.BlockSpec((1, h * HEAD_DIM, tq), lambda bi, i: (bi, 0, i)),
        out_shape=jax.ShapeDtypeStruct((b, h * HEAD_DIM, s), F32),
        compiler_params=_params(2),
        name="sb_attention",
    )(q, k, vt)


def _rwkv_kernel(p_ref, m_ref, w0_ref, w2_ref, a0_ref, a2_ref,
                 kk_ref, ka_ref, rk_ref, lng_ref, lnb_ref, bdm_ref, o_ref,
                 r_s, k_s, v_s, kn_s, al_s, lw_s,
                 wt_s, u0_s, o0_s, mrb_s, rt_s, bh_s, kh_s, pc_s, y_s, ht_s, *, tt, nb):
    c = RWKV_CHUNK
    g = D_GROUP
    n_chunks = tt // c

    @pl.when(pl.program_id(0) == 0)
    def _():
        ht_s[...] = jnp.zeros_like(ht_s)

    bdm = bdm_ref[...]
    bdm_f = bdm.astype(F32)

    for b in range(nb):
        p = p_ref[b]
        misc = m_ref[b]
        k = p[:, g:2 * g]

        w = -_softplus(-(w0_ref[...] + _dot(jnp.tanh(misc).astype(BF16), w2_ref[...]))) - 0.5
        alpha = _sigmoid(a0_ref[...] + _dot(misc.astype(BF16), a2_ref[...]))
        kn = k * kk_ref[...]
        ss = _dot_sel_rhs(kn * kn, bdm)
        r_s[b] = p[:, 0:g]
        k_s[b] = k * (1.0 + (alpha - 1.0) * ka_ref[...])
        v_s[b] = p[:, 2 * g:3 * g]
        kn_s[b] = kn * lax.rsqrt(jnp.maximum(ss, 1e-12))
        al_s[b] = alpha
        lw_s[b] = -jnp.exp(w)

    row = _iota((c, g), 0)
    col = _iota((c, g), 1) % c
    strict = col < row
    incl = col <= row
    eye = (col == row).astype(F32)
    lower_c = (_iota((c, c), 1) <= _iota((c, c), 0)).astype(BF16)
    level_masks = []
    m = 1
    while m < c:
        level_masks.append(jnp.logical_and(
            strict, jnp.logical_and(row // (2 * m) == col // (2 * m), row // m != col // m)))
        m *= 2

    def bd(x):
        return jnp.concatenate([x.astype(BF16)] * N_HEADS, axis=0) * bdm

    def mm(a, b_bf16):
        return _dot(a.astype(BF16), b_bf16)

    def mm_nt(a, b_bf16):
        return _dot_nt(a.astype(BF16), b_bf16)

    def each(f, *xs):
        return [f(*a) for a in zip(*xs)]

    def phase_a(chains, side):
        def tick():
            next(side, None)

        sls = [pl.ds(ci * c, c) for _, ci in chains]
        ld = lambda ref: [ref[b, sl, :] for (b, _), sl in zip(chains, sls)]
        r_c, k_c, v_c, kn_c, al_c, lw_c = (ld(s) for s in (r_s, k_s, v_s, kn_s, al_s, lw_s))
        cl = each(lambda x: _dot_sel_lhs(lower_c, x), lw_c)
        tick()
        cl_last = each(lambda x: x[c - 1:c, :], cl)
        a_t = each(lambda kn, x, lw: -kn * jnp.exp(x - lw), kn_c, cl, lw_c)
        r_t = each(lambda r, x: r * jnp.exp(x), r_c, cl)
        q_inv = each(lambda x: jnp.exp(-x), cl)
        p_rem = each(lambda xl, x: jnp.exp(xl - x), cl_last, cl)
        kna = each(lambda kn, al: kn * al, kn_c, al_c)
        ar = each(lambda a, r: jnp.concatenate([a, r], axis=0), a_t, r_t)
        s_b = each(lambda x, kb, qi: mm_nt(x, bd(kb * qi)), ar, kna, q_inv)
        tick()
        s_k = each(lambda x, kk, qi: mm_nt(x, bd(kk * qi)), ar, k_c, q_inv)
        tick()
        n = each(lambda x: jnp.where(strict, x[0:c], 0.0), s_b)
        a_ak = each(lambda x: jnp.where(strict, x[0:c], 0.0), s_k)
        m_rb = each(lambda x: jnp.where(incl, x[c:2 * c], 0.0), s_b)
        m_rk = each(lambda x: jnp.where(incl, x[c:2 * c], 0.0), s_k)

        inv = each(lambda x: eye + jnp.where(level_masks[0], x, 0.0), n)
        for lm in level_masks[1:]:
            half = each(lambda d, x: mm(d, bd(jnp.where(lm, x, 0.0))), inv, n)
            tick()
            inv = each(lambda d, hf: d + mm(hf, bd(d)), inv, half)
            tick()

        v_bd = each(bd, v_c)
        akv = each(mm, a_ak, v_bd)
        tick()
        wt = each(lambda d, a: mm(d, bd(a)), inv, a_t)
        tick()
        u0 = each(lambda d, x: mm(d, bd(x)), inv, akv)
        tick()
        o0 = each(mm, m_rk, v_bd)
        bh = each(lambda x, p: x * p, kna, p_rem)
        kh = each(lambda x, p: x * p, k_c, p_rem)
        for _ in side:
            pass
        for ref, vals in zip((wt_s, u0_s, o0_s, mrb_s, rt_s, bh_s, kh_s),
                             (wt, u0, o0, m_rb, r_t, bh, kh)):
            for (b, _), sl, val in zip(chains, sls, vals):
                ref[b, sl, :] = val
        for (b, ci), xl in zip(chains, cl_last):
            pc_s[b, pl.ds(ci * SUBLANES, SUBLANES), :] = jnp.broadcast_to(jnp.exp(xl), (SUBLANES, g))

    def phase_b_steps(chunk_ids):
        bs = list(range(nb))
        for ci in chunk_ids:
            sl = pl.ds(ci * c, c)
            ht = [ht_s[b] for b in bs]
            wr = [jnp.concatenate([wt_s[b, sl, :], rt_s[b, sl, :]], axis=0) for b in bs]
            wrh = each(lambda x, hh: mm_nt(x, hh.astype(BF16)), wr, ht)
            yield
            u = [x[0:c] + u0_s[b, sl, :] for x, b in zip(wrh, bs)]
            uv_t = [jnp.concatenate([x, v_s[b, sl, :]], axis=0).T for x, b in zip(u, bs)]
            bk = [jnp.concatenate([bh_s[b, sl, :], kh_s[b, sl, :]], axis=0) for b in bs]
            upd = each(lambda x, y: mm(x, y.astype(BF16)), uv_t, bk)
            mu_ = [mm(mrb_s[b, sl, :], bd(x)) for x, b in zip(u, bs)]
            yield
            for b in bs:
                p_c = pc_s[b, pl.ds(ci * SUBLANES, 1), :]
                y_s[b, sl, :] = wrh[b][c:2 * c] + mu_[b] + o0_s[b, sl, :]
                ht_s[b] = ht[b] * p_c + bdm_f * upd[b]
            yield

    pending = iter(())
    for gi in range(n_chunks // RWKV_UNROLL):
        chunk_ids = range(gi * RWKV_UNROLL, (gi + 1) * RWKV_UNROLL)
        phase_a([(b, ci) for ci in chunk_ids for b in range(nb)], pending)
        pending = phase_b_steps(chunk_ids)
    for _ in pending:
        pass

    inv_n = 1.0 / HEAD_DIM
    for b in range(nb):
        y = y_s[b]
        mean = _dot_sel_rhs(y, bdm) * inv_n
        yc = y - mean
        var = _dot_sel_rhs(yc * yc, bdm) * inv_n
        yn = yc * lax.rsqrt(var + GN_EPS) * lng_ref[...] + lnb_ref[...]
        bonus = _dot_sel_rhs(r_s[b] * k_s[b] * rk_ref[...], bdm) * v_s[b]
        o_ref[b] = yn + bonus


def _rwkv(p_rkv, misc, prm, bdm, *, tt):
    b, s, _ = p_rkv.shape
    g = D_GROUP
    seq = lambda width: pl.BlockSpec((b, tt, width), lambda ti: (0, ti, 0))
    big = pltpu.VMEM((b, tt, g), F32)
    return pl.pallas_call(
        functools.partial(_rwkv_kernel, tt=tt, nb=b),
        grid=(s // tt,),
        in_specs=[seq(N_RKV), seq(LANES)] + [_full_spec(a) for a in prm] + [_full_spec(bdm)],
        out_specs=seq(g),
        out_shape=jax.ShapeDtypeStruct((b, s, g), F32),
        scratch_shapes=[big] * 13
        + [pltpu.VMEM((b, tt // RWKV_CHUNK * SUBLANES, g), F32), big,
           pltpu.VMEM((b, g, g), F32)],
        compiler_params=_params(1),
        name="rwkv7",
    )(p_rkv, misc, *prm, bdm)


def _lru_tile(x_ref, cw_ref, cb_ref, wa_ref, ba_ref, wx_ref, bx_ref, lam_ref, o_ref,
              pad, a_s, u_s, h_s, *, tt, nb):
    @pl.when(pl.program_id(0) == 0)
    def _():
        pad[:, 0:SUBLANES, :] = jnp.zeros((nb, SUBLANES, D_GROUP), F32)
        h_s[...] = jnp.zeros_like(h_s)

    for b in range(nb):
        x = x_ref[b]
        pad[b, SUBLANES:SUBLANES + tt, :] = x
        xc = cw_ref[CONV_WIDTH - 1:CONV_WIDTH, :] * x + cb_ref[...]
        for d in range(1, CONV_WIDTH):
            tap = CONV_WIDTH - 1 - d
            xc = xc + cw_ref[tap:tap + 1, :] * pad[b, SUBLANES - d:SUBLANES - d + tt, :]
        pad[b, 0:SUBLANES, :] = x[tt - SUBLANES:tt, :]

        xb = xc.astype(BF16)
        r = _sigmoid(_dot(xb, wa_ref[...]) + ba_ref[...])
        i = _sigmoid(_dot(xb, wx_ref[...]) + bx_ref[...])
        log_a = -LRU_C * r * _softplus(-lam_ref[...])
        a_s[b] = jnp.exp(log_a)
        th = jnp.tanh(log_a)
        u_s[b] = jnp.sqrt(-2.0 * th / (1.0 - th)) * (i * xc)

    row = _iota((SUBLANES, D_GROUP), 0)

    def rows_above(x, k, fill):
        return jnp.where(row >= k, pltpu.roll(x, k, axis=0), fill)

    def group(gi, hs):
        base = pl.multiple_of(gi * SUBLANES, SUBLANES)
        carry = []
        for b in range(nb):
            a = a_s[b, pl.ds(base, SUBLANES), :]
            u = u_s[b, pl.ds(base, SUBLANES), :]
            k = 1
            while k < SUBLANES:
                u = a * rows_above(u, k, 0.0) + u
                a = a * rows_above(a, k, 1.0)
                k *= 2
            h = a * hs[b] + u
            o_ref[b, pl.ds(base, SUBLANES), :] = h
            carry.append(h[SUBLANES - 1:SUBLANES, :])
        return tuple(carry)

    hs = lax.fori_loop(0, tt // SUBLANES, group, tuple(h_s[b] for b in range(nb)))
    for b in range(nb):
        h_s[b] = hs[b]


def _outproj_kernel(x_ref, yft_ref, yst_ref, yr_ref, lx_ref, gates_ref, w_ref, fg_ref,
                    cw_ref, cb_ref, wa_ref, ba_ref, wx_ref, bx_ref, lam_ref, o_ref,
                    pad, a_s, u_s, yl_s, h_s, *, final, tm, nb):
    _lru_tile(lx_ref, cw_ref, cb_ref, wa_ref, ba_ref, wx_ref, bx_ref, lam_ref, yl_s,
              pad, a_s, u_s, h_s, tt=tm, nb=nb)
    for b in range(nb):
        acc = x_ref[b]
        ys = (yft_ref[b].T, yst_ref[b].T, yr_ref[b], yl_s[b])
        for gi, y in enumerate(ys):
            gate = gates_ref[b, :, gi * D_GROUP:(gi + 1) * D_GROUP]
            y = y * (gate * _sigmoid(gate))
            acc = acc + _dot(y.astype(BF16), w_ref[gi * D_GROUP:(gi + 1) * D_GROUP, :])
        if final:
            ms = jnp.mean(acc * acc, axis=-1, keepdims=True)
            acc = acc * lax.rsqrt(ms + RMS_EPS) * fg_ref[...]
        o_ref[b] = acc


def _outproj(x, y_fox_t, y_sb_t, y_rw, lx, gates, w, final_g, lru_prm, *, tm, final):
    b, s, _ = x.shape
    g = D_GROUP
    seq = lambda width: pl.BlockSpec((b, tm, width), lambda i: (0, i, 0))
    chan = pl.BlockSpec((b, g, tm), lambda i: (0, 0, i))
    big = pltpu.VMEM((b, tm, g), F32)
    return pl.pallas_call(
        functools.partial(_outproj_kernel, final=final, tm=tm, nb=b),
        grid=(s // tm,),
        in_specs=[seq(D_MODEL), chan, chan, seq(g), seq(g), seq(4 * g),
                  _full_spec(w), _full_spec(final_g)] + [_full_spec(a) for a in lru_prm],
        out_specs=seq(D_MODEL),
        out_shape=jax.ShapeDtypeStruct((b, s, D_MODEL), F32),
        scratch_shapes=[pltpu.VMEM((b, tm + SUBLANES, g), F32), big, big, big,
                        pltpu.VMEM((b, 1, g), F32)],
        compiler_params=_params(1),
        name="outproj",
    )(x, y_fox_t, y_sb_t, y_rw, lx, gates, w, final_g, *lru_prm)


def _w_in_segments():
    g, h, r = D_GROUP, N_HEADS, RWKV_LORA
    o_ff = 4 * g
    o_sb = o_ff + h
    o_rw = o_sb + 4 * g
    o_rg = o_rw + 3 * g + 2 * r
    o_lx = o_rg + g
    o_lg = o_lx + g
    segs = [(0, C_FQ, 2 * g), (o_sb, C_SQ, 2 * g), (2 * g, C_FV, g), (o_sb + 2 * g, C_SV, g),
             (3 * g, C_GATES, g), (o_sb + 3 * g, C_GATES + g, g), (o_rg, C_GATES + 2 * g, g),
             (o_lg, C_GATES + 3 * g, g),
             (o_rw, C_RKV, 3 * g), (o_lx, C_LX, g),
             (o_rw + 3 * g, C_MISC, 2 * r), (o_ff, C_MISC + FF_LANE, h)]
    return segs


def _w_in_kernel(w_ref, o_ref):
    o_ref[...] = jnp.zeros_like(o_ref)
    for src, dst, width in _w_in_segments():
        o_ref[0, :, dst:dst + width] = w_ref[0, :, src:src + width].astype(BF16)


def _permute_w_in(w_in, *, tr):
    depth, d, n_in = w_in.shape
    return pl.pallas_call(
        _w_in_kernel,
        grid=(depth, d // tr),
        in_specs=[pl.BlockSpec((1, tr, n_in), lambda l, i: (l, i, 0))],
        out_specs=pl.BlockSpec((1, tr, N_PROJ), lambda l, i: (l, i, 0)),
        out_shape=jax.ShapeDtypeStruct((depth, d, N_PROJ), BF16),
        compiler_params=_params(2),
        name="w_in_layout",
    )(w_in)


def _block_diag(w):
    h, n, _ = w.shape
    eye = jnp.eye(h, dtype=w.dtype)
    return jnp.einsum('hij,hk->hikj', w, eye).reshape(h * n, h * n)


def _pick_tile(s, pref):
    t = pref
    while s % t:
        t //= 2
    return t


def kernel(x, norm_g, w_in, b_forget, rwkv_mu, rwkv_w0, rwkv_w2, rwkv_a0, rwkv_a2, rwkv_k_k,
           rwkv_k_a, rwkv_r_k, rwkv_ln_g, rwkv_ln_b, lru_conv_w, lru_conv_b, lru_w_a, lru_b_a,
           lru_w_x, lru_b_x, lru_lambda, w_out, final_g):
    b, s, d = x.shape
    depth = w_in.shape[0]
    g, h, dh, r = D_GROUP, N_HEADS, HEAD_DIM, RWKV_LORA
    tm = _pick_tile(s, 512)
    tq = _pick_tile(s, 256)
    tk_fox = tq
    tt = _pick_tile(s, 512)
    row = lambda a: a.reshape(1, -1).astype(F32)

    bdm = _block_diag(jnp.ones((h, dh, dh), BF16))
    w_in_k = _permute_w_in(w_in, tr=LANES)
    for l in range(depth):
        fbias = jnp.zeros((1, LANES), F32).at[0, FF_LANE:FF_LANE + h].set(b_forget[l])
        mu = rwkv_mu[l]
        consts = [row(norm_g[l]), w_in_k, fbias, row(mu[:N_RKV]),
                  jnp.zeros((1, LANES), F32).at[0, :2 * r].set(mu[N_RKV:])]
        (fq, fk, qaug, kaug, fvt, sq, sk, svt, gates, rkv, lx, misc, f2,
         norms) = _inproj(x, consts, l, tm=tm)

        per_head = lambda a: a.transpose(0, 2, 1).reshape(-1)
        fend = per_head(f2[:, tq - 1::tq, FF_LANE:FF_LANE + h])
        qmax = per_head(jnp.repeat(jnp.sqrt(norms[:, :, 0, 0:h]), tm // tq, axis=1))
        kmax = jnp.sqrt(jnp.max(norms[:, :, 0, h:2 * h], axis=1)).reshape(-1)
        y_fox_t = _fox_attention(fend, qmax, kmax, fq, qaug, fk, kaug, fvt, tq=tq, tk=tk_fox)
        y_sb_t = _sb_attention(sq, sk, svt, tq=tq)

        pad_rows = lambda a, lo: jnp.zeros((LANES, g), F32).at[lo:lo + r].set(a).astype(BF16)
        rw_prm = [row(rwkv_w0[l]), pad_rows(rwkv_w2[l], 0), row(rwkv_a0[l]), pad_rows(rwkv_a2[l], r),
                  row(rwkv_k_k[l]), row(rwkv_k_a[l]), row(rwkv_r_k[l]), row(rwkv_ln_g[l]),
                  row(rwkv_ln_b[l])]
        y_rw = _rwkv(rkv, misc, rw_prm, bdm, tt=tt)

        lru_prm = [lru_conv_w[l].astype(F32), row(lru_conv_b[l]),
                   _block_diag(lru_w_a[l]).astype(BF16), row(lru_b_a[l]),
                   _block_diag(lru_w_x[l]).astype(BF16), row(lru_b_x[l]), row(lru_lambda[l])]
        x = _outproj(x, y_fox_t, y_sb_t, y_rw, lx, gates, w_out[l].astype(BF16), row(final_g),
                     lru_prm, tm=tm, final=(l == depth - 1))
    return x
```

```python
import functools

import jax
import jax.numpy as jnp
from jax import lax
from jax.experimental import pallas as pl
from jax.experimental.pallas import tpu as pltpu

F32 = jnp.float32
BF16 = jnp.bfloat16

D_MODEL = 1024
D_GROUP = 256
N_HEADS = 4
HEAD_DIM = 64
RWKV_LORA = 32
CONV_WIDTH = 4
LRU_C = 8.0
RMS_EPS = 1e-6
GN_EPS = 64e-5
N_RKV = 3 * D_GROUP

LANES = 128
SUBLANES = 8
VMEM_LIMIT_BYTES = 56 * 1024 * 1024

C_FQ = 0
C_FK = C_FQ + D_GROUP
C_SQ = C_FK + D_GROUP
C_SK = C_SQ + D_GROUP
C_FV = C_SK + D_GROUP
C_SV = C_FV + D_GROUP
C_GATES = C_SV + D_GROUP
C_RKV = C_GATES + 4 * D_GROUP
C_LX = C_RKV + N_RKV
C_MISC = C_LX + D_GROUP
N_PROJ = C_MISC + LANES
FF_LANE = 2 * RWKV_LORA
HEADS_PER_PAIR = LANES // HEAD_DIM
N_PAIRS = N_HEADS // HEADS_PER_PAIR
AUG_STRIDE = 8
AUG_K_F = 0
AUG_K_ONE = 3
AUG_Q_ONE = 0
AUG_Q_F = 3

CUMSUM_BLOCK = LANES
V_AUG = HEAD_DIM + 16
LOG2E = 1.4426950408889634

FOX_SKIP_LOG2 = -150.0
FOX_NORM_MARGIN = 1.02
FOX_F_SLACK = 1.0

NEG_BIG = -1e30
SB_SKIP_LOG2 = -150.0
RWKV_CHUNK = 64
RWKV_UNROLL = 4


def _dot(a, b):
    return jnp.dot(a, b, preferred_element_type=F32)


def _dot_nt(a, b):
    return lax.dot_general(a, b, (((1,), (1,)), ((), ())), preferred_element_type=F32)


def _split3(x):
    hi = x.astype(BF16)
    r1 = x - hi.astype(F32)
    mid = r1.astype(BF16)
    lo = (r1 - mid.astype(F32)).astype(BF16)
    return hi, mid, lo


def _dot_sel_lhs(sel, x):
    hi, mid, lo = _split3(x)
    return _dot(sel, hi) + (_dot(sel, mid) + _dot(sel, lo))


def _dot_sel_rhs(x, sel):
    return _dot(x.astype(BF16), sel)


def _softplus(x):
    return jnp.maximum(x, 0.0) + jnp.log(1.0 + jnp.exp(-jnp.abs(x)))


def _log_sigmoid(x):
    return jnp.minimum(x, 0.0) - jnp.log(1.0 + jnp.exp(-jnp.abs(x)))


def _log2_sigmoid_of_log2(x2):
    return jnp.minimum(x2, 0.0) - jnp.log2(1.0 + jnp.exp2(-jnp.abs(x2)))


def _sigmoid(x):
    return 1.0 / (1.0 + jnp.exp(-x))


def _iota(shape, dim):
    return lax.broadcasted_iota(jnp.int32, shape, dim)


def _full_spec(a):
    return pl.BlockSpec(a.shape, lambda *_: (0,) * a.ndim)


def _params(n_grid):
    return pltpu.CompilerParams(dimension_semantics=("arbitrary",) * n_grid,
                                vmem_limit_bytes=VMEM_LIMIT_BYTES)


def _inproj_kernel(x_ref, g_ref, w_ref, fb_ref, mu_rkv_ref, mu_misc_ref,
                   fq_ref, fk_ref, qaug_ref, kaug_ref, fvt_ref, sq_ref, sk_ref, svt_ref, gates_ref,
                   rkv_ref, lx_ref, misc_ref, f2_ref, norms_ref, ftot, v_stage, pad_rkv, pad_misc):
    @pl.when(pl.program_id(1) == 0)
    def _():
        ftot[...] = jnp.zeros_like(ftot)
        pad_rkv[0:SUBLANES, :] = jnp.zeros((SUBLANES, N_RKV), F32)
        pad_misc[0:SUBLANES, :] = jnp.zeros((SUBLANES, LANES), F32)

    x = x_ref[0]
    tm = x.shape[0]
    ms = jnp.mean(x * x, axis=-1, keepdims=True)
    h = (x * lax.rsqrt(ms + RMS_EPS) * g_ref[...]).astype(BF16)
    scale = HEAD_DIM ** -0.5

    def proj(c0, width):
        return _dot(h, w_ref[0, :, c0:c0 + width])

    misc = proj(C_MISC, LANES)
    gates_ref[0] = proj(C_GATES, 4 * D_GROUP)
    lx_ref[0] = proj(C_LX, D_GROUP)

    for val, pad, mu_ref, out_ref in ((proj(C_RKV, N_RKV), pad_rkv, mu_rkv_ref, rkv_ref),
                                      (misc, pad_misc, mu_misc_ref, misc_ref)):
        pad[SUBLANES:SUBLANES + tm, :] = val
        prev = pad[SUBLANES - 1:SUBLANES - 1 + tm, :]
        pad[0:SUBLANES, :] = val[tm - SUBLANES:tm, :]
        out_ref[0] = val + (prev - val) * mu_ref[...]

    lf = _log_sigmoid(misc + fb_ref[...])
    cb = CUMSUM_BLOCK
    lower = (_iota((cb, cb), 1) <= _iota((cb, cb), 0)).astype(BF16)
    run = ftot[...]
    blocks = []
    for r0 in range(0, tm, cb):
        blk = _dot_sel_lhs(lower, lf[r0:r0 + cb, :]) + run
        run = blk[cb - 1:cb, :]
        blocks.append(blk)
    ftot[...] = run
    f = jnp.concatenate(blocks, axis=0)
    f2 = f * LOG2E
    f2_ref[0] = f2
    hi = f2.astype(BF16).astype(F32)
    mid = (f2 - hi).astype(BF16).astype(F32)
    lo = (f2 - hi - mid).astype(BF16).astype(F32)

    pq = proj(C_FQ, D_GROUP) * (scale * LOG2E)
    pk = proj(C_FK, D_GROUP)
    sq = proj(C_SQ, D_GROUP) * (scale * LOG2E)
    sk = proj(C_SK, D_GROUP)
    for pr in range(N_PAIRS):
        cols = slice(pr * LANES, (pr + 1) * LANES)
        fq_ref[0, pr] = pq[:, cols].astype(BF16)
        fk_ref[0, pr] = pk[:, cols].astype(BF16)
        sq_ref[0, pr] = sq[:, cols].astype(BF16)
        sk_ref[0, pr] = sk[:, cols].astype(BF16)

    lane = _iota((tm, LANES), 1)
    in_group = lane % AUG_STRIDE
    valid = lane < AUG_STRIDE * N_HEADS
    aug_q = jnp.where(jnp.logical_and(valid, jnp.logical_and(in_group >= AUG_Q_ONE,
                                                             in_group < AUG_Q_ONE + 3)), 1.0, 0.0)
    aug_k = jnp.where(jnp.logical_and(valid, jnp.logical_and(in_group >= AUG_K_ONE,
                                                             in_group < AUG_K_ONE + 3)), 1.0, 0.0)
    for hd in range(N_HEADS):
        for i3, piece in enumerate((hi, mid, lo)):
            col = jnp.broadcast_to(piece[:, FF_LANE + hd:FF_LANE + hd + 1], (tm, LANES))
            aug_q = jnp.where(lane == AUG_STRIDE * hd + AUG_Q_F + i3, col, aug_q)
            aug_k = jnp.where(lane == AUG_STRIDE * hd + AUG_K_F + i3, -col, aug_k)
    qaug_ref[0] = aug_q.astype(BF16)
    kaug_ref[0] = aug_k.astype(BF16)

    head_of_row = _iota((D_GROUP, LANES), 0) // HEAD_DIM
    nlane = _iota((SUBLANES, LANES), 1)
    norms = jnp.zeros((SUBLANES, LANES), F32)
    for side, pv in enumerate((pq, pk)):
        sel = (head_of_row + side * N_HEADS == _iota((D_GROUP, LANES), 1)).astype(BF16)
        top = jnp.max(_dot((pv * pv).astype(BF16), sel), axis=0, keepdims=True)
        keep = jnp.logical_and(nlane >= side * N_HEADS, nlane < (side + 1) * N_HEADS)
        norms = jnp.where(keep, top, norms)
    norms_ref[0, 0] = norms
    v_stage[...] = proj(C_SV, D_GROUP)
    svt_ref[0] = v_stage[...].T.astype(BF16)
    v_stage[...] = proj(C_FV, D_GROUP)
    vt = v_stage[...].T.astype(BF16)
    ones_rows = (_iota((V_AUG - HEAD_DIM, tm), 0) == 0).astype(BF16)
    for hd in range(N_HEADS):
        fvt_ref[0, hd * V_AUG:hd * V_AUG + HEAD_DIM, :] = vt[hd * HEAD_DIM:(hd + 1) * HEAD_DIM, :]
        fvt_ref[0, hd * V_AUG + HEAD_DIM:(hd + 1) * V_AUG, :] = ones_rows


def _inproj(x, consts, layer, *, tm):
    b, s, _ = x.shape
    w_spec = pl.BlockSpec((1,) + consts[1].shape[1:], lambda bi, i: (layer, 0, 0))
    seq = lambda width: pl.BlockSpec((1, tm, width), lambda bi, i: (bi, i, 0))
    pairs = pl.BlockSpec((1, N_PAIRS, tm, LANES), lambda bi, i: (bi, 0, i, 0))
    pairs_shape = jax.ShapeDtypeStruct((b, N_PAIRS, s, LANES), BF16)
    chan = lambda rows: pl.BlockSpec((1, rows, tm), lambda bi, i: (bi, 0, i))
    chan_shape = lambda rows: jax.ShapeDtypeStruct((b, rows, s), BF16)
    seq_shape = lambda width, dtype=F32: jax.ShapeDtypeStruct((b, s, width), dtype)
    return pl.pallas_call(
        _inproj_kernel,
        grid=(b, s // tm),
        in_specs=[seq(D_MODEL), _full_spec(consts[0]), w_spec] + [_full_spec(a) for a in consts[2:]],
        out_specs=[pairs, pairs, seq(LANES), seq(LANES), chan(N_HEADS * V_AUG),
                   pairs, pairs, chan(D_GROUP),
                   seq(4 * D_GROUP), seq(N_RKV), seq(D_GROUP), seq(LANES), seq(LANES),
                   pl.BlockSpec((1, 1, SUBLANES, LANES), lambda bi, i: (bi, i, 0, 0))],
        out_shape=[pairs_shape, pairs_shape, seq_shape(LANES, BF16), seq_shape(LANES, BF16),
                   chan_shape(N_HEADS * V_AUG), pairs_shape, pairs_shape, chan_shape(D_GROUP),
                   seq_shape(4 * D_GROUP), seq_shape(N_RKV), seq_shape(D_GROUP), seq_shape(LANES),
                   seq_shape(LANES), jax.ShapeDtypeStruct((b, s // tm, SUBLANES, LANES), F32)],
        scratch_shapes=[pltpu.VMEM((1, LANES), F32), pltpu.VMEM((tm, D_GROUP), F32),
                        pltpu.VMEM((tm + SUBLANES, N_RKV), F32),
                        pltpu.VMEM((tm + SUBLANES, LANES), F32)],
        compiler_params=_params(2),
        name="inproj",
    )(x, *consts)


def _fox_kernel(fend_ref, qmax_ref, kmax_ref, q_ref, qaug_ref, k_ref, kaug_ref, vt_ref, o_ref,
                s_scr, p_scr, mx_scr, al_scr, m_scr, acc_scr, *, tq, tk, nb):
    pair = pl.program_id(0)
    i = pl.program_id(1)
    nq = pl.num_programs(1)
    chains = [(b, hh) for b in range(nb) for hh in range(HEADS_PER_PAIR)]
    ids = list(range(len(chains)))
    lane = _iota((tq, LANES), 1)
    q = []
    for b, hh in chains:
        own_half = lane // HEAD_DIM == hh
        own_aug = lane // AUG_STRIDE == pair * HEADS_PER_PAIR + hh
        q.append(jnp.concatenate(
            [jnp.where(own_half, q_ref[b, 0], jnp.zeros((), BF16)),
             jnp.where(own_aug, qaug_ref[b], jnp.zeros((), BF16))], axis=1))
    n_full = (i * tq) // tk

    bhs = [b * N_HEADS + pair * HEADS_PER_PAIR + hh for b, hh in chains]
    rows = [bh * nq for bh in bhs]
    slack = [FOX_NORM_MARGIN * 2.0 * qmax_ref[r + i] * kmax_ref[bh]
             + fend_ref[r + jnp.maximum(i - 1, 0)] + FOX_F_SLACK - FOX_SKIP_LOG2
             for bh, r in zip(bhs, rows)]

    def some_chain_live(j):
        last_key = (jnp.maximum(j, 0) + 1) * (tk // tq) - 1
        live = [sl >= fend_ref[r + last_key] for sl, r in zip(slack, rows)]
        return jnp.logical_and(j >= 0, functools.reduce(jnp.logical_or, live))

    base = lax.while_loop(some_chain_live, lambda j: j - 1, n_full - 1) + 1
    n_eff = n_full - base

    def key_rows(j):
        return pl.ds(pl.multiple_of(j * tk, tk), tk)

    def scores_to(j, slot):
        kp = [jnp.concatenate([k_ref[b, 0, key_rows(j), :], kaug_ref[b, key_rows(j), :]], axis=1)
              for b in range(nb)]
        s = [_dot_nt(kp[b], q[c]) for c, (b, _) in zip(ids, chains)]
        for c in ids:
            s_scr[c, slot] = s[c]
            mx_scr[c, slot] = jnp.max(s[c], axis=0, keepdims=True)

    def weighted_values(j, slot):
        return [_dot(vt_ref[b, hh * V_AUG:(hh + 1) * V_AUG, key_rows(j)], p_scr[c, slot])
                for c, (b, hh) in zip(ids, chains)]

    def softmax_to(s, mx, slot):
        m_old = [m_scr[c] for c in ids]
        m_new = [jnp.maximum(a, b) for a, b in zip(m_old, mx)]
        p = [jnp.exp2(a - b).astype(BF16) for a, b in zip(s, m_new)]
        for c in ids:
            m_scr[c] = m_new[c]
            al_scr[c, slot] = jnp.exp2(m_old[c] - m_new[c])
            p_scr[c, slot] = p[c]

    def stage(local, cur):
        j = base + local
        nxt = 1 - cur
        pv_prev = weighted_values(jnp.maximum(j - 1, 0), nxt)
        scores_to(j + 1, nxt)
        softmax_to([s_scr[c, cur] for c in ids], [mx_scr[c, cur] for c in ids], cur)
        for c in ids:
            acc_scr[c] = al_scr[c, nxt] * acc_scr[c] + pv_prev[c]

    def tail(cur):
        nxt = 1 - cur
        pv_prev = weighted_values(jnp.maximum(n_full - 1, 0), nxt)
        mask = n_full * tk + _iota((tk, tq), 0) <= i * tq + _iota((tk, tq), 1)
        s = [jnp.where(mask, s_scr[c, cur], NEG_BIG) for c in ids]
        softmax_to(s, [jnp.max(x, axis=0, keepdims=True) for x in s], cur)
        pv_last = weighted_values(n_full, cur)
        for c, (b, hh) in zip(ids, chains):
            acc = al_scr[c, cur] * (al_scr[c, nxt] * acc_scr[c] + pv_prev[c]) + pv_last[c]
            o_ref[b, hh * HEAD_DIM:(hh + 1) * HEAD_DIM, :] = (
                acc[0:HEAD_DIM] / acc[HEAD_DIM:HEAD_DIM + 1])

    m_scr[...] = jnp.full(m_scr.shape, NEG_BIG, F32)
    acc_scr[...] = jnp.zeros_like(acc_scr)
    p_scr[:, 1] = jnp.zeros((len(chains), tk, tq), BF16)
    al_scr[:, 1] = jnp.ones((len(chains), 1, tq), F32)
    scores_to(base, 0)

    def pair_of_stages(jj, carry):
        stage(2 * jj, 0)
        stage(2 * jj + 1, 1)
        return carry

    lax.fori_loop(0, n_eff // 2, pair_of_stages, 0)

    @pl.when(n_eff % 2 == 1)
    def _():
        stage(n_eff - 1, 0)
        tail(1)

    @pl.when(n_eff % 2 == 0)
    def _():
        tail(0)


def _fox_attention(fend, qmax, kmax, q, qaug, k, kaug, vt, *, tq, tk):
    b, _, s, _ = q.shape
    n_chains = b * HEADS_PER_PAIR
    pair_rows = HEADS_PER_PAIR * V_AUG
    grid_spec = pltpu.PrefetchScalarGridSpec(
        num_scalar_prefetch=3,
        grid=(N_PAIRS, s // tq),
        in_specs=[pl.BlockSpec((b, 1, tq, LANES), lambda p, i, *_: (0, p, i, 0)),
                  pl.BlockSpec((b, tq, LANES), lambda p, i, *_: (0, i, 0)),
                  pl.BlockSpec((b, 1, s, LANES), lambda p, i, *_: (0, p, 0, 0)),
                  pl.BlockSpec((b, s, LANES), lambda p, i, *_: (0, 0, 0)),
                  pl.BlockSpec((b, pair_rows, s), lambda p, i, *_: (0, p, 0))],
        out_specs=pl.BlockSpec((b, HEADS_PER_PAIR * HEAD_DIM, tq), lambda p, i, *_: (0, p, i)),
        scratch_shapes=[pltpu.VMEM((n_chains, 2, tk, tq), F32), pltpu.VMEM((n_chains, 2, tk, tq), BF16),
                        pltpu.VMEM((n_chains, 2, 1, tq), F32), pltpu.VMEM((n_chains, 2, 1, tq), F32),
                        pltpu.VMEM((n_chains, 1, tq), F32), pltpu.VMEM((n_chains, V_AUG, tq), F32)])
    return pl.pallas_call(
        functools.partial(_fox_kernel, tq=tq, tk=tk, nb=b),
        grid_spec=grid_spec,
        out_shape=jax.ShapeDtypeStruct((b, N_HEADS * HEAD_DIM, s), F32),
        compiler_params=_params(2),
        name="fox_attention",
    )(fend, qmax, kmax, q, qaug, k, kaug, vt)


def _sb_kernel(q_ref, k_ref, vt_ref, o_ref, *, tq):
    i = pl.program_id(1)
    tk = tq
    heads = list(range(N_HEADS))
    lane = _iota((tq, LANES), 1)
    q = [jnp.where(lane // HEAD_DIM == hd % HEADS_PER_PAIR, q_ref[0, hd // HEADS_PER_PAIR],
                   jnp.zeros((), BF16)) for hd in heads]
    later = (_iota((tk, tk), 0) < _iota((tk, tk), 1)).astype(BF16)

    def each(f, *xs):
        return [f(*a) for a in zip(*xs)]

    def block(j, rest_q, acc, masked):
        ks = pl.multiple_of(j * tk, tk)
        kp = [k_ref[0, pr, pl.ds(ks, tk), :] for pr in range(N_PAIRS)]
        z = [_dot_nt(kp[hd // HEADS_PER_PAIR], q[hd]) for hd in heads]
        log_keep = each(lambda x: _log2_sigmoid_of_log2(-x), z)
        if masked:
            mask = _iota((tk, tq), 0) < _iota((tk, tq), 1)
            log_keep = each(lambda x: jnp.where(mask, x, 0.0), log_keep)
        rest_in = each(lambda x: _dot(later, x.astype(BF16)), log_keep)
        att = each(lambda x, lk, ri, rq: jnp.exp2(x + lk + ri + rq), z, log_keep, rest_in, rest_q)
        if masked:
            att = each(lambda x: jnp.where(mask, x, 0.0), att)
        pv = [_dot(vt_ref[0, hd * HEAD_DIM:(hd + 1) * HEAD_DIM, pl.ds(ks, tk)],
                   att[hd].astype(BF16)) for hd in heads]
        acc = each(lambda a, x: a + x, acc, pv)
        rest_q = each(lambda rq, ri, lk: rq + ri[0:1, :] + lk[0:1, :], rest_q, rest_in, log_keep)
        return rest_q, acc

    rest_q, acc = block(i, [jnp.zeros((1, tq), F32)] * N_HEADS,
                        [jnp.zeros((HEAD_DIM, tq), F32)] * N_HEADS, True)

    def cond(c):
        j, rest_q, _ = c
        alive = functools.reduce(jnp.maximum, rest_q)
        return jnp.logical_and(j >= 0, jnp.max(alive) > SB_SKIP_LOG2)

    def body(c):
        j, rest_q, acc = c
        rest_q, acc = block(j, list(rest_q), list(acc), False)
        return j - 1, tuple(rest_q), tuple(acc)

    _, _, acc = lax.while_loop(cond, body, (i - 1, tuple(rest_q), tuple(acc)))
    o_ref[0] = jnp.concatenate(list(acc), axis=0)


def _sb_attention(q, k, vt, *, tq):
    b, _, s, _ = q.shape
    h = N_HEADS
    return pl.pallas_call(
        functools.partial(_sb_kernel, tq=tq),
        grid=(b, s // tq),
        in_specs=[pl.BlockSpec((1, N_PAIRS, tq, LANES), lambda bi, i: (bi, 0, i, 0)),
                  pl.BlockSpec((1, N_PAIRS, s, LANES), lambda bi, i: (bi, 0, 0, 0)),
                  pl.BlockSpec((1, h * HEAD_DIM, s), lambda bi, i: (bi, 0, 0))],
        out_specs=pl.BlockSpec((1, h * HEAD_DIM, tq), lambda bi, i: (bi, 0, i)),
        out_shape=jax.ShapeDtypeStruct((b, h * HEAD_DIM, s), F32),
        compiler_params=_params(2),
        name="sb_attention",
    )(q, k, vt)


def _rwkv_kernel(p_ref, m_ref, w0_ref, w2_ref, a0_ref, a2_ref,
                 kk_ref, ka_ref, rk_ref, lng_ref, lnb_ref, bdm_ref, o_ref,
                 r_s, k_s, v_s, kn_s, al_s, lw_s,
                 wt_s, u0_s, o0_s, mrb_s, rt_s, bh_s, kh_s, pc_s, y_s, ht_s, *, tt, nb):
    c = RWKV_CHUNK
    g = D_GROUP
    n_chunks = tt // c

    @pl.when(pl.program_id(0) == 0)
    def _():
        ht_s[...] = jnp.zeros_like(ht_s)

    bdm = bdm_ref[...]
    bdm_f = bdm.astype(F32)

    for b in range(nb):
        p = p_ref[b]
        misc = m_ref[b]
        k = p[:, g:2 * g]

        w = -_softplus(-(w0_ref[...] + _dot(jnp.tanh(misc).astype(BF16), w2_ref[...]))) - 0.5
        alpha = _sigmoid(a0_ref[...] + _dot(misc.astype(BF16), a2_ref[...]))
        kn = k * kk_ref[...]
        ss = _dot_sel_rhs(kn * kn, bdm)
        r_s[b] = p[:, 0:g]
        k_s[b] = k * (1.0 + (alpha - 1.0) * ka_ref[...])
        v_s[b] = p[:, 2 * g:3 * g]
        kn_s[b] = kn * lax.rsqrt(jnp.maximum(ss, 1e-12))
        al_s[b] = alpha
        lw_s[b] = -jnp.exp(w)

    row = _iota((c, g), 0)
    col = _iota((c, g), 1) % c
    strict = col < row
    incl = col <= row
    eye = (col == row).astype(F32)
    lower_c = (_iota((c, c), 1) <= _iota((c, c), 0)).astype(BF16)
    level_masks = []
    m = 1
    while m < c:
        level_masks.append(jnp.logical_and(
            strict, jnp.logical_and(row // (2 * m) == col // (2 * m), row // m != col // m)))
        m *= 2

    def bd(x):
        return jnp.concatenate([x.astype(BF16)] * N_HEADS, axis=0) * bdm

    def mm(a, b_bf16):
        return _dot(a.astype(BF16), b_bf16)

    def mm_nt(a, b_bf16):
        return _dot_nt(a.astype(BF16), b_bf16)

    def each(f, *xs):
        return [f(*a) for a in zip(*xs)]

    def phase_a(chains, side):
        def tick():
            next(side, None)

        sls = [pl.ds(ci * c, c) for _, ci in chains]
        ld = lambda ref: [ref[b, sl, :] for (b, _), sl in zip(chains, sls)]
        r_c, k_c, v_c, kn_c, al_c, lw_c = (ld(s) for s in (r_s, k_s, v_s, kn_s, al_s, lw_s))
        cl = each(lambda x: _dot_sel_lhs(lower_c, x), lw_c)
        tick()
        cl_last = each(lambda x: x[c - 1:c, :], cl)
        a_t = each(lambda kn, x, lw: -kn * jnp.exp(x - lw), kn_c, cl, lw_c)
        r_t = each(lambda r, x: r * jnp.exp(x), r_c, cl)
        q_inv = each(lambda x: jnp.exp(-x), cl)
        p_rem = each(lambda xl, x: jnp.exp(xl - x), cl_last, cl)
        kna = each(lambda kn, al: kn * al, kn_c, al_c)
        ar = each(lambda a, r: jnp.concatenate([a, r], axis=0), a_t, r_t)
        s_b = each(lambda x, kb, qi: mm_nt(x, bd(kb * qi)), ar, kna, q_inv)
        tick()
        s_k = each(lambda x, kk, qi: mm_nt(x, bd(kk * qi)), ar, k_c, q_inv)
        tick()
        n = each(lambda x: jnp.where(strict, x[0:c], 0.0), s_b)
        a_ak = each(lambda x: jnp.where(strict, x[0:c], 0.0), s_k)
        m_rb = each(lambda x: jnp.where(incl, x[c:2 * c], 0.0), s_b)
        m_rk = each(lambda x: jnp.where(incl, x[c:2 * c], 0.0), s_k)

        inv = each(lambda x: eye + jnp.where(level_masks[0], x, 0.0), n)
        for lm in level_masks[1:]:
            half = each(lambda d, x: mm(d, bd(jnp.where(lm, x, 0.0))), inv, n)
            tick()
            inv = each(lambda d, hf: d + mm(hf, bd(d)), inv, half)
            tick()

        v_bd = each(bd, v_c)
        akv = each(mm, a_ak, v_bd)
        tick()
        wt = each(lambda d, a: mm(d, bd(a)), inv, a_t)
        tick()
        u0 = each(lambda d, x: mm(d, bd(x)), inv, akv)
        tick()
        o0 = each(mm, m_rk, v_bd)
        bh = each(lambda x, p: x * p, kna, p_rem)
        kh = each(lambda x, p: x * p, k_c, p_rem)
        for _ in side:
            pass
        for ref, vals in zip((wt_s, u0_s, o0_s, mrb_s, rt_s, bh_s, kh_s),
                             (wt, u0, o0, m_rb, r_t, bh, kh)):
            for (b, _), sl, val in zip(chains, sls, vals):
                ref[b, sl, :] = val
        for (b, ci), xl in zip(chains, cl_last):
            pc_s[b, pl.ds(ci * SUBLANES, SUBLANES), :] = jnp.broadcast_to(jnp.exp(xl), (SUBLANES, g))

    def phase_b_steps(chunk_ids):
        bs = list(range(nb))
        for ci in chunk_ids:
            sl = pl.ds(ci * c, c)
            ht = [ht_s[b] for b in bs]
            wr = [jnp.concatenate([wt_s[b, sl, :], rt_s[b, sl, :]], axis=0) for b in bs]
            wrh = each(lambda x, hh: mm_nt(x, hh.astype(BF16)), wr, ht)
            yield
            u = [x[0:c] + u0_s[b, sl, :] for x, b in zip(wrh, bs)]
            uv_t = [jnp.concatenate([x, v_s[b, sl, :]], axis=0).T for x, b in zip(u, bs)]
            bk = [jnp.concatenate([bh_s[b, sl, :], kh_s[b, sl, :]], axis=0) for b in bs]
            upd = each(lambda x, y: mm(x, y.astype(BF16)), uv_t, bk)
            mu_ = [mm(mrb_s[b, sl, :], bd(x)) for x, b in zip(u, bs)]
            yield
            for b in bs:
                p_c = pc_s[b, pl.ds(ci * SUBLANES, 1), :]
                y_s[b, sl, :] = wrh[b][c:2 * c] + mu_[b] + o0_s[b, sl, :]
                ht_s[b] = ht[b] * p_c + bdm_f * upd[b]
            yield

    pending = iter(())
    for gi in range(n_chunks // RWKV_UNROLL):
        chunk_ids = range(gi * RWKV_UNROLL, (gi + 1) * RWKV_UNROLL)
        phase_a([(b, ci) for ci in chunk_ids for b in range(nb)], pending)
        pending = phase_b_steps(chunk_ids)
    for _ in pending:
        pass

    inv_n = 1.0 / HEAD_DIM
    for b in range(nb):
        y = y_s[b]
        mean = _dot_sel_rhs(y, bdm) * inv_n
        yc = y - mean
        var = _dot_sel_rhs(yc * yc, bdm) * inv_n
        yn = yc * lax.rsqrt(var + GN_EPS) * lng_ref[...] + lnb_ref[...]
        bonus = _dot_sel_rhs(r_s[b] * k_s[b] * rk_ref[...], bdm) * v_s[b]
        o_ref[b] = yn + bonus


def _rwkv(p_rkv, misc, prm, bdm, *, tt):
    b, s, _ = p_rkv.shape
    g = D_GROUP
    seq = lambda width: pl.BlockSpec((b, tt, width), lambda ti: (0, ti, 0))
    big = pltpu.VMEM((b, tt, g), F32)
    return pl.pallas_call(
        functools.partial(_rwkv_kernel, tt=tt, nb=b),
        grid=(s // tt,),
        in_specs=[seq(N_RKV), seq(LANES)] + [_full_spec(a) for a in prm] + [_full_spec(bdm)],
        out_specs=seq(g),
        out_shape=jax.ShapeDtypeStruct((b, s, g), F32),
        scratch_shapes=[big] * 13
        + [pltpu.VMEM((b, tt // RWKV_CHUNK * SUBLANES, g), F32), big,
           pltpu.VMEM((b, g, g), F32)],
        compiler_params=_params(1),
        name="rwkv7",
    )(p_rkv, misc, *prm, bdm)


def _lru_tile(x_ref, cw_ref, cb_ref, wa_ref, ba_ref, wx_ref, bx_ref, lam_ref, o_ref,
              pad, a_s, u_s, h_s, *, tt, nb):
    @pl.when(pl.program_id(0) == 0)
    def _():
        pad[:, 0:SUBLANES, :] = jnp.zeros((nb, SUBLANES, D_GROUP), F32)
        h_s[...] = jnp.zeros_like(h_s)

    for b in range(nb):
        x = x_ref[b]
        pad[b, SUBLANES:SUBLANES + tt, :] = x
        xc = cw_ref[CONV_WIDTH - 1:CONV_WIDTH, :] * x + cb_ref[...]
        for d in range(1, CONV_WIDTH):
            tap = CONV_WIDTH - 1 - d
            xc = xc + cw_ref[tap:tap + 1, :] * pad[b, SUBLANES - d:SUBLANES - d + tt, :]
        pad[b, 0:SUBLANES, :] = x[tt - SUBLANES:tt, :]

        xb = xc.astype(BF16)
        r = _sigmoid(_dot(xb, wa_ref[...]) + ba_ref[...])
        i = _sigmoid(_dot(xb, wx_ref[...]) + bx_ref[...])
        log_a = -LRU_C * r * _softplus(-lam_ref[...])
        a_s[b] = jnp.exp(log_a)
        th = jnp.tanh(log_a)
        u_s[b] = jnp.sqrt(-2.0 * th / (1.0 - th)) * (i * xc)

    row = _iota((SUBLANES, D_GROUP), 0)

    def rows_above(x, k, fill):
        return jnp.where(row >= k, pltpu.roll(x, k, axis=0), fill)

    def group(gi, hs):
        base = pl.multiple_of(gi * SUBLANES, SUBLANES)
        carry = []
        for b in range(nb):
            a = a_s[b, pl.ds(base, SUBLANES), :]
            u = u_s[b, pl.ds(base, SUBLANES), :]
            k = 1
            while k < SUBLANES:
                u = a * rows_above(u, k, 0.0) + u
                a = a * rows_above(a, k, 1.0)
                k *= 2
            h = a * hs[b] + u
            o_ref[b, pl.ds(base, SUBLANES), :] = h
            carry.append(h[SUBLANES - 1:SUBLANES, :])
        return tuple(carry)

    hs = lax.fori_loop(0, tt // SUBLANES, group, tuple(h_s[b] for b in range(nb)))
    for b in range(nb):
        h_s[b] = hs[b]


def _outproj_kernel(x_ref, yft_ref, yst_ref, yr_ref, lx_ref, gates_ref, w_ref, fg_ref,
                    cw_ref, cb_ref, wa_ref, ba_ref, wx_ref, bx_ref, lam_ref, o_ref,
                    pad, a_s, u_s, yl_s, h_s, *, final, tm, nb):
    _lru_tile(lx_ref, cw_ref, cb_ref, wa_ref, ba_ref, wx_ref, bx_ref, lam_ref, yl_s,
              pad, a_s, u_s, h_s, tt=tm, nb=nb)
    for b in range(nb):
        acc = x_ref[b]
        ys = (yft_ref[b].T, yst_ref[b].T, yr_ref[b], yl_s[b])
        for gi, y in enumerate(ys):
            gate = gates_ref[b, :, gi * D_GROUP:(gi + 1) * D_GROUP]
            y = y * (gate * _sigmoid(gate))
            acc = acc + _dot(y.astype(BF16), w_ref[gi * D_GROUP:(gi + 1) * D_GROUP, :])
        if final:
            ms = jnp.mean(acc * acc, axis=-1, keepdims=True)
            acc = acc * lax.rsqrt(ms + RMS_EPS) * fg_ref[...]
        o_ref[b] = acc


def _outproj(x, y_fox_t, y_sb_t, y_rw, lx, gates, w, final_g, lru_prm, *, tm, final):
    b, s, _ = x.shape
    g = D_GROUP
    seq = lambda width: pl.BlockSpec((b, tm, width), lambda i: (0, i, 0))
    chan = pl.BlockSpec((b, g, tm), lambda i: (0, 0, i))
    big = pltpu.VMEM((b, tm, g), F32)
    return pl.pallas_call(
        functools.partial(_outproj_kernel, final=final, tm=tm, nb=b),
        grid=(s // tm,),
        in_specs=[seq(D_MODEL), chan, chan, seq(g), seq(g), seq(4 * g),
                  _full_spec(w), _full_spec(final_g)] + [_full_spec(a) for a in lru_prm],
        out_specs=seq(D_MODEL),
        out_shape=jax.ShapeDtypeStruct((b, s, D_MODEL), F32),
        scratch_shapes=[pltpu.VMEM((b, tm + SUBLANES, g), F32), big, big, big,
                        pltpu.VMEM((b, 1, g), F32)],
        compiler_params=_params(1),
        name="outproj",
    )(x, y_fox_t, y_sb_t, y_rw, lx, gates, w, final_g, *lru_prm)


def _w_in_segments():
    g, h, r = D_GROUP, N_HEADS, RWKV_LORA
    o_ff = 4 * g
    o_sb = o_ff + h
    o_rw = o_sb + 4 * g
    o_rg = o_rw + 3 * g + 2 * r
    o_lx = o_rg + g
    o_lg = o_lx + g
    segs = [(0, C_FQ, 2 * g), (o_sb, C_SQ, 2 * g), (2 * g, C_FV, g), (o_sb + 2 * g, C_SV, g),
             (3 * g, C_GATES, g), (o_sb + 3 * g, C_GATES + g, g), (o_rg, C_GATES + 2 * g, g),
             (o_lg, C_GATES + 3 * g, g),
             (o_rw, C_RKV, 3 * g), (o_lx, C_LX, g),
             (o_rw + 3 * g, C_MISC, 2 * r), (o_ff, C_MISC + FF_LANE, h)]
    return segs


def _w_in_kernel(w_ref, o_ref):
    o_ref[...] = jnp.zeros_like(o_ref)
    for src, dst, width in _w_in_segments():
        o_ref[0, :, dst:dst + width] = w_ref[0, :, src:src + width].astype(BF16)


def _permute_w_in(w_in, *, tr):
    depth, d, n_in = w_in.shape
    return pl.pallas_call(
        _w_in_kernel,
        grid=(depth, d // tr),
        in_specs=[pl.BlockSpec((1, tr, n_in), lambda l, i: (l, i, 0))],
        out_specs=pl.BlockSpec((1, tr, N_PROJ), lambda l, i: (l, i, 0)),
        out_shape=jax.ShapeDtypeStruct((depth, d, N_PROJ), BF16),
        compiler_params=_params(2),
        name="w_in_layout",
    )(w_in)


def _block_diag(w):
    h, n, _ = w.shape
    eye = jnp.eye(h, dtype=w.dtype)
    return jnp.einsum('hij,hk->hikj', w, eye).reshape(h * n, h * n)


def _pick_tile(s, pref):
    t = pref
    while s % t:
        t //= 2
    return t


def kernel(x, norm_g, w_in, b_forget, rwkv_mu, rwkv_w0, rwkv_w2, rwkv_a0, rwkv_a2, rwkv_k_k,
           rwkv_k_a, rwkv_r_k, rwkv_ln_g, rwkv_ln_b, lru_conv_w, lru_conv_b, lru_w_a, lru_b_a,
           lru_w_x, lru_b_x, lru_lambda, w_out, final_g):
    b, s, d = x.shape
    depth = w_in.shape[0]
    g, h, dh, r = D_GROUP, N_HEADS, HEAD_DIM, RWKV_LORA
    tm = _pick_tile(s, 512)
    tq = _pick_tile(s, 256)
    tk_fox = tq
    tt = _pick_tile(s, 512)
    row = lambda a: a.reshape(1, -1).astype(F32)

    bdm = _block_diag(jnp.ones((h, dh, dh), BF16))
    w_in_k = _permute_w_in(w_in, tr=LANES)
    for l in range(depth):
        fbias = jnp.zeros((1, LANES), F32).at[0, FF_LANE:FF_LANE + h].set(b_forget[l])
        mu = rwkv_mu[l]
        consts = [row(norm_g[l]), w_in_k, fbias, row(mu[:N_RKV]),
                  jnp.zeros((1, LANES), F32).at[0, :2 * r].set(mu[N_RKV:])]
        (fq, fk, qaug, kaug, fvt, sq, sk, svt, gates, rkv, lx, misc, f2,
         norms) = _inproj(x, consts, l, tm=tm)

        per_head = lambda a: a.transpose(0, 2, 1).reshape(-1)
        fend = per_head(f2[:, tq - 1::tq, FF_LANE:FF_LANE + h])
        qmax = per_head(jnp.repeat(jnp.sqrt(norms[:, :, 0, 0:h]), tm // tq, axis=1))
        kmax = jnp.sqrt(jnp.max(norms[:, :, 0, h:2 * h], axis=1)).reshape(-1)
        y_fox_t = _fox_attention(fend, qmax, kmax, fq, qaug, fk, kaug, fvt, tq=tq, tk=tk_fox)
        y_sb_t = _sb_attention(sq, sk, svt, tq=tq)

        pad_rows = lambda a, lo: jnp.zeros((LANES, g), F32).at[lo:lo + r].set(a).astype(BF16)
        rw_prm = [row(rwkv_w0[l]), pad_rows(rwkv_w2[l], 0), row(rwkv_a0[l]), pad_rows(rwkv_a2[l], r),
                  row(rwkv_k_k[l]), row(rwkv_k_a[l]), row(rwkv_r_k[l]), row(rwkv_ln_g[l]),
                  row(rwkv_ln_b[l])]
        y_rw = _rwkv(rkv, misc, rw_prm, bdm, tt=tt)

        lru_prm = [lru_conv_w[l].astype(F32), row(lru_conv_b[l]),
                   _block_diag(lru_w_a[l]).astype(BF16), row(lru_b_a[l]),
                   _block_diag(lru_w_x[l]).astype(BF16), row(lru_b_x[l]), row(lru_lambda[l])]
        x = _outproj(x, y_fox_t, y_sb_t, y_rw, lx, gates, w_out[l].astype(BF16), row(final_g),
                     lru_prm, tm=tm, final=(l == depth - 1))
    return x
```

```python
import functools

import jax
import jax.numpy as jnp
from jax import lax
from jax.experimental import pallas as pl
from jax.experimental.pallas import tpu as pltpu

F32 = jnp.float32
BF16 = jnp.bfloat16

D_MODEL = 1024
D_GROUP = 256
N_HEADS = 4
HEAD_DIM = 64
RWKV_LORA = 32
CONV_WIDTH = 4
LRU_C = 8.0
RMS_EPS = 1e-6
GN_EPS = 64e-5
N_RKV = 3 * D_GROUP

LANES = 128
SUBLANES = 8
VMEM_LIMIT_BYTES = 56 * 1024 * 1024

C_FQ = 0
C_FK = C_FQ + D_GROUP
C_SQ = C_FK + D_GROUP
C_SK = C_SQ + D_GROUP
C_FV = C_SK + D_GROUP
C_SV = C_FV + D_GROUP
C_GATES = C_SV + D_GROUP
C_RKV = C_GATES + 4 * D_GROUP
C_LX = C_RKV + N_RKV
C_MISC = C_LX + D_GROUP
N_PROJ = C_MISC + LANES
FF_LANE = 2 * RWKV_LORA
HEADS_PER_PAIR = LANES // HEAD_DIM
N_PAIRS = N_HEADS // HEADS_PER_PAIR
AUG_STRIDE = 8
AUG_K_F = 0
AUG_K_ONE = 3
AUG_Q_ONE = 0
AUG_Q_F = 3

CUMSUM_BLOCK = LANES
V_AUG = HEAD_DIM + 16
LOG2E = 1.4426950408889634

FOX_SKIP_LOG2 = -150.0
FOX_NORM_MARGIN = 1.02
FOX_F_SLACK = 1.0

NEG_BIG = -1e30
SB_SKIP_LOG2 = -150.0
RWKV_CHUNK = 64
RWKV_UNROLL = 4


def _dot(a, b):
    return jnp.dot(a, b, preferred_element_type=F32)


def _dot_nt(a, b):
    return lax.dot_general(a, b, (((1,), (1,)), ((), ())), preferred_element_type=F32)


def _split3(x):
    hi = x.astype(BF16)
    r1 = x - hi.astype(F32)
    mid = r1.astype(BF16)
    lo = (r1 - mid.astype(F32)).astype(BF16)
    return hi, mid, lo


def _dot_sel_lhs(sel, x):
    hi, mid, lo = _split3(x)
    return _dot(sel, hi) + (_dot(sel, mid) + _dot(sel, lo))


def _dot_sel_rhs(x, sel):
    return _dot(x.astype(BF16), sel)


def _softplus(x):
    return jnp.maximum(x, 0.0) + jnp.log(1.0 + jnp.exp(-jnp.abs(x)))


def _log_sigmoid(x):
    return jnp.minimum(x, 0.0) - jnp.log(1.0 + jnp.exp(-jnp.abs(x)))


def _log2_sigmoid_of_log2(x2):
    return jnp.minimum(x2, 0.0) - jnp.log2(1.0 + jnp.exp2(-jnp.abs(x2)))


def _sigmoid(x):
    return 1.0 / (1.0 + jnp.exp(-x))


def _iota(shape, dim):
    return lax.broadcasted_iota(jnp.int32, shape, dim)


def _full_spec(a):
    return pl.BlockSpec(a.shape, lambda *_: (0,) * a.ndim)


def _params(n_grid):
    return pltpu.CompilerParams(dimension_semantics=("arbitrary",) * n_grid,
                                vmem_limit_bytes=VMEM_LIMIT_BYTES)


def _inproj_kernel(x_ref, g_ref, w_ref, fb_ref, mu_rkv_ref, mu_misc_ref,
                   fq_ref, fk_ref, qaug_ref, kaug_ref, fvt_ref, sq_ref, sk_ref, svt_ref, gates_ref,
                   rkv_ref, lx_ref, misc_ref, f2_ref, norms_ref, ftot, v_stage, pad_rkv, pad_misc):
    @pl.when(pl.program_id(1) == 0)
    def _():
        ftot[...] = jnp.zeros_like(ftot)
        pad_rkv[0:SUBLANES, :] = jnp.zeros((SUBLANES, N_RKV), F32)
        pad_misc[0:SUBLANES, :] = jnp.zeros((SUBLANES, LANES), F32)

    x = x_ref[0]
    tm = x.shape[0]
    ms = jnp.mean(x * x, axis=-1, keepdims=True)
    h = (x * lax.rsqrt(ms + RMS_EPS) * g_ref[...]).astype(BF16)
    scale = HEAD_DIM ** -0.5

    def proj(c0, width):
        return _dot(h, w_ref[0, :, c0:c0 + width])

    misc = proj(C_MISC, LANES)
    gates_ref[0] = proj(C_GATES, 4 * D_GROUP)
    lx_ref[0] = proj(C_LX, D_GROUP)

    for val, pad, mu_ref, out_ref in ((proj(C_RKV, N_RKV), pad_rkv, mu_rkv_ref, rkv_ref),
                                      (misc, pad_misc, mu_misc_ref, misc_ref)):
        pad[SUBLANES:SUBLANES + tm, :] = val
        prev = pad[SUBLANES - 1:SUBLANES - 1 + tm, :]
        pad[0:SUBLANES, :] = val[tm - SUBLANES:tm, :]
        out_ref[0] = val + (prev - val) * mu_ref[...]

    lf = _log_sigmoid(misc + fb_ref[...])
    cb = CUMSUM_BLOCK
    lower = (_iota((cb, cb), 1) <= _iota((cb, cb), 0)).astype(BF16)
    run = ftot[...]
    blocks = []
    for r0 in range(0, tm, cb):
        blk = _dot_sel_lhs(lower, lf[r0:r0 + cb, :]) + run
        run = blk[cb - 1:cb, :]
        blocks.append(blk)
    ftot[...] = run
    f = jnp.concatenate(blocks, axis=0)
    f2 = f * LOG2E
    f2_ref[0] = f2
    hi = f2.astype(BF16).astype(F32)
    mid = (f2 - hi).astype(BF16).astype(F32)
    lo = (f2 - hi - mid).astype(BF16).astype(F32)

    pq = proj(C_FQ, D_GROUP) * (scale * LOG2E)
    pk = proj(C_FK, D_GROUP)
    sq = proj(C_SQ, D_GROUP) * (scale * LOG2E)
    sk = proj(C_SK, D_GROUP)
    for pr in range(N_PAIRS):
        cols = slice(pr * LANES, (pr + 1) * LANES)
        fq_ref[0, pr] = pq[:, cols].astype(BF16)
        fk_ref[0, pr] = pk[:, cols].astype(BF16)
        sq_ref[0, pr] = sq[:, cols].astype(BF16)
        sk_ref[0, pr] = sk[:, cols].astype(BF16)

    lane = _iota((tm, LANES), 1)
    in_group = lane % AUG_STRIDE
    valid = lane < AUG_STRIDE * N_HEADS
    aug_q = jnp.where(jnp.logical_and(valid, jnp.logical_and(in_group >= AUG_Q_ONE,
                                                             in_group < AUG_Q_ONE + 3)), 1.0, 0.0)
    aug_k = jnp.where(jnp.logical_and(valid, jnp.logical_and(in_group >= AUG_K_ONE,
                                                             in_group < AUG_K_ONE + 3)), 1.0, 0.0)
    for hd in range(N_HEADS):
        for i3, piece in enumerate((hi, mid, lo)):
            col = jnp.broadcast_to(piece[:, FF_LANE + hd:FF_LANE + hd + 1], (tm, LANES))
            aug_q = jnp.where(lane == AUG_STRIDE * hd + AUG_Q_F + i3, col, aug_q)
            aug_k = jnp.where(lane == AUG_STRIDE * hd + AUG_K_F + i3, -col, aug_k)
    qaug_ref[0] = aug_q.astype(BF16)
    kaug_ref[0] = aug_k.astype(BF16)

    head_of_row = _iota((D_GROUP, LANES), 0) // HEAD_DIM
    nlane = _iota((SUBLANES, LANES), 1)
    norms = jnp.zeros((SUBLANES, LANES), F32)
    for side, pv in enumerate((pq, pk)):
        sel = (head_of_row + side * N_HEADS == _iota((D_GROUP, LANES), 1)).astype(BF16)
        top = jnp.max(_dot((pv * pv).astype(BF16), sel), axis=0, keepdims=True)
        keep = jnp.logical_and(nlane >= side * N_HEADS, nlane < (side + 1) * N_HEADS)
        norms = jnp.where(keep, top, norms)
    norms_ref[0, 0] = norms
    v_stage[...] = proj(C_SV, D_GROUP)
    svt_ref[0] = v_stage[...].T.astype(BF16)
    v_stage[...] = proj(C_FV, D_GROUP)
    vt = v_stage[...].T.astype(BF16)
    ones_rows = (_iota((V_AUG - HEAD_DIM, tm), 0) == 0).astype(BF16)
    for hd in range(N_HEADS):
        fvt_ref[0, hd * V_AUG:hd * V_AUG + HEAD_DIM, :] = vt[hd * HEAD_DIM:(hd + 1) * HEAD_DIM, :]
        fvt_ref[0, hd * V_AUG + HEAD_DIM:(hd + 1) * V_AUG, :] = ones_rows


def _inproj(x, consts, layer, *, tm):
    b, s, _ = x.shape
    w_spec = pl.BlockSpec((1,) + consts[1].shape[1:], lambda bi, i: (layer, 0, 0))
    seq = lambda width: pl.BlockSpec((1, tm, width), lambda bi, i: (bi, i, 0))
    pairs = pl.BlockSpec((1, N_PAIRS, tm, LANES), lambda bi, i: (bi, 0, i, 0))
    pairs_shape = jax.ShapeDtypeStruct((b, N_PAIRS, s, LANES), BF16)
    chan = lambda rows: pl.BlockSpec((1, rows, tm), lambda bi, i: (bi, 0, i))
    chan_shape = lambda rows: jax.ShapeDtypeStruct((b, rows, s), BF16)
    seq_shape = lambda width, dtype=F32: jax.ShapeDtypeStruct((b, s, width), dtype)
    return pl.pallas_call(
        _inproj_kernel,
        grid=(b, s // tm),
        in_specs=[seq(D_MODEL), _full_spec(consts[0]), w_spec] + [_full_spec(a) for a in consts[2:]],
        out_specs=[pairs, pairs, seq(LANES), seq(LANES), chan(N_HEADS * V_AUG),
                   pairs, pairs, chan(D_GROUP),
                   seq(4 * D_GROUP), seq(N_RKV), seq(D_GROUP), seq(LANES), seq(LANES),
                   pl.BlockSpec((1, 1, SUBLANES, LANES), lambda bi, i: (bi, i, 0, 0))],
        out_shape=[pairs_shape, pairs_shape, seq_shape(LANES, BF16), seq_shape(LANES, BF16),
                   chan_shape(N_HEADS * V_AUG), pairs_shape, pairs_shape, chan_shape(D_GROUP),
                   seq_shape(4 * D_GROUP), seq_shape(N_RKV), seq_shape(D_GROUP), seq_shape(LANES),
                   seq_shape(LANES), jax.ShapeDtypeStruct((b, s // tm, SUBLANES, LANES), F32)],
        scratch_shapes=[pltpu.VMEM((1, LANES), F32), pltpu.VMEM((tm, D_GROUP), F32),
                        pltpu.VMEM((tm + SUBLANES, N_RKV), F32),
                        pltpu.VMEM((tm + SUBLANES, LANES), F32)],
        compiler_params=_params(2),
        name="inproj",
    )(x, *consts)


def _fox_kernel(fend_ref, qmax_ref, kmax_ref, q_ref, qaug_ref, k_ref, kaug_ref, vt_ref, o_ref,
                s_scr, p_scr, mx_scr, al_scr, m_scr, acc_scr, *, tq, tk, nb):
    pair = pl.program_id(0)
    i = pl.program_id(1)
    nq = pl.num_programs(1)
    chains = [(b, hh) for b in range(nb) for hh in range(HEADS_PER_PAIR)]
    ids = list(range(len(chains)))
    lane = _iota((tq, LANES), 1)
    q = []
    for b, hh in chains:
        own_half = lane // HEAD_DIM == hh
        own_aug = lane // AUG_STRIDE == pair * HEADS_PER_PAIR + hh
        q.append(jnp.concatenate(
            [jnp.where(own_half, q_ref[b, 0], jnp.zeros((), BF16)),
             jnp.where(own_aug, qaug_ref[b], jnp.zeros((), BF16))], axis=1))
    n_full = (i * tq) // tk

    bhs = [b * N_HEADS + pair * HEADS_PER_PAIR + hh for b, hh in chains]
    rows = [bh * nq for bh in bhs]
    slack = [FOX_NORM_MARGIN * 2.0 * qmax_ref[r + i] * kmax_ref[bh]
             + fend_ref[r + jnp.maximum(i - 1, 0)] + FOX_F_SLACK - FOX_SKIP_LOG2
             for bh, r in zip(bhs, rows)]

    def some_chain_live(j):
        last_key = (jnp.maximum(j, 0) + 1) * (tk // tq) - 1
        live = [sl >= fend_ref[r + last_key] for sl, r in zip(slack, rows)]
        return jnp.logical_and(j >= 0, functools.reduce(jnp.logical_or, live))

    base = lax.while_loop(some_chain_live, lambda j: j - 1, n_full - 1) + 1
    n_eff = n_full - base

    def key_rows(j):
        return pl.ds(pl.multiple_of(j * tk, tk), tk)

    def scores_to(j, slot):
        kp = [jnp.concatenate([k_ref[b, 0, key_rows(j), :], kaug_ref[b, key_rows(j), :]], axis=1)
              for b in range(nb)]
        s = [_dot_nt(kp[b], q[c]) for c, (b, _) in zip(ids, chains)]
        for c in ids:
            s_scr[c, slot] = s[c]
            mx_scr[c, slot] = jnp.max(s[c], axis=0, keepdims=True)

    def weighted_values(j, slot):
        return [_dot(vt_ref[b, hh * V_AUG:(hh + 1) * V_AUG, key_rows(j)], p_scr[c, slot])
                for c, (b, hh) in zip(ids, chains)]

    def softmax_to(s, mx, slot):
        m_old = [m_scr[c] for c in ids]
        m_new = [jnp.maximum(a, b) for a, b in zip(m_old, mx)]
        p = [jnp.exp2(a - b).astype(BF16) for a, b in zip(s, m_new)]
        for c in ids:
            m_scr[c] = m_new[c]
            al_scr[c, slot] = jnp.exp2(m_old[c] - m_new[c])
            p_scr[c, slot] = p[c]

    def stage(local, cur):
        j = base + local
        nxt = 1 - cur
        pv_prev = weighted_values(jnp.maximum(j - 1, 0), nxt)
        scores_to(j + 1, nxt)
        softmax_to([s_scr[c, cur] for c in ids], [mx_scr[c, cur] for c in ids], cur)
        for c in ids:
            acc_scr[c] = al_scr[c, nxt] * acc_scr[c] + pv_prev[c]

    def tail(cur):
        nxt = 1 - cur
        pv_prev = weighted_values(jnp.maximum(n_full - 1, 0), nxt)
        mask = n_full * tk + _iota((tk, tq), 0) <= i * tq + _iota((tk, tq), 1)
        s = [jnp.where(mask, s_scr[c, cur], NEG_BIG) for c in ids]
        softmax_to(s, [jnp.max(x, axis=0, keepdims=True) for x in s], cur)
        pv_last = weighted_values(n_full, cur)
        for c, (b, hh) in zip(ids, chains):
            acc = al_scr[c, cur] * (al_scr[c, nxt] * acc_scr[c] + pv_prev[c]) + pv_last[c]
            o_ref[b, hh * HEAD_DIM:(hh + 1) * HEAD_DIM, :] = (
                acc[0:HEAD_DIM] / acc[HEAD_DIM:HEAD_DIM + 1])

    m_scr[...] = jnp.full(m_scr.shape, NEG_BIG, F32)
    acc_scr[...] = jnp.zeros_like(acc_scr)
    p_scr[:, 1] = jnp.zeros((len(chains), tk, tq), BF16)
    al_scr[:, 1] = jnp.ones((len(chains), 1, tq), F32)
    scores_to(base, 0)

    def pair_of_stages(jj, carry):
        stage(2 * jj, 0)
        stage(2 * jj + 1, 1)
        return carry

    lax.fori_loop(0, n_eff // 2, pair_of_stages, 0)

    @pl.when(n_eff % 2 == 1)
    def _():
        stage(n_eff - 1, 0)
        tail(1)

    @pl.when(n_eff % 2 == 0)
    def _():
        tail(0)


def _fox_attention(fend, qmax, kmax, q, qaug, k, kaug, vt, *, tq, tk):
    b, _, s, _ = q.shape
    n_chains = b * HEADS_PER_PAIR
    pair_rows = HEADS_PER_PAIR * V_AUG
    grid_spec = pltpu.PrefetchScalarGridSpec(
        num_scalar_prefetch=3,
        grid=(N_PAIRS, s // tq),
        in_specs=[pl.BlockSpec((b, 1, tq, LANES), lambda p, i, *_: (0, p, i, 0)),
                  pl.BlockSpec((b, tq, LANES), lambda p, i, *_: (0, i, 0)),
                  pl.BlockSpec((b, 1, s, LANES), lambda p, i, *_: (0, p, 0, 0)),
                  pl.BlockSpec((b, s, LANES), lambda p, i, *_: (0, 0, 0)),
                  pl.BlockSpec((b, pair_rows, s), lambda p, i, *_: (0, p, 0))],
        out_specs=pl.BlockSpec((b, HEADS_PER_PAIR * HEAD_DIM, tq), lambda p, i, *_: (0, p, i)),
        scratch_shapes=[pltpu.VMEM((n_chains, 2, tk, tq), F32), pltpu.VMEM((n_chains, 2, tk, tq), BF16),
                        pltpu.VMEM((n_chains, 2, 1, tq), F32), pltpu.VMEM((n_chains, 2, 1, tq), F32),
                        pltpu.VMEM((n_chains, 1, tq), F32), pltpu.VMEM((n_chains, V_AUG, tq), F32)])
    return pl.pallas_call(
        functools.partial(_fox_kernel, tq=tq, tk=tk, nb=b),
        grid_spec=grid_spec,
        out_shape=jax.ShapeDtypeStruct((b, N_HEADS * HEAD_DIM, s), F32),
        compiler_params=_params(2),
        name="fox_attention",
    )(fend, qmax, kmax, q, qaug, k, kaug, vt)


def _sb_kernel(q_ref, k_ref, vt_ref, o_ref, *, tq):
    i = pl.program_id(1)
    tk = tq
    heads = list(range(N_HEADS))
    lane = _iota((tq, LANES), 1)
    q = [jnp.where(lane // HEAD_DIM == hd % HEADS_PER_PAIR, q_ref[0, hd // HEADS_PER_PAIR],
                   jnp.zeros((), BF16)) for hd in heads]
    later = (_iota((tk, tk), 0) < _iota((tk, tk), 1)).astype(BF16)

    def each(f, *xs):
        return [f(*a) for a in zip(*xs)]

    def block(j, rest_q, acc, masked):
        ks = pl.multiple_of(j * tk, tk)
        kp = [k_ref[0, pr, pl.ds(ks, tk), :] for pr in range(N_PAIRS)]
        z = [_dot_nt(kp[hd // HEADS_PER_PAIR], q[hd]) for hd in heads]
        log_keep = each(lambda x: _log2_sigmoid_of_log2(-x), z)
        if masked:
            mask = _iota((tk, tq), 0) < _iota((tk, tq), 1)
            log_keep = each(lambda x: jnp.where(mask, x, 0.0), log_keep)
        rest_in = each(lambda x: _dot(later, x.astype(BF16)), log_keep)
        att = each(lambda x, lk, ri, rq: jnp.exp2(x + lk + ri + rq), z, log_keep, rest_in, rest_q)
        if masked:
            att = each(lambda x: jnp.where(mask, x, 0.0), att)
        pv = [_dot(vt_ref[0, hd * HEAD_DIM:(hd + 1) * HEAD_DIM, pl.ds(ks, tk)],
                   att[hd].astype(BF16)) for hd in heads]
        acc = each(lambda a, x: a + x, acc, pv)
        rest_q = each(lambda rq, ri, lk: rq + ri[0:1, :] + lk[0:1, :], rest_q, rest_in, log_keep)
        return rest_q, acc

    rest_q, acc = block(i, [jnp.zeros((1, tq), F32)] * N_HEADS,
                        [jnp.zeros((HEAD_DIM, tq), F32)] * N_HEADS, True)

    def cond(c):
        j, rest_q, _ = c
        alive = functools.reduce(jnp.maximum, rest_q)
        return jnp.logical_and(j >= 0, jnp.max(alive) > SB_SKIP_LOG2)

    def body(c):
        j, rest_q, acc = c
        rest_q, acc = block(j, list(rest_q), list(acc), False)
        return j - 1, tuple(rest_q), tuple(acc)

    _, _, acc = lax.while_loop(cond, body, (i - 1, tuple(rest_q), tuple(acc)))
    o_ref[0] = jnp.concatenate(list(acc), axis=0)


def _sb_attention(q, k, vt, *, tq):
    b, _, s, _ = q.shape
    h = N_HEADS
    return pl.pallas_call(
        functools.partial(_sb_kernel, tq=tq),
        grid=(b, s // tq),
        in_specs=[pl.BlockSpec((1, N_PAIRS, tq, LANES), lambda bi, i: (bi, 0, i, 0)),
                  pl.BlockSpec((1, N_PAIRS, s, LANES), lambda bi, i: (bi, 0, 0, 0)),
                  pl.BlockSpec((1, h * HEAD_DIM, s), lambda bi, i: (bi, 0, 0))],
        out_specs=pl.BlockSpec((1, h * HEAD_DIM, tq), lambda bi, i: (bi, 0, i)),
        out_shape=jax.ShapeDtypeStruct((b, h * HEAD_DIM, s), F32),
        compiler_params=_params(2),
        name="sb_attention",
    )(q, k, vt)


def _rwkv_kernel(p_ref, m_ref, w0_ref, w2_ref, a0_ref, a2_ref,
                 kk_ref, ka_ref, rk_ref, lng_ref, lnb_ref, bdm_ref, o_ref,
                 r_s, k_s, v_s, kn_s, al_s, lw_s,
                 wt_s, u0_s, o0_s, mrb_s, rt_s, bh_s, kh_s, pc_s, y_s, ht_s, *, tt, nb):
    c = RWKV_CHUNK
    g = D_GROUP
    n_chunks = tt // c

    @pl.when(pl.program_id(0) == 0)
    def _():
        ht_s[...] = jnp.zeros_like(ht_s)

    bdm = bdm_ref[...]
    bdm_f = bdm.astype(F32)

    for b in range(nb):
        p = p_ref[b]
        misc = m_ref[b]
        k = p[:, g:2 * g]

        w = -_softplus(-(w0_ref[...] + _dot(jnp.tanh(misc).astype(BF16), w2_ref[...]))) - 0.5
        alpha = _sigmoid(a0_ref[...] + _dot(misc.astype(BF16), a2_ref[...]))
        kn = k * kk_ref[...]
        ss = _dot_sel_rhs(kn * kn, bdm)
        r_s[b] = p[:, 0:g]
        k_s[b] = k * (1.0 + (alpha - 1.0) * ka_ref[...])
        v_s[b] = p[:, 2 * g:3 * g]
        kn_s[b] = kn * lax.rsqrt(jnp.maximum(ss, 1e-12))
        al_s[b] = alpha
        lw_s[b] = -jnp.exp(w)

    row = _iota((c, g), 0)
    col = _iota((c, g), 1) % c
    strict = col < row
    incl = col <= row
    eye = (col == row).astype(F32)
    lower_c = (_iota((c, c), 1) <= _iota((c, c), 0)).astype(BF16)
    level_masks = []
    m = 1
    while m < c:
        level_masks.append(jnp.logical_and(
            strict, jnp.logical_and(row // (2 * m) == col // (2 * m), row // m != col // m)))
        m *= 2

    def bd(x):
        return jnp.concatenate([x.astype(BF16)] * N_HEADS, axis=0) * bdm

    def mm(a, b_bf16):
        return _dot(a.astype(BF16), b_bf16)

    def mm_nt(a, b_bf16):
        return _dot_nt(a.astype(BF16), b_bf16)

    def each(f, *xs):
        return [f(*a) for a in zip(*xs)]

    def phase_a(chains, side):
        def tick():
            next(side, None)

        sls = [pl.ds(ci * c, c) for _, ci in chains]
        ld = lambda ref: [ref[b, sl, :] for (b, _), sl in zip(chains, sls)]
        r_c, k_c, v_c, kn_c, al_c, lw_c = (ld(s) for s in (r_s, k_s, v_s, kn_s, al_s, lw_s))
        cl = each(lambda x: _dot_sel_lhs(lower_c, x), lw_c)
        tick()
        cl_last = each(lambda x: x[c - 1:c, :], cl)
        a_t = each(lambda kn, x, lw: -kn * jnp.exp(x - lw), kn_c, cl, lw_c)
        r_t = each(lambda r, x: r * jnp.exp(x), r_c, cl)
        q_inv = each(lambda x: jnp.exp(-x), cl)
        p_rem = each(lambda xl, x: jnp.exp(xl - x), cl_last, cl)
        kna = each(lambda kn, al: kn * al, kn_c, al_c)
        ar = each(lambda a, r: jnp.concatenate([a, r], axis=0), a_t, r_t)
        s_b = each(lambda x, kb, qi: mm_nt(x, bd(kb * qi)), ar, kna, q_inv)
        tick()
        s_k = each(lambda x, kk, qi: mm_nt(x, bd(kk * qi)), ar, k_c, q_inv)
        tick()
        n = each(lambda x: jnp.where(strict, x[0:c], 0.0), s_b)
        a_ak = each(lambda x: jnp.where(strict, x[0:c], 0.0), s_k)
        m_rb = each(lambda x: jnp.where(incl, x[c:2 * c], 0.0), s_b)
        m_rk = each(lambda x: jnp.where(incl, x[c:2 * c], 0.0), s_k)

        inv = each(lambda x: eye + jnp.where(level_masks[0], x, 0.0), n)
        for lm in level_masks[1:]:
            half = each(lambda d, x: mm(d, bd(jnp.where(lm, x, 0.0))), inv, n)
            tick()
            inv = each(lambda d, hf: d + mm(hf, bd(d)), inv, half)
            tick()

        v_bd = each(bd, v_c)
        akv = each(mm, a_ak, v_bd)
        tick()
        wt = each(lambda d, a: mm(d, bd(a)), inv, a_t)
        tick()
        u0 = each(lambda d, x: mm(d, bd(x)), inv, akv)
        tick()
        o0 = each(mm, m_rk, v_bd)
        bh = each(lambda x, p: x * p, kna, p_rem)
        kh = each(lambda x, p: x * p, k_c, p_rem)
        for _ in side:
            pass
        for ref, vals in zip((wt_s, u0_s, o0_s, mrb_s, rt_s, bh_s, kh_s),
                             (wt, u0, o0, m_rb, r_t, bh, kh)):
            for (b, _), sl, val in zip(chains, sls, vals):
                ref[b, sl, :] = val
        for (b, ci), xl in zip(chains, cl_last):
            pc_s[b, pl.ds(ci * SUBLANES, SUBLANES), :] = jnp.broadcast_to(jnp.exp(xl), (SUBLANES, g))

    def phase_b_steps(chunk_ids):
        bs = list(range(nb))
        for ci in chunk_ids:
            sl = pl.ds(ci * c, c)
            ht = [ht_s[b] for b in bs]
            wr = [jnp.concatenate([wt_s[b, sl, :], rt_s[b, sl, :]], axis=0) for b in bs]
            wrh = each(lambda x, hh: mm_nt(x, hh.astype(BF16)), wr, ht)
            yield
            u = [x[0:c] + u0_s[b, sl, :] for x, b in zip(wrh, bs)]
            uv_t = [jnp.concatenate([x, v_s[b, sl, :]], axis=0).T for x, b in zip(u, bs)]
            bk = [jnp.concatenate([bh_s[b, sl, :], kh_s[b, sl, :]], axis=0) for b in bs]
            upd = each(lambda x, y: mm(x, y.astype(BF16)), uv_t, bk)
            mu_ = [mm(mrb_s[b, sl, :], bd(x)) for x, b in zip(u, bs)]
            yield
            for b in bs:
                p_c = pc_s[b, pl.ds(ci * SUBLANES, 1), :]
                y_s[b, sl, :] = wrh[b][c:2 * c] + mu_[b] + o0_s[b, sl, :]
                ht_s[b] = ht[b] * p_c + bdm_f * upd[b]
            yield

    pending = iter(())
    for gi in range(n_chunks // RWKV_UNROLL):
        chunk_ids = range(gi * RWKV_UNROLL, (gi + 1) * RWKV_UNROLL)
        phase_a([(b, ci) for ci in chunk_ids for b in range(nb)], pending)
        pending = phase_b_steps(chunk_ids)
    for _ in pending:
        pass

    inv_n = 1.0 / HEAD_DIM
    for b in range(nb):
        y = y_s[b]
        mean = _dot_sel_rhs(y, bdm) * inv_n
        yc = y - mean
        var = _dot_sel_rhs(yc * yc, bdm) * inv_n
        yn = yc * lax.rsqrt(var + GN_EPS) * lng_ref[...] + lnb_ref[...]
        bonus = _dot_sel_rhs(r_s[b] * k_s[b] * rk_ref[...], bdm) * v_s[b]
        o_ref[b] = yn + bonus


def _rwkv(p_rkv, misc, prm, bdm, *, tt):
    b, s, _ = p_rkv.shape
    g = D_GROUP
    seq = lambda width: pl.BlockSpec((b, tt, width), lambda ti: (0, ti, 0))
    big = pltpu.VMEM((b, tt, g), F32)
    return pl.pallas_call(
        functools.partial(_rwkv_kernel, tt=tt, nb=b),
        grid=(s // tt,),
        in_specs=[seq(N_RKV), seq(LANES)] + [_full_spec(a) for a in prm] + [_full_spec(bdm)],
        out_specs=seq(g),
        out_shape=jax.ShapeDtypeStruct((b, s, g), F32),
        scratch_shapes=[big] * 13
        + [pltpu.VMEM((b, tt // RWKV_CHUNK * SUBLANES, g), F32), big,
           pltpu.VMEM((b, g, g), F32)],
        compiler_params=_params(1),
        name="rwkv7",
    )(p_rkv, misc, *prm, bdm)


def _lru_tile(x_ref, cw_ref, cb_ref, wa_ref, ba_ref, wx_ref, bx_ref, lam_ref, o_ref,
              pad, a_s, u_s, h_s, *, tt, nb):
    @pl.when(pl.program_id(0) == 0)
    def _():
        pad[:, 0:SUBLANES, :] = jnp.zeros((nb, SUBLANES, D_GROUP), F32)
        h_s[...] = jnp.zeros_like(h_s)

    for b in range(nb):
        x = x_ref[b]
        pad[b, SUBLANES:SUBLANES + tt, :] = x
        xc = cw_ref[CONV_WIDTH - 1:CONV_WIDTH, :] * x + cb_ref[...]
        for d in range(1, CONV_WIDTH):
            tap = CONV_WIDTH - 1 - d
            xc = xc + cw_ref[tap:tap + 1, :] * pad[b, SUBLANES - d:SUBLANES - d + tt, :]
        pad[b, 0:SUBLANES, :] = x[tt - SUBLANES:tt, :]

        xb = xc.astype(BF16)
        r = _sigmoid(_dot(xb, wa_ref[...]) + ba_ref[...])
        i = _sigmoid(_dot(xb, wx_ref[...]) + bx_ref[...])
        log_a = -LRU_C * r * _softplus(-lam_ref[...])
        a_s[b] = jnp.exp(log_a)
        th = jnp.tanh(log_a)
        u_s[b] = jnp.sqrt(-2.0 * th / (1.0 - th)) * (i * xc)

    row = _iota((SUBLANES, D_GROUP), 0)

    def rows_above(x, k, fill):
        return jnp.where(row >= k, pltpu.roll(x, k, axis=0), fill)

    def group(gi, hs):
        base = pl.multiple_of(gi * SUBLANES, SUBLANES)
        carry = []
        for b in range(nb):
            a = a_s[b, pl.ds(base, SUBLANES), :]
            u = u_s[b, pl.ds(base, SUBLANES), :]
            k = 1
            while k < SUBLANES:
                u = a * rows_above(u, k, 0.0) + u
                a = a * rows_above(a, k, 1.0)
                k *= 2
            h = a * hs[b] + u
            o_ref[b, pl.ds(base, SUBLANES), :] = h
            carry.append(h[SUBLANES - 1:SUBLANES, :])
        return tuple(carry)

    hs = lax.fori_loop(0, tt // SUBLANES, group, tuple(h_s[b] for b in range(nb)))
    for b in range(nb):
        h_s[b] = hs[b]


def _outproj_kernel(x_ref, yft_ref, yst_ref, yr_ref, lx_ref, gates_ref, w_ref, fg_ref,
                    cw_ref, cb_ref, wa_ref, ba_ref, wx_ref, bx_ref, lam_ref, o_ref,
                    pad, a_s, u_s, yl_s, h_s, *, final, tm, nb):
    _lru_tile(lx_ref, cw_ref, cb_ref, wa_ref, ba_ref, wx_ref, bx_ref, lam_ref, yl_s,
              pad, a_s, u_s, h_s, tt=tm, nb=nb)
    for b in range(nb):
        acc = x_ref[b]
        ys = (yft_ref[b].T, yst_ref[b].T, yr_ref[b], yl_s[b])
        for gi, y in enumerate(ys):
            gate = gates_ref[b, :, gi * D_GROUP:(gi + 1) * D_GROUP]
            y = y * (gate * _sigmoid(gate))
            acc = acc + _dot(y.astype(BF16), w_ref[gi * D_GROUP:(gi + 1) * D_GROUP, :])
        if final:
            ms = jnp.mean(acc * acc, axis=-1, keepdims=True)
            acc = acc * lax.rsqrt(ms + RMS_EPS) * fg_ref[...]
        o_ref[b] = acc


def _outproj(x, y_fox_t, y_sb_t, y_rw, lx, gates, w, final_g, lru_prm, *, tm, final):
    b, s, _ = x.shape
    g = D_GROUP
    seq = lambda width: pl.BlockSpec((b, tm, width), lambda i: (0, i, 0))
    chan = pl.BlockSpec((b, g, tm), lambda i: (0, 0, i))
    big = pltpu.VMEM((b, tm, g), F32)
    return pl.pallas_call(
        functools.partial(_outproj_kernel, final=final, tm=tm, nb=b),
        grid=(s // tm,),
        in_specs=[seq(D_MODEL), chan, chan, seq(g), seq(g), seq(4 * g),
                  _full_spec(w), _full_spec(final_g)] + [_full_spec(a) for a in lru_prm],
        out_specs=seq(D_MODEL),
        out_shape=jax.ShapeDtypeStruct((b, s, D_MODEL), F32),
        scratch_shapes=[pltpu.VMEM((b, tm + SUBLANES, g), F32), big, big, big,
                        pltpu.VMEM((b, 1, g), F32)],
        compiler_params=_params(1),
        name="outproj",
    )(x, y_fox_t, y_sb_t, y_rw, lx, gates, w, final_g, *lru_prm)


def _w_in_segments():
    g, h, r = D_GROUP, N_HEADS, RWKV_LORA
    o_ff = 4 * g
    o_sb = o_ff + h
    o_rw = o_sb + 4 * g
    o_rg = o_rw + 3 * g + 2 * r
    o_lx = o_rg + g
    o_lg = o_lx + g
    segs = [(0, C_FQ, 2 * g), (o_sb, C_SQ, 2 * g), (2 * g, C_FV, g), (o_sb + 2 * g, C_SV, g),
             (3 * g, C_GATES, g), (o_sb + 3 * g, C_GATES + g, g), (o_rg, C_GATES + 2 * g, g),
             (o_lg, C_GATES + 3 * g, g),
             (o_rw, C_RKV, 3 * g), (o_lx, C_LX, g),
             (o_rw + 3 * g, C_MISC, 2 * r), (o_ff, C_MISC + FF_LANE, h)]
    return segs


def _w_in_kernel(w_ref, o_ref):
    o_ref[...] = jnp.zeros_like(o_ref)
    for src, dst, width in _w_in_segments():
        o_ref[0, :, dst:dst + width] = w_ref[0, :, src:src + width].astype(BF16)


def _permute_w_in(w_in, *, tr):
    depth, d, n_in = w_in.shape
    return pl.pallas_call(
        _w_in_kernel,
        grid=(depth, d // tr),
        in_specs=[pl.BlockSpec((1, tr, n_in), lambda l, i: (l, i, 0))],
        out_specs=pl.BlockSpec((1, tr, N_PROJ), lambda l, i: (l, i, 0)),
        out_shape=jax.ShapeDtypeStruct((depth, d, N_PROJ), BF16),
        compiler_params=_params(2),
        name="w_in_layout",
    )(w_in)


def _block_diag(w):
    h, n, _ = w.shape
    eye = jnp.eye(h, dtype=w.dtype)
    return jnp.einsum('hij,hk->hikj', w, eye).reshape(h * n, h * n)


def _pick_tile(s, pref):
    t = pref
    while s % t:
        t //= 2
    return t


def kernel(x, norm_g, w_in, b_forget, rwkv_mu, rwkv_w0, rwkv_w2, rwkv_a0, rwkv_a2, rwkv_k_k,
           rwkv_k_a, rwkv_r_k, rwkv_ln_g, rwkv_ln_b, lru_conv_w, lru_conv_b, lru_w_a, lru_b_a,
           lru_w_x, lru_b_x, lru_lambda, w_out, final_g):
    b, s, d = x.shape
    depth = w_in.shape[0]
    g, h, dh, r = D_GROUP, N_HEADS, HEAD_DIM, RWKV_LORA
    tm = _pick_tile(s, 512)
    tq = _pick_tile(s, 256)
    tt = _pick_tile(s, 512)
    row = lambda a: a.reshape(1, -1).astype(F32)

    bdm = _block_diag(jnp.ones((h, dh, dh), BF16))
    w_in_k = _permute_w_in(w_in, tr=LANES)
    for l in range(depth):
        fbias = jnp.zeros((1, LANES), F32).at[0, FF_LANE:FF_LANE + h].set(b_forget[l])
        mu = rwkv_mu[l]
        consts = [row(norm_g[l]), w_in_k, fbias, row(mu[:N_RKV]),
                  jnp.zeros((1, LANES), F32).at[0, :2 * r].set(mu[N_RKV:])]
        (fq, fk, qaug, kaug, fvt, sq, sk, svt, gates, rkv, lx, misc, f2,
         norms) = _inproj(x, consts, l, tm=tm)

        per_head = lambda a: a.transpose(0, 2, 1).reshape(-1)
        fend = per_head(f2[:, tq - 1::tq, FF_LANE:FF_LANE + h])
        qmax = per_head(jnp.repeat(jnp.sqrt(norms[:, :, 0, 0:h]), tm // tq, axis=1))
        kmax = jnp.sqrt(jnp.max(norms[:, :, 0, h:2 * h], axis=1)).reshape(-1)
        y_fox_t = _fox_attention(fend, qmax, kmax, fq, qaug, fk, kaug, fvt, tq=tq, tk=tq)
        y_sb_t = _sb_attention(sq, sk, svt, tq=tq)

        pad_rows = lambda a, lo: jnp.zeros((LANES, g), F32).at[lo:lo + r].set(a).astype(BF16)
        rw_prm = [row(rwkv_w0[l]), pad_rows(rwkv_w2[l], 0), row(rwkv_a0[l]), pad_rows(rwkv_a2[l], r),
                  row(rwkv_k_k[l]), row(rwkv_k_a[l]), row(rwkv_r_k[l]), row(rwkv_ln_g[l]),
                  row(rwkv_ln_b[l])]
        y_rw = _rwkv(rkv, misc, rw_prm, bdm, tt=tt)

        lru_prm = [lru_conv_w[l].astype(F32), row(lru_conv_b[l]),
                   _block_diag(lru_w_a[l]).astype(BF16), row(lru_b_a[l]),
                   _block_diag(lru_w_x[l]).astype(BF16), row(lru_b_x[l]), row(lru_lambda[l])]
        x = _outproj(x, y_fox_t, y_sb_t, y_rw, lx, gates, w_out[l].astype(BF16), row(final_g),
                     lru_prm, tm=tm, final=(l == depth - 1))
    return x
```

```python
import functools

import jax
import jax.numpy as jnp
from jax import lax
from jax.experimental import pallas as pl
from jax.experimental.pallas import tpu as pltpu

F32 = jnp.float32
BF16 = jnp.bfloat16

D_MODEL = 1024
D_GROUP = 256
N_HEADS = 4
HEAD_DIM = 64
RWKV_LORA = 32
CONV_WIDTH = 4
LRU_C = 8.0
RMS_EPS = 1e-6
GN_EPS = 64e-5
N_RKV = 3 * D_GROUP

LANES = 128
SUBLANES = 8
VMEM_LIMIT_BYTES = 56 * 1024 * 1024

C_FQ = 0
C_FK = C_FQ + D_GROUP
C_SQ = C_FK + D_GROUP
C_SK = C_SQ + D_GROUP
C_FV = C_SK + D_GROUP
C_SV = C_FV + D_GROUP
C_GATES = C_SV + D_GROUP
C_RKV = C_GATES + 4 * D_GROUP
C_LX = C_RKV + N_RKV
C_MISC = C_LX + D_GROUP
N_PROJ = C_MISC + LANES
FF_LANE = 2 * RWKV_LORA
HEADS_PER_PAIR = LANES // HEAD_DIM
N_PAIRS = N_HEADS // HEADS_PER_PAIR
AUG_STRIDE = 8
AUG_K_F = 0
AUG_K_ONE = 3
AUG_Q_ONE = 0
AUG_Q_F = 3

CUMSUM_BLOCK = LANES
V_AUG = HEAD_DIM + 16
LOG2E = 1.4426950408889634

FOX_SKIP_LOG2 = -150.0
FOX_NORM_MARGIN = 1.02
FOX_F_SLACK = 1.0

NEG_BIG = -1e30
SB_SKIP_LOG2 = -150.0
RWKV_CHUNK = 64
RWKV_UNROLL = 4


def _dot(a, b):
    return jnp.dot(a, b, preferred_element_type=F32)


def _dot_nt(a, b):
    return lax.dot_general(a, b, (((1,), (1,)), ((), ())), preferred_element_type=F32)


def _split3(x):
    hi = x.astype(BF16)
    r1 = x - hi.astype(F32)
    mid = r1.astype(BF16)
    lo = (r1 - mid.astype(F32)).astype(BF16)
    return hi, mid, lo


def _dot_sel_lhs(sel, x):
    hi, mid, lo = _split3(x)
    return _dot(sel, hi) + (_dot(sel, mid) + _dot(sel, lo))


def _dot_sel_rhs(x, sel):
    return _dot(x.astype(BF16), sel)


def _softplus(x):
    return jnp.maximum(x, 0.0) + jnp.log(1.0 + jnp.exp(-jnp.abs(x)))


def _log_sigmoid(x):
    return jnp.minimum(x, 0.0) - jnp.log(1.0 + jnp.exp(-jnp.abs(x)))


def _log2_sigmoid_of_log2(x2):
    return jnp.minimum(x2, 0.0) - jnp.log2(1.0 + jnp.exp2(-jnp.abs(x2)))


def _sigmoid(x):
    return 1.0 / (1.0 + jnp.exp(-x))


def _iota(shape, dim):
    return lax.broadcasted_iota(jnp.int32, shape, dim)


def _full_spec(a):
    return pl.BlockSpec(a.shape, lambda *_: (0,) * a.ndim)


def _params(n_grid):
    return pltpu.CompilerParams(dimension_semantics=("arbitrary",) * n_grid,
                                vmem_limit_bytes=VMEM_LIMIT_BYTES)


def _inproj_kernel(x_ref, g_ref, w_ref, fb_ref, mu_rkv_ref, mu_misc_ref,
                   fq_ref, fk_ref, qaug_ref, kaug_ref, fvt_ref, sq_ref, sk_ref, svt_ref, gates_ref,
                   rkv_ref, lx_ref, misc_ref, f2_ref, norms_ref, ftot, v_stage, pad_rkv, pad_misc):
    @pl.when(pl.program_id(1) == 0)
    def _():
        ftot[...] = jnp.zeros_like(ftot)
        pad_rkv[0:SUBLANES, :] = jnp.zeros((SUBLANES, N_RKV), F32)
        pad_misc[0:SUBLANES, :] = jnp.zeros((SUBLANES, LANES), F32)

    x = x_ref[0]
    tm = x.shape[0]
    ms = jnp.mean(x * x, axis=-1, keepdims=True)
    h = (x * lax.rsqrt(ms + RMS_EPS) * g_ref[...]).astype(BF16)
    scale = HEAD_DIM ** -0.5

    def proj(c0, width):
        return _dot(h, w_ref[0, :, c0:c0 + width])

    misc = proj(C_MISC, LANES)
    gates_ref[0] = proj(C_GATES, 4 * D_GROUP)
    lx_ref[0] = proj(C_LX, D_GROUP)

    for val, pad, mu_ref, out_ref in ((proj(C_RKV, N_RKV), pad_rkv, mu_rkv_ref, rkv_ref),
                                      (misc, pad_misc, mu_misc_ref, misc_ref)):
        pad[SUBLANES:SUBLANES + tm, :] = val
        prev = pad[SUBLANES - 1:SUBLANES - 1 + tm, :]
        pad[0:SUBLANES, :] = val[tm - SUBLANES:tm, :]
        out_ref[0] = val + (prev - val) * mu_ref[...]

    lf = _log_sigmoid(misc + fb_ref[...])
    cb = CUMSUM_BLOCK
    lower = (_iota((cb, cb), 1) <= _iota((cb, cb), 0)).astype(BF16)
    run = ftot[...]
    blocks = []
    for r0 in range(0, tm, cb):
        blk = _dot_sel_lhs(lower, lf[r0:r0 + cb, :]) + run
        run = blk[cb - 1:cb, :]
        blocks.append(blk)
    ftot[...] = run
    f = jnp.concatenate(blocks, axis=0)
    f2 = f * LOG2E
    f2_ref[0] = f2
    hi = f2.astype(BF16).astype(F32)
    mid = (f2 - hi).astype(BF16).astype(F32)
    lo = (f2 - hi - mid).astype(BF16).astype(F32)

    pq = proj(C_FQ, D_GROUP) * (scale * LOG2E)
    pk = proj(C_FK, D_GROUP)
    sq = proj(C_SQ, D_GROUP) * (scale * LOG2E)
    sk = proj(C_SK, D_GROUP)
    for pr in range(N_PAIRS):
        cols = slice(pr * LANES, (pr + 1) * LANES)
        fq_ref[0, pr] = pq[:, cols].astype(BF16)
        fk_ref[0, pr] = pk[:, cols].astype(BF16)
        sq_ref[0, pr] = sq[:, cols].astype(BF16)
        sk_ref[0, pr] = sk[:, cols].astype(BF16)

    lane = _iota((tm, LANES), 1)
    in_group = lane % AUG_STRIDE
    valid = lane < AUG_STRIDE * N_HEADS
    aug_q = jnp.where(jnp.logical_and(valid, jnp.logical_and(in_group >= AUG_Q_ONE,
                                                             in_group < AUG_Q_ONE + 3)), 1.0, 0.0)
    aug_k = jnp.where(jnp.logical_and(valid, jnp.logical_and(in_group >= AUG_K_ONE,
                                                             in_group < AUG_K_ONE + 3)), 1.0, 0.0)
    for hd in range(N_HEADS):
        for i3, piece in enumerate((hi, mid, lo)):
            col = jnp.broadcast_to(piece[:, FF_LANE + hd:FF_LANE + hd + 1], (tm, LANES))
            aug_q = jnp.where(lane == AUG_STRIDE * hd + AUG_Q_F + i3, col, aug_q)
            aug_k = jnp.where(lane == AUG_STRIDE * hd + AUG_K_F + i3, -col, aug_k)
    qaug_ref[0] = aug_q.astype(BF16)
    kaug_ref[0] = aug_k.astype(BF16)

    head_of_row = _iota((D_GROUP, LANES), 0) // HEAD_DIM
    nlane = _iota((SUBLANES, LANES), 1)
    norms = jnp.zeros((SUBLANES, LANES), F32)
    for side, pv in enumerate((pq, pk)):
        sel = (head_of_row + side * N_HEADS == _iota((D_GROUP, LANES), 1)).astype(BF16)
        top = jnp.max(_dot((pv * pv).astype(BF16), sel), axis=0, keepdims=True)
        keep = jnp.logical_and(nlane >= side * N_HEADS, nlane < (side + 1) * N_HEADS)
        norms = jnp.where(keep, top, norms)
    norms_ref[0, 0] = norms
    v_stage[...] = proj(C_SV, D_GROUP)
    svt_ref[0] = v_stage[...].T.astype(BF16)
    v_stage[...] = proj(C_FV, D_GROUP)
    vt = v_stage[...].T.astype(BF16)
    ones_rows = (_iota((V_AUG - HEAD_DIM, tm), 0) == 0).astype(BF16)
    for hd in range(N_HEADS):
        fvt_ref[0, hd * V_AUG:hd * V_AUG + HEAD_DIM, :] = vt[hd * HEAD_DIM:(hd + 1) * HEAD_DIM, :]
        fvt_ref[0, hd * V_AUG + HEAD_DIM:(hd + 1) * V_AUG, :] = ones_rows


def _inproj(x, consts, layer, *, tm):
    b, s, _ = x.shape
    w_spec = pl.BlockSpec((1,) + consts[1].shape[1:], lambda bi, i: (layer, 0, 0))
    seq = lambda width: pl.BlockSpec((1, tm, width), lambda bi, i: (bi, i, 0))
    pairs = pl.BlockSpec((1, N_PAIRS, tm, LANES), lambda bi, i: (bi, 0, i, 0))
    pairs_shape = jax.ShapeDtypeStruct((b, N_PAIRS, s, LANES), BF16)
    chan = lambda rows: pl.BlockSpec((1, rows, tm), lambda bi, i: (bi, 0, i))
    chan_shape = lambda rows: jax.ShapeDtypeStruct((b, rows, s), BF16)
    seq_shape = lambda width, dtype=F32: jax.ShapeDtypeStruct((b, s, width), dtype)
    return pl.pallas_call(
        _inproj_kernel,
        grid=(b, s // tm),
        in_specs=[seq(D_MODEL), _full_spec(consts[0]), w_spec] + [_full_spec(a) for a in consts[2:]],
        out_specs=[pairs, pairs, seq(LANES), seq(LANES), chan(N_HEADS * V_AUG),
                   pairs, pairs, chan(D_GROUP),
                   seq(4 * D_GROUP), seq(N_RKV), seq(D_GROUP), seq(LANES), seq(LANES),
                   pl.BlockSpec((1, 1, SUBLANES, LANES), lambda bi, i: (bi, i, 0, 0))],
        out_shape=[pairs_shape, pairs_shape, seq_shape(LANES, BF16), seq_shape(LANES, BF16),
                   chan_shape(N_HEADS * V_AUG), pairs_shape, pairs_shape, chan_shape(D_GROUP),
                   seq_shape(4 * D_GROUP), seq_shape(N_RKV), seq_shape(D_GROUP), seq_shape(LANES),
                   seq_shape(LANES), jax.ShapeDtypeStruct((b, s // tm, SUBLANES, LANES), F32)],
        scratch_shapes=[pltpu.VMEM((1, LANES), F32), pltpu.VMEM((tm, D_GROUP), F32),
                        pltpu.VMEM((tm + SUBLANES, N_RKV), F32),
                        pltpu.VMEM((tm + SUBLANES, LANES), F32)],
        compiler_params=_params(2),
        name="inproj",
    )(x, *consts)


def _fox_kernel(fend_ref, qmax_ref, kmax_ref, q_ref, qaug_ref, k_ref, kaug_ref, vt_ref, o_ref,
                s_scr, p_scr, mx_scr, al_scr, m_scr, acc_scr, *, tq, tk, nb):
    pair = pl.program_id(0)
    i = pl.program_id(1)
    nq = pl.num_programs(1)
    chains = [(b, hh) for b in range(nb) for hh in range(HEADS_PER_PAIR)]
    ids = list(range(len(chains)))
    lane = _iota((tq, LANES), 1)
    q = []
    for b, hh in chains:
        own_half = lane // HEAD_DIM == hh
        own_aug = lane // AUG_STRIDE == pair * HEADS_PER_PAIR + hh
        q.append(jnp.concatenate(
            [jnp.where(own_half, q_ref[b, 0], jnp.zeros((), BF16)),
             jnp.where(own_aug, qaug_ref[b], jnp.zeros((), BF16))], axis=1))
    n_full = (i * tq) // tk

    bhs = [b * N_HEADS + pair * HEADS_PER_PAIR + hh for b, hh in chains]
    rows = [bh * nq for bh in bhs]
    slack = [FOX_NORM_MARGIN * 2.0 * qmax_ref[r + i] * kmax_ref[bh]
             + fend_ref[r + jnp.maximum(i - 1, 0)] + FOX_F_SLACK - FOX_SKIP_LOG2
             for bh, r in zip(bhs, rows)]

    def some_chain_live(j):
        last_key = (jnp.maximum(j, 0) + 1) * (tk // tq) - 1
        live = [sl >= fend_ref[r + last_key] for sl, r in zip(slack, rows)]
        return jnp.logical_and(j >= 0, functools.reduce(jnp.logical_or, live))

    base = lax.while_loop(some_chain_live, lambda j: j - 1, n_full - 1) + 1
    n_eff = n_full - base

    def key_rows(j):
        return pl.ds(pl.multiple_of(j * tk, tk), tk)

    def scores_to(j, slot):
        kp = [jnp.concatenate([k_ref[b, 0, key_rows(j), :], kaug_ref[b, key_rows(j), :]], axis=1)
              for b in range(nb)]
        s = [_dot_nt(kp[b], q[c]) for c, (b, _) in zip(ids, chains)]
        for c in ids:
            s_scr[c, slot] = s[c]
            mx_scr[c, slot] = jnp.max(s[c], axis=0, keepdims=True)

    def weighted_values(j, slot):
        return [_dot(vt_ref[b, hh * V_AUG:(hh + 1) * V_AUG, key_rows(j)], p_scr[c, slot])
                for c, (b, hh) in zip(ids, chains)]

    def softmax_to(s, mx, slot):
        m_old = [m_scr[c] for c in ids]
        m_new = [jnp.maximum(a, b) for a, b in zip(m_old, mx)]
        p = [jnp.exp2(a - b).astype(BF16) for a, b in zip(s, m_new)]
        for c in ids:
            m_scr[c] = m_new[c]
            al_scr[c, slot] = jnp.exp2(m_old[c] - m_new[c])
            p_scr[c, slot] = p[c]

    def stage(local, cur):
        j = base + local
        nxt = 1 - cur
        pv_prev = weighted_values(jnp.maximum(j - 1, 0), nxt)
        scores_to(j + 1, nxt)
        softmax_to([s_scr[c, cur] for c in ids], [mx_scr[c, cur] for c in ids], cur)
        for c in ids:
            acc_scr[c] = al_scr[c, nxt] * acc_scr[c] + pv_prev[c]

    def tail(cur):
        nxt = 1 - cur
        pv_prev = weighted_values(jnp.maximum(n_full - 1, 0), nxt)
        mask = n_full * tk + _iota((tk, tq), 0) <= i * tq + _iota((tk, tq), 1)
        s = [jnp.where(mask, s_scr[c, cur], NEG_BIG) for c in ids]
        softmax_to(s, [jnp.max(x, axis=0, keepdims=True) for x in s], cur)
        pv_last = weighted_values(n_full, cur)
        for c, (b, hh) in zip(ids, chains):
            acc = al_scr[c, cur] * (al_scr[c, nxt] * acc_scr[c] + pv_prev[c]) + pv_last[c]
            o_ref[b, hh * HEAD_DIM:(hh + 1) * HEAD_DIM, :] = (
                acc[0:HEAD_DIM] / acc[HEAD_DIM:HEAD_DIM + 1])

    m_scr[...] = jnp.full(m_scr.shape, NEG_BIG, F32)
    acc_scr[...] = jnp.zeros_like(acc_scr)
    p_scr[:, 1] = jnp.zeros((len(chains), tk, tq), BF16)
    al_scr[:, 1] = jnp.ones((len(chains), 1, tq), F32)
    scores_to(base, 0)

    def pair_of_stages(jj, carry):
        stage(2 * jj, 0)
        stage(2 * jj + 1, 1)
        return carry

    lax.fori_loop(0, n_eff // 2, pair_of_stages, 0)

    @pl.when(n_eff % 2 == 1)
    def _():
        stage(n_eff - 1, 0)
        tail(1)

    @pl.when(n_eff % 2 == 0)
    def _():
        tail(0)


def _fox_attention(fend, qmax, kmax, q, qaug, k, kaug, vt, *, tq, tk):
    b, _, s, _ = q.shape
    n_chains = b * HEADS_PER_PAIR
    pair_rows = HEADS_PER_PAIR * V_AUG
    grid_spec = pltpu.PrefetchScalarGridSpec(
        num_scalar_prefetch=3,
        grid=(N_PAIRS, s // tq),
        in_specs=[pl.BlockSpec((b, 1, tq, LANES), lambda p, i, *_: (0, p, i, 0)),
                  pl.BlockSpec((b, tq, LANES), lambda p, i, *_: (0, i, 0)),
                  pl.BlockSpec((b, 1, s, LANES), lambda p, i, *_: (0, p, 0, 0)),
                  pl.BlockSpec((b, s, LANES), lambda p, i, *_: (0, 0, 0)),
                  pl.BlockSpec((b, pair_rows, s), lambda p, i, *_: (0, p, 0))],
        out_specs=pl.BlockSpec((b, HEADS_PER_PAIR * HEAD_DIM, tq), lambda p, i, *_: (0, p, i)),
        scratch_shapes=[pltpu.VMEM((n_chains, 2, tk, tq), F32), pltpu.VMEM((n_chains, 2, tk, tq), BF16),
                        pltpu.VMEM((n_chains, 2, 1, tq), F32), pltpu.VMEM((n_chains, 2, 1, tq), F32),
                        pltpu.VMEM((n_chains, 1, tq), F32), pltpu.VMEM((n_chains, V_AUG, tq), F32)])
    return pl.pallas_call(
        functools.partial(_fox_kernel, tq=tq, tk=tk, nb=b),
        grid_spec=grid_spec,
        out_shape=jax.ShapeDtypeStruct((b, N_HEADS * HEAD_DIM, s), F32),
        compiler_params=_params(2),
        name="fox_attention",
    )(fend, qmax, kmax, q, qaug, k, kaug, vt)


def _sb_kernel(q_ref, k_ref, vt_ref, o_ref, *, tq, nb):
    i = pl.program_id(0)
    tk = tq
    chains = [(b, hd) for b in range(nb) for hd in range(N_HEADS)]
    lane = _iota((tq, LANES), 1)
    q = [jnp.where(lane // HEAD_DIM == hd % HEADS_PER_PAIR, q_ref[b, hd // HEADS_PER_PAIR],
                   jnp.zeros((), BF16)) for b, hd in chains]
    later = (_iota((tk, tk), 0) < _iota((tk, tk), 1)).astype(BF16)

    def each(f, *xs):
        return [f(*a) for a in zip(*xs)]

    def block(j, rest_q, acc, masked):
        ks = pl.multiple_of(j * tk, tk)
        kp = {(b, pr): k_ref[b, pr, pl.ds(ks, tk), :] for b in range(nb) for pr in range(N_PAIRS)}
        z = [_dot_nt(kp[(b, hd // HEADS_PER_PAIR)], qq)
             for (b, hd), qq in zip(chains, q)]
        log_keep = each(lambda x: _log2_sigmoid_of_log2(-x), z)
        if masked:
            mask = _iota((tk, tq), 0) < _iota((tk, tq), 1)
            log_keep = each(lambda x: jnp.where(mask, x, 0.0), log_keep)
        rest_in = each(lambda x: _dot(later, x.astype(BF16)), log_keep)
        att = each(lambda x, lk, ri, rq: jnp.exp2(x + lk + ri + rq), z, log_keep, rest_in, rest_q)
        if masked:
            att = each(lambda x: jnp.where(mask, x, 0.0), att)
        pv = [_dot(vt_ref[b, hd * HEAD_DIM:(hd + 1) * HEAD_DIM, pl.ds(ks, tk)], a.astype(BF16))
              for (b, hd), a in zip(chains, att)]
        acc = each(lambda a, x: a + x, acc, pv)
        rest_q = each(lambda rq, ri, lk: rq + ri[0:1, :] + lk[0:1, :], rest_q, rest_in, log_keep)
        return rest_q, acc

    rest_q, acc = block(i, [jnp.zeros((1, tq), F32)] * len(chains),
                        [jnp.zeros((HEAD_DIM, tq), F32)] * len(chains), True)

    def cond(c):
        j, rest_q, _ = c
        alive = functools.reduce(jnp.maximum, rest_q)
        return jnp.logical_and(j >= 0, jnp.max(alive) > SB_SKIP_LOG2)

    def body(c):
        j, rest_q, acc = c
        rest_q, acc = block(j, list(rest_q), list(acc), False)
        return j - 1, tuple(rest_q), tuple(acc)

    _, _, acc = lax.while_loop(cond, body, (i - 1, tuple(rest_q), tuple(acc)))
    for b in range(nb):
        o_ref[b] = jnp.concatenate(list(acc[b * N_HEADS:(b + 1) * N_HEADS]), axis=0)


def _sb_attention(q, k, vt, *, tq):
    b, _, s, _ = q.shape
    h = N_HEADS
    return pl.pallas_call(
        functools.partial(_sb_kernel, tq=tq, nb=b),
        grid=(s // tq,),
        in_specs=[pl.BlockSpec((b, N_PAIRS, tq, LANES), lambda i: (0, 0, i, 0)),
                  pl.BlockSpec((b, N_PAIRS, s, LANES), lambda i: (0, 0, 0, 0)),
                  pl.BlockSpec((b, h * HEAD_DIM, s), lambda i: (0, 0, 0))],
        out_specs=pl.BlockSpec((b, h * HEAD_DIM, tq), lambda i: (0, 0, i)),
        out_shape=jax.ShapeDtypeStruct((b, h * HEAD_DIM, s), F32),
        compiler_params=_params(1),
        name="sb_attention",
    )(q, k, vt)


def _rwkv_kernel(p_ref, m_ref, w0_ref, w2_ref, a0_ref, a2_ref,
                 kk_ref, ka_ref, rk_ref, lng_ref, lnb_ref, bdm_ref, o_ref,
                 r_s, k_s, v_s, kn_s, al_s, lw_s,
                 wt_s, u0_s, o0_s, mrb_s, rt_s, bh_s, kh_s, pc_s, y_s, ht_s, *, tt, nb):
    c = RWKV_CHUNK
    g = D_GROUP
    n_chunks = tt // c

    @pl.when(pl.program_id(0) == 0)
    def _():
        ht_s[...] = jnp.zeros_like(ht_s)

    bdm = bdm_ref[...]
    bdm_f = bdm.astype(F32)

    for b in range(nb):
        p = p_ref[b]
        misc = m_ref[b]
        k = p[:, g:2 * g]

        w = -_softplus(-(w0_ref[...] + _dot(jnp.tanh(misc).astype(BF16), w2_ref[...]))) - 0.5
        alpha = _sigmoid(a0_ref[...] + _dot(misc.astype(BF16), a2_ref[...]))
        kn = k * kk_ref[...]
        ss = _dot_sel_rhs(kn * kn, bdm)
        r_s[b] = p[:, 0:g]
        k_s[b] = k * (1.0 + (alpha - 1.0) * ka_ref[...])
        v_s[b] = p[:, 2 * g:3 * g]
        kn_s[b] = kn * lax.rsqrt(jnp.maximum(ss, 1e-12))
        al_s[b] = alpha
        lw_s[b] = -jnp.exp(w)

    row = _iota((c, g), 0)
    col = _iota((c, g), 1) % c
    strict = col < row
    incl = col <= row
    eye = (col == row).astype(F32)
    lower_c = (_iota((c, c), 1) <= _iota((c, c), 0)).astype(BF16)
    level_masks = []
    m = 1
    while m < c:
        level_masks.append(jnp.logical_and(
            strict, jnp.logical_and(row // (2 * m) == col // (2 * m), row // m != col // m)))
        m *= 2

    def bd(x):
        return jnp.concatenate([x.astype(BF16)] * N_HEADS, axis=0) * bdm

    def mm(a, b_bf16):
        return _dot(a.astype(BF16), b_bf16)

    def mm_nt(a, b_bf16):
        return _dot_nt(a.astype(BF16), b_bf16)

    def each(f, *xs):
        return [f(*a) for a in zip(*xs)]

    def phase_a(chains, side):
        def tick():
            next(side, None)

        sls = [pl.ds(ci * c, c) for _, ci in chains]
        ld = lambda ref: [ref[b, sl, :] for (b, _), sl in zip(chains, sls)]
        r_c, k_c, v_c, kn_c, al_c, lw_c = (ld(s) for s in (r_s, k_s, v_s, kn_s, al_s, lw_s))
        cl = each(lambda x: _dot_sel_lhs(lower_c, x), lw_c)
        tick()
        cl_last = each(lambda x: x[c - 1:c, :], cl)
        a_t = each(lambda kn, x, lw: -kn * jnp.exp(x - lw), kn_c, cl, lw_c)
        r_t = each(lambda r, x: r * jnp.exp(x), r_c, cl)
        q_inv = each(lambda x: jnp.exp(-x), cl)
        p_rem = each(lambda xl, x: jnp.exp(xl - x), cl_last, cl)
        kna = each(lambda kn, al: kn * al, kn_c, al_c)
        ar = each(lambda a, r: jnp.concatenate([a, r], axis=0), a_t, r_t)
        s_b = each(lambda x, kb, qi: mm_nt(x, bd(kb * qi)), ar, kna, q_inv)
        tick()
        s_k = each(lambda x, kk, qi: mm_nt(x, bd(kk * qi)), ar, k_c, q_inv)
        tick()
        n = each(lambda x: jnp.where(strict, x[0:c], 0.0), s_b)
        a_ak = each(lambda x: jnp.where(strict, x[0:c], 0.0), s_k)
        m_rb = each(lambda x: jnp.where(incl, x[c:2 * c], 0.0), s_b)
        m_rk = each(lambda x: jnp.where(incl, x[c:2 * c], 0.0), s_k)

        inv = each(lambda x: eye + jnp.where(level_masks[0], x, 0.0), n)
        for lm in level_masks[1:]:
            half = each(lambda d, x: mm(d, bd(jnp.where(lm, x, 0.0))), inv, n)
            tick()
            inv = each(lambda d, hf: d + mm(hf, bd(d)), inv, half)
            tick()

        v_bd = each(bd, v_c)
        akv = each(mm, a_ak, v_bd)
        tick()
        wt = each(lambda d, a: mm(d, bd(a)), inv, a_t)
        tick()
        u0 = each(lambda d, x: mm(d, bd(x)), inv, akv)
        tick()
        o0 = each(mm, m_rk, v_bd)
        bh = each(lambda x, p: x * p, kna, p_rem)
        kh = each(lambda x, p: x * p, k_c, p_rem)
        for _ in side:
            pass
        for ref, vals in zip((wt_s, u0_s, o0_s, mrb_s, rt_s, bh_s, kh_s),
                             (wt, u0, o0, m_rb, r_t, bh, kh)):
            for (b, _), sl, val in zip(chains, sls, vals):
                ref[b, sl, :] = val
        for (b, ci), xl in zip(chains, cl_last):
            pc_s[b, pl.ds(ci * SUBLANES, SUBLANES), :] = jnp.broadcast_to(jnp.exp(xl), (SUBLANES, g))

    def phase_b_steps(chunk_ids):
        bs = list(range(nb))
        for ci in chunk_ids:
            sl = pl.ds(ci * c, c)
            ht = [ht_s[b] for b in bs]
            wr = [jnp.concatenate([wt_s[b, sl, :], rt_s[b, sl, :]], axis=0) for b in bs]
            wrh = each(lambda x, hh: mm_nt(x, hh.astype(BF16)), wr, ht)
            yield
            u = [x[0:c] + u0_s[b, sl, :] for x, b in zip(wrh, bs)]
            uv_t = [jnp.concatenate([x, v_s[b, sl, :]], axis=0).T for x, b in zip(u, bs)]
            bk = [jnp.concatenate([bh_s[b, sl, :], kh_s[b, sl, :]], axis=0) for b in bs]
            upd = each(lambda x, y: mm(x, y.astype(BF16)), uv_t, bk)
            mu_ = [mm(mrb_s[b, sl, :], bd(x)) for x, b in zip(u, bs)]
            yield
            for b in bs:
                p_c = pc_s[b, pl.ds(ci * SUBLANES, 1), :]
                y_s[b, sl, :] = wrh[b][c:2 * c] + mu_[b] + o0_s[b, sl, :]
                ht_s[b] = ht[b] * p_c + bdm_f * upd[b]
            yield

    pending = iter(())
    for gi in range(n_chunks // RWKV_UNROLL):
        chunk_ids = range(gi * RWKV_UNROLL, (gi + 1) * RWKV_UNROLL)
        phase_a([(b, ci) for ci in chunk_ids for b in range(nb)], pending)
        pending = phase_b_steps(chunk_ids)
    for _ in pending:
        pass

    inv_n = 1.0 / HEAD_DIM
    for b in range(nb):
        y = y_s[b]
        mean = _dot_sel_rhs(y, bdm) * inv_n
        yc = y - mean
        var = _dot_sel_rhs(yc * yc, bdm) * inv_n
        yn = yc * lax.rsqrt(var + GN_EPS) * lng_ref[...] + lnb_ref[...]
        bonus = _dot_sel_rhs(r_s[b] * k_s[b] * rk_ref[...], bdm) * v_s[b]
        o_ref[b] = yn + bonus


def _rwkv(p_rkv, misc, prm, bdm, *, tt):
    b, s, _ = p_rkv.shape
    g = D_GROUP
    seq = lambda width: pl.BlockSpec((b, tt, width), lambda ti: (0, ti, 0))
    big = pltpu.VMEM((b, tt, g), F32)
    return pl.pallas_call(
        functools.partial(_rwkv_kernel, tt=tt, nb=b),
        grid=(s // tt,),
        in_specs=[seq(N_RKV), seq(LANES)] + [_full_spec(a) for a in prm] + [_full_spec(bdm)],
        out_specs=seq(g),
        out_shape=jax.ShapeDtypeStruct((b, s, g), F32),
        scratch_shapes=[big] * 13
        + [pltpu.VMEM((b, tt // RWKV_CHUNK * SUBLANES, g), F32), big,
           pltpu.VMEM((b, g, g), F32)],
        compiler_params=_params(1),
        name="rwkv7",
    )(p_rkv, misc, *prm, bdm)


def _lru_tile(x_ref, cw_ref, cb_ref, wa_ref, ba_ref, wx_ref, bx_ref, lam_ref, o_ref,
              pad, a_s, u_s, h_s, *, tt, nb):
    @pl.when(pl.program_id(0) == 0)
    def _():
        pad[:, 0:SUBLANES, :] = jnp.zeros((nb, SUBLANES, D_GROUP), F32)
        h_s[...] = jnp.zeros_like(h_s)

    for b in range(nb):
        x = x_ref[b]
        pad[b, SUBLANES:SUBLANES + tt, :] = x
        xc = cw_ref[CONV_WIDTH - 1:CONV_WIDTH, :] * x + cb_ref[...]
        for d in range(1, CONV_WIDTH):
            tap = CONV_WIDTH - 1 - d
            xc = xc + cw_ref[tap:tap + 1, :] * pad[b, SUBLANES - d:SUBLANES - d + tt, :]
        pad[b, 0:SUBLANES, :] = x[tt - SUBLANES:tt, :]

        xb = xc.astype(BF16)
        r = _sigmoid(_dot(xb, wa_ref[...]) + ba_ref[...])
        i = _sigmoid(_dot(xb, wx_ref[...]) + bx_ref[...])
        log_a = -LRU_C * r * _softplus(-lam_ref[...])
        a_s[b] = jnp.exp(log_a)
        th = jnp.tanh(log_a)
        u_s[b] = jnp.sqrt(-2.0 * th / (1.0 - th)) * (i * xc)

    row = _iota((SUBLANES, D_GROUP), 0)

    def rows_above(x, k, fill):
        return jnp.where(row >= k, pltpu.roll(x, k, axis=0), fill)

    def group(gi, hs):
        base = pl.multiple_of(gi * SUBLANES, SUBLANES)
        carry = []
        for b in range(nb):
            a = a_s[b, pl.ds(base, SUBLANES), :]
            u = u_s[b, pl.ds(base, SUBLANES), :]
            k = 1
            while k < SUBLANES:
                u = a * rows_above(u, k, 0.0) + u
                a = a * rows_above(a, k, 1.0)
                k *= 2
            h = a * hs[b] + u
            o_ref[b, pl.ds(base, SUBLANES), :] = h
            carry.append(h[SUBLANES - 1:SUBLANES, :])
        return tuple(carry)

    hs = lax.fori_loop(0, tt // SUBLANES, group, tuple(h_s[b] for b in range(nb)))
    for b in range(nb):
        h_s[b] = hs[b]


def _outproj_kernel(x_ref, yft_ref, yst_ref, yr_ref, lx_ref, gates_ref, w_ref, fg_ref,
                    cw_ref, cb_ref, wa_ref, ba_ref, wx_ref, bx_ref, lam_ref, o_ref,
                    pad, a_s, u_s, yl_s, h_s, *, final, tm, nb):
    _lru_tile(lx_ref, cw_ref, cb_ref, wa_ref, ba_ref, wx_ref, bx_ref, lam_ref, yl_s,
              pad, a_s, u_s, h_s, tt=tm, nb=nb)
    for b in range(nb):
        acc = x_ref[b]
        ys = (yft_ref[b].T, yst_ref[b].T, yr_ref[b], yl_s[b])
        for gi, y in enumerate(ys):
            gate = gates_ref[b, :, gi * D_GROUP:(gi + 1) * D_GROUP]
            y = y * (gate * _sigmoid(gate))
            acc = acc + _dot(y.astype(BF16), w_ref[gi * D_GROUP:(gi + 1) * D_GROUP, :])
        if final:
            ms = jnp.mean(acc * acc, axis=-1, keepdims=True)
            acc = acc * lax.rsqrt(ms + RMS_EPS) * fg_ref[...]
        o_ref[b] = acc


def _outproj(x, y_fox_t, y_sb_t, y_rw, lx, gates, w, final_g, lru_prm, *, tm, final):
    b, s, _ = x.shape
    g = D_GROUP
    seq = lambda width: pl.BlockSpec((b, tm, width), lambda i: (0, i, 0))
    chan = pl.BlockSpec((b, g, tm), lambda i: (0, 0, i))
    big = pltpu.VMEM((b, tm, g), F32)
    return pl.pallas_call(
        functools.partial(_outproj_kernel, final=final, tm=tm, nb=b),
        grid=(s // tm,),
        in_specs=[seq(D_MODEL), chan, chan, seq(g), seq(g), seq(4 * g),
                  _full_spec(w), _full_spec(final_g)] + [_full_spec(a) for a in lru_prm],
        out_specs=seq(D_MODEL),
        out_shape=jax.ShapeDtypeStruct((b, s, D_MODEL), F32),
        scratch_shapes=[pltpu.VMEM((b, tm + SUBLANES, g), F32), big, big, big,
                        pltpu.VMEM((b, 1, g), F32)],
        compiler_params=_params(1),
        name="outproj",
    )(x, y_fox_t, y_sb_t, y_rw, lx, gates, w, final_g, *lru_prm)


def _w_in_segments():
    g, h, r = D_GROUP, N_HEADS, RWKV_LORA
    o_ff = 4 * g
    o_sb = o_ff + h
    o_rw = o_sb + 4 * g
    o_rg = o_rw + 3 * g + 2 * r
    o_lx = o_rg + g
    o_lg = o_lx + g
    segs = [(0, C_FQ, 2 * g), (o_sb, C_SQ, 2 * g), (2 * g, C_FV, g), (o_sb + 2 * g, C_SV, g),
             (3 * g, C_GATES, g), (o_sb + 3 * g, C_GATES + g, g), (o_rg, C_GATES + 2 * g, g),
             (o_lg, C_GATES + 3 * g, g),
             (o_rw, C_RKV, 3 * g), (o_lx, C_LX, g),
             (o_rw + 3 * g, C_MISC, 2 * r), (o_ff, C_MISC + FF_LANE, h)]
    return segs


def _w_in_kernel(w_ref, o_ref):
    o_ref[...] = jnp.zeros_like(o_ref)
    for src, dst, width in _w_in_segments():
        o_ref[0, :, dst:dst + width] = w_ref[0, :, src:src + width].astype(BF16)


def _permute_w_in(w_in, *, tr):
    depth, d, n_in = w_in.shape
    return pl.pallas_call(
        _w_in_kernel,
        grid=(depth, d // tr),
        in_specs=[pl.BlockSpec((1, tr, n_in), lambda l, i: (l, i, 0))],
        out_specs=pl.BlockSpec((1, tr, N_PROJ), lambda l, i: (l, i, 0)),
        out_shape=jax.ShapeDtypeStruct((depth, d, N_PROJ), BF16),
        compiler_params=_params(2),
        name="w_in_layout",
    )(w_in)


def _block_diag(w):
    h, n, _ = w.shape
    eye = jnp.eye(h, dtype=w.dtype)
    return jnp.einsum('hij,hk->hikj', w, eye).reshape(h * n, h * n)


def _pick_tile(s, pref):
    t = pref
    while s % t:
        t //= 2
    return t


def kernel(x, norm_g, w_in, b_forget, rwkv_mu, rwkv_w0, rwkv_w2, rwkv_a0, rwkv_a2, rwkv_k_k,
           rwkv_k_a, rwkv_r_k, rwkv_ln_g, rwkv_ln_b, lru_conv_w, lru_conv_b, lru_w_a, lru_b_a,
           lru_w_x, lru_b_x, lru_lambda, w_out, final_g):
    b, s, d = x.shape
    depth = w_in.shape[0]
    g, h, dh, r = D_GROUP, N_HEADS, HEAD_DIM, RWKV_LORA
    tm = _pick_tile(s, 512)
    tq = _pick_tile(s, 256)
    tt = _pick_tile(s, 512)
    row = lambda a: a.reshape(1, -1).astype(F32)

    bdm = _block_diag(jnp.ones((h, dh, dh), BF16))
    w_in_k = _permute_w_in(w_in, tr=LANES)
    for l in range(depth):
        fbias = jnp.zeros((1, LANES), F32).at[0, FF_LANE:FF_LANE + h].set(b_forget[l])
        mu = rwkv_mu[l]
        consts = [row(norm_g[l]), w_in_k, fbias, row(mu[:N_RKV]),
                  jnp.zeros((1, LANES), F32).at[0, :2 * r].set(mu[N_RKV:])]
        (fq, fk, qaug, kaug, fvt, sq, sk, svt, gates, rkv, lx, misc, f2,
         norms) = _inproj(x, consts, l, tm=tm)

        per_head = lambda a: a.transpose(0, 2, 1).reshape(-1)
        fend = per_head(f2[:, tq - 1::tq, FF_LANE:FF_LANE + h])
        qmax = per_head(jnp.repeat(jnp.sqrt(norms[:, :, 0, 0:h]), tm // tq, axis=1))
        kmax = jnp.sqrt(jnp.max(norms[:, :, 0, h:2 * h], axis=1)).reshape(-1)
        y_fox_t = _fox_attention(fend, qmax, kmax, fq, qaug, fk, kaug, fvt, tq=tq, tk=tq)
        y_sb_t = _sb_attention(sq, sk, svt, tq=tq)

        pad_rows = lambda a, lo: jnp.zeros((LANES, g), F32).at[lo:lo + r].set(a).astype(BF16)
        rw_prm = [row(rwkv_w0[l]), pad_rows(rwkv_w2[l], 0), row(rwkv_a0[l]), pad_rows(rwkv_a2[l], r),
                  row(rwkv_k_k[l]), row(rwkv_k_a[l]), row(rwkv_r_k[l]), row(rwkv_ln_g[l]),
                  row(rwkv_ln_b[l])]
        y_rw = _rwkv(rkv, misc, rw_prm, bdm, tt=tt)

        lru_prm = [lru_conv_w[l].astype(F32), row(lru_conv_b[l]),
                   _block_diag(lru_w_a[l]).astype(BF16), row(lru_b_a[l]),
                   _block_diag(lru_w_x[l]).astype(BF16), row(lru_b_x[l]), row(lru_lambda[l])]
        x = _outproj(x, y_fox_t, y_sb_t, y_rw, lx, gates, w_out[l].astype(BF16), row(final_g),
                     lru_prm, tm=tm, final=(l == depth - 1))
    return x
```

```python
import functools

import jax
import jax.numpy as jnp
from jax import lax
from jax.experimental import pallas as pl
from jax.experimental.pallas import tpu as pltpu

F32 = jnp.float32
BF16 = jnp.bfloat16

D_MODEL = 1024
D_GROUP = 256
N_HEADS = 4
HEAD_DIM = 64
RWKV_LORA = 32
CONV_WIDTH = 4
LRU_C = 8.0
RMS_EPS = 1e-6
GN_EPS = 64e-5
N_RKV = 3 * D_GROUP

LANES = 128
SUBLANES = 8
VMEM_LIMIT_BYTES = 56 * 1024 * 1024

C_FQ = 0
C_FK = C_FQ + D_GROUP
C_SQ = C_FK + D_GROUP
C_SK = C_SQ + D_GROUP
C_FV = C_SK + D_GROUP
C_SV = C_FV + D_GROUP
C_GATES = C_SV + D_GROUP
C_RKV = C_GATES + 4 * D_GROUP
C_LX = C_RKV + N_RKV
C_MISC = C_LX + D_GROUP
N_PROJ = C_MISC + LANES
FF_LANE = 2 * RWKV_LORA
HEADS_PER_PAIR = LANES // HEAD_DIM
N_PAIRS = N_HEADS // HEADS_PER_PAIR
AUG_STRIDE = 8
AUG_K_F = 0
AUG_K_ONE = 3
AUG_Q_ONE = 0
AUG_Q_F = 3

CUMSUM_BLOCK = LANES
V_AUG = HEAD_DIM + 16
LOG2E = 1.4426950408889634

FOX_SKIP_LOG2 = -150.0
FOX_NORM_MARGIN = 1.02
FOX_TILES_PER_STEP = 2
FOX_F_SLACK = 1.0

NEG_BIG = -1e30
SB_SKIP_LOG2 = -150.0
RWKV_CHUNK = 64
RWKV_UNROLL = 4


def _dot(a, b):
    return jnp.dot(a, b, preferred_element_type=F32)


def _dot_nt(a, b):
    return lax.dot_general(a, b, (((1,), (1,)), ((), ())), preferred_element_type=F32)


def _split3(x):
    hi = x.astype(BF16)
    r1 = x - hi.astype(F32)
    mid = r1.astype(BF16)
    lo = (r1 - mid.astype(F32)).astype(BF16)
    return hi, mid, lo


def _dot_sel_lhs(sel, x):
    hi, mid, lo = _split3(x)
    return _dot(sel, hi) + (_dot(sel, mid) + _dot(sel, lo))


def _dot_sel_rhs(x, sel):
    return _dot(x.astype(BF16), sel)


def _softplus(x):
    return jnp.maximum(x, 0.0) + jnp.log(1.0 + jnp.exp(-jnp.abs(x)))


def _log_sigmoid(x):
    return jnp.minimum(x, 0.0) - jnp.log(1.0 + jnp.exp(-jnp.abs(x)))


def _log2_sigmoid_of_log2(x2):
    return jnp.minimum(x2, 0.0) - jnp.log2(1.0 + jnp.exp2(-jnp.abs(x2)))


def _sigmoid(x):
    return 1.0 / (1.0 + jnp.exp(-x))


def _iota(shape, dim):
    return lax.broadcasted_iota(jnp.int32, shape, dim)


def _full_spec(a):
    return pl.BlockSpec(a.shape, lambda *_: (0,) * a.ndim)


def _params(n_grid):
    return pltpu.CompilerParams(dimension_semantics=("arbitrary",) * n_grid,
                                vmem_limit_bytes=VMEM_LIMIT_BYTES)


def _inproj_kernel(x_ref, g_ref, w_ref, fb_ref, mu_rkv_ref, mu_misc_ref,
                   fq_ref, fk_ref, qaug_ref, kaug_ref, fvt_ref, sq_ref, sk_ref, svt_ref, gates_ref,
                   rkv_ref, lx_ref, misc_ref, f2_ref, norms_ref, ftot, v_stage, pad_rkv, pad_misc):
    @pl.when(pl.program_id(1) == 0)
    def _():
        ftot[...] = jnp.zeros_like(ftot)
        pad_rkv[0:SUBLANES, :] = jnp.zeros((SUBLANES, N_RKV), F32)
        pad_misc[0:SUBLANES, :] = jnp.zeros((SUBLANES, LANES), F32)

    x = x_ref[0]
    tm = x.shape[0]
    ms = jnp.mean(x * x, axis=-1, keepdims=True)
    h = (x * lax.rsqrt(ms + RMS_EPS) * g_ref[...]).astype(BF16)
    scale = HEAD_DIM ** -0.5

    def proj(c0, width):
        return _dot(h, w_ref[0, :, c0:c0 + width])

    misc = proj(C_MISC, LANES)
    gates_ref[0] = proj(C_GATES, 4 * D_GROUP)
    lx_ref[0] = proj(C_LX, D_GROUP)

    for val, pad, mu_ref, out_ref in ((proj(C_RKV, N_RKV), pad_rkv, mu_rkv_ref, rkv_ref),
                                      (misc, pad_misc, mu_misc_ref, misc_ref)):
        pad[SUBLANES:SUBLANES + tm, :] = val
        prev = pad[SUBLANES - 1:SUBLANES - 1 + tm, :]
        pad[0:SUBLANES, :] = val[tm - SUBLANES:tm, :]
        out_ref[0] = val + (prev - val) * mu_ref[...]

    lf = _log_sigmoid(misc + fb_ref[...])
    cb = CUMSUM_BLOCK
    lower = (_iota((cb, cb), 1) <= _iota((cb, cb), 0)).astype(BF16)
    run = ftot[...]
    blocks = []
    for r0 in range(0, tm, cb):
        blk = _dot_sel_lhs(lower, lf[r0:r0 + cb, :]) + run
        run = blk[cb - 1:cb, :]
        blocks.append(blk)
    ftot[...] = run
    f = jnp.concatenate(blocks, axis=0)
    f2 = f * LOG2E
    f2_ref[0] = f2
    hi = f2.astype(BF16).astype(F32)
    mid = (f2 - hi).astype(BF16).astype(F32)
    lo = (f2 - hi - mid).astype(BF16).astype(F32)

    pq = proj(C_FQ, D_GROUP) * (scale * LOG2E)
    pk = proj(C_FK, D_GROUP)
    sq = proj(C_SQ, D_GROUP) * (scale * LOG2E)
    sk = proj(C_SK, D_GROUP)
    for pr in range(N_PAIRS):
        cols = slice(pr * LANES, (pr + 1) * LANES)
        fq_ref[0, pr] = pq[:, cols].astype(BF16)
        fk_ref[0, pr] = pk[:, cols].astype(BF16)
        sq_ref[0, pr] = sq[:, cols].astype(BF16)
        sk_ref[0, pr] = sk[:, cols].astype(BF16)

    lane = _iota((tm, LANES), 1)
    in_group = lane % AUG_STRIDE
    valid = lane < AUG_STRIDE * N_HEADS
    aug_q = jnp.where(jnp.logical_and(valid, jnp.logical_and(in_group >= AUG_Q_ONE,
                                                             in_group < AUG_Q_ONE + 3)), 1.0, 0.0)
    aug_k = jnp.where(jnp.logical_and(valid, jnp.logical_and(in_group >= AUG_K_ONE,
                                                             in_group < AUG_K_ONE + 3)), 1.0, 0.0)
    for hd in range(N_HEADS):
        for i3, piece in enumerate((hi, mid, lo)):
            col = jnp.broadcast_to(piece[:, FF_LANE + hd:FF_LANE + hd + 1], (tm, LANES))
            aug_q = jnp.where(lane == AUG_STRIDE * hd + AUG_Q_F + i3, col, aug_q)
            aug_k = jnp.where(lane == AUG_STRIDE * hd + AUG_K_F + i3, -col, aug_k)
    qaug_ref[0] = aug_q.astype(BF16)
    kaug_ref[0] = aug_k.astype(BF16)

    head_of_row = _iota((D_GROUP, LANES), 0) // HEAD_DIM
    nlane = _iota((SUBLANES, LANES), 1)
    norms = jnp.zeros((SUBLANES, LANES), F32)
    for side, pv in enumerate((pq, pk)):
        sel = (head_of_row + side * N_HEADS == _iota((D_GROUP, LANES), 1)).astype(BF16)
        top = jnp.max(_dot((pv * pv).astype(BF16), sel), axis=0, keepdims=True)
        keep = jnp.logical_and(nlane >= side * N_HEADS, nlane < (side + 1) * N_HEADS)
        norms = jnp.where(keep, top, norms)
    norms_ref[0, 0] = norms
    v_stage[...] = proj(C_SV, D_GROUP)
    svt_ref[0] = v_stage[...].T.astype(BF16)
    v_stage[...] = proj(C_FV, D_GROUP)
    vt = v_stage[...].T.astype(BF16)
    ones_rows = (_iota((V_AUG - HEAD_DIM, tm), 0) == 0).astype(BF16)
    for hd in range(N_HEADS):
        fvt_ref[0, hd * V_AUG:hd * V_AUG + HEAD_DIM, :] = vt[hd * HEAD_DIM:(hd + 1) * HEAD_DIM, :]
        fvt_ref[0, hd * V_AUG + HEAD_DIM:(hd + 1) * V_AUG, :] = ones_rows


def _inproj(x, consts, layer, *, tm):
    b, s, _ = x.shape
    w_spec = pl.BlockSpec((1,) + consts[1].shape[1:], lambda bi, i: (layer, 0, 0))
    seq = lambda width: pl.BlockSpec((1, tm, width), lambda bi, i: (bi, i, 0))
    pairs = pl.BlockSpec((1, N_PAIRS, tm, LANES), lambda bi, i: (bi, 0, i, 0))
    pairs_shape = jax.ShapeDtypeStruct((b, N_PAIRS, s, LANES), BF16)
    chan = lambda rows: pl.BlockSpec((1, rows, tm), lambda bi, i: (bi, 0, i))
    chan_shape = lambda rows: jax.ShapeDtypeStruct((b, rows, s), BF16)
    seq_shape = lambda width, dtype=F32: jax.ShapeDtypeStruct((b, s, width), dtype)
    return pl.pallas_call(
        _inproj_kernel,
        grid=(b, s // tm),
        in_specs=[seq(D_MODEL), _full_spec(consts[0]), w_spec] + [_full_spec(a) for a in consts[2:]],
        out_specs=[pairs, pairs, seq(LANES), seq(LANES), chan(N_HEADS * V_AUG),
                   pairs, pairs, chan(D_GROUP),
                   seq(4 * D_GROUP), seq(N_RKV), seq(D_GROUP), seq(LANES), seq(LANES),
                   pl.BlockSpec((1, 1, SUBLANES, LANES), lambda bi, i: (bi, i, 0, 0))],
        out_shape=[pairs_shape, pairs_shape, seq_shape(LANES, BF16), seq_shape(LANES, BF16),
                   chan_shape(N_HEADS * V_AUG), pairs_shape, pairs_shape, chan_shape(D_GROUP),
                   seq_shape(4 * D_GROUP), seq_shape(N_RKV), seq_shape(D_GROUP), seq_shape(LANES),
                   seq_shape(LANES), jax.ShapeDtypeStruct((b, s // tm, SUBLANES, LANES), F32)],
        scratch_shapes=[pltpu.VMEM((1, LANES), F32), pltpu.VMEM((tm, D_GROUP), F32),
                        pltpu.VMEM((tm + SUBLANES, N_RKV), F32),
                        pltpu.VMEM((tm + SUBLANES, LANES), F32)],
        compiler_params=_params(2),
        name="inproj",
    )(x, *consts)


def _fox_kernel(fend_ref, qmax_ref, kmax_ref, q_ref, qaug_ref, k_ref, kaug_ref, vt_ref, o_ref,
                s_scr, p_scr, mx_scr, al_scr, m_scr, acc_scr, *, tq, tk, nb):
    pair = pl.program_id(0)
    chains = [(b, hh) for b in range(nb) for hh in range(HEADS_PER_PAIR)]
    ids = list(range(len(chains)))
    for sub in range(FOX_TILES_PER_STEP):
        _fox_tile(fend_ref, qmax_ref, kmax_ref, q_ref, qaug_ref, k_ref, kaug_ref, vt_ref, o_ref,
                  s_scr, p_scr, mx_scr, al_scr, m_scr, acc_scr, tq=tq, tk=tk, nb=nb, pair=pair,
                  chains=chains, ids=ids, sub=sub)


def _fox_tile(fend_ref, qmax_ref, kmax_ref, q_ref, qaug_ref, k_ref, kaug_ref, vt_ref, o_ref,
              s_scr, p_scr, mx_scr, al_scr, m_scr, acc_scr, *, tq, tk, nb, pair, chains, ids, sub):
    i = pl.program_id(1) * FOX_TILES_PER_STEP + sub
    nq = pl.num_programs(1) * FOX_TILES_PER_STEP
    cols = slice(sub * tq, (sub + 1) * tq)
    lane = _iota((tq, LANES), 1)
    q = []
    for b, hh in chains:
        own_half = lane // HEAD_DIM == hh
        own_aug = lane // AUG_STRIDE == pair * HEADS_PER_PAIR + hh
        q.append(jnp.concatenate(
            [jnp.where(own_half, q_ref[b, 0, cols, :], jnp.zeros((), BF16)),
             jnp.where(own_aug, qaug_ref[b, cols, :], jnp.zeros((), BF16))], axis=1))
    n_full = (i * tq) // tk

    bhs = [b * N_HEADS + pair * HEADS_PER_PAIR + hh for b, hh in chains]
    rows = [bh * nq for bh in bhs]
    slack = [FOX_NORM_MARGIN * 2.0 * qmax_ref[r + i] * kmax_ref[bh]
             + fend_ref[r + jnp.maximum(i - 1, 0)] + FOX_F_SLACK - FOX_SKIP_LOG2
             for bh, r in zip(bhs, rows)]

    def some_chain_live(j):
        last_key = (jnp.maximum(j, 0) + 1) * (tk // tq) - 1
        live = [sl >= fend_ref[r + last_key] for sl, r in zip(slack, rows)]
        return jnp.logical_and(j >= 0, functools.reduce(jnp.logical_or, live))

    base = lax.while_loop(some_chain_live, lambda j: j - 1, n_full - 1) + 1
    n_eff = n_full - base

    def key_rows(j):
        return pl.ds(pl.multiple_of(j * tk, tk), tk)

    def scores_to(j, slot):
        kp = [jnp.concatenate([k_ref[b, 0, key_rows(j), :], kaug_ref[b, key_rows(j), :]], axis=1)
              for b in range(nb)]
        s = [_dot_nt(kp[b], q[c]) for c, (b, _) in zip(ids, chains)]
        for c in ids:
            s_scr[c, slot] = s[c]
            mx_scr[c, slot] = jnp.max(s[c], axis=0, keepdims=True)

    def weighted_values(j, slot):
        return [_dot(vt_ref[b, hh * V_AUG:(hh + 1) * V_AUG, key_rows(j)], p_scr[c, slot])
                for c, (b, hh) in zip(ids, chains)]

    def softmax_to(s, mx, slot):
        m_old = [m_scr[c] for c in ids]
        m_new = [jnp.maximum(a, b) for a, b in zip(m_old, mx)]
        p = [jnp.exp2(a - b).astype(BF16) for a, b in zip(s, m_new)]
        for c in ids:
            m_scr[c] = m_new[c]
            al_scr[c, slot] = jnp.exp2(m_old[c] - m_new[c])
            p_scr[c, slot] = p[c]

    def stage(local, cur):
        j = base + local
        nxt = 1 - cur
        pv_prev = weighted_values(jnp.maximum(j - 1, 0), nxt)
        scores_to(j + 1, nxt)
        softmax_to([s_scr[c, cur] for c in ids], [mx_scr[c, cur] for c in ids], cur)
        for c in ids:
            acc_scr[c] = al_scr[c, nxt] * acc_scr[c] + pv_prev[c]

    def tail(cur):
        nxt = 1 - cur
        pv_prev = weighted_values(jnp.maximum(n_full - 1, 0), nxt)
        mask = n_full * tk + _iota((tk, tq), 0) <= i * tq + _iota((tk, tq), 1)
        s = [jnp.where(mask, s_scr[c, cur], NEG_BIG) for c in ids]
        softmax_to(s, [jnp.max(x, axis=0, keepdims=True) for x in s], cur)
        pv_last = weighted_values(n_full, cur)
        for c, (b, hh) in zip(ids, chains):
            acc = al_scr[c, cur] * (al_scr[c, nxt] * acc_scr[c] + pv_prev[c]) + pv_last[c]
            o_ref[b, hh * HEAD_DIM:(hh + 1) * HEAD_DIM, cols] = (
                acc[0:HEAD_DIM] / acc[HEAD_DIM:HEAD_DIM + 1])

    m_scr[...] = jnp.full(m_scr.shape, NEG_BIG, F32)
    acc_scr[...] = jnp.zeros_like(acc_scr)
    p_scr[:, 1] = jnp.zeros((len(chains), tk, tq), BF16)
    al_scr[:, 1] = jnp.ones((len(chains), 1, tq), F32)
    scores_to(base, 0)

    def pair_of_stages(jj, carry):
        stage(2 * jj, 0)
        stage(2 * jj + 1, 1)
        return carry

    lax.fori_loop(0, n_eff // 2, pair_of_stages, 0)

    @pl.when(n_eff % 2 == 1)
    def _():
        stage(n_eff - 1, 0)
        tail(1)

    @pl.when(n_eff % 2 == 0)
    def _():
        tail(0)


def _fox_attention(fend, qmax, kmax, q, qaug, k, kaug, vt, *, tq, tk):
    b, _, s, _ = q.shape
    n_chains = b * HEADS_PER_PAIR
    pair_rows = HEADS_PER_PAIR * V_AUG
    tstep = tq * FOX_TILES_PER_STEP
    grid_spec = pltpu.PrefetchScalarGridSpec(
        num_scalar_prefetch=3,
        grid=(N_PAIRS, s // tstep),
        in_specs=[pl.BlockSpec((b, 1, tstep, LANES), lambda p, i, *_: (0, p, i, 0)),
                  pl.BlockSpec((b, tstep, LANES), lambda p, i, *_: (0, i, 0)),
                  pl.BlockSpec((b, 1, s, LANES), lambda p, i, *_: (0, p, 0, 0)),
                  pl.BlockSpec((b, s, LANES), lambda p, i, *_: (0, 0, 0)),
                  pl.BlockSpec((b, pair_rows, s), lambda p, i, *_: (0, p, 0))],
        out_specs=pl.BlockSpec((b, HEADS_PER_PAIR * HEAD_DIM, tstep), lambda p, i, *_: (0, p, i)),
        scratch_shapes=[pltpu.VMEM((n_chains, 2, tk, tq), F32), pltpu.VMEM((n_chains, 2, tk, tq), BF16),
                        pltpu.VMEM((n_chains, 2, 1, tq), F32), pltpu.VMEM((n_chains, 2, 1, tq), F32),
                        pltpu.VMEM((n_chains, 1, tq), F32), pltpu.VMEM((n_chains, V_AUG, tq), F32)])
    return pl.pallas_call(
        functools.partial(_fox_kernel, tq=tq, tk=tk, nb=b),
        grid_spec=grid_spec,
        out_shape=jax.ShapeDtypeStruct((b, N_HEADS * HEAD_DIM, s), F32),
        compiler_params=_params(2),
        name="fox_attention",
    )(fend, qmax, kmax, q, qaug, k, kaug, vt)


def _sb_kernel(q_ref, k_ref, vt_ref, o_ref, *, tq, nb):
    i = pl.program_id(0)
    tk = tq
    chains = [(b, hd) for b in range(nb) for hd in range(N_HEADS)]
    lane = _iota((tq, LANES), 1)
    q = [jnp.where(lane // HEAD_DIM == hd % HEADS_PER_PAIR, q_ref[b, hd // HEADS_PER_PAIR],
                   jnp.zeros((), BF16)) for b, hd in chains]
    later = (_iota((tk, tk), 0) < _iota((tk, tk), 1)).astype(BF16)

    def each(f, *xs):
        return [f(*a) for a in zip(*xs)]

    def block(j, rest_q, acc, masked):
        ks = pl.multiple_of(j * tk, tk)
        kp = {(b, pr): k_ref[b, pr, pl.ds(ks, tk), :] for b in range(nb) for pr in range(N_PAIRS)}
        z = [_dot_nt(kp[(b, hd // HEADS_PER_PAIR)], qq)
             for (b, hd), qq in zip(chains, q)]
        log_keep = each(lambda x: _log2_sigmoid_of_log2(-x), z)
        if masked:
            mask = _iota((tk, tq), 0) < _iota((tk, tq), 1)
            log_keep = each(lambda x: jnp.where(mask, x, 0.0), log_keep)
        rest_in = each(lambda x: _dot(later, x.astype(BF16)), log_keep)
        att = each(lambda x, lk, ri, rq: jnp.exp2(x + lk + ri + rq), z, log_keep, rest_in, rest_q)
        if masked:
            att = each(lambda x: jnp.where(mask, x, 0.0), att)
        pv = [_dot(vt_ref[b, hd * HEAD_DIM:(hd + 1) * HEAD_DIM, pl.ds(ks, tk)], a.astype(BF16))
              for (b, hd), a in zip(chains, att)]
        acc = each(lambda a, x: a + x, acc, pv)
        rest_q = each(lambda rq, ri, lk: rq + ri[0:1, :] + lk[0:1, :], rest_q, rest_in, log_keep)
        return rest_q, acc

    rest_q, acc = block(i, [jnp.zeros((1, tq), F32)] * len(chains),
                        [jnp.zeros((HEAD_DIM, tq), F32)] * len(chains), True)

    def cond(c):
        j, rest_q, _ = c
        alive = functools.reduce(jnp.maximum, rest_q)
        return jnp.logical_and(j >= 0, jnp.max(alive) > SB_SKIP_LOG2)

    def body(c):
        j, rest_q, acc = c
        rest_q, acc = block(j, list(rest_q), list(acc), False)
        return j - 1, tuple(rest_q), tuple(acc)

    _, _, acc = lax.while_loop(cond, body, (i - 1, tuple(rest_q), tuple(acc)))
    for b in range(nb):
        o_ref[b] = jnp.concatenate(list(acc[b * N_HEADS:(b + 1) * N_HEADS]), axis=0)


def _sb_attention(q, k, vt, *, tq):
    b, _, s, _ = q.shape
    h = N_HEADS
    return pl.pallas_call(
        functools.partial(_sb_kernel, tq=tq, nb=b),
        grid=(s // tq,),
        in_specs=[pl.BlockSpec((b, N_PAIRS, tq, LANES), lambda i: (0, 0, i, 0)),
                  pl.BlockSpec((b, N_PAIRS, s, LANES), lambda i: (0, 0, 0, 0)),
                  pl.BlockSpec((b, h * HEAD_DIM, s), lambda i: (0, 0, 0))],
        out_specs=pl.BlockSpec((b, h * HEAD_DIM, tq), lambda i: (0, 0, i)),
        out_shape=jax.ShapeDtypeStruct((b, h * HEAD_DIM, s), F32),
        compiler_params=_params(1),
        name="sb_attention",
    )(q, k, vt)


def _rwkv_kernel(p_ref, m_ref, w0_ref, w2_ref, a0_ref, a2_ref,
                 kk_ref, ka_ref, rk_ref, lng_ref, lnb_ref, bdm_ref, o_ref,
                 r_s, k_s, v_s, kn_s, al_s, lw_s,
                 wt_s, u0_s, o0_s, mrb_s, rt_s, bh_s, kh_s, pc_s, y_s, ht_s, *, tt, nb):
    c = RWKV_CHUNK
    g = D_GROUP
    n_chunks = tt // c

    @pl.when(pl.program_id(0) == 0)
    def _():
        ht_s[...] = jnp.zeros_like(ht_s)

    bdm = bdm_ref[...]
    bdm_f = bdm.astype(F32)

    for b in range(nb):
        p = p_ref[b]
        misc = m_ref[b]
        k = p[:, g:2 * g]

        w = -_softplus(-(w0_ref[...] + _dot(jnp.tanh(misc).astype(BF16), w2_ref[...]))) - 0.5
        alpha = _sigmoid(a0_ref[...] + _dot(misc.astype(BF16), a2_ref[...]))
        kn = k * kk_ref[...]
        ss = _dot_sel_rhs(kn * kn, bdm)
        r_s[b] = p[:, 0:g]
        k_s[b] = k * (1.0 + (alpha - 1.0) * ka_ref[...])
        v_s[b] = p[:, 2 * g:3 * g]
        kn_s[b] = kn * lax.rsqrt(jnp.maximum(ss, 1e-12))
        al_s[b] = alpha
        lw_s[b] = -jnp.exp(w)

    row = _iota((c, g), 0)
    col = _iota((c, g), 1) % c
    strict = col < row
    incl = col <= row
    eye = (col == row).astype(F32)
    lower_c = (_iota((c, c), 1) <= _iota((c, c), 0)).astype(BF16)
    level_masks = []
    m = 1
    while m < c:
        level_masks.append(jnp.logical_and(
            strict, jnp.logical_and(row // (2 * m) == col // (2 * m), row // m != col // m)))
        m *= 2

    def bd(x):
        return jnp.concatenate([x.astype(BF16)] * N_HEADS, axis=0) * bdm

    def mm(a, b_bf16):
        return _dot(a.astype(BF16), b_bf16)

    def mm_nt(a, b_bf16):
        return _dot_nt(a.astype(BF16), b_bf16)

    def each(f, *xs):
        return [f(*a) for a in zip(*xs)]

    def phase_a(chains, side):
        def tick():
            next(side, None)

        sls = [pl.ds(ci * c, c) for _, ci in chains]
        ld = lambda ref: [ref[b, sl, :] for (b, _), sl in zip(chains, sls)]
        r_c, k_c, v_c, kn_c, al_c, lw_c = (ld(s) for s in (r_s, k_s, v_s, kn_s, al_s, lw_s))
        cl = each(lambda x: _dot_sel_lhs(lower_c, x), lw_c)
        tick()
        cl_last = each(lambda x: x[c - 1:c, :], cl)
        a_t = each(lambda kn, x, lw: -kn * jnp.exp(x - lw), kn_c, cl, lw_c)
        r_t = each(lambda r, x: r * jnp.exp(x), r_c, cl)
        q_inv = each(lambda x: jnp.exp(-x), cl)
        p_rem = each(lambda xl, x: jnp.exp(xl - x), cl_last, cl)
        kna = each(lambda kn, al: kn * al, kn_c, al_c)
        ar = each(lambda a, r: jnp.concatenate([a, r], axis=0), a_t, r_t)
        s_b = each(lambda x, kb, qi: mm_nt(x, bd(kb * qi)), ar, kna, q_inv)
        tick()
        s_k = each(lambda x, kk, qi: mm_nt(x, bd(kk * qi)), ar, k_c, q_inv)
        tick()
        n = each(lambda x: jnp.where(strict, x[0:c], 0.0), s_b)
        a_ak = each(lambda x: jnp.where(strict, x[0:c], 0.0), s_k)
        m_rb = each(lambda x: jnp.where(incl, x[c:2 * c], 0.0), s_b)
        m_rk = each(lambda x: jnp.where(incl, x[c:2 * c], 0.0), s_k)

        inv = each(lambda x: eye + jnp.where(level_masks[0], x, 0.0), n)
        for lm in level_masks[1:]:
            half = each(lambda d, x: mm(d, bd(jnp.where(lm, x, 0.0))), inv, n)
            tick()
            inv = each(lambda d, hf: d + mm(hf, bd(d)), inv, half)
            tick()

        v_bd = each(bd, v_c)
        akv = each(mm, a_ak, v_bd)
        tick()
        wt = each(lambda d, a: mm(d, bd(a)), inv, a_t)
        tick()
        u0 = each(lambda d, x: mm(d, bd(x)), inv, akv)
        tick()
        o0 = each(mm, m_rk, v_bd)
        bh = each(lambda x, p: x * p, kna, p_rem)
        kh = each(lambda x, p: x * p, k_c, p_rem)
        for _ in side:
            pass
        for ref, vals in zip((wt_s, u0_s, o0_s, mrb_s, rt_s, bh_s, kh_s),
                             (wt, u0, o0, m_rb, r_t, bh, kh)):
            for (b, _), sl, val in zip(chains, sls, vals):
                ref[b, sl, :] = val
        for (b, ci), xl in zip(chains, cl_last):
            pc_s[b, pl.ds(ci * SUBLANES, SUBLANES), :] = jnp.broadcast_to(jnp.exp(xl), (SUBLANES, g))

    def phase_b_steps(chunk_ids):
        bs = list(range(nb))
        for ci in chunk_ids:
            sl = pl.ds(ci * c, c)
            ht = [ht_s[b] for b in bs]
            wr = [jnp.concatenate([wt_s[b, sl, :], rt_s[b, sl, :]], axis=0) for b in bs]
            wrh = each(lambda x, hh: mm_nt(x, hh.astype(BF16)), wr, ht)
            yield
            u = [x[0:c] + u0_s[b, sl, :] for x, b in zip(wrh, bs)]
            uv_t = [jnp.concatenate([x, v_s[b, sl, :]], axis=0).T for x, b in zip(u, bs)]
            bk = [jnp.concatenate([bh_s[b, sl, :], kh_s[b, sl, :]], axis=0) for b in bs]
            upd = each(lambda x, y: mm(x, y.astype(BF16)), uv_t, bk)
            mu_ = [mm(mrb_s[b, sl, :], bd(x)) for x, b in zip(u, bs)]
            yield
            for b in bs:
                p_c = pc_s[b, pl.ds(ci * SUBLANES, 1), :]
                y_s[b, sl, :] = wrh[b][c:2 * c] + mu_[b] + o0_s[b, sl, :]
                ht_s[b] = ht[b] * p_c + bdm_f * upd[b]
            yield

    pending = iter(())
    for gi in range(n_chunks // RWKV_UNROLL):
        chunk_ids = range(gi * RWKV_UNROLL, (gi + 1) * RWKV_UNROLL)
        phase_a([(b, ci) for ci in chunk_ids for b in range(nb)], pending)
        pending = phase_b_steps(chunk_ids)
    for _ in pending:
        pass

    inv_n = 1.0 / HEAD_DIM
    for b in range(nb):
        y = y_s[b]
        mean = _dot_sel_rhs(y, bdm) * inv_n
        yc = y - mean
        var = _dot_sel_rhs(yc * yc, bdm) * inv_n
        yn = yc * lax.rsqrt(var + GN_EPS) * lng_ref[...] + lnb_ref[...]
        bonus = _dot_sel_rhs(r_s[b] * k_s[b] * rk_ref[...], bdm) * v_s[b]
        o_ref[b] = yn + bonus


def _rwkv(p_rkv, misc, prm, bdm, *, tt):
    b, s, _ = p_rkv.shape
    g = D_GROUP
    seq = lambda width: pl.BlockSpec((b, tt, width), lambda ti: (0, ti, 0))
    big = pltpu.VMEM((b, tt, g), F32)
    return pl.pallas_call(
        functools.partial(_rwkv_kernel, tt=tt, nb=b),
        grid=(s // tt,),
        in_specs=[seq(N_RKV), seq(LANES)] + [_full_spec(a) for a in prm] + [_full_spec(bdm)],
        out_specs=seq(g),
        out_shape=jax.ShapeDtypeStruct((b, s, g), F32),
        scratch_shapes=[big] * 13
        + [pltpu.VMEM((b, tt // RWKV_CHUNK * SUBLANES, g), F32), big,
           pltpu.VMEM((b, g, g), F32)],
        compiler_params=_params(1),
        name="rwkv7",
    )(p_rkv, misc, *prm, bdm)


def _lru_tile(x_ref, cw_ref, cb_ref, wa_ref, ba_ref, wx_ref, bx_ref, lam_ref, o_ref,
              pad, a_s, u_s, h_s, *, tt, nb):
    @pl.when(pl.program_id(0) == 0)
    def _():
        pad[:, 0:SUBLANES, :] = jnp.zeros((nb, SUBLANES, D_GROUP), F32)
        h_s[...] = jnp.zeros_like(h_s)

    for b in range(nb):
        x = x_ref[b]
        pad[b, SUBLANES:SUBLANES + tt, :] = x
        xc = cw_ref[CONV_WIDTH - 1:CONV_WIDTH, :] * x + cb_ref[...]
        for d in range(1, CONV_WIDTH):
            tap = CONV_WIDTH - 1 - d
            xc = xc + cw_ref[tap:tap + 1, :] * pad[b, SUBLANES - d:SUBLANES - d + tt, :]
        pad[b, 0:SUBLANES, :] = x[tt - SUBLANES:tt, :]

        xb = xc.astype(BF16)
        r = _sigmoid(_dot(xb, wa_ref[...]) + ba_ref[...])
        i = _sigmoid(_dot(xb, wx_ref[...]) + bx_ref[...])
        log_a = -LRU_C * r * _softplus(-lam_ref[...])
        a_s[b] = jnp.exp(log_a)
        th = jnp.tanh(log_a)
        u_s[b] = jnp.sqrt(-2.0 * th / (1.0 - th)) * (i * xc)

    row = _iota((SUBLANES, D_GROUP), 0)

    def rows_above(x, k, fill):
        return jnp.where(row >= k, pltpu.roll(x, k, axis=0), fill)

    def group(gi, hs):
        base = pl.multiple_of(gi * SUBLANES, SUBLANES)
        carry = []
        for b in range(nb):
            a = a_s[b, pl.ds(base, SUBLANES), :]
            u = u_s[b, pl.ds(base, SUBLANES), :]
            k = 1
            while k < SUBLANES:
                u = a * rows_above(u, k, 0.0) + u
                a = a * rows_above(a, k, 1.0)
                k *= 2
            h = a * hs[b] + u
            o_ref[b, pl.ds(base, SUBLANES), :] = h
            carry.append(h[SUBLANES - 1:SUBLANES, :])
        return tuple(carry)

    hs = lax.fori_loop(0, tt // SUBLANES, group, tuple(h_s[b] for b in range(nb)))
    for b in range(nb):
        h_s[b] = hs[b]


def _outproj_kernel(x_ref, yft_ref, yst_ref, yr_ref, lx_ref, gates_ref, w_ref, fg_ref,
                    cw_ref, cb_ref, wa_ref, ba_ref, wx_ref, bx_ref, lam_ref, o_ref,
                    pad, a_s, u_s, yl_s, h_s, *, final, tm, nb):
    _lru_tile(lx_ref, cw_ref, cb_ref, wa_ref, ba_ref, wx_ref, bx_ref, lam_ref, yl_s,
              pad, a_s, u_s, h_s, tt=tm, nb=nb)
    for b in range(nb):
        acc = x_ref[b]
        ys = (yft_ref[b].T, yst_ref[b].T, yr_ref[b], yl_s[b])
        for gi, y in enumerate(ys):
            gate = gates_ref[b, :, gi * D_GROUP:(gi + 1) * D_GROUP]
            y = y * (gate * _sigmoid(gate))
            acc = acc + _dot(y.astype(BF16), w_ref[gi * D_GROUP:(gi + 1) * D_GROUP, :])
        if final:
            ms = jnp.mean(acc * acc, axis=-1, keepdims=True)
            acc = acc * lax.rsqrt(ms + RMS_EPS) * fg_ref[...]
        o_ref[b] = acc


def _outproj(x, y_fox_t, y_sb_t, y_rw, lx, gates, w, final_g, lru_prm, *, tm, final):
    b, s, _ = x.shape
    g = D_GROUP
    seq = lambda width: pl.BlockSpec((b, tm, width), lambda i: (0, i, 0))
    chan = pl.BlockSpec((b, g, tm), lambda i: (0, 0, i))
    big = pltpu.VMEM((b, tm, g), F32)
    return pl.pallas_call(
        functools.partial(_outproj_kernel, final=final, tm=tm, nb=b),
        grid=(s // tm,),
        in_specs=[seq(D_MODEL), chan, chan, seq(g), seq(g), seq(4 * g),
                  _full_spec(w), _full_spec(final_g)] + [_full_spec(a) for a in lru_prm],
        out_specs=seq(D_MODEL),
        out_shape=jax.ShapeDtypeStruct((b, s, D_MODEL), F32),
        scratch_shapes=[pltpu.VMEM((b, tm + SUBLANES, g), F32), big, big, big,
                        pltpu.VMEM((b, 1, g), F32)],
        compiler_params=_params(1),
        name="outproj",
    )(x, y_fox_t, y_sb_t, y_rw, lx, gates, w, final_g, *lru_prm)


def _w_in_segments():
    g, h, r = D_GROUP, N_HEADS, RWKV_LORA
    o_ff = 4 * g
    o_sb = o_ff + h
    o_rw = o_sb + 4 * g
    o_rg = o_rw + 3 * g + 2 * r
    o_lx = o_rg + g
    o_lg = o_lx + g
    segs = [(0, C_FQ, 2 * g), (o_sb, C_SQ, 2 * g), (2 * g, C_FV, g), (o_sb + 2 * g, C_SV, g),
             (3 * g, C_GATES, g), (o_sb + 3 * g, C_GATES + g, g), (o_rg, C_GATES + 2 * g, g),
             (o_lg, C_GATES + 3 * g, g),
             (o_rw, C_RKV, 3 * g), (o_lx, C_LX, g),
             (o_rw + 3 * g, C_MISC, 2 * r), (o_ff, C_MISC + FF_LANE, h)]
    return segs


def _w_in_kernel(w_ref, o_ref):
    o_ref[...] = jnp.zeros_like(o_ref)
    for src, dst, width in _w_in_segments():
        o_ref[0, :, dst:dst + width] = w_ref[0, :, src:src + width].astype(BF16)


def _permute_w_in(w_in, *, tr):
    depth, d, n_in = w_in.shape
    return pl.pallas_call(
        _w_in_kernel,
        grid=(depth, d // tr),
        in_specs=[pl.BlockSpec((1, tr, n_in), lambda l, i: (l, i, 0))],
        out_specs=pl.BlockSpec((1, tr, N_PROJ), lambda l, i: (l, i, 0)),
        out_shape=jax.ShapeDtypeStruct((depth, d, N_PROJ), BF16),
        compiler_params=_params(2),
        name="w_in_layout",
    )(w_in)


def _block_diag(w):
    h, n, _ = w.shape
    eye = jnp.eye(h, dtype=w.dtype)
    return jnp.einsum('hij,hk->hikj', w, eye).reshape(h * n, h * n)


def _pick_tile(s, pref):
    t = pref
    while s % t:
        t //= 2
    return t


def kernel(x, norm_g, w_in, b_forget, rwkv_mu, rwkv_w0, rwkv_w2, rwkv_a0, rwkv_a2, rwkv_k_k,
           rwkv_k_a, rwkv_r_k, rwkv_ln_g, rwkv_ln_b, lru_conv_w, lru_conv_b, lru_w_a, lru_b_a,
           lru_w_x, lru_b_x, lru_lambda, w_out, final_g):
    b, s, d = x.shape
    depth = w_in.shape[0]
    g, h, dh, r = D_GROUP, N_HEADS, HEAD_DIM, RWKV_LORA
    tm = _pick_tile(s, 512)
    tq = _pick_tile(s, 256)
    tt = _pick_tile(s, 512)
    row = lambda a: a.reshape(1, -1).astype(F32)

    bdm = _block_diag(jnp.ones((h, dh, dh), BF16))
    w_in_k = _permute_w_in(w_in, tr=LANES)
    for l in range(depth):
        fbias = jnp.zeros((1, LANES), F32).at[0, FF_LANE:FF_LANE + h].set(b_forget[l])
        mu = rwkv_mu[l]
        consts = [row(norm_g[l]), w_in_k, fbias, row(mu[:N_RKV]),
                  jnp.zeros((1, LANES), F32).at[0, :2 * r].set(mu[N_RKV:])]
        (fq, fk, qaug, kaug, fvt, sq, sk, svt, gates, rkv, lx, misc, f2,
         norms) = _inproj(x, consts, l, tm=tm)

        per_head = lambda a: a.transpose(0, 2, 1).reshape(-1)
        fend = per_head(f2[:, tq - 1::tq, FF_LANE:FF_LANE + h])
        qmax = per_head(jnp.repeat(jnp.sqrt(norms[:, :, 0, 0:h]), tm // tq, axis=1))
        kmax = jnp.sqrt(jnp.max(norms[:, :, 0, h:2 * h], axis=1)).reshape(-1)
        y_fox_t = _fox_attention(fend, qmax, kmax, fq, qaug, fk, kaug, fvt, tq=tq, tk=tq)
        y_sb_t = _sb_attention(sq, sk, svt, tq=tq)

        pad_rows = lambda a, lo: jnp.zeros((LANES, g), F32).at[lo:lo + r].set(a).astype(BF16)
        rw_prm = [row(rwkv_w0[l]), pad_rows(rwkv_w2[l], 0), row(rwkv_a0[l]), pad_rows(rwkv_a2[l], r),
                  row(rwkv_k_k[l]), row(rwkv_k_a[l]), row(rwkv_r_k[l]), row(rwkv_ln_g[l]),
                  row(rwkv_ln_b[l])]
        y_rw = _rwkv(rkv, misc, rw_prm, bdm, tt=tt)

        lru_prm = [lru_conv_w[l].astype(F32), row(lru_conv_b[l]),
                   _block_diag(lru_w_a[l]).astype(BF16), row(lru_b_a[l]),
                   _block_diag(lru_w_x[l]).astype(BF16), row(lru_b_x[l]), row(lru_lambda[l])]
        x = _outproj(x, y_fox_t, y_sb_t, y_rw, lx, gates, w_out[l].astype(BF16), row(final_g),
                     lru_prm, tm=tm, final=(l == depth - 1))
    return x
```
